```python
import math
import jax, jax.numpy as jnp
from jax import lax
import numpy as np

D_MODEL = 2048
BATCH = 4
SEQ = 4096
DEPTH = 2

EPS = 1e-6
S5_WIDTH = D_MODEL // 4
S5_GROUP = 16
S5_GROUPS = S5_WIDTH // S5_GROUP
S5_STATE = 64
GDN_HEAD_DIM = 128
GDN_WIDTH = D_MODEL // 4
GDN_HEADS = GDN_WIDTH // GDN_HEAD_DIM
GDN_CONV = 4
GDN_CHUNK = 64
NSA_HEAD_DIM = 64
NSA_WIDTH = D_MODEL // 2
NSA_HEADS = NSA_WIDTH // NSA_HEAD_DIM
NSA_KV_HEADS = 4
NSA_KV_WIDTH = NSA_KV_HEADS * NSA_HEAD_DIM
CMP_LEN = 32
CMP_STRIDE = 16
CMP_HIDDEN = 256
SLC_BLOCK = 64
SLC_TOPK = 16
WINDOW = 512
Q_BLOCK = 128
ROPE_THETA = 10000.0
BIG = 1e9
N_GROUPS = 4
EXPERTS_PER_GROUP = 8
N_EXPERTS = N_GROUPS * EXPERTS_PER_GROUP
TOP_K = 2
D_EXPERT = 512
MOE_BLOCK = 128
IN_SIZES = (S5_WIDTH, 3 * GDN_WIDTH, GDN_WIDTH, GDN_HEADS, GDN_HEADS, NSA_WIDTH, 6 * NSA_KV_WIDTH, 3 * NSA_HEADS)
D_IN = sum(IN_SIZES)

kernel_name = 'hybrid_s5_gdn_nsa_hmoe_block'


def rms_norm(x, gain):
    xf = x.astype(jnp.float32)
    y = xf * lax.rsqrt(jnp.mean(xf * xf, axis=-1, keepdims=True) + EPS)
    return (y * gain.astype(jnp.float32)).astype(x.dtype)


def l2_norm(x):
    return x * lax.rsqrt(jnp.sum(x * x, axis=-1, keepdims=True) + EPS)


def rope_tables(pos, dim):
    inv_freq = ROPE_THETA ** (-jnp.arange(0, dim, 2, dtype=jnp.float32) / dim)
    ang = pos.astype(jnp.float32)[:, None] * inv_freq[None, :]
    return jnp.cos(ang)[:, None, :], jnp.sin(ang)[:, None, :]


def rope(x, cos, sin):
    x1, x2 = jnp.split(x, 2, axis=-1)
    return jnp.concatenate([x1 * cos - x2 * sin, x2 * cos + x1 * sin], axis=-1).astype(x.dtype)


def masked_softmax(s, mask):
    s = jnp.where(mask, s, -jnp.inf)
    m = jnp.max(s, axis=-1, keepdims=True)
    m = jnp.where(jnp.isfinite(m), m, 0.0)
    p = jnp.exp(s - m)
    return p / jnp.maximum(jnp.sum(p, axis=-1, keepdims=True), jnp.finfo(jnp.float32).tiny)


def s5_mixer(u, lam_re, lam_im, b_re, b_im, c_re, c_im, d_skip, log_dt, glu_w, glu_b):
    Bsz, S, _ = u.shape
    f32 = jnp.float32
    lam = lax.complex(lam_re.astype(f32), lam_im.astype(f32))
    dt = jnp.exp(log_dt.astype(f32))[:, None]
    lam_bar = jnp.exp(lam * dt)
    b = lax.complex(b_re.astype(f32), b_im.astype(f32))
    b_bar = ((lam_bar - 1.0) / lam)[:, :, None] * b
    c = lax.complex(c_re.astype(f32), c_im.astype(f32))
    ug = u.astype(f32).reshape(Bsz, S, S5_GROUPS, S5_GROUP)
    bu = jnp.einsum('gph,bsgh->bsgp', b_bar, ug.astype(jnp.complex64))
    a = jnp.broadcast_to(lam_bar, bu.shape)

    def combine(left, right):
        a_l, b_l = left
        a_r, b_r = right
        return a_r * a_l, a_r * b_l + b_r

    _, states = lax.associative_scan(combine, (a, bu), axis=1)
    y = jnp.einsum('ghp,bsgp->bsgh', c, states).real + d_skip.astype(f32).reshape(S5_GROUPS, S5_GROUP) * ug
    y = jax.nn.gelu(y.reshape(Bsz, S, S5_WIDTH))
    y = y * jax.nn.sigmoid(y @ glu_w.astype(f32) + glu_b.astype(f32))
    return y.astype(u.dtype)


def short_causal_conv(x, w):
    C = x.shape[-1]
    return lax.conv_general_dilated(x, w[:, None, :].astype(x.dtype), window_strides=(1,),
                                    padding=[(GDN_CONV - 1, 0)], dimension_numbers=('NWC', 'WIO', 'NWC'),
                                    feature_group_count=C)


def gated_delta_rule(q, k, v, g, beta):
    Bsz, S, H, dk = q.shape
    dv = v.shape[-1]
    C = GDN_CHUNK
    N = S // C

    def chunks(t):
        return t.reshape(Bsz, N, C, H, -1).transpose(1, 0, 3, 2, 4)

    qc, kc, vc = chunks(q), chunks(k), chunks(v)
    gc = g.reshape(Bsz, N, C, H).transpose(1, 0, 3, 2)
    bc = beta.reshape(Bsz, N, C, H).transpose(1, 0, 3, 2)
    gcum = jnp.cumsum(gc, axis=-1)
    incl = jnp.tril(jnp.ones((C, C), dtype=bool))
    strict = jnp.tril(jnp.ones((C, C), dtype=bool), -1)
    decay = jnp.exp(jnp.where(incl, gcum[..., :, None] - gcum[..., None, :], -jnp.inf))
    k_beta = kc * bc[..., None]
    a_mat = jnp.where(strict, jnp.einsum('nbhid,nbhjd->nbhij', k_beta, kc) * decay, 0.0)
    eye = jnp.eye(C, dtype=jnp.float32)
    t_mat = lax.linalg.triangular_solve(a_mat + eye, jnp.broadcast_to(eye, a_mat.shape),
                                        left_side=True, lower=True, unit_diagonal=True)
    u = t_mat @ (vc * bc[..., None])
    w = t_mat @ (k_beta * jnp.exp(gcum)[..., None])
    qk = jnp.einsum('nbhid,nbhjd->nbhij', qc, kc) * decay

    def step(state, inp):
        q_n, k_n, u_n, w_n, g_n, qk_n = inp
        v_new = u_n - w_n @ state
        o = (q_n * jnp.exp(g_n)[..., None]) @ state + qk_n @ v_new
        g_last = g_n[..., -1:]
        state = state * jnp.exp(g_last)[..., None] + jnp.einsum(
            'bhcd,bhce->bhde', k_n * jnp.exp(g_last - g_n)[..., None], v_new)
        return state, o

    state0 = jnp.zeros((Bsz, H, dk, dv), jnp.float32)
    _, o = lax.scan(step, state0, (qc, kc, u, w, gcum, qk))
    return o.transpose(1, 0, 3, 2, 4).reshape(Bsz, S, H, dv)


def gdn_mixer(qkv, z, a, b, conv_w, a_log, dt_bias, norm_w):
    Bsz, S, _ = qkv.shape
    f32 = jnp.float32
    qkv = jax.nn.silu(short_causal_conv(qkv, conv_w)).astype(f32)
    q, k, v = (t.reshape(Bsz, S, GDN_HEADS, GDN_HEAD_DIM) for t in jnp.split(qkv, 3, axis=-1))
    q = l2_norm(q) * GDN_HEAD_DIM ** -0.5
    k = l2_norm(k)
    beta = jax.nn.sigmoid(b.astype(f32))
    g = -jnp.exp(a_log.astype(f32)) * jax.nn.softplus(a.astype(f32) + dt_bias.astype(f32))
    o = gated_delta_rule(q, k, v, g, beta)
    o = rms_norm(o, norm_w) * jax.nn.silu(z.astype(f32).reshape(Bsz, S, GDN_HEADS, GDN_HEAD_DIM))
    return o.reshape(Bsz, S, GDN_WIDTH).astype(z.dtype)


def nsa_mixer(q, kv, gate_logits, q_norm, k_norm, cmp_pe, cmp_w1, cmp_w2, cos, sin):
    Bsz, S, _ = q.shape
    f32 = jnp.float32
    G, R, dh = NSA_KV_HEADS, NSA_HEADS // NSA_KV_HEADS, NSA_HEAD_DIM
    scale = dh ** -0.5
    q = rope(rms_norm(q.reshape(Bsz, S, NSA_HEADS, dh), q_norm), cos, sin)
    kc_raw, vc_raw, ks, vs, kw, vw = (t.reshape(Bsz, S, G, dh) for t in jnp.split(kv, 6, axis=-1))
    ks = rope(rms_norm(ks, k_norm[1]), cos, sin)
    kw = rope(rms_norm(kw, k_norm[2]), cos, sin)

    n_cmp = (S - CMP_LEN) // CMP_STRIDE + 1
    win_idx = jnp.arange(n_cmp)[:, None] * CMP_STRIDE + jnp.arange(CMP_LEN)[None, :]
    cmp_end = win_idx[:, -1]

    def compress(t, pe, w1, w2):
        blocks = t[:, win_idx] + pe[:, None, :]
        blocks = jnp.moveaxis(blocks, 3, 2).reshape(Bsz, n_cmp, G, CMP_LEN * dh)
        return jax.nn.gelu(blocks @ w1) @ w2

    cos_c, sin_c = rope_tables(cmp_end, dh)
    k_cmp = rope(rms_norm(compress(kc_raw, cmp_pe[0], cmp_w1[0], cmp_w2[0]), k_norm[0]), cos_c, sin_c)
    v_cmp = compress(vc_raw, cmp_pe[1], cmp_w1[1], cmp_w2[1])

    n_slc = S // SLC_BLOCK
    n_top = min(SLC_TOPK, n_slc)
    c0 = jnp.arange(n_cmp) * CMP_STRIDE
    s0 = jnp.arange(n_slc) * SLC_BLOCK
    overlap = jnp.clip(jnp.minimum(c0[:, None] + CMP_LEN, s0[None, :] + SLC_BLOCK)
                       - jnp.maximum(c0[:, None], s0[None, :]), 0, None).astype(f32) / CMP_LEN
    k_blocks = ks.reshape(Bsz, n_slc, SLC_BLOCK, G, dh).transpose(0, 3, 1, 2, 4)
    v_blocks = vs.reshape(Bsz, n_slc, SLC_BLOCK, G, dh).transpose(0, 3, 1, 2, 4)
    k_pad = jnp.pad(kw, ((0, 0), (WINDOW, 0), (0, 0), (0, 0)))
    v_pad = jnp.pad(vw, ((0, 0), (WINDOW, 0), (0, 0), (0, 0)))
    gates = jax.nn.sigmoid(gate_logits.astype(f32)).reshape(Bsz, S, G, R, 3)

    n_qb = S // Q_BLOCK
    qb = q.reshape(Bsz, n_qb, Q_BLOCK, G, R, dh).transpose(1, 0, 2, 3, 4, 5)
    gb = gates.reshape(Bsz, n_qb, Q_BLOCK, G, R, 3).transpose(1, 0, 2, 3, 4, 5)
    b_ix = jnp.arange(Bsz)[:, None, None, None]
    g_ix = jnp.arange(G)[None, None, :, None]
    j_ix = jnp.arange(n_slc)
    n_sel_keys = n_top * SLC_BLOCK

    def attend_block(args):
        q_blk, g_blk, blk = args
        t = blk * Q_BLOCK + jnp.arange(Q_BLOCK)
        s = jnp.einsum('bqgrd,bngd->bqgrn', q_blk, k_cmp).astype(f32) * scale
        valid = (cmp_end[None, :] <= t[:, None])[None, :, None, None, :]
        p_cmp = masked_softmax(s, valid)
        o_cmp = jnp.einsum('bqgrn,bngd->bqgrd', p_cmp.astype(v_cmp.dtype), v_cmp)
        imp = jnp.einsum('bqgrn,nj->bqgj', p_cmp, overlap)
        cur = (t // SLC_BLOCK)[:, None]
        forced = (j_ix[None, :] == 0) | (j_ix[None, :] == cur) | (j_ix[None, :] == cur - 1)
        started = s0[None, :] <= t[:, None]
        imp = jnp.where(forced[None, :, None, :], BIG, jnp.where(started[None, :, None, :], imp, -BIG))
        _, sel = lax.top_k(imp, n_top)
        k_sel = k_blocks[b_ix, g_ix, sel].reshape(Bsz, Q_BLOCK, G, n_sel_keys, dh)
        v_sel = v_blocks[b_ix, g_ix, sel].reshape(Bsz, Q_BLOCK, G, n_sel_keys, dh)
        pos = (sel[..., None] * SLC_BLOCK + jnp.arange(SLC_BLOCK)).reshape(Bsz, Q_BLOCK, G, n_sel_keys)
        s = jnp.einsum('bqgrd,bqgkd->bqgrk', q_blk, k_sel).astype(f32) * scale
        ok = (pos <= t[None, :, None, None])[:, :, :, None, :]
        p = masked_softmax(s, ok)
        o_slc = jnp.einsum('bqgrk,bqgkd->bqgrd', p.astype(v_sel.dtype), v_sel)
        k_win = lax.dynamic_slice_in_dim(k_pad, blk * Q_BLOCK, Q_BLOCK + WINDOW, axis=1)
        v_win = lax.dynamic_slice_in_dim(v_pad, blk * Q_BLOCK, Q_BLOCK + WINDOW, axis=1)
        kpos = blk * Q_BLOCK - WINDOW + jnp.arange(Q_BLOCK + WINDOW)
        diff = t[:, None] - kpos[None, :]
        in_win = (diff >= 0) & (diff < WINDOW) & (kpos[None, :] >= 0)
        s = jnp.einsum('bqgrd,bkgd->bqgrk', q_blk, k_win).astype(f32) * scale
        p = masked_softmax(s, in_win[None, :, None, None, :])
        o_win = jnp.einsum('bqgrk,bkgd->bqgrd', p.astype(v_win.dtype), v_win)
        o = g_blk[..., 0:1] * o_cmp + g_blk[..., 1:2] * o_slc + g_blk[..., 2:3] * o_win
        return o.astype(q_blk.dtype)

    out = lax.map(attend_block, (qb, gb, jnp.arange(n_qb)))
    return out.transpose(1, 0, 2, 3, 4, 5).reshape(Bsz, S, NSA_WIDTH)


def hybrid_mixer(a, w_in, lam_re, lam_im, b_re, b_im, c_re, c_im, s5_d, s5_log_dt, glu_w, glu_b,
                 s5_out_norm, conv_w, a_log, dt_bias, gdn_norm_w, q_norm, k_norm, cmp_pe, cmp_w1,
                 cmp_w2, nsa_out_norm, w_out, cos, sin):
    proj = a @ w_in
    u, qkv, z, ga, gb, q, kv, gate_logits = jnp.split(proj, np.cumsum(IN_SIZES)[:-1].tolist(), axis=-1)
    y_s5 = rms_norm(s5_mixer(u, lam_re, lam_im, b_re, b_im, c_re, c_im, s5_d, s5_log_dt, glu_w, glu_b), s5_out_norm)
    y_gdn = gdn_mixer(qkv, z, ga, gb, conv_w, a_log, dt_bias, gdn_norm_w)
    y_nsa = rms_norm(nsa_mixer(q, kv, gate_logits, q_norm, k_norm, cmp_pe, cmp_w1, cmp_w2, cos, sin), nsa_out_norm)
    return jnp.concatenate([y_s5, y_gdn.astype(y_s5.dtype), y_nsa.astype(y_s5.dtype)], axis=-1) @ w_out


def hier_moe(h, group_w, group_b, expert_w, expert_b, w_gate, w_up, w_down):
    Bsz, S, D = h.shape
    f32 = jnp.float32
    x = h.reshape(-1, D)
    T = x.shape[0]
    g_logits = (x @ group_w).astype(f32) + group_b.astype(f32)
    grp = jnp.argmax(g_logits, axis=-1)
    p_grp = jnp.take_along_axis(jax.nn.softmax(g_logits, axis=-1), grp[:, None], axis=-1)
    e_logits = ((x @ expert_w).astype(f32) + expert_b.astype(f32)).reshape(T, N_GROUPS, EXPERTS_PER_GROUP)
    e_logits = jnp.take_along_axis(e_logits, grp[:, None, None], axis=1)[:, 0]
    top_logit, top_local = lax.top_k(e_logits, TOP_K)
    weights = jax.nn.softmax(top_logit, axis=-1) * p_grp
    experts = grp[:, None] * EXPERTS_PER_GROUP + top_local
    n_assign = T * TOP_K
    e_flat = experts.reshape(-1)
    w_flat = weights.reshape(-1)
    tok_flat = jnp.repeat(jnp.arange(T, dtype=jnp.int32), TOP_K)
    order = jnp.argsort(e_flat)
    e_sorted = e_flat[order]
    counts = jnp.bincount(e_flat, length=N_EXPERTS)
    padded = (counts + MOE_BLOCK - 1) // MOE_BLOCK * MOE_BLOCK
    pad_start = jnp.cumsum(padded) - padded
    start = jnp.cumsum(counts) - counts
    dest = pad_start[e_sorted] + jnp.arange(n_assign) - start[e_sorted]
    n_blk = (n_assign + MOE_BLOCK - 1) // MOE_BLOCK + N_EXPERTS
    cap = n_blk * MOE_BLOCK
    slot_tok = jnp.full((cap,), T, jnp.int32).at[dest].set(tok_flat[order])
    slot_w = jnp.zeros((cap,), f32).at[dest].set(w_flat[order])
    blk_expert = jnp.minimum(jnp.searchsorted(jnp.cumsum(padded), jnp.arange(n_blk) * MOE_BLOCK, side='right'),
                             N_EXPERTS - 1)
    x_pad = jnp.concatenate([x, jnp.zeros((1, D), x.dtype)], axis=0)

    def expert_block(args):
        tok, wt, e = args
        xt = x_pad[tok]
        hid = jax.nn.silu(xt @ w_gate[e]) * (xt @ w_up[e])
        return (hid @ w_down[e]) * wt[:, None].astype(x.dtype)

    y = lax.map(expert_block, (slot_tok.reshape(n_blk, MOE_BLOCK), slot_w.reshape(n_blk, MOE_BLOCK), blk_expert))
    out = jnp.zeros((T + 1, D), y.dtype).at[slot_tok].add(y.reshape(cap, D))[:T]
    return out.reshape(Bsz, S, D).astype(h.dtype)


def setup_inputs(seed: int = 0) -> dict:
    key = jax.random.key(seed)
    ks = jax.random.split(key, 40)
    L, f32 = DEPTH, jnp.float32

    def nrm(k, shape, s):
        return s * jax.random.normal(k, shape, f32)

    def gain(k, shape):
        return 1.0 + 0.02 * jax.random.normal(k, shape, f32)

    log_dt_lo, log_dt_hi = math.log(1e-3), math.log(1e-1)
    gdn_dt = jnp.exp(jax.random.uniform(ks[15], (L, GDN_HEADS), f32, log_dt_lo, log_dt_hi))
    return {
        'x': jax.random.normal(ks[0], (BATCH, SEQ, D_MODEL), f32),
        'ln_mix': gain(ks[1], (L, D_MODEL)),
        'w_in': nrm(ks[2], (L, D_MODEL, D_IN), D_MODEL ** -0.5),
        's5_lambda_re': -0.5 + nrm(ks[3], (L, S5_GROUPS, S5_STATE), 0.01),
        's5_lambda_im': jnp.pi * jnp.arange(S5_STATE, dtype=f32) + nrm(ks[4], (L, S5_GROUPS, S5_STATE), 0.01),
        's5_b_re': nrm(ks[5], (L, S5_GROUPS, S5_STATE, S5_GROUP), (2 * S5_GROUP) ** -0.5),
        's5_b_im': nrm(ks[6], (L, S5_GROUPS, S5_STATE, S5_GROUP), (2 * S5_GROUP) ** -0.5),
        's5_c_re': nrm(ks[7], (L, S5_GROUPS, S5_GROUP, S5_STATE), S5_STATE ** -0.5),
        's5_c_im': nrm(ks[8], (L, S5_GROUPS, S5_GROUP, S5_STATE), S5_STATE ** -0.5),
        's5_d': nrm(ks[9], (L, S5_WIDTH), 0.5),
        's5_log_dt': jax.random.uniform(ks[10], (L, S5_GROUPS), f32, log_dt_lo, log_dt_hi),
        's5_glu_w': nrm(ks[11], (L, S5_WIDTH, S5_WIDTH), S5_WIDTH ** -0.5),
        's5_glu_b': nrm(ks[12], (L, S5_WIDTH), 0.01),
        's5_out_norm': gain(ks[13], (L, S5_WIDTH)),
        'gdn_conv_w': nrm(ks[14], (L, GDN_CONV, 3 * GDN_WIDTH), GDN_CONV ** -0.5),
        'gdn_a_log': jnp.log(jax.random.uniform(ks[16], (L, GDN_HEADS), f32, 1.0, 16.0)),
        'gdn_dt_bias': gdn_dt + jnp.log(-jnp.expm1(-gdn_dt)),
        'gdn_norm_w': gain(ks[17], (L, GDN_HEAD_DIM)),
        'nsa_q_norm': gain(ks[18], (L, NSA_HEAD_DIM)),
        'nsa_k_norm': gain(ks[19], (L, 3, NSA_HEAD_DIM)),
        'nsa_cmp_pe': nrm(ks[20], (L, 2, CMP_LEN, NSA_HEAD_DIM), 0.1),
        'nsa_cmp_w1': nrm(ks[21], (L, 2, CMP_LEN * NSA_HEAD_DIM, CMP_HIDDEN), (CMP_LEN * NSA_HEAD_DIM) ** -0.5),
        'nsa_cmp_w2': nrm(ks[22], (L, 2, CMP_HIDDEN, NSA_HEAD_DIM), CMP_HIDDEN ** -0.5),
        'nsa_out_norm': gain(ks[23], (L, NSA_WIDTH)),
        'w_out': nrm(ks[24], (L, D_MODEL, D_MODEL), D_MODEL ** -0.5),
        'ln_ffn': gain(ks[25], (L, D_MODEL)),
        'moe_group_w': nrm(ks[26], (L, D_MODEL, N_GROUPS), D_MODEL ** -0.5),
        'moe_group_b': nrm(ks[27], (L, N_GROUPS), 0.01),
        'moe_expert_w': nrm(ks[28], (L, D_MODEL, N_EXPERTS), D_MODEL ** -0.5),
        'moe_expert_b': nrm(ks[29], (L, N_EXPERTS), 0.01),
        'moe_w_gate': nrm(ks[30], (L, N_EXPERTS, D_MODEL, D_EXPERT), D_MODEL ** -0.5),
        'moe_w_up': nrm(ks[31], (L, N_EXPERTS, D_MODEL, D_EXPERT), D_MODEL ** -0.5),
        'moe_w_down': nrm(ks[32], (L, N_EXPERTS, D_EXPERT, D_MODEL), D_EXPERT ** -0.5),
    }


def reference(x, ln_mix, w_in, s5_lambda_re, s5_lambda_im, s5_b_re, s5_b_im, s5_c_re, s5_c_im, s5_d,
              s5_log_dt, s5_glu_w, s5_glu_b, s5_out_norm, gdn_conv_w, gdn_a_log, gdn_dt_bias, gdn_norm_w,
              nsa_q_norm, nsa_k_norm, nsa_cmp_pe, nsa_cmp_w1, nsa_cmp_w2, nsa_out_norm, w_out, ln_ffn,
              moe_group_w, moe_group_b, moe_expert_w, moe_expert_b, moe_w_gate, moe_w_up, moe_w_down):
    S = x.shape[1]
    cos, sin = rope_tables(jnp.arange(S), NSA_HEAD_DIM)
    for l in range(DEPTH):
        y = hybrid_mixer(rms_norm(x, ln_mix[l]), w_in[l], s5_lambda_re[l], s5_lambda_im[l], s5_b_re[l],
                         s5_b_im[l], s5_c_re[l], s5_c_im[l], s5_d[l], s5_log_dt[l], s5_glu_w[l], s5_glu_b[l],
                         s5_out_norm[l], gdn_conv_w[l], gdn_a_log[l], gdn_dt_bias[l], gdn_norm_w[l],
                         nsa_q_norm[l], nsa_k_norm[l], nsa_cmp_pe[l], nsa_cmp_w1[l], nsa_cmp_w2[l],
                         nsa_out_norm[l], w_out[l], cos, sin)
        h = x + y.astype(x.dtype)
        x = h + hier_moe(rms_norm(h, ln_ffn[l]), moe_group_w[l], moe_group_b[l], moe_expert_w[l],
                         moe_expert_b[l], moe_w_gate[l], moe_w_up[l], moe_w_down[l])
    return x
```

```python
import math
import jax, jax.numpy as jnp
from jax import lax
import numpy as np
from jax.experimental import pallas as pl
from jax.experimental.pallas import tpu as pltpu

D_MODEL = 2048
BATCH = 4
SEQ = 4096
DEPTH = 2

EPS = 1e-6
S5_WIDTH = D_MODEL // 4
S5_GROUP = 16
S5_GROUPS = S5_WIDTH // S5_GROUP
S5_STATE = 64
GDN_HEAD_DIM = 128
GDN_WIDTH = D_MODEL // 4
GDN_HEADS = GDN_WIDTH // GDN_HEAD_DIM
GDN_CONV = 4
GDN_CHUNK = 64
NSA_HEAD_DIM = 64
NSA_WIDTH = D_MODEL // 2
NSA_HEADS = NSA_WIDTH // NSA_HEAD_DIM
NSA_KV_HEADS = 4
NSA_KV_WIDTH = NSA_KV_HEADS * NSA_HEAD_DIM
CMP_LEN = 32
CMP_STRIDE = 16
CMP_HIDDEN = 256
SLC_BLOCK = 64
SLC_TOPK = 16
WINDOW = 512
Q_BLOCK = 128
ROPE_THETA = 10000.0
BIG = 1e9
N_GROUPS = 4
EXPERTS_PER_GROUP = 8
N_EXPERTS = N_GROUPS * EXPERTS_PER_GROUP
TOP_K = 2
D_EXPERT = 512
MOE_BLOCK = 128
IN_SIZES = (S5_WIDTH, 3 * GDN_WIDTH, GDN_WIDTH, GDN_HEADS, GDN_HEADS, NSA_WIDTH, 6 * NSA_KV_WIDTH, 3 * NSA_HEADS)
D_IN = sum(IN_SIZES)


def _matmul_body(a_ref, w_ref, o_ref):
    o_ref[...] = jnp.dot(a_ref[...].astype(jnp.bfloat16), w_ref[...].astype(jnp.bfloat16),
                         preferred_element_type=jnp.float32)


def _matmul(a, w, tm=512, tn=512):
    M, K = a.shape
    N = w.shape[1]
    n_pad = (-N) % tn
    if n_pad:
        w = jnp.pad(w, ((0, 0), (0, n_pad)))
    Np = N + n_pad
    out = pl.pallas_call(
        _matmul_body,
        grid=(Np // tn, M // tm),
        in_specs=[pl.BlockSpec((tm, K), lambda j, i: (i, 0)),
                  pl.BlockSpec((K, tn), lambda j, i: (0, j))],
        out_specs=pl.BlockSpec((tm, tn), lambda j, i: (i, j)),
        out_shape=jax.ShapeDtypeStruct((M, Np), jnp.float32),
        compiler_params=pltpu.CompilerParams(vmem_limit_bytes=48 * 1024 * 1024),
    )(a, w)
    return out[:, :N] if n_pad else out


def rms_norm(x, gain):
    xf = x.astype(jnp.float32)
    y = xf * lax.rsqrt(jnp.mean(xf * xf, axis=-1, keepdims=True) + EPS)
    return (y * gain.astype(jnp.float32)).astype(x.dtype)


def l2_norm(x):
    return x * lax.rsqrt(jnp.sum(x * x, axis=-1, keepdims=True) + EPS)


def rope_tables(pos, dim):
    inv_freq = ROPE_THETA ** (-jnp.arange(0, dim, 2, dtype=jnp.float32) / dim)
    ang = pos.astype(jnp.float32)[:, None] * inv_freq[None, :]
    return jnp.cos(ang)[:, None, :], jnp.sin(ang)[:, None, :]


def rope(x, cos, sin):
    x1, x2 = jnp.split(x, 2, axis=-1)
    return jnp.concatenate([x1 * cos - x2 * sin, x2 * cos + x1 * sin], axis=-1).astype(x.dtype)


def masked_softmax(s, mask):
    s = jnp.where(mask, s, -jnp.inf)
    m = jnp.max(s, axis=-1, keepdims=True)
    m = jnp.where(jnp.isfinite(m), m, 0.0)
    p = jnp.exp(s - m)
    return p / jnp.maximum(jnp.sum(p, axis=-1, keepdims=True), jnp.finfo(jnp.float32).tiny)


def s5_mixer(u, lam_re, lam_im, b_re, b_im, c_re, c_im, d_skip, log_dt, glu_w, glu_b):
    Bsz, S, _ = u.shape
    f32 = jnp.float32
    lam = lax.complex(lam_re.astype(f32), lam_im.astype(f32))
    dt = jnp.exp(log_dt.astype(f32))[:, None]
    lam_bar = jnp.exp(lam * dt)
    b = lax.complex(b_re.astype(f32), b_im.astype(f32))
    b_bar = ((lam_bar - 1.0) / lam)[:, :, None] * b
    c = lax.complex(c_re.astype(f32), c_im.astype(f32))
    ug = u.astype(f32).reshape(Bsz, S, S5_GROUPS, S5_GROUP)
    bu = jnp.einsum('gph,bsgh->bsgp', b_bar, ug.astype(jnp.complex64))
    a = jnp.broadcast_to(lam_bar, bu.shape)

    def combine(left, right):
        a_l, b_l = left
        a_r, b_r = right
        return a_r * a_l, a_r * b_l + b_r

    _, states = lax.associative_scan(combine, (a, bu), axis=1)
    y = jnp.einsum('ghp,bsgp->bsgh', c, states).real + d_skip.astype(f32).reshape(S5_GROUPS, S5_GROUP) * ug
    y = jax.nn.gelu(y.reshape(Bsz, S, S5_WIDTH))
    y = y * jax.nn.sigmoid(y @ glu_w.astype(f32) + glu_b.astype(f32))
    return y.astype(u.dtype)


def short_causal_conv(x, w):
    C = x.shape[-1]
    return lax.conv_general_dilated(x, w[:, None, :].astype(x.dtype), window_strides=(1,),
                                    padding=[(GDN_CONV - 1, 0)], dimension_numbers=('NWC', 'WIO', 'NWC'),
                                    feature_group_count=C)


def gated_delta_rule(q, k, v, g, beta):
    Bsz, S, H, dk = q.shape
    dv = v.shape[-1]
    C = GDN_CHUNK
    N = S // C

    def chunks(t):
        return t.reshape(Bsz, N, C, H, -1).transpose(1, 0, 3, 2, 4)

    qc, kc, vc = chunks(q), chunks(k), chunks(v)
    gc = g.reshape(Bsz, N, C, H).transpose(1, 0, 3, 2)
    bc = beta.reshape(Bsz, N, C, H).transpose(1, 0, 3, 2)
    gcum = jnp.cumsum(gc, axis=-1)
    incl = jnp.tril(jnp.ones((C, C), dtype=bool))
    strict = jnp.tril(jnp.ones((C, C), dtype=bool), -1)
    decay = jnp.exp(jnp.where(incl, gcum[..., :, None] - gcum[..., None, :], -jnp.inf))
    k_beta = kc * bc[..., None]
    a_mat = jnp.where(strict, jnp.einsum('nbhid,nbhjd->nbhij', k_beta, kc) * decay, 0.0)
    eye = jnp.eye(C, dtype=jnp.float32)
    t_mat = lax.linalg.triangular_solve(a_mat + eye, jnp.broadcast_to(eye, a_mat.shape),
                                        left_side=True, lower=True, unit_diagonal=True)
    u = t_mat @ (vc * bc[..., None])
    w = t_mat @ (k_beta * jnp.exp(gcum)[..., None])
    qk = jnp.einsum('nbhid,nbhjd->nbhij', qc, kc) * decay

    def step(state, inp):
        q_n, k_n, u_n, w_n, g_n, qk_n = inp
        v_new = u_n - w_n @ state
        o = (q_n * jnp.exp(g_n)[..., None]) @ state + qk_n @ v_new
        g_last = g_n[..., -1:]
        state = state * jnp.exp(g_last)[..., None] + jnp.einsum(
            'bhcd,bhce->bhde', k_n * jnp.exp(g_last - g_n)[..., None], v_new)
        return state, o

    state0 = jnp.zeros((Bsz, H, dk, dv), jnp.float32)
    _, o = lax.scan(step, state0, (qc, kc, u, w, gcum, qk))
    return o.transpose(1, 0, 3, 2, 4).reshape(Bsz, S, H, dv)


def gdn_mixer(qkv, z, a, b, conv_w, a_log, dt_bias, norm_w):
    Bsz, S, _ = qkv.shape
    f32 = jnp.float32
    qkv = jax.nn.silu(short_causal_conv(qkv, conv_w)).astype(f32)
    q, k, v = (t.reshape(Bsz, S, GDN_HEADS, GDN_HEAD_DIM) for t in jnp.split(qkv, 3, axis=-1))
    q = l2_norm(q) * GDN_HEAD_DIM ** -0.5
    k = l2_norm(k)
    beta = jax.nn.sigmoid(b.astype(f32))
    g = -jnp.exp(a_log.astype(f32)) * jax.nn.softplus(a.astype(f32) + dt_bias.astype(f32))
    o = gated_delta_rule(q, k, v, g, beta)
    o = rms_norm(o, norm_w) * jax.nn.silu(z.astype(f32).reshape(Bsz, S, GDN_HEADS, GDN_HEAD_DIM))
    return o.reshape(Bsz, S, GDN_WIDTH).astype(z.dtype)


def nsa_mixer(q, kv, gate_logits, q_norm, k_norm, cmp_pe, cmp_w1, cmp_w2, cos, sin):
    Bsz, S, _ = q.shape
    f32 = jnp.float32
    G, R, dh = NSA_KV_HEADS, NSA_HEADS // NSA_KV_HEADS, NSA_HEAD_DIM
    scale = dh ** -0.5
    q = rope(rms_norm(q.reshape(Bsz, S, NSA_HEADS, dh), q_norm), cos, sin)
    kc_raw, vc_raw, ks, vs, kw, vw = (t.reshape(Bsz, S, G, dh) for t in jnp.split(kv, 6, axis=-1))
    ks = rope(rms_norm(ks, k_norm[1]), cos, sin)
    kw = rope(rms_norm(kw, k_norm[2]), cos, sin)

    n_cmp = (S - CMP_LEN) // CMP_STRIDE + 1
    win_idx = jnp.arange(n_cmp)[:, None] * CMP_STRIDE + jnp.arange(CMP_LEN)[None, :]
    cmp_end = win_idx[:, -1]

    def compress(t, pe, w1, w2):
        blocks = t[:, win_idx] + pe[:, None, :]
        blocks = jnp.moveaxis(blocks, 3, 2).reshape(Bsz, n_cmp, G, CMP_LEN * dh)
        return jax.nn.gelu(blocks @ w1) @ w2

    cos_c, sin_c = rope_tables(cmp_end, dh)
    k_cmp = rope(rms_norm(compress(kc_raw, cmp_pe[0], cmp_w1[0], cmp_w2[0]), k_norm[0]), cos_c, sin_c)
    v_cmp = compress(vc_raw, cmp_pe[1], cmp_w1[1], cmp_w2[1])

    n_slc = S // SLC_BLOCK
    n_top = min(SLC_TOPK, n_slc)
    c0 = jnp.arange(n_cmp) * CMP_STRIDE
    s0 = jnp.arange(n_slc) * SLC_BLOCK
    overlap = jnp.clip(jnp.minimum(c0[:, None] + CMP_LEN, s0[None, :] + SLC_BLOCK)
                       - jnp.maximum(c0[:, None], s0[None, :]), 0, None).astype(f32) / CMP_LEN
    k_blocks = ks.reshape(Bsz, n_slc, SLC_BLOCK, G, dh).transpose(0, 3, 1, 2, 4)
    v_blocks = vs.reshape(Bsz, n_slc, SLC_BLOCK, G, dh).transpose(0, 3, 1, 2, 4)
    k_pad = jnp.pad(kw, ((0, 0), (WINDOW, 0), (0, 0), (0, 0)))
    v_pad = jnp.pad(vw, ((0, 0), (WINDOW, 0), (0, 0), (0, 0)))
    gates = jax.nn.sigmoid(gate_logits.astype(f32)).reshape(Bsz, S, G, R, 3)

    n_qb = S // Q_BLOCK
    qb = q.reshape(Bsz, n_qb, Q_BLOCK, G, R, dh).transpose(1, 0, 2, 3, 4, 5)
    gb = gates.reshape(Bsz, n_qb, Q_BLOCK, G, R, 3).transpose(1, 0, 2, 3, 4, 5)
    b_ix = jnp.arange(Bsz)[:, None, None, None]
    g_ix = jnp.arange(G)[None, None, :, None]
    j_ix = jnp.arange(n_slc)
    n_sel_keys = n_top * SLC_BLOCK

    def attend_block(args):
        q_blk, g_blk, blk = args
        t = blk * Q_BLOCK + jnp.arange(Q_BLOCK)
        s = jnp.einsum('bqgrd,bngd->bqgrn', q_blk, k_cmp).astype(f32) * scale
        valid = (cmp_end[None, :] <= t[:, None])[None, :, None, None, :]
        p_cmp = masked_softmax(s, valid)
        o_cmp = jnp.einsum('bqgrn,bngd->bqgrd', p_cmp.astype(v_cmp.dtype), v_cmp)
        imp = jnp.einsum('bqgrn,nj->bqgj', p_cmp, overlap)
        cur = (t // SLC_BLOCK)[:, None]
        forced = (j_ix[None, :] == 0) | (j_ix[None, :] == cur) | (j_ix[None, :] == cur - 1)
        started = s0[None, :] <= t[:, None]
        imp = jnp.where(forced[None, :, None, :], BIG, jnp.where(started[None, :, None, :], imp, -BIG))
        _, sel = lax.top_k(imp, n_top)
        k_sel = k_blocks[b_ix, g_ix, sel].reshape(Bsz, Q_BLOCK, G, n_sel_keys, dh)
        v_sel = v_blocks[b_ix, g_ix, sel].reshape(Bsz, Q_BLOCK, G, n_sel_keys, dh)
        pos = (sel[..., None] * SLC_BLOCK + jnp.arange(SLC_BLOCK)).reshape(Bsz, Q_BLOCK, G, n_sel_keys)
        s = jnp.einsum('bqgrd,bqgkd->bqgrk', q_blk, k_sel).astype(f32) * scale
        ok = (pos <= t[None, :, None, None])[:, :, :, None, :]
        p = masked_softmax(s, ok)
        o_slc = jnp.einsum('bqgrk,bqgkd->bqgrd', p.astype(v_sel.dtype), v_sel)
        k_win = lax.dynamic_slice_in_dim(k_pad, blk * Q_BLOCK, Q_BLOCK + WINDOW, axis=1)
        v_win = lax.dynamic_slice_in_dim(v_pad, blk * Q_BLOCK, Q_BLOCK + WINDOW, axis=1)
        kpos = blk * Q_BLOCK - WINDOW + jnp.arange(Q_BLOCK + WINDOW)
        diff = t[:, None] - kpos[None, :]
        in_win = (diff >= 0) & (diff < WINDOW) & (kpos[None, :] >= 0)
        s = jnp.einsum('bqgrd,bkgd->bqgrk', q_blk, k_win).astype(f32) * scale
        p = masked_softmax(s, in_win[None, :, None, None, :])
        o_win = jnp.einsum('bqgrk,bkgd->bqgrd', p.astype(v_win.dtype), v_win)
        o = g_blk[..., 0:1] * o_cmp + g_blk[..., 1:2] * o_slc + g_blk[..., 2:3] * o_win
        return o.astype(q_blk.dtype)

    out = lax.map(attend_block, (qb, gb, jnp.arange(n_qb)))
    return out.transpose(1, 0, 2, 3, 4, 5).reshape(Bsz, S, NSA_WIDTH)


def hybrid_mixer(a, w_in, lam_re, lam_im, b_re, b_im, c_re, c_im, s5_d, s5_log_dt, glu_w, glu_b,
                 s5_out_norm, conv_w, a_log, dt_bias, gdn_norm_w, q_norm, k_norm, cmp_pe, cmp_w1,
                 cmp_w2, nsa_out_norm, w_out, cos, sin):
    proj = _matmul(a.reshape(-1, D_MODEL), w_in).reshape(a.shape[0], a.shape[1], D_IN)
    u, qkv, z, ga, gb, q, kv, gate_logits = jnp.split(proj, np.cumsum(IN_SIZES)[:-1].tolist(), axis=-1)
    y_s5 = rms_norm(s5_mixer(u, lam_re, lam_im, b_re, b_im, c_re, c_im, s5_d, s5_log_dt, glu_w, glu_b), s5_out_norm)
    y_gdn = gdn_mixer(qkv, z, ga, gb, conv_w, a_log, dt_bias, gdn_norm_w)
    y_nsa = rms_norm(nsa_mixer(q, kv, gate_logits, q_norm, k_norm, cmp_pe, cmp_w1, cmp_w2, cos, sin), nsa_out_norm)
    cat = jnp.concatenate([y_s5, y_gdn.astype(y_s5.dtype), y_nsa.astype(y_s5.dtype)], axis=-1)
    return _matmul(cat.reshape(-1, D_MODEL), w_out).reshape(cat.shape)


def hier_moe(h, group_w, group_b, expert_w, expert_b, w_gate, w_up, w_down):
    Bsz, S, D = h.shape
    f32 = jnp.float32
    x = h.reshape(-1, D)
    T = x.shape[0]
    g_logits = (x @ group_w).astype(f32) + group_b.astype(f32)
    grp = jnp.argmax(g_logits, axis=-1)
    p_grp = jnp.take_along_axis(jax.nn.softmax(g_logits, axis=-1), grp[:, None], axis=-1)
    e_logits = ((x @ expert_w).astype(f32) + expert_b.astype(f32)).reshape(T, N_GROUPS, EXPERTS_PER_GROUP)
    e_logits = jnp.take_along_axis(e_logits, grp[:, None, None], axis=1)[:, 0]
    top_logit, top_local = lax.top_k(e_logits, TOP_K)
    weights = jax.nn.softmax(top_logit, axis=-1) * p_grp
    experts = grp[:, None] * EXPERTS_PER_GROUP + top_local
    n_assign = T * TOP_K
    e_flat = experts.reshape(-1)
    w_flat = weights.reshape(-1)
    tok_flat = jnp.repeat(jnp.arange(T, dtype=jnp.int32), TOP_K)
    order = jnp.argsort(e_flat)
    e_sorted = e_flat[order]
    counts = jnp.bincount(e_flat, length=N_EXPERTS)
    padded = (counts + MOE_BLOCK - 1) // MOE_BLOCK * MOE_BLOCK
    pad_start = jnp.cumsum(padded) - padded
    start = jnp.cumsum(counts) - counts
    dest = pad_start[e_sorted] + jnp.arange(n_assign) - start[e_sorted]
    n_blk = (n_assign + MOE_BLOCK - 1) // MOE_BLOCK + N_EXPERTS
    cap = n_blk * MOE_BLOCK
    slot_tok = jnp.full((cap,), T, jnp.int32).at[dest].set(tok_flat[order])
    slot_w = jnp.zeros((cap,), f32).at[dest].set(w_flat[order])
    blk_expert = jnp.minimum(jnp.searchsorted(jnp.cumsum(padded), jnp.arange(n_blk) * MOE_BLOCK, side='right'),
                             N_EXPERTS - 1)
    x_pad = jnp.concatenate([x, jnp.zeros((1, D), x.dtype)], axis=0)

    def expert_block(args):
        tok, wt, e = args
        xt = x_pad[tok]
        hid = jax.nn.silu(xt @ w_gate[e]) * (xt @ w_up[e])
        return (hid @ w_down[e]) * wt[:, None].astype(x.dtype)

    y = lax.map(expert_block, (slot_tok.reshape(n_blk, MOE_BLOCK), slot_w.reshape(n_blk, MOE_BLOCK), blk_expert))
    out = jnp.zeros((T + 1, D), y.dtype).at[slot_tok].add(y.reshape(cap, D))[:T]
    return out.reshape(Bsz, S, D).astype(h.dtype)


def kernel(x, ln_mix, w_in, s5_lambda_re, s5_lambda_im, s5_b_re, s5_b_im, s5_c_re, s5_c_im, s5_d,
           s5_log_dt, s5_glu_w, s5_glu_b, s5_out_norm, gdn_conv_w, gdn_a_log, gdn_dt_bias, gdn_norm_w,
           nsa_q_norm, nsa_k_norm, nsa_cmp_pe, nsa_cmp_w1, nsa_cmp_w2, nsa_out_norm, w_out, ln_ffn,
           moe_group_w, moe_group_b, moe_expert_w, moe_expert_b, moe_w_gate, moe_w_up, moe_w_down):
    S = x.shape[1]
    cos, sin = rope_tables(jnp.arange(S), NSA_HEAD_DIM)
    for l in range(DEPTH):
        y = hybrid_mixer(rms_norm(x, ln_mix[l]), w_in[l], s5_lambda_re[l], s5_lambda_im[l], s5_b_re[l],
                         s5_b_im[l], s5_c_re[l], s5_c_im[l], s5_d[l], s5_log_dt[l], s5_glu_w[l], s5_glu_b[l],
                         s5_out_norm[l], gdn_conv_w[l], gdn_a_log[l], gdn_dt_bias[l], gdn_norm_w[l],
                         nsa_q_norm[l], nsa_k_norm[l], nsa_cmp_pe[l], nsa_cmp_w1[l], nsa_cmp_w2[l],
                         nsa_out_norm[l], w_out[l], cos, sin)
        h = x + y.astype(x.dtype)
        x = h + hier_moe(rms_norm(h, ln_ffn[l]), moe_group_w[l], moe_group_b[l], moe_expert_w[l],
                         moe_expert_b[l], moe_w_gate[l], moe_w_up[l], moe_w_down[l])
    return x
```

```python
import math
import jax, jax.numpy as jnp
from jax import lax
import numpy as np
from jax.experimental import pallas as pl
from jax.experimental.pallas import tpu as pltpu

D_MODEL = 2048
BATCH = 4
SEQ = 4096
DEPTH = 2

EPS = 1e-6
S5_WIDTH = D_MODEL // 4
S5_GROUP = 16
S5_GROUPS = S5_WIDTH // S5_GROUP
S5_STATE = 64
GDN_HEAD_DIM = 128
GDN_WIDTH = D_MODEL // 4
GDN_HEADS = GDN_WIDTH // GDN_HEAD_DIM
GDN_CONV = 4
GDN_CHUNK = 64
NSA_HEAD_DIM = 64
NSA_WIDTH = D_MODEL // 2
NSA_HEADS = NSA_WIDTH // NSA_HEAD_DIM
NSA_KV_HEADS = 4
NSA_KV_WIDTH = NSA_KV_HEADS * NSA_HEAD_DIM
CMP_LEN = 32
CMP_STRIDE = 16
CMP_HIDDEN = 256
SLC_BLOCK = 64
SLC_TOPK = 16
WINDOW = 512
Q_BLOCK = 128
ROPE_THETA = 10000.0
BIG = 1e9
N_GROUPS = 4
EXPERTS_PER_GROUP = 8
N_EXPERTS = N_GROUPS * EXPERTS_PER_GROUP
TOP_K = 2
D_EXPERT = 512
MOE_BLOCK = 128
IN_SIZES = (S5_WIDTH, 3 * GDN_WIDTH, GDN_WIDTH, GDN_HEADS, GDN_HEADS, NSA_WIDTH, 6 * NSA_KV_WIDTH, 3 * NSA_HEADS)
D_IN = sum(IN_SIZES)


def _matmul_body(a_ref, w_ref, o_ref):
    o_ref[...] = jnp.dot(a_ref[...].astype(jnp.bfloat16), w_ref[...].astype(jnp.bfloat16),
                         preferred_element_type=jnp.float32)


def _matmul(a, w, tm=512, tn=512):
    M, K = a.shape
    N = w.shape[1]
    n_pad = (-N) % tn
    if n_pad:
        w = jnp.pad(w, ((0, 0), (0, n_pad)))
    Np = N + n_pad
    out = pl.pallas_call(
        _matmul_body,
        grid=(Np // tn, M // tm),
        in_specs=[pl.BlockSpec((tm, K), lambda j, i: (i, 0)),
                  pl.BlockSpec((K, tn), lambda j, i: (0, j))],
        out_specs=pl.BlockSpec((tm, tn), lambda j, i: (i, j)),
        out_shape=jax.ShapeDtypeStruct((M, Np), jnp.float32),
        compiler_params=pltpu.CompilerParams(vmem_limit_bytes=48 * 1024 * 1024),
    )(a, w)
    return out[:, :N] if n_pad else out


def rms_norm(x, gain):
    xf = x.astype(jnp.float32)
    y = xf * lax.rsqrt(jnp.mean(xf * xf, axis=-1, keepdims=True) + EPS)
    return (y * gain.astype(jnp.float32)).astype(x.dtype)


def l2_norm(x):
    return x * lax.rsqrt(jnp.sum(x * x, axis=-1, keepdims=True) + EPS)


def rope_tables(pos, dim):
    inv_freq = ROPE_THETA ** (-jnp.arange(0, dim, 2, dtype=jnp.float32) / dim)
    ang = pos.astype(jnp.float32)[:, None] * inv_freq[None, :]
    return jnp.cos(ang)[:, None, :], jnp.sin(ang)[:, None, :]


def rope(x, cos, sin):
    x1, x2 = jnp.split(x, 2, axis=-1)
    return jnp.concatenate([x1 * cos - x2 * sin, x2 * cos + x1 * sin], axis=-1).astype(x.dtype)


def masked_softmax(s, mask):
    s = jnp.where(mask, s, -jnp.inf)
    m = jnp.max(s, axis=-1, keepdims=True)
    m = jnp.where(jnp.isfinite(m), m, 0.0)
    p = jnp.exp(s - m)
    return p / jnp.maximum(jnp.sum(p, axis=-1, keepdims=True), jnp.finfo(jnp.float32).tiny)


S5_OCT = 8
S5_NOCT = S5_GROUPS // S5_OCT
S5_OCT_IN = S5_OCT * S5_GROUP
S5_OCT_ST = S5_OCT * S5_STATE
SUBLANES = 8
S5_SCAN_SHIFTS = (1, 2, 4)


def _s5_constants(lam_re, lam_im, b_re, b_im, c_re, c_im, log_dt):
    f32 = jnp.float32
    lr, li = lam_re.astype(f32), lam_im.astype(f32)
    dt = jnp.exp(log_dt.astype(f32))[:, None]

    def lam_pow(k):
        mag = jnp.exp(k * lr * dt)
        return mag * jnp.cos(k * li * dt), mag * jnp.sin(k * li * dt)

    ar, ai = lam_pow(1.0)
    nr, ni = ar - 1.0, ai
    den = lr * lr + li * li
    qr, qi = (nr * lr + ni * li) / den, (ni * lr - nr * li) / den
    br, bi = b_re.astype(f32), b_im.astype(f32)
    bbr = qr[:, :, None] * br - qi[:, :, None] * bi
    bbi = qr[:, :, None] * bi + qi[:, :, None] * br
    eye = jnp.eye(S5_OCT, dtype=f32)

    def blk_b(t):
        t = t.reshape(S5_NOCT, S5_OCT, S5_STATE, S5_GROUP)
        return jnp.einsum('qgph,gk->qghkp', t, eye).reshape(S5_NOCT, S5_OCT_IN, S5_OCT_ST)

    def blk_c(t):
        t = t.reshape(S5_NOCT, S5_OCT, S5_GROUP, S5_STATE)
        return jnp.einsum('qghp,gk->qgpkh', t, eye).reshape(S5_NOCT, S5_OCT_ST, S5_OCT_IN)

    b_mat = jnp.stack([blk_b(bbr), blk_b(bbi)]).astype(jnp.bfloat16)
    c_mat = jnp.stack([blk_c(c_re.astype(f32)), blk_c(-c_im.astype(f32))]).astype(jnp.bfloat16)
    rows = jnp.arange(SUBLANES, dtype=f32)[:, None, None]
    tiles = []
    for k in S5_SCAN_SHIFTS:
        pr, pi = lam_pow(jnp.full_like(rows, float(k)))
        keep = rows >= k
        tiles.append(jnp.stack([jnp.where(keep, pr, 0.0), jnp.where(keep, pi, 0.0)]))
    pr, pi = lam_pow(rows + 1.0)
    tiles.append(jnp.stack([pr, pi]))
    a_mat = jnp.stack(tiles).reshape(len(tiles), 2, SUBLANES, S5_GROUPS * S5_STATE)
    return b_mat, c_mat, a_mat


def _s5_body(u_ref, b_ref, c_ref, a_ref, d_ref, gw_ref, gb_ref, gain_ref, o_ref, st_ref, xr_ref, xi_ref):
    bf16 = jnp.bfloat16
    n_tiles = u_ref.shape[0] // SUBLANES

    @pl.when(pl.program_id(1) == 0)
    def _():
        st_ref[...] = jnp.zeros_like(st_ref)

    u = u_ref[...]
    ys = []
    for q in range(S5_NOCT):
        lanes = slice(q * S5_OCT_ST, (q + 1) * S5_OCT_ST)
        uq = u[:, q * S5_OCT_IN:(q + 1) * S5_OCT_IN].astype(bf16)
        xr_ref[...] = jnp.dot(uq, b_ref[0, q], preferred_element_type=jnp.float32)
        xi_ref[...] = jnp.dot(uq, b_ref[1, q], preferred_element_type=jnp.float32)

        def tile_step(i, carry, lanes=lanes):
            cr, ci = carry
            r0 = pl.multiple_of(i * SUBLANES, SUBLANES)
            xr = xr_ref[pl.ds(r0, SUBLANES), :]
            xi = xi_ref[pl.ds(r0, SUBLANES), :]
            for lvl, k in enumerate(S5_SCAN_SHIFTS):
                pr, pi = a_ref[lvl, 0, :, lanes], a_ref[lvl, 1, :, lanes]
                sr, si = pltpu.roll(xr, k, 0), pltpu.roll(xi, k, 0)
                xr, xi = xr + (pr * sr - pi * si), xi + (pr * si + pi * sr)
            pr, pi = a_ref[len(S5_SCAN_SHIFTS), 0, :, lanes], a_ref[len(S5_SCAN_SHIFTS), 1, :, lanes]
            cr = jnp.broadcast_to(cr, xr.shape)
            ci = jnp.broadcast_to(ci, xi.shape)
            xr, xi = xr + (pr * cr - pi * ci), xi + (pr * ci + pi * cr)
            xr_ref[pl.ds(r0, SUBLANES), :] = xr
            xi_ref[pl.ds(r0, SUBLANES), :] = xi
            return xr[SUBLANES - 1:, :], xi[SUBLANES - 1:, :]

        cr, ci = lax.fori_loop(0, n_tiles, tile_step, (st_ref[0, :, lanes], st_ref[1, :, lanes]))
        st_ref[0, :, lanes] = cr
        st_ref[1, :, lanes] = ci
        ys.append(jnp.dot(xr_ref[...].astype(bf16), c_ref[0, q], preferred_element_type=jnp.float32)
                  + jnp.dot(xi_ref[...].astype(bf16), c_ref[1, q], preferred_element_type=jnp.float32))
    y = jnp.concatenate(ys, axis=1) + d_ref[...] * u
    y = jax.nn.gelu(y)
    z = jnp.dot(y.astype(bf16), gw_ref[...], preferred_element_type=jnp.float32) + gb_ref[...]
    y = y * jax.nn.sigmoid(z)
    y = y * lax.rsqrt(jnp.mean(y * y, axis=-1, keepdims=True) + EPS)
    o_ref[...] = y * gain_ref[...]


def s5_mixer_normed(u, lam_re, lam_im, b_re, b_im, c_re, c_im, d_skip, log_dt, glu_w, glu_b, out_norm, chunk=512):
    Bsz, S, W = u.shape
    f32 = jnp.float32
    L = min(chunk, S)
    b_mat, c_mat, a_mat = _s5_constants(lam_re, lam_im, b_re, b_im, c_re, c_im, log_dt)
    n_state = S5_GROUPS * S5_STATE
    full = lambda *shape: pl.BlockSpec(shape, lambda b, c: (0,) * len(shape))
    return pl.pallas_call(
        _s5_body,
        grid=(Bsz, S // L),
        in_specs=[pl.BlockSpec((None, L, W), lambda b, c: (b, c, 0)),
                  full(*b_mat.shape), full(*c_mat.shape), full(*a_mat.shape),
                  full(1, W), full(W, W), full(1, W), full(1, W)],
        out_specs=pl.BlockSpec((None, L, W), lambda b, c: (b, c, 0)),
        out_shape=jax.ShapeDtypeStruct((Bsz, S, W), f32),
        scratch_shapes=[pltpu.VMEM((2, 1, n_state), f32),
                        pltpu.VMEM((L, S5_OCT_ST), f32), pltpu.VMEM((L, S5_OCT_ST), f32)],
        compiler_params=pltpu.CompilerParams(dimension_semantics=("arbitrary", "arbitrary"),
                                             vmem_limit_bytes=48 * 1024 * 1024),
        name="s5_mixer",
    )(u.astype(f32), b_mat, c_mat, a_mat, d_skip.astype(f32).reshape(1, W), glu_w.astype(jnp.bfloat16),
      glu_b.astype(f32).reshape(1, W), out_norm.astype(f32).reshape(1, W))


def short_causal_conv(x, w):
    C = x.shape[-1]
    return lax.conv_general_dilated(x, w[:, None, :].astype(x.dtype), window_strides=(1,),
                                    padding=[(GDN_CONV - 1, 0)], dimension_numbers=('NWC', 'WIO', 'NWC'),
                                    feature_group_count=C)


def gated_delta_rule(q, k, v, g, beta):
    Bsz, S, H, dk = q.shape
    dv = v.shape[-1]
    C = GDN_CHUNK
    N = S // C

    def chunks(t):
        return t.reshape(Bsz, N, C, H, -1).transpose(1, 0, 3, 2, 4)

    qc, kc, vc = chunks(q), chunks(k), chunks(v)
    gc = g.reshape(Bsz, N, C, H).transpose(1, 0, 3, 2)
    bc = beta.reshape(Bsz, N, C, H).transpose(1, 0, 3, 2)
    gcum = jnp.cumsum(gc, axis=-1)
    incl = jnp.tril(jnp.ones((C, C), dtype=bool))
    strict = jnp.tril(jnp.ones((C, C), dtype=bool), -1)
    decay = jnp.exp(jnp.where(incl, gcum[..., :, None] - gcum[..., None, :], -jnp.inf))
    k_beta = kc * bc[..., None]
    a_mat = jnp.where(strict, jnp.einsum('nbhid,nbhjd->nbhij', k_beta, kc) * decay, 0.0)
    eye = jnp.eye(C, dtype=jnp.float32)
    t_mat = lax.linalg.triangular_solve(a_mat + eye, jnp.broadcast_to(eye, a_mat.shape),
                                        left_side=True, lower=True, unit_diagonal=True)
    u = t_mat @ (vc * bc[..., None])
    w = t_mat @ (k_beta * jnp.exp(gcum)[..., None])
    qk = jnp.einsum('nbhid,nbhjd->nbhij', qc, kc) * decay

    def step(state, inp):
        q_n, k_n, u_n, w_n, g_n, qk_n = inp
        v_new = u_n - w_n @ state
        o = (q_n * jnp.exp(g_n)[..., None]) @ state + qk_n @ v_new
        g_last = g_n[..., -1:]
        state = state * jnp.exp(g_last)[..., None] + jnp.einsum(
            'bhcd,bhce->bhde', k_n * jnp.exp(g_last - g_n)[..., None], v_new)
        return state, o

    state0 = jnp.zeros((Bsz, H, dk, dv), jnp.float32)
    _, o = lax.scan(step, state0, (qc, kc, u, w, gcum, qk))
    return o.transpose(1, 0, 3, 2, 4).reshape(Bsz, S, H, dv)


def gdn_mixer(qkv, z, a, b, conv_w, a_log, dt_bias, norm_w):
    Bsz, S, _ = qkv.shape
    f32 = jnp.float32
    qkv = jax.nn.silu(short_causal_conv(qkv, conv_w)).astype(f32)
    q, k, v = (t.reshape(Bsz, S, GDN_HEADS, GDN_HEAD_DIM) for t in jnp.split(qkv, 3, axis=-1))
    q = l2_norm(q) * GDN_HEAD_DIM ** -0.5
    k = l2_norm(k)
    beta = jax.nn.sigmoid(b.astype(f32))
    g = -jnp.exp(a_log.astype(f32)) * jax.nn.softplus(a.astype(f32) + dt_bias.astype(f32))
    o = gated_delta_rule(q, k, v, g, beta)
    o = rms_norm(o, norm_w) * jax.nn.silu(z.astype(f32).reshape(Bsz, S, GDN_HEADS, GDN_HEAD_DIM))
    return o.reshape(Bsz, S, GDN_WIDTH).astype(z.dtype)


def nsa_mixer(q, kv, gate_logits, q_norm, k_norm, cmp_pe, cmp_w1, cmp_w2, cos, sin):
    Bsz, S, _ = q.shape
    f32 = jnp.float32
    G, R, dh = NSA_KV_HEADS, NSA_HEADS // NSA_KV_HEADS, NSA_HEAD_DIM
    scale = dh ** -0.5
    q = rope(rms_norm(q.reshape(Bsz, S, NSA_HEADS, dh), q_norm), cos, sin)
    kc_raw, vc_raw, ks, vs, kw, vw = (t.reshape(Bsz, S, G, dh) for t in jnp.split(kv, 6, axis=-1))
    ks = rope(rms_norm(ks, k_norm[1]), cos, sin)
    kw = rope(rms_norm(kw, k_norm[2]), cos, sin)

    n_cmp = (S - CMP_LEN) // CMP_STRIDE + 1
    win_idx = jnp.arange(n_cmp)[:, None] * CMP_STRIDE + jnp.arange(CMP_LEN)[None, :]
    cmp_end = win_idx[:, -1]

    def compress(t, pe, w1, w2):
        blocks = t[:, win_idx] + pe[:, None, :]
        blocks = jnp.moveaxis(blocks, 3, 2).reshape(Bsz, n_cmp, G, CMP_LEN * dh)
        return jax.nn.gelu(blocks @ w1) @ w2

    cos_c, sin_c = rope_tables(cmp_end, dh)
    k_cmp = rope(rms_norm(compress(kc_raw, cmp_pe[0], cmp_w1[0], cmp_w2[0]), k_norm[0]), cos_c, sin_c)
    v_cmp = compress(vc_raw, cmp_pe[1], cmp_w1[1], cmp_w2[1])

    n_slc = S // SLC_BLOCK
    n_top = min(SLC_TOPK, n_slc)
    c0 = jnp.arange(n_cmp) * CMP_STRIDE
    s0 = jnp.arange(n_slc) * SLC_BLOCK
    overlap = jnp.clip(jnp.minimum(c0[:, None] + CMP_LEN, s0[None, :] + SLC_BLOCK)
                       - jnp.maximum(c0[:, None], s0[None, :]), 0, None).astype(f32) / CMP_LEN
    k_blocks = ks.reshape(Bsz, n_slc, SLC_BLOCK, G, dh).transpose(0, 3, 1, 2, 4)
    v_blocks = vs.reshape(Bsz, n_slc, SLC_BLOCK, G, dh).transpose(0, 3, 1, 2, 4)
    k_pad = jnp.pad(kw, ((0, 0), (WINDOW, 0), (0, 0), (0, 0)))
    v_pad = jnp.pad(vw, ((0, 0), (WINDOW, 0), (0, 0), (0, 0)))
    gates = jax.nn.sigmoid(gate_logits.astype(f32)).reshape(Bsz, S, G, R, 3)

    n_qb = S // Q_BLOCK
    qb = q.reshape(Bsz, n_qb, Q_BLOCK, G, R, dh).transpose(1, 0, 2, 3, 4, 5)
    gb = gates.reshape(Bsz, n_qb, Q_BLOCK, G, R, 3).transpose(1, 0, 2, 3, 4, 5)
    b_ix = jnp.arange(Bsz)[:, None, None, None]
    g_ix = jnp.arange(G)[None, None, :, None]
    j_ix = jnp.arange(n_slc)
    n_sel_keys = n_top * SLC_BLOCK

    def attend_block(args):
        q_blk, g_blk, blk = args
        t = blk * Q_BLOCK + jnp.arange(Q_BLOCK)
        s = jnp.einsum('bqgrd,bngd->bqgrn', q_blk, k_cmp).astype(f32) * scale
        valid = (cmp_end[None, :] <= t[:, None])[None, :, None, None, :]
        p_cmp = masked_softmax(s, valid)
        o_cmp = jnp.einsum('bqgrn,bngd->bqgrd', p_cmp.astype(v_cmp.dtype), v_cmp)
        imp = jnp.einsum('bqgrn,nj->bqgj', p_cmp, overlap)
        cur = (t // SLC_BLOCK)[:, None]
        forced = (j_ix[None, :] == 0) | (j_ix[None, :] == cur) | (j_ix[None, :] == cur - 1)
        started = s0[None, :] <= t[:, None]
        imp = jnp.where(forced[None, :, None, :], BIG, jnp.where(started[None, :, None, :], imp, -BIG))
        _, sel = lax.top_k(imp, n_top)
        k_sel = k_blocks[b_ix, g_ix, sel].reshape(Bsz, Q_BLOCK, G, n_sel_keys, dh)
        v_sel = v_blocks[b_ix, g_ix, sel].reshape(Bsz, Q_BLOCK, G, n_sel_keys, dh)
        pos = (sel[..., None] * SLC_BLOCK + jnp.arange(SLC_BLOCK)).reshape(Bsz, Q_BLOCK, G, n_sel_keys)
        s = jnp.einsum('bqgrd,bqgkd->bqgrk', q_blk, k_sel).astype(f32) * scale
        ok = (pos <= t[None, :, None, None])[:, :, :, None, :]
        p = masked_softmax(s, ok)
        o_slc = jnp.einsum('bqgrk,bqgkd->bqgrd', p.astype(v_sel.dtype), v_sel)
        k_win = lax.dynamic_slice_in_dim(k_pad, blk * Q_BLOCK, Q_BLOCK + WINDOW, axis=1)
        v_win = lax.dynamic_slice_in_dim(v_pad, blk * Q_BLOCK, Q_BLOCK + WINDOW, axis=1)
        kpos = blk * Q_BLOCK - WINDOW + jnp.arange(Q_BLOCK + WINDOW)
        diff = t[:, None] - kpos[None, :]
        in_win = (diff >= 0) & (diff < WINDOW) & (kpos[None, :] >= 0)
        s = jnp.einsum('bqgrd,bkgd->bqgrk', q_blk, k_win).astype(f32) * scale
        p = masked_softmax(s, in_win[None, :, None, None, :])
        o_win = jnp.einsum('bqgrk,bkgd->bqgrd', p.astype(v_win.dtype), v_win)
        o = g_blk[..., 0:1] * o_cmp + g_blk[..., 1:2] * o_slc + g_blk[..., 2:3] * o_win
        return o.astype(q_blk.dtype)

    out = lax.map(attend_block, (qb, gb, jnp.arange(n_qb)))
    return out.transpose(1, 0, 2, 3, 4, 5).reshape(Bsz, S, NSA_WIDTH)


def hybrid_mixer(a, w_in, lam_re, lam_im, b_re, b_im, c_re, c_im, s5_d, s5_log_dt, glu_w, glu_b,
                 s5_out_norm, conv_w, a_log, dt_bias, gdn_norm_w, q_norm, k_norm, cmp_pe, cmp_w1,
                 cmp_w2, nsa_out_norm, w_out, cos, sin):
    proj = _matmul(a.reshape(-1, D_MODEL), w_in).reshape(a.shape[0], a.shape[1], D_IN)
    u, qkv, z, ga, gb, q, kv, gate_logits = jnp.split(proj, np.cumsum(IN_SIZES)[:-1].tolist(), axis=-1)
    y_s5 = s5_mixer_normed(u, lam_re, lam_im, b_re, b_im, c_re, c_im, s5_d, s5_log_dt, glu_w, glu_b, s5_out_norm)
    y_gdn = gdn_mixer(qkv, z, ga, gb, conv_w, a_log, dt_bias, gdn_norm_w)
    y_nsa = rms_norm(nsa_mixer(q, kv, gate_logits, q_norm, k_norm, cmp_pe, cmp_w1, cmp_w2, cos, sin), nsa_out_norm)
    cat = jnp.concatenate([y_s5, y_gdn.astype(y_s5.dtype), y_nsa.astype(y_s5.dtype)], axis=-1)
    return _matmul(cat.reshape(-1, D_MODEL), w_out).reshape(cat.shape)


def hier_moe(h, group_w, group_b, expert_w, expert_b, w_gate, w_up, w_down):
    Bsz, S, D = h.shape
    f32 = jnp.float32
    x = h.reshape(-1, D)
    T = x.shape[0]
    g_logits = (x @ group_w).astype(f32) + group_b.astype(f32)
    grp = jnp.argmax(g_logits, axis=-1)
    p_grp = jnp.take_along_axis(jax.nn.softmax(g_logits, axis=-1), grp[:, None], axis=-1)
    e_logits = ((x @ expert_w).astype(f32) + expert_b.astype(f32)).reshape(T, N_GROUPS, EXPERTS_PER_GROUP)
    e_logits = jnp.take_along_axis(e_logits, grp[:, None, None], axis=1)[:, 0]
    top_logit, top_local = lax.top_k(e_logits, TOP_K)
    weights = jax.nn.softmax(top_logit, axis=-1) * p_grp
    experts = grp[:, None] * EXPERTS_PER_GROUP + top_local
    n_assign = T * TOP_K
    e_flat = experts.reshape(-1)
    w_flat = weights.reshape(-1)
    tok_flat = jnp.repeat(jnp.arange(T, dtype=jnp.int32), TOP_K)
    order = jnp.argsort(e_flat)
    e_sorted = e_flat[order]
    counts = jnp.bincount(e_flat, length=N_EXPERTS)
    padded = (counts + MOE_BLOCK - 1) // MOE_BLOCK * MOE_BLOCK
    pad_start = jnp.cumsum(padded) - padded
    start = jnp.cumsum(counts) - counts
    dest = pad_start[e_sorted] + jnp.arange(n_assign) - start[e_sorted]
    n_blk = (n_assign + MOE_BLOCK - 1) // MOE_BLOCK + N_EXPERTS
    cap = n_blk * MOE_BLOCK
    slot_tok = jnp.full((cap,), T, jnp.int32).at[dest].set(tok_flat[order])
    slot_w = jnp.zeros((cap,), f32).at[dest].set(w_flat[order])
    blk_expert = jnp.minimum(jnp.searchsorted(jnp.cumsum(padded), jnp.arange(n_blk) * MOE_BLOCK, side='right'),
                             N_EXPERTS - 1)
    x_pad = jnp.concatenate([x, jnp.zeros((1, D), x.dtype)], axis=0)

    def expert_block(args):
        tok, wt, e = args
        xt = x_pad[tok]
        hid = jax.nn.silu(xt @ w_gate[e]) * (xt @ w_up[e])
        return (hid @ w_down[e]) * wt[:, None].astype(x.dtype)

    y = lax.map(expert_block, (slot_tok.reshape(n_blk, MOE_BLOCK), slot_w.reshape(n_blk, MOE_BLOCK), blk_expert))
    out = jnp.zeros((T + 1, D), y.dtype).at[slot_tok].add(y.reshape(cap, D))[:T]
    return out.reshape(Bsz, S, D).astype(h.dtype)


def kernel(x, ln_mix, w_in, s5_lambda_re, s5_lambda_im, s5_b_re, s5_b_im, s5_c_re, s5_c_im, s5_d,
           s5_log_dt, s5_glu_w, s5_glu_b, s5_out_norm, gdn_conv_w, gdn_a_log, gdn_dt_bias, gdn_norm_w,
           nsa_q_norm, nsa_k_norm, nsa_cmp_pe, nsa_cmp_w1, nsa_cmp_w2, nsa_out_norm, w_out, ln_ffn,
           moe_group_w, moe_group_b, moe_expert_w, moe_expert_b, moe_w_gate, moe_w_up, moe_w_down):
    S = x.shape[1]
    cos, sin = rope_tables(jnp.arange(S), NSA_HEAD_DIM)
    for l in range(DEPTH):
        y = hybrid_mixer(rms_norm(x, ln_mix[l]), w_in[l], s5_lambda_re[l], s5_lambda_im[l], s5_b_re[l],
                         s5_b_im[l], s5_c_re[l], s5_c_im[l], s5_d[l], s5_log_dt[l], s5_glu_w[l], s5_glu_b[l],
                         s5_out_norm[l], gdn_conv_w[l], gdn_a_log[l], gdn_dt_bias[l], gdn_norm_w[l],
                         nsa_q_norm[l], nsa_k_norm[l], nsa_cmp_pe[l], nsa_cmp_w1[l], nsa_cmp_w2[l],
                         nsa_out_norm[l], w_out[l], cos, sin)
        h = x + y.astype(x.dtype)
        x = h + hier_moe(rms_norm(h, ln_ffn[l]), moe_group_w[l], moe_group_b[l], moe_expert_w[l],
                         moe_expert_b[l], moe_w_gate[l], moe_w_up[l], moe_w_down[l])
    return x
```

```python
import functools
import math
import jax, jax.numpy as jnp
from jax import lax
import numpy as np
from jax.experimental import pallas as pl
from jax.experimental.pallas import tpu as pltpu

D_MODEL = 2048
BATCH = 4
SEQ = 4096
DEPTH = 2

EPS = 1e-6
S5_WIDTH = D_MODEL // 4
S5_GROUP = 16
S5_GROUPS = S5_WIDTH // S5_GROUP
S5_STATE = 64
GDN_HEAD_DIM = 128
GDN_WIDTH = D_MODEL // 4
GDN_HEADS = GDN_WIDTH // GDN_HEAD_DIM
GDN_CONV = 4
GDN_CHUNK = 64
NSA_HEAD_DIM = 64
NSA_WIDTH = D_MODEL // 2
NSA_HEADS = NSA_WIDTH // NSA_HEAD_DIM
NSA_KV_HEADS = 4
NSA_KV_WIDTH = NSA_KV_HEADS * NSA_HEAD_DIM
CMP_LEN = 32
CMP_STRIDE = 16
CMP_HIDDEN = 256
SLC_BLOCK = 64
SLC_TOPK = 16
WINDOW = 512
Q_BLOCK = 128
ROPE_THETA = 10000.0
BIG = 1e9
N_GROUPS = 4
EXPERTS_PER_GROUP = 8
N_EXPERTS = N_GROUPS * EXPERTS_PER_GROUP
TOP_K = 2
D_EXPERT = 512
MOE_BLOCK = 128
IN_SIZES = (S5_WIDTH, 3 * GDN_WIDTH, GDN_WIDTH, GDN_HEADS, GDN_HEADS, NSA_WIDTH, 6 * NSA_KV_WIDTH, 3 * NSA_HEADS)
D_IN = sum(IN_SIZES)


def _matmul_body(a_ref, w_ref, o_ref):
    o_ref[...] = jnp.dot(a_ref[...].astype(jnp.bfloat16), w_ref[...].astype(jnp.bfloat16),
                         preferred_element_type=jnp.float32)


def _matmul(a, w, tm=512, tn=512):
    M, K = a.shape
    N = w.shape[1]
    n_pad = (-N) % tn
    if n_pad:
        w = jnp.pad(w, ((0, 0), (0, n_pad)))
    Np = N + n_pad
    out = pl.pallas_call(
        _matmul_body,
        grid=(Np // tn, M // tm),
        in_specs=[pl.BlockSpec((tm, K), lambda j, i: (i, 0)),
                  pl.BlockSpec((K, tn), lambda j, i: (0, j))],
        out_specs=pl.BlockSpec((tm, tn), lambda j, i: (i, j)),
        out_shape=jax.ShapeDtypeStruct((M, Np), jnp.float32),
        compiler_params=pltpu.CompilerParams(vmem_limit_bytes=48 * 1024 * 1024),
    )(a, w)
    return out[:, :N] if n_pad else out


def rms_norm(x, gain):
    xf = x.astype(jnp.float32)
    y = xf * lax.rsqrt(jnp.mean(xf * xf, axis=-1, keepdims=True) + EPS)
    return (y * gain.astype(jnp.float32)).astype(x.dtype)


def l2_norm(x):
    return x * lax.rsqrt(jnp.sum(x * x, axis=-1, keepdims=True) + EPS)


def rope_tables(pos, dim):
    inv_freq = ROPE_THETA ** (-jnp.arange(0, dim, 2, dtype=jnp.float32) / dim)
    ang = pos.astype(jnp.float32)[:, None] * inv_freq[None, :]
    return jnp.cos(ang)[:, None, :], jnp.sin(ang)[:, None, :]


def rope(x, cos, sin):
    x1, x2 = jnp.split(x, 2, axis=-1)
    return jnp.concatenate([x1 * cos - x2 * sin, x2 * cos + x1 * sin], axis=-1).astype(x.dtype)


def masked_softmax(s, mask):
    s = jnp.where(mask, s, -jnp.inf)
    m = jnp.max(s, axis=-1, keepdims=True)
    m = jnp.where(jnp.isfinite(m), m, 0.0)
    p = jnp.exp(s - m)
    return p / jnp.maximum(jnp.sum(p, axis=-1, keepdims=True), jnp.finfo(jnp.float32).tiny)


S5_OCT = 8
S5_NOCT = S5_GROUPS // S5_OCT
S5_OCT_IN = S5_OCT * S5_GROUP
S5_OCT_ST = S5_OCT * S5_STATE
SUBLANES = 8
S5_SCAN_SHIFTS = (1, 2, 4)


def _s5_constants(lam_re, lam_im, b_re, b_im, c_re, c_im, log_dt):
    f32 = jnp.float32
    lr, li = lam_re.astype(f32), lam_im.astype(f32)
    dt = jnp.exp(log_dt.astype(f32))[:, None]

    def lam_pow(k):
        mag = jnp.exp(k * lr * dt)
        return mag * jnp.cos(k * li * dt), mag * jnp.sin(k * li * dt)

    ar, ai = lam_pow(1.0)
    nr, ni = ar - 1.0, ai
    den = lr * lr + li * li
    qr, qi = (nr * lr + ni * li) / den, (ni * lr - nr * li) / den
    br, bi = b_re.astype(f32), b_im.astype(f32)
    bbr = qr[:, :, None] * br - qi[:, :, None] * bi
    bbi = qr[:, :, None] * bi + qi[:, :, None] * br
    eye = jnp.eye(S5_OCT, dtype=f32)

    def blk_b(t):
        t = t.reshape(S5_NOCT, S5_OCT, S5_STATE, S5_GROUP)
        return jnp.einsum('qgph,gk->qghkp', t, eye).reshape(S5_NOCT, S5_OCT_IN, S5_OCT_ST)

    def blk_c(t):
        t = t.reshape(S5_NOCT, S5_OCT, S5_GROUP, S5_STATE)
        return jnp.einsum('qghp,gk->qgpkh', t, eye).reshape(S5_NOCT, S5_OCT_ST, S5_OCT_IN)

    b_mat = jnp.stack([blk_b(bbr), blk_b(bbi)]).astype(jnp.bfloat16)
    c_mat = jnp.stack([blk_c(c_re.astype(f32)), blk_c(-c_im.astype(f32))]).astype(jnp.bfloat16)
    rows = jnp.arange(SUBLANES, dtype=f32)[:, None, None]
    tiles = []
    for k in S5_SCAN_SHIFTS:
        pr, pi = lam_pow(jnp.full_like(rows, float(k)))
        keep = rows >= k
        tiles.append(jnp.stack([jnp.where(keep, pr, 0.0), jnp.where(keep, pi, 0.0)]))
    pr, pi = lam_pow(rows + 1.0)
    tiles.append(jnp.stack([pr, pi]))
    a_mat = jnp.stack(tiles).reshape(len(tiles), 2, SUBLANES, S5_GROUPS * S5_STATE)
    return b_mat, c_mat, a_mat


def _s5_body(u_ref, b_ref, c_ref, a_ref, d_ref, gw_ref, gb_ref, gain_ref, o_ref, st_ref, xr_ref, xi_ref):
    bf16 = jnp.bfloat16
    n_tiles = u_ref.shape[0] // SUBLANES

    @pl.when(pl.program_id(1) == 0)
    def _():
        st_ref[...] = jnp.zeros_like(st_ref)

    u = u_ref[...]
    ys = []
    for q in range(S5_NOCT):
        lanes = slice(q * S5_OCT_ST, (q + 1) * S5_OCT_ST)
        uq = u[:, q * S5_OCT_IN:(q + 1) * S5_OCT_IN].astype(bf16)
        xr_ref[...] = jnp.dot(uq, b_ref[0, q], preferred_element_type=jnp.float32)
        xi_ref[...] = jnp.dot(uq, b_ref[1, q], preferred_element_type=jnp.float32)

        def tile_step(i, carry, lanes=lanes):
            cr, ci = carry
            r0 = pl.multiple_of(i * SUBLANES, SUBLANES)
            xr = xr_ref[pl.ds(r0, SUBLANES), :]
            xi = xi_ref[pl.ds(r0, SUBLANES), :]
            for lvl, k in enumerate(S5_SCAN_SHIFTS):
                pr, pi = a_ref[lvl, 0, :, lanes], a_ref[lvl, 1, :, lanes]
                sr, si = pltpu.roll(xr, k, 0), pltpu.roll(xi, k, 0)
                xr, xi = xr + (pr * sr - pi * si), xi + (pr * si + pi * sr)
            pr, pi = a_ref[len(S5_SCAN_SHIFTS), 0, :, lanes], a_ref[len(S5_SCAN_SHIFTS), 1, :, lanes]
            cr = jnp.broadcast_to(cr, xr.shape)
            ci = jnp.broadcast_to(ci, xi.shape)
            xr, xi = xr + (pr * cr - pi * ci), xi + (pr * ci + pi * cr)
            xr_ref[pl.ds(r0, SUBLANES), :] = xr
            xi_ref[pl.ds(r0, SUBLANES), :] = xi
            return xr[SUBLANES - 1:, :], xi[SUBLANES - 1:, :]

        cr, ci = lax.fori_loop(0, n_tiles, tile_step, (st_ref[0, :, lanes], st_ref[1, :, lanes]))
        st_ref[0, :, lanes] = cr
        st_ref[1, :, lanes] = ci
        ys.append(jnp.dot(xr_ref[...].astype(bf16), c_ref[0, q], preferred_element_type=jnp.float32)
                  + jnp.dot(xi_ref[...].astype(bf16), c_ref[1, q], preferred_element_type=jnp.float32))
    y = jnp.concatenate(ys, axis=1) + d_ref[...] * u
    y = jax.nn.gelu(y)
    z = jnp.dot(y.astype(bf16), gw_ref[...], preferred_element_type=jnp.float32) + gb_ref[...]
    y = y * jax.nn.sigmoid(z)
    y = y * lax.rsqrt(jnp.mean(y * y, axis=-1, keepdims=True) + EPS)
    o_ref[...] = y * gain_ref[...]


def s5_mixer_normed(u, lam_re, lam_im, b_re, b_im, c_re, c_im, d_skip, log_dt, glu_w, glu_b, out_norm, chunk=512):
    Bsz, S, W = u.shape
    f32 = jnp.float32
    L = min(chunk, S)
    b_mat, c_mat, a_mat = _s5_constants(lam_re, lam_im, b_re, b_im, c_re, c_im, log_dt)
    n_state = S5_GROUPS * S5_STATE
    full = lambda *shape: pl.BlockSpec(shape, lambda b, c: (0,) * len(shape))
    return pl.pallas_call(
        _s5_body,
        grid=(Bsz, S // L),
        in_specs=[pl.BlockSpec((None, L, W), lambda b, c: (b, c, 0)),
                  full(*b_mat.shape), full(*c_mat.shape), full(*a_mat.shape),
                  full(1, W), full(W, W), full(1, W), full(1, W)],
        out_specs=pl.BlockSpec((None, L, W), lambda b, c: (b, c, 0)),
        out_shape=jax.ShapeDtypeStruct((Bsz, S, W), f32),
        scratch_shapes=[pltpu.VMEM((2, 1, n_state), f32),
                        pltpu.VMEM((L, S5_OCT_ST), f32), pltpu.VMEM((L, S5_OCT_ST), f32)],
        compiler_params=pltpu.CompilerParams(dimension_semantics=("arbitrary", "arbitrary"),
                                             vmem_limit_bytes=48 * 1024 * 1024),
        name="s5_mixer",
    )(u.astype(f32), b_mat, c_mat, a_mat, d_skip.astype(f32).reshape(1, W), glu_w.astype(jnp.bfloat16),
      glu_b.astype(f32).reshape(1, W), out_norm.astype(f32).reshape(1, W))


def short_causal_conv(x, w):
    C = x.shape[-1]
    return lax.conv_general_dilated(x, w[:, None, :].astype(x.dtype), window_strides=(1,),
                                    padding=[(GDN_CONV - 1, 0)], dimension_numbers=('NWC', 'WIO', 'NWC'),
                                    feature_group_count=C)


def gated_delta_rule(q, k, v, g, beta):
    Bsz, S, H, dk = q.shape
    dv = v.shape[-1]
    C = GDN_CHUNK
    N = S // C

    def chunks(t):
        return t.reshape(Bsz, N, C, H, -1).transpose(1, 0, 3, 2, 4)

    qc, kc, vc = chunks(q), chunks(k), chunks(v)
    gc = g.reshape(Bsz, N, C, H).transpose(1, 0, 3, 2)
    bc = beta.reshape(Bsz, N, C, H).transpose(1, 0, 3, 2)
    gcum = jnp.cumsum(gc, axis=-1)
    incl = jnp.tril(jnp.ones((C, C), dtype=bool))
    strict = jnp.tril(jnp.ones((C, C), dtype=bool), -1)
    decay = jnp.exp(jnp.where(incl, gcum[..., :, None] - gcum[..., None, :], -jnp.inf))
    k_beta = kc * bc[..., None]
    a_mat = jnp.where(strict, jnp.einsum('nbhid,nbhjd->nbhij', k_beta, kc) * decay, 0.0)
    eye = jnp.eye(C, dtype=jnp.float32)
    t_mat = lax.linalg.triangular_solve(a_mat + eye, jnp.broadcast_to(eye, a_mat.shape),
                                        left_side=True, lower=True, unit_diagonal=True)
    u = t_mat @ (vc * bc[..., None])
    w = t_mat @ (k_beta * jnp.exp(gcum)[..., None])
    qk = jnp.einsum('nbhid,nbhjd->nbhij', qc, kc) * decay

    def step(state, inp):
        q_n, k_n, u_n, w_n, g_n, qk_n = inp
        v_new = u_n - w_n @ state
        o = (q_n * jnp.exp(g_n)[..., None]) @ state + qk_n @ v_new
        g_last = g_n[..., -1:]
        state = state * jnp.exp(g_last)[..., None] + jnp.einsum(
            'bhcd,bhce->bhde', k_n * jnp.exp(g_last - g_n)[..., None], v_new)
        return state, o

    state0 = jnp.zeros((Bsz, H, dk, dv), jnp.float32)
    _, o = lax.scan(step, state0, (qc, kc, u, w, gcum, qk))
    return o.transpose(1, 0, 3, 2, 4).reshape(Bsz, S, H, dv)


def gdn_mixer(qkv, z, a, b, conv_w, a_log, dt_bias, norm_w):
    Bsz, S, _ = qkv.shape
    f32 = jnp.float32
    qkv = jax.nn.silu(short_causal_conv(qkv, conv_w)).astype(f32)
    q, k, v = (t.reshape(Bsz, S, GDN_HEADS, GDN_HEAD_DIM) for t in jnp.split(qkv, 3, axis=-1))
    q = l2_norm(q) * GDN_HEAD_DIM ** -0.5
    k = l2_norm(k)
    beta = jax.nn.sigmoid(b.astype(f32))
    g = -jnp.exp(a_log.astype(f32)) * jax.nn.softplus(a.astype(f32) + dt_bias.astype(f32))
    o = gated_delta_rule(q, k, v, g, beta)
    o = rms_norm(o, norm_w) * jax.nn.silu(z.astype(f32).reshape(Bsz, S, GDN_HEADS, GDN_HEAD_DIM))
    return o.reshape(Bsz, S, GDN_WIDTH).astype(z.dtype)


def nsa_mixer(q, kv, gate_logits, q_norm, k_norm, cmp_pe, cmp_w1, cmp_w2, cos, sin):
    Bsz, S, _ = q.shape
    f32 = jnp.float32
    G, R, dh = NSA_KV_HEADS, NSA_HEADS // NSA_KV_HEADS, NSA_HEAD_DIM
    scale = dh ** -0.5
    q = rope(rms_norm(q.reshape(Bsz, S, NSA_HEADS, dh), q_norm), cos, sin)
    kc_raw, vc_raw, ks, vs, kw, vw = (t.reshape(Bsz, S, G, dh) for t in jnp.split(kv, 6, axis=-1))
    ks = rope(rms_norm(ks, k_norm[1]), cos, sin)
    kw = rope(rms_norm(kw, k_norm[2]), cos, sin)

    n_cmp = (S - CMP_LEN) // CMP_STRIDE + 1
    win_idx = jnp.arange(n_cmp)[:, None] * CMP_STRIDE + jnp.arange(CMP_LEN)[None, :]
    cmp_end = win_idx[:, -1]

    def compress(t, pe, w1, w2):
        blocks = t[:, win_idx] + pe[:, None, :]
        blocks = jnp.moveaxis(blocks, 3, 2).reshape(Bsz, n_cmp, G, CMP_LEN * dh)
        return jax.nn.gelu(blocks @ w1) @ w2

    cos_c, sin_c = rope_tables(cmp_end, dh)
    k_cmp = rope(rms_norm(compress(kc_raw, cmp_pe[0], cmp_w1[0], cmp_w2[0]), k_norm[0]), cos_c, sin_c)
    v_cmp = compress(vc_raw, cmp_pe[1], cmp_w1[1], cmp_w2[1])

    n_slc = S // SLC_BLOCK
    n_top = min(SLC_TOPK, n_slc)
    c0 = jnp.arange(n_cmp) * CMP_STRIDE
    s0 = jnp.arange(n_slc) * SLC_BLOCK
    overlap = jnp.clip(jnp.minimum(c0[:, None] + CMP_LEN, s0[None, :] + SLC_BLOCK)
                       - jnp.maximum(c0[:, None], s0[None, :]), 0, None).astype(f32) / CMP_LEN
    gates = jax.nn.sigmoid(gate_logits.astype(f32)).reshape(Bsz, S, G, R * 3).transpose(0, 2, 1, 3)

    bf16 = jnp.bfloat16
    n_cmp_pad = S // CMP_STRIDE
    heads = lambda t: t.astype(bf16).transpose(0, 2, 1, 3)
    pad_cmp = lambda t: jnp.pad(heads(t), ((0, 0), (0, 0), (0, n_cmp_pad - n_cmp), (0, 0)))
    q_heads = (q * scale).astype(bf16).reshape(Bsz, S, G, R, dh).transpose(0, 2, 3, 1, 4)
    overlap_t = jnp.pad(overlap.T, ((0, 0), (0, n_cmp_pad - n_cmp))).astype(bf16)
    expand = (jnp.arange(n_slc)[:, None] == (jnp.arange(S) // SLC_BLOCK)[None, :]).astype(bf16)
    out = _nsa_attention(q_heads, pad_cmp(k_cmp), pad_cmp(v_cmp), heads(ks), heads(vs), heads(kw), heads(vw),
                         gates, overlap_t, expand, n_top)
    return out.transpose(0, 3, 1, 2, 4).reshape(Bsz, S, NSA_WIDTH)


NSA_R = NSA_HEADS // NSA_KV_HEADS
SLC_TILE = 512
MASK_NEG = -1e30


def _dot_nt(a, b):
    return lax.dot_general(a, b, (((1,), (1,)), ((), ())), preferred_element_type=jnp.float32)


def _nsa_body(q_ref, kc_ref, vc_ref, ks_ref, vs_ref, kw_ref, vw_ref, gate_ref, ovt_ref, exp_ref, o_ref, bias_ref,
              *, n_top):
    f32, bf16 = jnp.float32, jnp.bfloat16
    R, QB, dh = q_ref.shape
    S = ks_ref.shape[0]
    n_slc, n_cmp_pad = ovt_ref.shape
    M = R * QB
    t0 = pl.program_id(2) * QB
    q = q_ref[...].reshape(M, dh)
    t_rows = t0 + (lax.broadcasted_iota(jnp.int32, (M, 1), 0) & (QB - 1))

    s = _dot_nt(q, kc_ref[...])
    n_idx = lax.broadcasted_iota(jnp.int32, s.shape, 1)
    valid = (n_idx * CMP_STRIDE + (CMP_LEN - 1) <= t_rows) & (n_idx < n_cmp_pad - 1)
    s = jnp.where(valid, s, -jnp.inf)
    m = jnp.max(s, axis=-1, keepdims=True)
    m = jnp.where(jnp.isfinite(m), m, 0.0)
    p = jnp.exp(s - m)
    p = p / jnp.maximum(jnp.sum(p, axis=-1, keepdims=True), jnp.finfo(f32).tiny)
    o_cmp = jnp.dot(p.astype(bf16), vc_ref[...], preferred_element_type=f32)

    p_sum = p[0:QB]
    for r in range(1, R):
        p_sum = p_sum + p[r * QB:(r + 1) * QB]
    imp = jnp.zeros((n_slc, QB), f32)
    rem = p_sum
    for _ in range(3):
        part = rem.astype(bf16)
        imp = imp + _dot_nt(ovt_ref[...], part)
        rem = rem - part.astype(f32)
    j_idx = lax.broadcasted_iota(jnp.int32, (n_slc, QB), 0)
    t_lane = t0 + lax.broadcasted_iota(jnp.int32, (n_slc, QB), 1)
    cur = lax.shift_right_logical(t_lane, 6)
    forced = (j_idx == 0) | (j_idx == cur) | (j_idx == cur - 1)
    started = j_idx * SLC_BLOCK <= t_lane
    imp = jnp.where(forced, BIG, jnp.where(started, imp, -BIG))
    rank = jnp.zeros((n_slc, QB), f32)
    for jp in range(n_slc):
        row = imp[jp:jp + 1, :]
        ahead = (row > imp) | ((row == imp) & (j_idx > jp))
        rank = rank + jnp.where(ahead, 1.0, 0.0)
    sel_t = jnp.where(rank < n_top, 1.0, 0.0)
    sel = jnp.dot(sel_t.T.astype(bf16), exp_ref[...], preferred_element_type=f32)
    k_idx = lax.broadcasted_iota(jnp.int32, (QB, S), 1)
    t_q = t0 + lax.broadcasted_iota(jnp.int32, (QB, S), 0)
    bias_ref[...] = jnp.where((sel > 0.5) & (k_idx <= t_q), 0.0, MASK_NEG)

    def slc_step(kt, carry):
        m, l, acc = carry
        k0 = pl.multiple_of(kt * SLC_TILE, SLC_TILE)
        s = _dot_nt(q, ks_ref[pl.ds(k0, SLC_TILE), :])
        bias = bias_ref[:, pl.ds(k0, SLC_TILE)]
        s = s + jnp.concatenate([bias] * R, axis=0)
        m_new = jnp.maximum(m, jnp.max(s, axis=-1, keepdims=True))
        alpha = jnp.exp(m - m_new)
        p = jnp.exp(s - m_new)
        l = alpha * l + jnp.sum(p, axis=-1, keepdims=True)
        acc = alpha * acc + jnp.dot(p.astype(bf16), vs_ref[pl.ds(k0, SLC_TILE), :], preferred_element_type=f32)
        return m_new, l, acc

    n_tiles = (t0 + QB - 1) // SLC_TILE + 1
    init = (jnp.full((M, 1), MASK_NEG, f32), jnp.zeros((M, 1), f32), jnp.zeros((M, dh), f32))
    _, l, acc = lax.fori_loop(0, n_tiles, slc_step, init)
    o_slc = acc / l

    band = kw_ref.shape[0] if kw_ref.shape[0] < WINDOW + QB else WINDOW + QB
    start = pl.multiple_of(jnp.maximum(t0 + QB - band, 0), QB)
    s = _dot_nt(q, kw_ref[pl.ds(start, band), :])
    diff = t_rows - (start + lax.broadcasted_iota(jnp.int32, s.shape, 1))
    s = jnp.where((diff >= 0) & (diff < WINDOW), s, -jnp.inf)
    p = jnp.exp(s - jnp.max(s, axis=-1, keepdims=True))
    o_win = jnp.dot(p.astype(bf16), vw_ref[pl.ds(start, band), :], preferred_element_type=f32)
    o_win = o_win / jnp.sum(p, axis=-1, keepdims=True)

    gate = gate_ref[...]
    for r in range(R):
        rows = slice(r * QB, (r + 1) * QB)
        o_ref[r] = (gate[:, 3 * r:3 * r + 1] * o_cmp[rows] + gate[:, 3 * r + 1:3 * r + 2] * o_slc[rows]
                    + gate[:, 3 * r + 2:3 * r + 3] * o_win[rows])


def _nsa_attention(q, kc, vc, ks, vs, kw, vw, gates, overlap_t, expand, n_top):
    Bsz, G, R, S, dh = q.shape
    n_cmp_pad = kc.shape[2]
    n_slc = overlap_t.shape[0]
    per_head = lambda n: pl.BlockSpec((None, None, n, dh), lambda b, g, i: (b, g, 0, 0))
    const = lambda a: pl.BlockSpec(a.shape, lambda b, g, i: (0, 0))
    return pl.pallas_call(
        functools.partial(_nsa_body, n_top=n_top),
        grid=(Bsz, G, S // Q_BLOCK),
        in_specs=[pl.BlockSpec((None, None, R, Q_BLOCK, dh), lambda b, g, i: (b, g, 0, i, 0)),
                  per_head(n_cmp_pad), per_head(n_cmp_pad), per_head(S), per_head(S), per_head(S), per_head(S),
                  pl.BlockSpec((None, None, Q_BLOCK, 3 * R), lambda b, g, i: (b, g, i, 0)),
                  const(overlap_t), const(expand)],
        out_specs=pl.BlockSpec((None, None, R, Q_BLOCK, dh), lambda b, g, i: (b, g, 0, i, 0)),
        out_shape=jax.ShapeDtypeStruct((Bsz, G, R, S, dh), jnp.float32),
        scratch_shapes=[pltpu.VMEM((Q_BLOCK, S), jnp.float32)],
        compiler_params=pltpu.CompilerParams(dimension_semantics=("arbitrary", "arbitrary", "arbitrary"),
                                             vmem_limit_bytes=48 * 1024 * 1024),
        name="nsa_attention",
    )(q, kc, vc, ks, vs, kw, vw, gates, overlap_t, expand)


def hybrid_mixer(a, w_in, lam_re, lam_im, b_re, b_im, c_re, c_im, s5_d, s5_log_dt, glu_w, glu_b,
                 s5_out_norm, conv_w, a_log, dt_bias, gdn_norm_w, q_norm, k_norm, cmp_pe, cmp_w1,
                 cmp_w2, nsa_out_norm, w_out, cos, sin):
    proj = _matmul(a.reshape(-1, D_MODEL), w_in).reshape(a.shape[0], a.shape[1], D_IN)
    u, qkv, z, ga, gb, q, kv, gate_logits = jnp.split(proj, np.cumsum(IN_SIZES)[:-1].tolist(), axis=-1)
    y_s5 = s5_mixer_normed(u, lam_re, lam_im, b_re, b_im, c_re, c_im, s5_d, s5_log_dt, glu_w, glu_b, s5_out_norm)
    y_gdn = gdn_mixer(qkv, z, ga, gb, conv_w, a_log, dt_bias, gdn_norm_w)
    y_nsa = rms_norm(nsa_mixer(q, kv, gate_logits, q_norm, k_norm, cmp_pe, cmp_w1, cmp_w2, cos, sin), nsa_out_norm)
    cat = jnp.concatenate([y_s5, y_gdn.astype(y_s5.dtype), y_nsa.astype(y_s5.dtype)], axis=-1)
    return _matmul(cat.reshape(-1, D_MODEL), w_out).reshape(cat.shape)


def hier_moe(h, group_w, group_b, expert_w, expert_b, w_gate, w_up, w_down):
    Bsz, S, D = h.shape
    f32 = jnp.float32
    x = h.reshape(-1, D)
    T = x.shape[0]
    g_logits = (x @ group_w).astype(f32) + group_b.astype(f32)
    grp = jnp.argmax(g_logits, axis=-1)
    p_grp = jnp.take_along_axis(jax.nn.softmax(g_logits, axis=-1), grp[:, None], axis=-1)
    e_logits = ((x @ expert_w).astype(f32) + expert_b.astype(f32)).reshape(T, N_GROUPS, EXPERTS_PER_GROUP)
    e_logits = jnp.take_along_axis(e_logits, grp[:, None, None], axis=1)[:, 0]
    top_logit, top_local = lax.top_k(e_logits, TOP_K)
    weights = jax.nn.softmax(top_logit, axis=-1) * p_grp
    experts = grp[:, None] * EXPERTS_PER_GROUP + top_local
    n_assign = T * TOP_K
    e_flat = experts.reshape(-1)
    w_flat = weights.reshape(-1)
    tok_flat = jnp.repeat(jnp.arange(T, dtype=jnp.int32), TOP_K)
    order = jnp.argsort(e_flat)
    e_sorted = e_flat[order]
    counts = jnp.bincount(e_flat, length=N_EXPERTS)
    padded = (counts + MOE_BLOCK - 1) // MOE_BLOCK * MOE_BLOCK
    pad_start = jnp.cumsum(padded) - padded
    start = jnp.cumsum(counts) - counts
    dest = pad_start[e_sorted] + jnp.arange(n_assign) - start[e_sorted]
    n_blk = (n_assign + MOE_BLOCK - 1) // MOE_BLOCK + N_EXPERTS
    cap = n_blk * MOE_BLOCK
    slot_tok = jnp.full((cap,), T, jnp.int32).at[dest].set(tok_flat[order])
    slot_w = jnp.zeros((cap,), f32).at[dest].set(w_flat[order])
    blk_expert = jnp.minimum(jnp.searchsorted(jnp.cumsum(padded), jnp.arange(n_blk) * MOE_BLOCK, side='right'),
                             N_EXPERTS - 1)
    x_pad = jnp.concatenate([x, jnp.zeros((1, D), x.dtype)], axis=0)

    def expert_block(args):
        tok, wt, e = args
        xt = x_pad[tok]
        hid = jax.nn.silu(xt @ w_gate[e]) * (xt @ w_up[e])
        return (hid @ w_down[e]) * wt[:, None].astype(x.dtype)

    y = lax.map(expert_block, (slot_tok.reshape(n_blk, MOE_BLOCK), slot_w.reshape(n_blk, MOE_BLOCK), blk_expert))
    out = jnp.zeros((T + 1, D), y.dtype).at[slot_tok].add(y.reshape(cap, D))[:T]
    return out.reshape(Bsz, S, D).astype(h.dtype)


def kernel(x, ln_mix, w_in, s5_lambda_re, s5_lambda_im, s5_b_re, s5_b_im, s5_c_re, s5_c_im, s5_d,
           s5_log_dt, s5_glu_w, s5_glu_b, s5_out_norm, gdn_conv_w, gdn_a_log, gdn_dt_bias, gdn_norm_w,
           nsa_q_norm, nsa_k_norm, nsa_cmp_pe, nsa_cmp_w1, nsa_cmp_w2, nsa_out_norm, w_out, ln_ffn,
           moe_group_w, moe_group_b, moe_expert_w, moe_expert_b, moe_w_gate, moe_w_up, moe_w_down):
    S = x.shape[1]
    cos, sin = rope_tables(jnp.arange(S), NSA_HEAD_DIM)
    for l in range(DEPTH):
        y = hybrid_mixer(rms_norm(x, ln_mix[l]), w_in[l], s5_lambda_re[l], s5_lambda_im[l], s5_b_re[l],
                         s5_b_im[l], s5_c_re[l], s5_c_im[l], s5_d[l], s5_log_dt[l], s5_glu_w[l], s5_glu_b[l],
                         s5_out_norm[l], gdn_conv_w[l], gdn_a_log[l], gdn_dt_bias[l], gdn_norm_w[l],
                         nsa_q_norm[l], nsa_k_norm[l], nsa_cmp_pe[l], nsa_cmp_w1[l], nsa_cmp_w2[l],
                         nsa_out_norm[l], w_out[l], cos, sin)
        h = x + y.astype(x.dtype)
        x = h + hier_moe(rms_norm(h, ln_ffn[l]), moe_group_w[l], moe_group_b[l], moe_expert_w[l],
                         moe_expert_b[l], moe_w_gate[l], moe_w_up[l], moe_w_down[l])
    return x
```

```python
import functools
import math
import jax, jax.numpy as jnp
from jax import lax
import numpy as np
from jax.experimental import pallas as pl
from jax.experimental.pallas import tpu as pltpu

D_MODEL = 2048
BATCH = 4
SEQ = 4096
DEPTH = 2

EPS = 1e-6
S5_WIDTH = D_MODEL // 4
S5_GROUP = 16
S5_GROUPS = S5_WIDTH // S5_GROUP
S5_STATE = 64
GDN_HEAD_DIM = 128
GDN_WIDTH = D_MODEL // 4
GDN_HEADS = GDN_WIDTH // GDN_HEAD_DIM
GDN_CONV = 4
GDN_CHUNK = 64
NSA_HEAD_DIM = 64
NSA_WIDTH = D_MODEL // 2
NSA_HEADS = NSA_WIDTH // NSA_HEAD_DIM
NSA_KV_HEADS = 4
NSA_KV_WIDTH = NSA_KV_HEADS * NSA_HEAD_DIM
CMP_LEN = 32
CMP_STRIDE = 16
CMP_HIDDEN = 256
SLC_BLOCK = 64
SLC_TOPK = 16
WINDOW = 512
Q_BLOCK = 128
ROPE_THETA = 10000.0
BIG = 1e9
N_GROUPS = 4
EXPERTS_PER_GROUP = 8
N_EXPERTS = N_GROUPS * EXPERTS_PER_GROUP
TOP_K = 2
D_EXPERT = 512
MOE_BLOCK = 128
IN_SIZES = (S5_WIDTH, 3 * GDN_WIDTH, GDN_WIDTH, GDN_HEADS, GDN_HEADS, NSA_WIDTH, 6 * NSA_KV_WIDTH, 3 * NSA_HEADS)
D_IN = sum(IN_SIZES)


def _matmul_body(a_ref, w_ref, o_ref):
    o_ref[...] = jnp.dot(a_ref[...].astype(jnp.bfloat16), w_ref[...].astype(jnp.bfloat16),
                         preferred_element_type=jnp.float32)


def _matmul(a, w, tm=512, tn=512):
    M, K = a.shape
    N = w.shape[1]
    n_pad = (-N) % tn
    if n_pad:
        w = jnp.pad(w, ((0, 0), (0, n_pad)))
    Np = N + n_pad
    out = pl.pallas_call(
        _matmul_body,
        grid=(Np // tn, M // tm),
        in_specs=[pl.BlockSpec((tm, K), lambda j, i: (i, 0)),
                  pl.BlockSpec((K, tn), lambda j, i: (0, j))],
        out_specs=pl.BlockSpec((tm, tn), lambda j, i: (i, j)),
        out_shape=jax.ShapeDtypeStruct((M, Np), jnp.float32),
        compiler_params=pltpu.CompilerParams(vmem_limit_bytes=48 * 1024 * 1024),
    )(a, w)
    return out[:, :N] if n_pad else out


def rms_norm(x, gain):
    xf = x.astype(jnp.float32)
    y = xf * lax.rsqrt(jnp.mean(xf * xf, axis=-1, keepdims=True) + EPS)
    return (y * gain.astype(jnp.float32)).astype(x.dtype)


def l2_norm(x):
    return x * lax.rsqrt(jnp.sum(x * x, axis=-1, keepdims=True) + EPS)


def rope_tables(pos, dim):
    inv_freq = ROPE_THETA ** (-jnp.arange(0, dim, 2, dtype=jnp.float32) / dim)
    ang = pos.astype(jnp.float32)[:, None] * inv_freq[None, :]
    return jnp.cos(ang)[:, None, :], jnp.sin(ang)[:, None, :]


def rope(x, cos, sin):
    x1, x2 = jnp.split(x, 2, axis=-1)
    return jnp.concatenate([x1 * cos - x2 * sin, x2 * cos + x1 * sin], axis=-1).astype(x.dtype)


def masked_softmax(s, mask):
    s = jnp.where(mask, s, -jnp.inf)
    m = jnp.max(s, axis=-1, keepdims=True)
    m = jnp.where(jnp.isfinite(m), m, 0.0)
    p = jnp.exp(s - m)
    return p / jnp.maximum(jnp.sum(p, axis=-1, keepdims=True), jnp.finfo(jnp.float32).tiny)


S5_OCT = 8
S5_NOCT = S5_GROUPS // S5_OCT
S5_OCT_IN = S5_OCT * S5_GROUP
S5_OCT_ST = S5_OCT * S5_STATE
SUBLANES = 8
S5_SCAN_SHIFTS = (1, 2, 4)


def _s5_constants(lam_re, lam_im, b_re, b_im, c_re, c_im, log_dt):
    f32 = jnp.float32
    lr, li = lam_re.astype(f32), lam_im.astype(f32)
    dt = jnp.exp(log_dt.astype(f32))[:, None]

    def lam_pow(k):
        mag = jnp.exp(k * lr * dt)
        return mag * jnp.cos(k * li * dt), mag * jnp.sin(k * li * dt)

    ar, ai = lam_pow(1.0)
    nr, ni = ar - 1.0, ai
    den = lr * lr + li * li
    qr, qi = (nr * lr + ni * li) / den, (ni * lr - nr * li) / den
    br, bi = b_re.astype(f32), b_im.astype(f32)
    bbr = qr[:, :, None] * br - qi[:, :, None] * bi
    bbi = qr[:, :, None] * bi + qi[:, :, None] * br
    eye = jnp.eye(S5_OCT, dtype=f32)

    def blk_b(t):
        t = t.reshape(S5_NOCT, S5_OCT, S5_STATE, S5_GROUP)
        return jnp.einsum('qgph,gk->qghkp', t, eye).reshape(S5_NOCT, S5_OCT_IN, S5_OCT_ST)

    def blk_c(t):
        t = t.reshape(S5_NOCT, S5_OCT, S5_GROUP, S5_STATE)
        return jnp.einsum('qghp,gk->qgpkh', t, eye).reshape(S5_NOCT, S5_OCT_ST, S5_OCT_IN)

    b_mat = jnp.stack([blk_b(bbr), blk_b(bbi)]).astype(jnp.bfloat16)
    c_mat = jnp.stack([blk_c(c_re.astype(f32)), blk_c(-c_im.astype(f32))]).astype(jnp.bfloat16)
    rows = jnp.arange(SUBLANES, dtype=f32)[:, None, None]
    tiles = []
    for k in S5_SCAN_SHIFTS:
        pr, pi = lam_pow(jnp.full_like(rows, float(k)))
        keep = rows >= k
        tiles.append(jnp.stack([jnp.where(keep, pr, 0.0), jnp.where(keep, pi, 0.0)]))
    pr, pi = lam_pow(rows + 1.0)
    tiles.append(jnp.stack([pr, pi]))
    a_mat = jnp.stack(tiles).reshape(len(tiles), 2, SUBLANES, S5_GROUPS * S5_STATE)
    return b_mat, c_mat, a_mat


def _s5_body(u_ref, b_ref, c_ref, a_ref, d_ref, gw_ref, gb_ref, gain_ref, o_ref, st_ref, xr_ref, xi_ref):
    bf16 = jnp.bfloat16
    n_tiles = u_ref.shape[0] // SUBLANES

    @pl.when(pl.program_id(1) == 0)
    def _():
        st_ref[...] = jnp.zeros_like(st_ref)

    u = u_ref[...]
    ys = []
    for q in range(S5_NOCT):
        lanes = slice(q * S5_OCT_ST, (q + 1) * S5_OCT_ST)
        uq = u[:, q * S5_OCT_IN:(q + 1) * S5_OCT_IN].astype(bf16)
        xr_ref[...] = jnp.dot(uq, b_ref[0, q], preferred_element_type=jnp.float32)
        xi_ref[...] = jnp.dot(uq, b_ref[1, q], preferred_element_type=jnp.float32)

        def tile_step(i, carry, lanes=lanes):
            cr, ci = carry
            r0 = pl.multiple_of(i * SUBLANES, SUBLANES)
            xr = xr_ref[pl.ds(r0, SUBLANES), :]
            xi = xi_ref[pl.ds(r0, SUBLANES), :]
            for lvl, k in enumerate(S5_SCAN_SHIFTS):
                pr, pi = a_ref[lvl, 0, :, lanes], a_ref[lvl, 1, :, lanes]
                sr, si = pltpu.roll(xr, k, 0), pltpu.roll(xi, k, 0)
                xr, xi = xr + (pr * sr - pi * si), xi + (pr * si + pi * sr)
            pr, pi = a_ref[len(S5_SCAN_SHIFTS), 0, :, lanes], a_ref[len(S5_SCAN_SHIFTS), 1, :, lanes]
            cr = jnp.broadcast_to(cr, xr.shape)
            ci = jnp.broadcast_to(ci, xi.shape)
            xr, xi = xr + (pr * cr - pi * ci), xi + (pr * ci + pi * cr)
            xr_ref[pl.ds(r0, SUBLANES), :] = xr
            xi_ref[pl.ds(r0, SUBLANES), :] = xi
            return xr[SUBLANES - 1:, :], xi[SUBLANES - 1:, :]

        cr, ci = lax.fori_loop(0, n_tiles, tile_step, (st_ref[0, :, lanes], st_ref[1, :, lanes]))
        st_ref[0, :, lanes] = cr
        st_ref[1, :, lanes] = ci
        ys.append(jnp.dot(xr_ref[...].astype(bf16), c_ref[0, q], preferred_element_type=jnp.float32)
                  + jnp.dot(xi_ref[...].astype(bf16), c_ref[1, q], preferred_element_type=jnp.float32))
    y = jnp.concatenate(ys, axis=1) + d_ref[...] * u
    y = jax.nn.gelu(y)
    z = jnp.dot(y.astype(bf16), gw_ref[...], preferred_element_type=jnp.float32) + gb_ref[...]
    y = y * jax.nn.sigmoid(z)
    y = y * lax.rsqrt(jnp.mean(y * y, axis=-1, keepdims=True) + EPS)
    o_ref[...] = y * gain_ref[...]


def s5_mixer_normed(u, lam_re, lam_im, b_re, b_im, c_re, c_im, d_skip, log_dt, glu_w, glu_b, out_norm, chunk=512):
    Bsz, S, W = u.shape
    f32 = jnp.float32
    L = min(chunk, S)
    b_mat, c_mat, a_mat = _s5_constants(lam_re, lam_im, b_re, b_im, c_re, c_im, log_dt)
    n_state = S5_GROUPS * S5_STATE
    full = lambda *shape: pl.BlockSpec(shape, lambda b, c: (0,) * len(shape))
    return pl.pallas_call(
        _s5_body,
        grid=(Bsz, S // L),
        in_specs=[pl.BlockSpec((None, L, W), lambda b, c: (b, c, 0)),
                  full(*b_mat.shape), full(*c_mat.shape), full(*a_mat.shape),
                  full(1, W), full(W, W), full(1, W), full(1, W)],
        out_specs=pl.BlockSpec((None, L, W), lambda b, c: (b, c, 0)),
        out_shape=jax.ShapeDtypeStruct((Bsz, S, W), f32),
        scratch_shapes=[pltpu.VMEM((2, 1, n_state), f32),
                        pltpu.VMEM((L, S5_OCT_ST), f32), pltpu.VMEM((L, S5_OCT_ST), f32)],
        compiler_params=pltpu.CompilerParams(dimension_semantics=("arbitrary", "arbitrary"),
                                             vmem_limit_bytes=48 * 1024 * 1024),
        name="s5_mixer",
    )(u.astype(f32), b_mat, c_mat, a_mat, d_skip.astype(f32).reshape(1, W), glu_w.astype(jnp.bfloat16),
      glu_b.astype(f32).reshape(1, W), out_norm.astype(f32).reshape(1, W))


def short_causal_conv(x, w):
    C = x.shape[-1]
    return lax.conv_general_dilated(x, w[:, None, :].astype(x.dtype), window_strides=(1,),
                                    padding=[(GDN_CONV - 1, 0)], dimension_numbers=('NWC', 'WIO', 'NWC'),
                                    feature_group_count=C)


def gated_delta_rule(q, k, v, g, beta):
    Bsz, S, H, dk = q.shape
    dv = v.shape[-1]
    C = GDN_CHUNK
    N = S // C

    def chunks(t):
        return t.reshape(Bsz, N, C, H, -1).transpose(1, 0, 3, 2, 4)

    qc, kc, vc = chunks(q), chunks(k), chunks(v)
    gc = g.reshape(Bsz, N, C, H).transpose(1, 0, 3, 2)
    bc = beta.reshape(Bsz, N, C, H).transpose(1, 0, 3, 2)
    gcum = jnp.cumsum(gc, axis=-1)
    incl = jnp.tril(jnp.ones((C, C), dtype=bool))
    strict = jnp.tril(jnp.ones((C, C), dtype=bool), -1)
    decay = jnp.exp(jnp.where(incl, gcum[..., :, None] - gcum[..., None, :], -jnp.inf))
    k_beta = kc * bc[..., None]
    a_mat = jnp.where(strict, jnp.einsum('nbhid,nbhjd->nbhij', k_beta, kc) * decay, 0.0)
    eye = jnp.eye(C, dtype=jnp.float32)
    t_mat = lax.linalg.triangular_solve(a_mat + eye, jnp.broadcast_to(eye, a_mat.shape),
                                        left_side=True, lower=True, unit_diagonal=True)
    u = t_mat @ (vc * bc[..., None])
    w = t_mat @ (k_beta * jnp.exp(gcum)[..., None])
    qk = jnp.einsum('nbhid,nbhjd->nbhij', qc, kc) * decay

    def step(state, inp):
        q_n, k_n, u_n, w_n, g_n, qk_n = inp
        v_new = u_n - w_n @ state
        o = (q_n * jnp.exp(g_n)[..., None]) @ state + qk_n @ v_new
        g_last = g_n[..., -1:]
        state = state * jnp.exp(g_last)[..., None] + jnp.einsum(
            'bhcd,bhce->bhde', k_n * jnp.exp(g_last - g_n)[..., None], v_new)
        return state, o

    state0 = jnp.zeros((Bsz, H, dk, dv), jnp.float32)
    _, o = lax.scan(step, state0, (qc, kc, u, w, gcum, qk))
    return o.transpose(1, 0, 3, 2, 4).reshape(Bsz, S, H, dv)


def gdn_mixer(qkv, z, a, b, conv_w, a_log, dt_bias, norm_w):
    Bsz, S, _ = qkv.shape
    f32 = jnp.float32
    qkv = jax.nn.silu(short_causal_conv(qkv, conv_w)).astype(f32)
    q, k, v = (t.reshape(Bsz, S, GDN_HEADS, GDN_HEAD_DIM) for t in jnp.split(qkv, 3, axis=-1))
    q = l2_norm(q) * GDN_HEAD_DIM ** -0.5
    k = l2_norm(k)
    beta = jax.nn.sigmoid(b.astype(f32))
    g = -jnp.exp(a_log.astype(f32)) * jax.nn.softplus(a.astype(f32) + dt_bias.astype(f32))
    o = gated_delta_rule(q, k, v, g, beta)
    o = rms_norm(o, norm_w) * jax.nn.silu(z.astype(f32).reshape(Bsz, S, GDN_HEADS, GDN_HEAD_DIM))
    return o.reshape(Bsz, S, GDN_WIDTH).astype(z.dtype)


def nsa_mixer(q, kv, gate_logits, q_norm, k_norm, cmp_pe, cmp_w1, cmp_w2, cos, sin):
    Bsz, S, _ = q.shape
    f32 = jnp.float32
    G, R, dh = NSA_KV_HEADS, NSA_HEADS // NSA_KV_HEADS, NSA_HEAD_DIM
    scale = dh ** -0.5
    q = rope(rms_norm(q.reshape(Bsz, S, NSA_HEADS, dh), q_norm), cos, sin)
    kc_raw, vc_raw, ks, vs, kw, vw = (t.reshape(Bsz, S, G, dh) for t in jnp.split(kv, 6, axis=-1))
    ks = rope(rms_norm(ks, k_norm[1]), cos, sin)
    kw = rope(rms_norm(kw, k_norm[2]), cos, sin)

    n_cmp = (S - CMP_LEN) // CMP_STRIDE + 1
    win_idx = jnp.arange(n_cmp)[:, None] * CMP_STRIDE + jnp.arange(CMP_LEN)[None, :]
    cmp_end = win_idx[:, -1]

    def compress(t, pe, w1, w2):
        blocks = t[:, win_idx] + pe[:, None, :]
        blocks = jnp.moveaxis(blocks, 3, 2).reshape(Bsz, n_cmp, G, CMP_LEN * dh)
        return jax.nn.gelu(blocks @ w1) @ w2

    cos_c, sin_c = rope_tables(cmp_end, dh)
    k_cmp = rope(rms_norm(compress(kc_raw, cmp_pe[0], cmp_w1[0], cmp_w2[0]), k_norm[0]), cos_c, sin_c)
    v_cmp = compress(vc_raw, cmp_pe[1], cmp_w1[1], cmp_w2[1])

    n_slc = S // SLC_BLOCK
    n_top = min(SLC_TOPK, n_slc)
    c0 = jnp.arange(n_cmp) * CMP_STRIDE
    s0 = jnp.arange(n_slc) * SLC_BLOCK
    overlap = jnp.clip(jnp.minimum(c0[:, None] + CMP_LEN, s0[None, :] + SLC_BLOCK)
                       - jnp.maximum(c0[:, None], s0[None, :]), 0, None).astype(f32) / CMP_LEN
    gates = jax.nn.sigmoid(gate_logits.astype(f32)).reshape(Bsz, S, G, R * 3).transpose(0, 2, 1, 3)

    bf16 = jnp.bfloat16
    n_cmp_pad = S // CMP_STRIDE
    heads = lambda t: t.astype(bf16).transpose(0, 2, 1, 3)
    pad_cmp = lambda t: jnp.pad(heads(t), ((0, 0), (0, 0), (0, n_cmp_pad - n_cmp), (0, 0)))
    q_heads = (q * scale).astype(bf16).reshape(Bsz, S, G, R, dh).transpose(0, 2, 3, 1, 4)
    overlap_t = jnp.pad(overlap.T, ((0, 0), (0, n_cmp_pad - n_cmp))).astype(bf16)
    expand = (jnp.arange(n_slc)[:, None] == (jnp.arange(S) // SLC_BLOCK)[None, :]).astype(bf16)
    out = _nsa_attention(q_heads, pad_cmp(k_cmp), pad_cmp(v_cmp), heads(ks), heads(vs), heads(kw), heads(vw),
                         gates, overlap_t, expand, n_top)
    return out.transpose(0, 3, 1, 2, 4).reshape(Bsz, S, NSA_WIDTH)


NSA_R = NSA_HEADS // NSA_KV_HEADS
SLC_TILE = 512
MASK_NEG = -1e30


def _dot_nt(a, b):
    return lax.dot_general(a, b, (((1,), (1,)), ((), ())), preferred_element_type=jnp.float32)


def _nsa_body(q_ref, kc_ref, vc_ref, ks_ref, vs_ref, kw_ref, vw_ref, gate_ref, ovt_ref, exp_ref, o_ref, bias_ref,
              *, n_top):
    f32, bf16 = jnp.float32, jnp.bfloat16
    R, QB, dh = q_ref.shape
    S = ks_ref.shape[0]
    n_slc, n_cmp_pad = ovt_ref.shape
    M = R * QB
    t0 = pl.program_id(2) * QB
    q = q_ref[...].reshape(M, dh)
    t_rows = t0 + (lax.broadcasted_iota(jnp.int32, (M, 1), 0) & (QB - 1))

    s = _dot_nt(q, kc_ref[...])
    n_idx = lax.broadcasted_iota(jnp.int32, s.shape, 1)
    valid = (n_idx * CMP_STRIDE + (CMP_LEN - 1) <= t_rows) & (n_idx < n_cmp_pad - 1)
    s = jnp.where(valid, s, -jnp.inf)
    m = jnp.max(s, axis=-1, keepdims=True)
    m = jnp.where(jnp.isfinite(m), m, 0.0)
    p = jnp.exp(s - m)
    p = p / jnp.maximum(jnp.sum(p, axis=-1, keepdims=True), jnp.finfo(f32).tiny)
    o_cmp = jnp.dot(p.astype(bf16), vc_ref[...], preferred_element_type=f32)

    p_sum = p[0:QB]
    for r in range(1, R):
        p_sum = p_sum + p[r * QB:(r + 1) * QB]
    imp = jnp.zeros((n_slc, QB), f32)
    rem = p_sum
    for _ in range(3):
        part = rem.astype(bf16)
        imp = imp + _dot_nt(ovt_ref[...], part)
        rem = rem - part.astype(f32)
    j_idx = lax.broadcasted_iota(jnp.int32, (n_slc, QB), 0)
    t_lane = t0 + lax.broadcasted_iota(jnp.int32, (n_slc, QB), 1)
    cur = lax.shift_right_logical(t_lane, 6)
    forced = (j_idx == 0) | (j_idx == cur) | (j_idx == cur - 1)
    started = j_idx * SLC_BLOCK <= t_lane
    imp = jnp.where(forced, BIG, jnp.where(started, imp, -BIG))
    rank = jnp.zeros((n_slc, QB), f32)
    for jp in range(n_slc):
        row = imp[jp:jp + 1, :]
        ahead = (row > imp) | ((row == imp) & (j_idx > jp))
        rank = rank + jnp.where(ahead, 1.0, 0.0)
    sel_t = jnp.where(rank < n_top, 1.0, 0.0)
    sel = jnp.dot(sel_t.T.astype(bf16), exp_ref[...], preferred_element_type=f32)
    k_idx = lax.broadcasted_iota(jnp.int32, (QB, S), 1)
    t_q = t0 + lax.broadcasted_iota(jnp.int32, (QB, S), 0)
    bias_ref[...] = jnp.where((sel > 0.5) & (k_idx <= t_q), 0.0, MASK_NEG)

    def slc_step(kt, carry):
        m, l, acc = carry
        k0 = pl.multiple_of(kt * SLC_TILE, SLC_TILE)
        s = _dot_nt(q, ks_ref[pl.ds(k0, SLC_TILE), :])
        bias = bias_ref[:, pl.ds(k0, SLC_TILE)]
        s = s + jnp.concatenate([bias] * R, axis=0)
        m_new = jnp.maximum(m, jnp.max(s, axis=-1, keepdims=True))
        alpha = jnp.exp(m - m_new)
        p = jnp.exp(s - m_new)
        l = alpha * l + jnp.sum(p, axis=-1, keepdims=True)
        acc = alpha * acc + jnp.dot(p.astype(bf16), vs_ref[pl.ds(k0, SLC_TILE), :], preferred_element_type=f32)
        return m_new, l, acc

    n_tiles = (t0 + QB - 1) // SLC_TILE + 1
    init = (jnp.full((M, 1), MASK_NEG, f32), jnp.zeros((M, 1), f32), jnp.zeros((M, dh), f32))
    _, l, acc = lax.fori_loop(0, n_tiles, slc_step, init)
    o_slc = acc / l

    band = kw_ref.shape[0] if kw_ref.shape[0] < WINDOW + QB else WINDOW + QB
    start = pl.multiple_of(jnp.maximum(t0 + QB - band, 0), QB)
    s = _dot_nt(q, kw_ref[pl.ds(start, band), :])
    diff = t_rows - (start + lax.broadcasted_iota(jnp.int32, s.shape, 1))
    s = jnp.where((diff >= 0) & (diff < WINDOW), s, -jnp.inf)
    p = jnp.exp(s - jnp.max(s, axis=-1, keepdims=True))
    o_win = jnp.dot(p.astype(bf16), vw_ref[pl.ds(start, band), :], preferred_element_type=f32)
    o_win = o_win / jnp.sum(p, axis=-1, keepdims=True)

    gate = gate_ref[...]
    for r in range(R):
        rows = slice(r * QB, (r + 1) * QB)
        o_ref[r] = (gate[:, 3 * r:3 * r + 1] * o_cmp[rows] + gate[:, 3 * r + 1:3 * r + 2] * o_slc[rows]
                    + gate[:, 3 * r + 2:3 * r + 3] * o_win[rows])


def _nsa_attention(q, kc, vc, ks, vs, kw, vw, gates, overlap_t, expand, n_top):
    Bsz, G, R, S, dh = q.shape
    n_cmp_pad = kc.shape[2]
    n_slc = overlap_t.shape[0]
    per_head = lambda n: pl.BlockSpec((None, None, n, dh), lambda b, g, i: (b, g, 0, 0))
    const = lambda a: pl.BlockSpec(a.shape, lambda b, g, i: (0, 0))
    return pl.pallas_call(
        functools.partial(_nsa_body, n_top=n_top),
        grid=(Bsz, G, S // Q_BLOCK),
        in_specs=[pl.BlockSpec((None, None, R, Q_BLOCK, dh), lambda b, g, i: (b, g, 0, i, 0)),
                  per_head(n_cmp_pad), per_head(n_cmp_pad), per_head(S), per_head(S), per_head(S), per_head(S),
                  pl.BlockSpec((None, None, Q_BLOCK, 3 * R), lambda b, g, i: (b, g, i, 0)),
                  const(overlap_t), const(expand)],
        out_specs=pl.BlockSpec((None, None, R, Q_BLOCK, dh), lambda b, g, i: (b, g, 0, i, 0)),
        out_shape=jax.ShapeDtypeStruct((Bsz, G, R, S, dh), jnp.float32),
        scratch_shapes=[pltpu.VMEM((Q_BLOCK, S), jnp.float32)],
        compiler_params=pltpu.CompilerParams(dimension_semantics=("arbitrary", "arbitrary", "arbitrary"),
                                             vmem_limit_bytes=48 * 1024 * 1024),
        name="nsa_attention",
    )(q, kc, vc, ks, vs, kw, vw, gates, overlap_t, expand)


def hybrid_mixer(a, w_in, lam_re, lam_im, b_re, b_im, c_re, c_im, s5_d, s5_log_dt, glu_w, glu_b,
                 s5_out_norm, conv_w, a_log, dt_bias, gdn_norm_w, q_norm, k_norm, cmp_pe, cmp_w1,
                 cmp_w2, nsa_out_norm, w_out, cos, sin):
    proj = _matmul(a.reshape(-1, D_MODEL), w_in).reshape(a.shape[0], a.shape[1], D_IN)
    u, qkv, z, ga, gb, q, kv, gate_logits = jnp.split(proj, np.cumsum(IN_SIZES)[:-1].tolist(), axis=-1)
    y_s5 = s5_mixer_normed(u, lam_re, lam_im, b_re, b_im, c_re, c_im, s5_d, s5_log_dt, glu_w, glu_b, s5_out_norm)
    y_gdn = gdn_mixer(qkv, z, ga, gb, conv_w, a_log, dt_bias, gdn_norm_w)
    y_nsa = rms_norm(nsa_mixer(q, kv, gate_logits, q_norm, k_norm, cmp_pe, cmp_w1, cmp_w2, cos, sin), nsa_out_norm)
    cat = jnp.concatenate([y_s5, y_gdn.astype(y_s5.dtype), y_nsa.astype(y_s5.dtype)], axis=-1)
    return _matmul(cat.reshape(-1, D_MODEL), w_out).reshape(cat.shape)


def hier_moe(h, group_w, group_b, expert_w, expert_b, w_gate, w_up, w_down):
    Bsz, S, D = h.shape
    f32 = jnp.float32
    x = h.reshape(-1, D)
    T = x.shape[0]
    g_logits = (x @ group_w).astype(f32) + group_b.astype(f32)
    grp = jnp.argmax(g_logits, axis=-1)
    p_grp = jnp.take_along_axis(jax.nn.softmax(g_logits, axis=-1), grp[:, None], axis=-1)
    e_logits = ((x @ expert_w).astype(f32) + expert_b.astype(f32)).reshape(T, N_GROUPS, EXPERTS_PER_GROUP)
    e_logits = jnp.take_along_axis(e_logits, grp[:, None, None], axis=1)[:, 0]
    top_logit, top_local = lax.top_k(e_logits, TOP_K)
    weights = jax.nn.softmax(top_logit, axis=-1) * p_grp
    experts = grp[:, None] * EXPERTS_PER_GROUP + top_local
    n_assign = T * TOP_K
    e_flat = experts.reshape(-1)
    w_flat = weights.reshape(-1)
    tok_flat = jnp.repeat(jnp.arange(T, dtype=jnp.int32), TOP_K)
    order = jnp.argsort(e_flat)
    e_sorted = e_flat[order]
    counts = jnp.bincount(e_flat, length=N_EXPERTS)
    padded = (counts + MOE_ROWS - 1) // MOE_ROWS * MOE_ROWS
    pad_start = jnp.cumsum(padded) - padded
    start = jnp.cumsum(counts) - counts
    dest = (pad_start[e_sorted] + jnp.arange(n_assign) - start[e_sorted]).astype(jnp.int32)
    n_blk = (n_assign + MOE_ROWS - 1) // MOE_ROWS + N_EXPERTS
    cap = n_blk * MOE_ROWS
    slot_tok = jnp.full((cap,), T, jnp.int32).at[dest].set(tok_flat[order])
    slot_w = jnp.zeros((cap,), f32).at[dest].set(w_flat[order])
    blk_expert = jnp.minimum(jnp.searchsorted(jnp.cumsum(padded), jnp.arange(n_blk) * MOE_ROWS, side='right'),
                             N_EXPERTS - 1).astype(jnp.int32)
    n_used = (jnp.sum(padded) // MOE_ROWS).astype(jnp.int32).reshape(1)
    x_pad = jnp.concatenate([x, jnp.zeros((1, D), x.dtype)], axis=0).astype(jnp.bfloat16)
    y = _moe_ffn(blk_expert, n_used, x_pad[slot_tok], slot_w.reshape(cap, 1), w_gate.astype(jnp.bfloat16),
                 w_up.astype(jnp.bfloat16), w_down.astype(jnp.bfloat16))
    slot_of = jnp.zeros((n_assign,), jnp.int32).at[order].set(dest).reshape(T, TOP_K)
    out = y[slot_of[:, 0]]
    for k in range(1, TOP_K):
        out = out + y[slot_of[:, k]]
    return out.reshape(Bsz, S, D).astype(h.dtype)


MOE_ROWS = 256


def _moe_ffn_body(be_ref, nu_ref, x_ref, w_ref, wg_ref, wu_ref, wd_ref, o_ref):
    del be_ref
    used = pl.program_id(0) < nu_ref[0]

    @pl.when(used)
    def _():
        x = x_ref[...]
        gate = jnp.dot(x, wg_ref[...], preferred_element_type=jnp.float32)
        up = jnp.dot(x, wu_ref[...], preferred_element_type=jnp.float32)
        hid = (jax.nn.silu(gate) * up).astype(jnp.bfloat16)
        o_ref[...] = jnp.dot(hid, wd_ref[...], preferred_element_type=jnp.float32) * w_ref[...]

    @pl.when(jnp.logical_not(used))
    def _():
        o_ref[...] = jnp.zeros_like(o_ref)


def _moe_ffn(blk_expert, n_used, xs, ws, w_gate, w_up, w_down):
    cap, D = xs.shape
    n_blk = cap // MOE_ROWS
    De = w_gate.shape[-1]
    rows = lambda n: pl.BlockSpec((MOE_ROWS, n), lambda i, be, nu: (i, 0))
    return pl.pallas_call(
        _moe_ffn_body,
        grid_spec=pltpu.PrefetchScalarGridSpec(
            num_scalar_prefetch=2, grid=(n_blk,),
            in_specs=[rows(D), rows(1),
                      pl.BlockSpec((None, D, De), lambda i, be, nu: (be[i], 0, 0)),
                      pl.BlockSpec((None, D, De), lambda i, be, nu: (be[i], 0, 0)),
                      pl.BlockSpec((None, De, D), lambda i, be, nu: (be[i], 0, 0))],
            out_specs=rows(D)),
        out_shape=jax.ShapeDtypeStruct((cap, D), jnp.float32),
        compiler_params=pltpu.CompilerParams(dimension_semantics=("arbitrary",),
                                             vmem_limit_bytes=48 * 1024 * 1024),
        name="moe_ffn",
    )(blk_expert, n_used, xs, ws, w_gate, w_up, w_down)


def kernel(x, ln_mix, w_in, s5_lambda_re, s5_lambda_im, s5_b_re, s5_b_im, s5_c_re, s5_c_im, s5_d,
           s5_log_dt, s5_glu_w, s5_glu_b, s5_out_norm, gdn_conv_w, gdn_a_log, gdn_dt_bias, gdn_norm_w,
           nsa_q_norm, nsa_k_norm, nsa_cmp_pe, nsa_cmp_w1, nsa_cmp_w2, nsa_out_norm, w_out, ln_ffn,
           moe_group_w, moe_group_b, moe_expert_w, moe_expert_b, moe_w_gate, moe_w_up, moe_w_down):
    S = x.shape[1]
    cos, sin = rope_tables(jnp.arange(S), NSA_HEAD_DIM)
    for l in range(DEPTH):
        y = hybrid_mixer(rms_norm(x, ln_mix[l]), w_in[l], s5_lambda_re[l], s5_lambda_im[l], s5_b_re[l],
                         s5_b_im[l], s5_c_re[l], s5_c_im[l], s5_d[l], s5_log_dt[l], s5_glu_w[l], s5_glu_b[l],
                         s5_out_norm[l], gdn_conv_w[l], gdn_a_log[l], gdn_dt_bias[l], gdn_norm_w[l],
                         nsa_q_norm[l], nsa_k_norm[l], nsa_cmp_pe[l], nsa_cmp_w1[l], nsa_cmp_w2[l],
                         nsa_out_norm[l], w_out[l], cos, sin)
        h = x + y.astype(x.dtype)
        x = h + hier_moe(rms_norm(h, ln_ffn[l]), moe_group_w[l], moe_group_b[l], moe_expert_w[l],
                         moe_expert_b[l], moe_w_gate[l], moe_w_up[l], moe_w_down[l])
    return x
```

```python
import functools
import math
import jax, jax.numpy as jnp
from jax import lax
import numpy as np
from jax.experimental import pallas as pl
from jax.experimental.pallas import tpu as pltpu

D_MODEL = 2048
BATCH = 4
SEQ = 4096
DEPTH = 2

EPS = 1e-6
S5_WIDTH = D_MODEL // 4
S5_GROUP = 16
S5_GROUPS = S5_WIDTH // S5_GROUP
S5_STATE = 64
GDN_HEAD_DIM = 128
GDN_WIDTH = D_MODEL // 4
GDN_HEADS = GDN_WIDTH // GDN_HEAD_DIM
GDN_CONV = 4
GDN_CHUNK = 64
NSA_HEAD_DIM = 64
NSA_WIDTH = D_MODEL // 2
NSA_HEADS = NSA_WIDTH // NSA_HEAD_DIM
NSA_KV_HEADS = 4
NSA_KV_WIDTH = NSA_KV_HEADS * NSA_HEAD_DIM
CMP_LEN = 32
CMP_STRIDE = 16
CMP_HIDDEN = 256
SLC_BLOCK = 64
SLC_TOPK = 16
WINDOW = 512
Q_BLOCK = 128
ROPE_THETA = 10000.0
BIG = 1e9
N_GROUPS = 4
EXPERTS_PER_GROUP = 8
N_EXPERTS = N_GROUPS * EXPERTS_PER_GROUP
TOP_K = 2
D_EXPERT = 512
MOE_BLOCK = 128
IN_SIZES = (S5_WIDTH, 3 * GDN_WIDTH, GDN_WIDTH, GDN_HEADS, GDN_HEADS, NSA_WIDTH, 6 * NSA_KV_WIDTH, 3 * NSA_HEADS)
D_IN = sum(IN_SIZES)


def _matmul_body(a_ref, w_ref, o_ref):
    o_ref[...] = jnp.dot(a_ref[...].astype(jnp.bfloat16), w_ref[...].astype(jnp.bfloat16),
                         preferred_element_type=jnp.float32)


def _matmul(a, w, tm=512, tn=512):
    M, K = a.shape
    N = w.shape[1]
    n_pad = (-N) % tn
    if n_pad:
        w = jnp.pad(w, ((0, 0), (0, n_pad)))
    Np = N + n_pad
    out = pl.pallas_call(
        _matmul_body,
        grid=(Np // tn, M // tm),
        in_specs=[pl.BlockSpec((tm, K), lambda j, i: (i, 0)),
                  pl.BlockSpec((K, tn), lambda j, i: (0, j))],
        out_specs=pl.BlockSpec((tm, tn), lambda j, i: (i, j)),
        out_shape=jax.ShapeDtypeStruct((M, Np), jnp.float32),
        compiler_params=pltpu.CompilerParams(vmem_limit_bytes=48 * 1024 * 1024),
    )(a, w)
    return out[:, :N] if n_pad else out


def rms_norm(x, gain):
    xf = x.astype(jnp.float32)
    y = xf * lax.rsqrt(jnp.mean(xf * xf, axis=-1, keepdims=True) + EPS)
    return (y * gain.astype(jnp.float32)).astype(x.dtype)


def l2_norm(x):
    return x * lax.rsqrt(jnp.sum(x * x, axis=-1, keepdims=True) + EPS)


def rope_tables(pos, dim):
    inv_freq = ROPE_THETA ** (-jnp.arange(0, dim, 2, dtype=jnp.float32) / dim)
    ang = pos.astype(jnp.float32)[:, None] * inv_freq[None, :]
    return jnp.cos(ang)[:, None, :], jnp.sin(ang)[:, None, :]


def rope(x, cos, sin):
    x1, x2 = jnp.split(x, 2, axis=-1)
    return jnp.concatenate([x1 * cos - x2 * sin, x2 * cos + x1 * sin], axis=-1).astype(x.dtype)


def masked_softmax(s, mask):
    s = jnp.where(mask, s, -jnp.inf)
    m = jnp.max(s, axis=-1, keepdims=True)
    m = jnp.where(jnp.isfinite(m), m, 0.0)
    p = jnp.exp(s - m)
    return p / jnp.maximum(jnp.sum(p, axis=-1, keepdims=True), jnp.finfo(jnp.float32).tiny)


S5_OCT = 8
S5_NOCT = S5_GROUPS // S5_OCT
S5_OCT_IN = S5_OCT * S5_GROUP
S5_OCT_ST = S5_OCT * S5_STATE
SUBLANES = 8
S5_SCAN_SHIFTS = (1, 2, 4)


def _s5_constants(lam_re, lam_im, b_re, b_im, c_re, c_im, log_dt):
    f32 = jnp.float32
    lr, li = lam_re.astype(f32), lam_im.astype(f32)
    dt = jnp.exp(log_dt.astype(f32))[:, None]

    def lam_pow(k):
        mag = jnp.exp(k * lr * dt)
        return mag * jnp.cos(k * li * dt), mag * jnp.sin(k * li * dt)

    ar, ai = lam_pow(1.0)
    nr, ni = ar - 1.0, ai
    den = lr * lr + li * li
    qr, qi = (nr * lr + ni * li) / den, (ni * lr - nr * li) / den
    br, bi = b_re.astype(f32), b_im.astype(f32)
    bbr = qr[:, :, None] * br - qi[:, :, None] * bi
    bbi = qr[:, :, None] * bi + qi[:, :, None] * br
    eye = jnp.eye(S5_OCT, dtype=f32)

    def blk_b(t):
        t = t.reshape(S5_NOCT, S5_OCT, S5_STATE, S5_GROUP)
        return jnp.einsum('qgph,gk->qghkp', t, eye).reshape(S5_NOCT, S5_OCT_IN, S5_OCT_ST)

    def blk_c(t):
        t = t.reshape(S5_NOCT, S5_OCT, S5_GROUP, S5_STATE)
        return jnp.einsum('qghp,gk->qgpkh', t, eye).reshape(S5_NOCT, S5_OCT_ST, S5_OCT_IN)

    b_mat = jnp.stack([blk_b(bbr), blk_b(bbi)]).astype(jnp.bfloat16)
    c_mat = jnp.stack([blk_c(c_re.astype(f32)), blk_c(-c_im.astype(f32))]).astype(jnp.bfloat16)
    rows = jnp.arange(SUBLANES, dtype=f32)[:, None, None]
    tiles = []
    for k in S5_SCAN_SHIFTS:
        pr, pi = lam_pow(jnp.full_like(rows, float(k)))
        keep = rows >= k
        tiles.append(jnp.stack([jnp.where(keep, pr, 0.0), jnp.where(keep, pi, 0.0)]))
    pr, pi = lam_pow(rows + 1.0)
    tiles.append(jnp.stack([pr, pi]))
    a_mat = jnp.stack(tiles).reshape(len(tiles), 2, SUBLANES, S5_GROUPS * S5_STATE)
    return b_mat, c_mat, a_mat


def _s5_body(u_ref, b_ref, c_ref, a_ref, d_ref, gw_ref, gb_ref, gain_ref, o_ref, st_ref, xr_ref, xi_ref):
    bf16 = jnp.bfloat16
    n_tiles = u_ref.shape[0] // SUBLANES

    @pl.when(pl.program_id(1) == 0)
    def _():
        st_ref[...] = jnp.zeros_like(st_ref)

    u = u_ref[...]
    ys = []
    for q in range(S5_NOCT):
        lanes = slice(q * S5_OCT_ST, (q + 1) * S5_OCT_ST)
        uq = u[:, q * S5_OCT_IN:(q + 1) * S5_OCT_IN].astype(bf16)
        xr_ref[...] = jnp.dot(uq, b_ref[0, q], preferred_element_type=jnp.float32)
        xi_ref[...] = jnp.dot(uq, b_ref[1, q], preferred_element_type=jnp.float32)

        def tile_step(i, carry, lanes=lanes):
            cr, ci = carry
            r0 = pl.multiple_of(i * SUBLANES, SUBLANES)
            xr = xr_ref[pl.ds(r0, SUBLANES), :]
            xi = xi_ref[pl.ds(r0, SUBLANES), :]
            for lvl, k in enumerate(S5_SCAN_SHIFTS):
                pr, pi = a_ref[lvl, 0, :, lanes], a_ref[lvl, 1, :, lanes]
                sr, si = pltpu.roll(xr, k, 0), pltpu.roll(xi, k, 0)
                xr, xi = xr + (pr * sr - pi * si), xi + (pr * si + pi * sr)
            pr, pi = a_ref[len(S5_SCAN_SHIFTS), 0, :, lanes], a_ref[len(S5_SCAN_SHIFTS), 1, :, lanes]
            cr = jnp.broadcast_to(cr, xr.shape)
            ci = jnp.broadcast_to(ci, xi.shape)
            xr, xi = xr + (pr * cr - pi * ci), xi + (pr * ci + pi * cr)
            xr_ref[pl.ds(r0, SUBLANES), :] = xr
            xi_ref[pl.ds(r0, SUBLANES), :] = xi
            return xr[SUBLANES - 1:, :], xi[SUBLANES - 1:, :]

        cr, ci = lax.fori_loop(0, n_tiles, tile_step, (st_ref[0, :, lanes], st_ref[1, :, lanes]))
        st_ref[0, :, lanes] = cr
        st_ref[1, :, lanes] = ci
        ys.append(jnp.dot(xr_ref[...].astype(bf16), c_ref[0, q], preferred_element_type=jnp.float32)
                  + jnp.dot(xi_ref[...].astype(bf16), c_ref[1, q], preferred_element_type=jnp.float32))
    y = jnp.concatenate(ys, axis=1) + d_ref[...] * u
    y = jax.nn.gelu(y)
    z = jnp.dot(y.astype(bf16), gw_ref[...], preferred_element_type=jnp.float32) + gb_ref[...]
    y = y * jax.nn.sigmoid(z)
    y = y * lax.rsqrt(jnp.mean(y * y, axis=-1, keepdims=True) + EPS)
    o_ref[...] = y * gain_ref[...]


def s5_mixer_normed(u, lam_re, lam_im, b_re, b_im, c_re, c_im, d_skip, log_dt, glu_w, glu_b, out_norm, chunk=512):
    Bsz, S, W = u.shape
    f32 = jnp.float32
    L = min(chunk, S)
    b_mat, c_mat, a_mat = _s5_constants(lam_re, lam_im, b_re, b_im, c_re, c_im, log_dt)
    n_state = S5_GROUPS * S5_STATE
    full = lambda *shape: pl.BlockSpec(shape, lambda b, c: (0,) * len(shape))
    return pl.pallas_call(
        _s5_body,
        grid=(Bsz, S // L),
        in_specs=[pl.BlockSpec((None, L, W), lambda b, c: (b, c, 0)),
                  full(*b_mat.shape), full(*c_mat.shape), full(*a_mat.shape),
                  full(1, W), full(W, W), full(1, W), full(1, W)],
        out_specs=pl.BlockSpec((None, L, W), lambda b, c: (b, c, 0)),
        out_shape=jax.ShapeDtypeStruct((Bsz, S, W), f32),
        scratch_shapes=[pltpu.VMEM((2, 1, n_state), f32),
                        pltpu.VMEM((L, S5_OCT_ST), f32), pltpu.VMEM((L, S5_OCT_ST), f32)],
        compiler_params=pltpu.CompilerParams(dimension_semantics=("arbitrary", "arbitrary"),
                                             vmem_limit_bytes=48 * 1024 * 1024),
        name="s5_mixer",
    )(u.astype(f32), b_mat, c_mat, a_mat, d_skip.astype(f32).reshape(1, W), glu_w.astype(jnp.bfloat16),
      glu_b.astype(f32).reshape(1, W), out_norm.astype(f32).reshape(1, W))


GDN_ROWS = 256
GDN_PACK = GDN_HEADS * GDN_CHUNK
_HI = lax.Precision.HIGHEST


def _gdn_packed_unit_lower_inverse(a_mat, row, col, blk_diag):
    f32 = jnp.float32

    def mm(x, y):
        y_bd = jnp.where(blk_diag, jnp.concatenate([y] * GDN_HEADS, axis=0), 0.0)
        return jnp.dot(x, y_bd, precision=_HI, preferred_element_type=f32)

    same16 = (row >> 4) == (col >> 4)
    same32 = (row >> 5) == (col >> 5)
    eye = jnp.where(row == col, 1.0, 0.0)
    d1 = jnp.where(same16, a_mat, 0.0)
    d2 = mm(d1, d1)
    d4 = mm(d2, d2)
    d8 = mm(d4, d4)
    t = mm(mm(mm(eye - d1, eye + d2), eye + d4), eye + d8)
    for off in (jnp.where(same32 & jnp.logical_not(same16), a_mat, 0.0), jnp.where(same32, 0.0, a_mat)):
        t = t - mm(mm(t, off), t)
    return t


def _gdn_body(qkv_ref, z_ref, ab_ref, cw_ref, alog_ref, dtb_ref, nw_ref, o_ref, halo_ref, st_ref):
    f32, bf16 = jnp.float32, jnp.bfloat16
    L = qkv_ref.shape[0]
    C, H, dh, W = GDN_CHUNK, GDN_HEADS, GDN_HEAD_DIM, GDN_WIDTH

    @pl.when(pl.program_id(1) == 0)
    def _():
        halo_ref[...] = jnp.zeros_like(halo_ref)
        st_ref[...] = jnp.zeros_like(st_ref)

    x = qkv_ref[...]
    prev = halo_ref[...]
    row8 = lax.broadcasted_iota(jnp.int32, prev.shape, 0)
    acc = cw_ref[GDN_CONV - 1:GDN_CONV, :] * x
    for s in range(1, GDN_CONV):
        xs = pltpu.roll(x, s, 0)
        head = jnp.where(row8 < s, pltpu.roll(prev, s, 0), xs[0:SUBLANES])
        xs = jnp.concatenate([head, xs[SUBLANES:]], axis=0)
        acc = acc + cw_ref[GDN_CONV - 1 - s:GDN_CONV - s, :] * xs
    halo_ref[...] = x[L - SUBLANES:L]
    y = jax.nn.silu(acc)

    a_in, b_in = ab_ref[:, 0:H], ab_ref[:, H:2 * H]
    beta_all = jax.nn.sigmoid(b_in)
    g_all = -jnp.exp(alog_ref[...]) * jax.nn.softplus(a_in + dtb_ref[...])

    row = lax.broadcasted_iota(jnp.int32, (C, GDN_PACK), 0)
    col = lax.broadcasted_iota(jnp.int32, (C, GDN_PACK), 1) & (C - 1)
    incl, strict = row >= col, row > col
    eye_p = jnp.where(row == col, 1.0, 0.0)
    bd_r = lax.broadcasted_iota(jnp.int32, (GDN_PACK, GDN_PACK), 0)
    bd_c = lax.broadcasted_iota(jnp.int32, (GDN_PACK, GDN_PACK), 1)
    blk_diag = (bd_r >> 6) == (bd_c >> 6)
    tri = jnp.where(lax.broadcasted_iota(jnp.int32, (C, C), 0) >= lax.broadcasted_iota(jnp.int32, (C, C), 1), 1.0, 0.0)
    ones_c = jnp.ones((C, C), f32)
    head_of_lane = lax.broadcasted_iota(jnp.int32, (H, GDN_PACK), 1) >> 6
    spread = jnp.where(head_of_lane == lax.broadcasted_iota(jnp.int32, (H, GDN_PACK), 0), 1.0, 0.0)

    def l2n(t):
        return t * lax.rsqrt(jnp.sum(t * t, axis=-1, keepdims=True) + EPS)

    for c in range(L // C):
        rows = slice(c * C, (c + 1) * C)
        beta, g = beta_all[rows], g_all[rows]
        gc = jnp.dot(tri, g, precision=_HI, preferred_element_type=f32)
        gi = jnp.dot(gc, spread, precision=_HI, preferred_element_type=f32)
        gj = jnp.dot(ones_c, gi * eye_p, precision=_HI, preferred_element_type=f32)
        decay = jnp.exp(jnp.where(incl, gi - gj, -jnp.inf))
        egc = jnp.exp(gc)
        qs, ks, vs, kbs, kk, qk = [], [], [], [], [], []
        for h in range(H):
            q = l2n(y[rows, h * dh:(h + 1) * dh]) * dh ** -0.5
            k = l2n(y[rows, W + h * dh:W + (h + 1) * dh])
            v = y[rows, 2 * W + h * dh:2 * W + (h + 1) * dh]
            kb = k * beta[:, h:h + 1]
            qs.append(q), ks.append(k), vs.append(v), kbs.append(kb)
            kk.append(_dot_nt(kb.astype(bf16), k.astype(bf16)))
            qk.append(_dot_nt(q.astype(bf16), k.astype(bf16)))
        a_mat = jnp.where(strict, jnp.concatenate(kk, axis=1) * decay, 0.0)
        qk_p = jnp.concatenate(qk, axis=1) * decay
        t_p = _gdn_packed_unit_lower_inverse(a_mat, row, col, blk_diag)
        outs = []
        for h in range(H):
            t_h = t_p[:, h * C:(h + 1) * C].astype(bf16)
            e_h = egc[:, h:h + 1]
            u = jnp.dot(t_h, (vs[h] * beta[:, h:h + 1]).astype(bf16), preferred_element_type=f32)
            w = jnp.dot(t_h, (kbs[h] * e_h).astype(bf16), preferred_element_type=f32)
            state = st_ref[h]
            sb = state.astype(bf16)
            v_new = u - jnp.dot(w.astype(bf16), sb, preferred_element_type=f32)
            o = (jnp.dot((qs[h] * e_h).astype(bf16), sb, preferred_element_type=f32)
                 + jnp.dot(qk_p[:, h * C:(h + 1) * C].astype(bf16), v_new.astype(bf16), preferred_element_type=f32))
            g_last = gc[C - 1:C, h:h + 1]
            k_dec = (ks[h] * jnp.exp(g_last - gc[:, h:h + 1])).astype(bf16)
            st_ref[h] = state * jnp.exp(g_last) + lax.dot_general(
                k_dec, v_new.astype(bf16), (((0,), (0,)), ((), ())), preferred_element_type=f32)
            o = o * lax.rsqrt(jnp.mean(o * o, axis=-1, keepdims=True) + EPS) * nw_ref[...]
            outs.append(o * jax.nn.silu(z_ref[rows, h * dh:(h + 1) * dh]))
        o_ref[rows, :] = jnp.concatenate(outs, axis=1)


def gdn_mixer(qkv, z, a, b, conv_w, a_log, dt_bias, norm_w):
    Bsz, S, W3 = qkv.shape
    f32 = jnp.float32
    L = min(GDN_ROWS, S)
    H, W = GDN_HEADS, GDN_WIDTH
    ab = jnp.concatenate([a, b], axis=-1).astype(f32)
    full = lambda *shape: pl.BlockSpec(shape, lambda bi, ci: (0,) * len(shape))
    rows = lambda n: pl.BlockSpec((None, L, n), lambda bi, ci: (bi, ci, 0))
    return pl.pallas_call(
        _gdn_body,
        grid=(Bsz, S // L),
        in_specs=[rows(W3), rows(W), rows(2 * H), full(GDN_CONV, W3), full(1, H), full(1, H), full(1, GDN_HEAD_DIM)],
        out_specs=rows(W),
        out_shape=jax.ShapeDtypeStruct((Bsz, S, W), f32),
        scratch_shapes=[pltpu.VMEM((SUBLANES, W3), f32), pltpu.VMEM((H, GDN_HEAD_DIM, GDN_HEAD_DIM), f32)],
        compiler_params=pltpu.CompilerParams(dimension_semantics=("arbitrary", "arbitrary"),
                                             vmem_limit_bytes=48 * 1024 * 1024),
        name="gdn_mixer",
    )(qkv.astype(f32), z.astype(f32), ab, conv_w.astype(f32), a_log.astype(f32).reshape(1, H),
      dt_bias.astype(f32).reshape(1, H), norm_w.astype(f32).reshape(1, GDN_HEAD_DIM))


def nsa_mixer(q, kv, gate_logits, q_norm, k_norm, cmp_pe, cmp_w1, cmp_w2, cos, sin):
    Bsz, S, _ = q.shape
    f32 = jnp.float32
    G, R, dh = NSA_KV_HEADS, NSA_HEADS // NSA_KV_HEADS, NSA_HEAD_DIM
    scale = dh ** -0.5
    q = rope(rms_norm(q.reshape(Bsz, S, NSA_HEADS, dh), q_norm), cos, sin)
    kc_raw, vc_raw, ks, vs, kw, vw = (t.reshape(Bsz, S, G, dh) for t in jnp.split(kv, 6, axis=-1))
    ks = rope(rms_norm(ks, k_norm[1]), cos, sin)
    kw = rope(rms_norm(kw, k_norm[2]), cos, sin)

    n_cmp = (S - CMP_LEN) // CMP_STRIDE + 1
    win_idx = jnp.arange(n_cmp)[:, None] * CMP_STRIDE + jnp.arange(CMP_LEN)[None, :]
    cmp_end = win_idx[:, -1]

    def compress(t, pe, w1, w2):
        blocks = t[:, win_idx] + pe[:, None, :]
        blocks = jnp.moveaxis(blocks, 3, 2).reshape(Bsz, n_cmp, G, CMP_LEN * dh)
        return jax.nn.gelu(blocks @ w1) @ w2

    cos_c, sin_c = rope_tables(cmp_end, dh)
    k_cmp = rope(rms_norm(compress(kc_raw, cmp_pe[0], cmp_w1[0], cmp_w2[0]), k_norm[0]), cos_c, sin_c)
    v_cmp = compress(vc_raw, cmp_pe[1], cmp_w1[1], cmp_w2[1])

    n_slc = S // SLC_BLOCK
    n_top = min(SLC_TOPK, n_slc)
    c0 = jnp.arange(n_cmp) * CMP_STRIDE
    s0 = jnp.arange(n_slc) * SLC_BLOCK
    overlap = jnp.clip(jnp.minimum(c0[:, None] + CMP_LEN, s0[None, :] + SLC_BLOCK)
                       - jnp.maximum(c0[:, None], s0[None, :]), 0, None).astype(f32) / CMP_LEN
    gates = jax.nn.sigmoid(gate_logits.astype(f32)).reshape(Bsz, S, G, R * 3).transpose(0, 2, 1, 3)

    bf16 = jnp.bfloat16
    n_cmp_pad = S // CMP_STRIDE
    heads = lambda t: t.astype(bf16).transpose(0, 2, 1, 3)
    pad_cmp = lambda t: jnp.pad(heads(t), ((0, 0), (0, 0), (0, n_cmp_pad - n_cmp), (0, 0)))
    q_heads = (q * scale).astype(bf16).reshape(Bsz, S, G, R, dh).transpose(0, 2, 3, 1, 4)
    overlap_t = jnp.pad(overlap.T, ((0, 0), (0, n_cmp_pad - n_cmp))).astype(bf16)
    expand = (jnp.arange(n_slc)[:, None] == (jnp.arange(S) // SLC_BLOCK)[None, :]).astype(bf16)
    out = _nsa_attention(q_heads, pad_cmp(k_cmp), pad_cmp(v_cmp), heads(ks), heads(vs), heads(kw), heads(vw),
                         gates, overlap_t, expand, n_top)
    return out.transpose(0, 3, 1, 2, 4).reshape(Bsz, S, NSA_WIDTH)


NSA_R = NSA_HEADS // NSA_KV_HEADS
SLC_TILE = 512
MASK_NEG = -1e30


def _dot_nt(a, b):
    return lax.dot_general(a, b, (((1,), (1,)), ((), ())), preferred_element_type=jnp.float32)


def _nsa_body(q_ref, kc_ref, vc_ref, ks_ref, vs_ref, kw_ref, vw_ref, gate_ref, ovt_ref, exp_ref, o_ref, bias_ref,
              *, n_top):
    f32, bf16 = jnp.float32, jnp.bfloat16
    R, QB, dh = q_ref.shape
    S = ks_ref.shape[0]
    n_slc, n_cmp_pad = ovt_ref.shape
    M = R * QB
    t0 = pl.program_id(2) * QB
    q = q_ref[...].reshape(M, dh)
    t_rows = t0 + (lax.broadcasted_iota(jnp.int32, (M, 1), 0) & (QB - 1))

    s = _dot_nt(q, kc_ref[...])
    n_idx = lax.broadcasted_iota(jnp.int32, s.shape, 1)
    valid = (n_idx * CMP_STRIDE + (CMP_LEN - 1) <= t_rows) & (n_idx < n_cmp_pad - 1)
    s = jnp.where(valid, s, -jnp.inf)
    m = jnp.max(s, axis=-1, keepdims=True)
    m = jnp.where(jnp.isfinite(m), m, 0.0)
    p = jnp.exp(s - m)
    p = p / jnp.maximum(jnp.sum(p, axis=-1, keepdims=True), jnp.finfo(f32).tiny)
    o_cmp = jnp.dot(p.astype(bf16), vc_ref[...], preferred_element_type=f32)

    p_sum = p[0:QB]
    for r in range(1, R):
        p_sum = p_sum + p[r * QB:(r + 1) * QB]
    imp = jnp.zeros((n_slc, QB), f32)
    rem = p_sum
    for _ in range(3):
        part = rem.astype(bf16)
        imp = imp + _dot_nt(ovt_ref[...], part)
        rem = rem - part.astype(f32)
    j_idx = lax.broadcasted_iota(jnp.int32, (n_slc, QB), 0)
    t_lane = t0 + lax.broadcasted_iota(jnp.int32, (n_slc, QB), 1)
    cur = lax.shift_right_logical(t_lane, 6)
    forced = (j_idx == 0) | (j_idx == cur) | (j_idx == cur - 1)
    started = j_idx * SLC_BLOCK <= t_lane
    imp = jnp.where(forced, BIG, jnp.where(started, imp, -BIG))
    rank = jnp.zeros((n_slc, QB), f32)
    for jp in range(n_slc):
        row = imp[jp:jp + 1, :]
        ahead = (row > imp) | ((row == imp) & (j_idx > jp))
        rank = rank + jnp.where(ahead, 1.0, 0.0)
    sel_t = jnp.where(rank < n_top, 1.0, 0.0)
    sel = jnp.dot(sel_t.T.astype(bf16), exp_ref[...], preferred_element_type=f32)
    k_idx = lax.broadcasted_iota(jnp.int32, (QB, S), 1)
    t_q = t0 + lax.broadcasted_iota(jnp.int32, (QB, S), 0)
    bias_ref[...] = jnp.where((sel > 0.5) & (k_idx <= t_q), 0.0, MASK_NEG)

    def slc_step(kt, carry):
        m, l, acc = carry
        k0 = pl.multiple_of(kt * SLC_TILE, SLC_TILE)
        s = _dot_nt(q, ks_ref[pl.ds(k0, SLC_TILE), :])
        bias = bias_ref[:, pl.ds(k0, SLC_TILE)]
        s = s + jnp.concatenate([bias] * R, axis=0)
        m_new = jnp.maximum(m, jnp.max(s, axis=-1, keepdims=True))
        alpha = jnp.exp(m - m_new)
        p = jnp.exp(s - m_new)
        l = alpha * l + jnp.sum(p, axis=-1, keepdims=True)
        acc = alpha * acc + jnp.dot(p.astype(bf16), vs_ref[pl.ds(k0, SLC_TILE), :], preferred_element_type=f32)
        return m_new, l, acc

    n_tiles = (t0 + QB - 1) // SLC_TILE + 1
    init = (jnp.full((M, 1), MASK_NEG, f32), jnp.zeros((M, 1), f32), jnp.zeros((M, dh), f32))
    _, l, acc = lax.fori_loop(0, n_tiles, slc_step, init)
    o_slc = acc / l

    band = kw_ref.shape[0] if kw_ref.shape[0] < WINDOW + QB else WINDOW + QB
    start = pl.multiple_of(jnp.maximum(t0 + QB - band, 0), QB)
    s = _dot_nt(q, kw_ref[pl.ds(start, band), :])
    diff = t_rows - (start + lax.broadcasted_iota(jnp.int32, s.shape, 1))
    s = jnp.where((diff >= 0) & (diff < WINDOW), s, -jnp.inf)
    p = jnp.exp(s - jnp.max(s, axis=-1, keepdims=True))
    o_win = jnp.dot(p.astype(bf16), vw_ref[pl.ds(start, band), :], preferred_element_type=f32)
    o_win = o_win / jnp.sum(p, axis=-1, keepdims=True)

    gate = gate_ref[...]
    for r in range(R):
        rows = slice(r * QB, (r + 1) * QB)
        o_ref[r] = (gate[:, 3 * r:3 * r + 1] * o_cmp[rows] + gate[:, 3 * r + 1:3 * r + 2] * o_slc[rows]
                    + gate[:, 3 * r + 2:3 * r + 3] * o_win[rows])


def _nsa_attention(q, kc, vc, ks, vs, kw, vw, gates, overlap_t, expand, n_top):
    Bsz, G, R, S, dh = q.shape
    n_cmp_pad = kc.shape[2]
    n_slc = overlap_t.shape[0]
    per_head = lambda n: pl.BlockSpec((None, None, n, dh), lambda b, g, i: (b, g, 0, 0))
    const = lambda a: pl.BlockSpec(a.shape, lambda b, g, i: (0, 0))
    return pl.pallas_call(
        functools.partial(_nsa_body, n_top=n_top),
        grid=(Bsz, G, S // Q_BLOCK),
        in_specs=[pl.BlockSpec((None, None, R, Q_BLOCK, dh), lambda b, g, i: (b, g, 0, i, 0)),
                  per_head(n_cmp_pad), per_head(n_cmp_pad), per_head(S), per_head(S), per_head(S), per_head(S),
                  pl.BlockSpec((None, None, Q_BLOCK, 3 * R), lambda b, g, i: (b, g, i, 0)),
                  const(overlap_t), const(expand)],
        out_specs=pl.BlockSpec((None, None, R, Q_BLOCK, dh), lambda b, g, i: (b, g, 0, i, 0)),
        out_shape=jax.ShapeDtypeStruct((Bsz, G, R, S, dh), jnp.float32),
        scratch_shapes=[pltpu.VMEM((Q_BLOCK, S), jnp.float32)],
        compiler_params=pltpu.CompilerParams(dimension_semantics=("arbitrary", "arbitrary", "arbitrary"),
                                             vmem_limit_bytes=48 * 1024 * 1024),
        name="nsa_attention",
    )(q, kc, vc, ks, vs, kw, vw, gates, overlap_t, expand)


def hybrid_mixer(a, w_in, lam_re, lam_im, b_re, b_im, c_re, c_im, s5_d, s5_log_dt, glu_w, glu_b,
                 s5_out_norm, conv_w, a_log, dt_bias, gdn_norm_w, q_norm, k_norm, cmp_pe, cmp_w1,
                 cmp_w2, nsa_out_norm, w_out, cos, sin):
    proj = _matmul(a.reshape(-1, D_MODEL), w_in).reshape(a.shape[0], a.shape[1], D_IN)
    u, qkv, z, ga, gb, q, kv, gate_logits = jnp.split(proj, np.cumsum(IN_SIZES)[:-1].tolist(), axis=-1)
    y_s5 = s5_mixer_normed(u, lam_re, lam_im, b_re, b_im, c_re, c_im, s5_d, s5_log_dt, glu_w, glu_b, s5_out_norm)
    y_gdn = gdn_mixer(qkv, z, ga, gb, conv_w, a_log, dt_bias, gdn_norm_w)
    y_nsa = rms_norm(nsa_mixer(q, kv, gate_logits, q_norm, k_norm, cmp_pe, cmp_w1, cmp_w2, cos, sin), nsa_out_norm)
    cat = jnp.concatenate([y_s5, y_gdn.astype(y_s5.dtype), y_nsa.astype(y_s5.dtype)], axis=-1)
    return _matmul(cat.reshape(-1, D_MODEL), w_out).reshape(cat.shape)


def hier_moe(h, group_w, group_b, expert_w, expert_b, w_gate, w_up, w_down):
    Bsz, S, D = h.shape
    f32 = jnp.float32
    x = h.reshape(-1, D)
    T = x.shape[0]
    g_logits = (x @ group_w).astype(f32) + group_b.astype(f32)
    grp = jnp.argmax(g_logits, axis=-1)
    p_grp = jnp.take_along_axis(jax.nn.softmax(g_logits, axis=-1), grp[:, None], axis=-1)
    e_logits = ((x @ expert_w).astype(f32) + expert_b.astype(f32)).reshape(T, N_GROUPS, EXPERTS_PER_GROUP)
    e_logits = jnp.take_along_axis(e_logits, grp[:, None, None], axis=1)[:, 0]
    top_logit, top_local = lax.top_k(e_logits, TOP_K)
    weights = jax.nn.softmax(top_logit, axis=-1) * p_grp
    experts = grp[:, None] * EXPERTS_PER_GROUP + top_local
    n_assign = T * TOP_K
    e_flat = experts.reshape(-1)
    w_flat = weights.reshape(-1)
    tok_flat = jnp.repeat(jnp.arange(T, dtype=jnp.int32), TOP_K)
    order = jnp.argsort(e_flat)
    e_sorted = e_flat[order]
    counts = jnp.bincount(e_flat, length=N_EXPERTS)
    padded = (counts + MOE_ROWS - 1) // MOE_ROWS * MOE_ROWS
    pad_start = jnp.cumsum(padded) - padded
    start = jnp.cumsum(counts) - counts
    dest = (pad_start[e_sorted] + jnp.arange(n_assign) - start[e_sorted]).astype(jnp.int32)
    n_blk = (n_assign + MOE_ROWS - 1) // MOE_ROWS + N_EXPERTS
    cap = n_blk * MOE_ROWS
    slot_tok = jnp.full((cap,), T, jnp.int32).at[dest].set(tok_flat[order])
    slot_w = jnp.zeros((cap,), f32).at[dest].set(w_flat[order])
    blk_expert = jnp.minimum(jnp.searchsorted(jnp.cumsum(padded), jnp.arange(n_blk) * MOE_ROWS, side='right'),
                             N_EXPERTS - 1).astype(jnp.int32)
    n_used = (jnp.sum(padded) // MOE_ROWS).astype(jnp.int32).reshape(1)
    x_pad = jnp.concatenate([x, jnp.zeros((1, D), x.dtype)], axis=0).astype(jnp.bfloat16)
    y = _moe_ffn(blk_expert, n_used, x_pad[slot_tok], slot_w.reshape(cap, 1), w_gate.astype(f32),
                 w_up.astype(f32), w_down.astype(f32))
    slot_of = jnp.zeros((n_assign,), jnp.int32).at[order].set(dest).reshape(T, TOP_K)
    out = y[slot_of[:, 0]]
    for k in range(1, TOP_K):
        out = out + y[slot_of[:, k]]
    return out.reshape(Bsz, S, D).astype(h.dtype)


MOE_ROWS = 256


def _moe_ffn_body(be_ref, nu_ref, x_ref, w_ref, wg_ref, wu_ref, wd_ref, o_ref, wg_bf, wu_bf, wd_bf):
    i = pl.program_id(0)
    used = i < nu_ref[0]

    @pl.when((i == 0) | (be_ref[i] != be_ref[jnp.maximum(i - 1, 0)]))
    def _():
        wg_bf[...] = wg_ref[...].astype(jnp.bfloat16)
        wu_bf[...] = wu_ref[...].astype(jnp.bfloat16)
        wd_bf[...] = wd_ref[...].astype(jnp.bfloat16)

    @pl.when(used)
    def _():
        x = x_ref[...]
        gate = jnp.dot(x, wg_bf[...], preferred_element_type=jnp.float32)
        up = jnp.dot(x, wu_bf[...], preferred_element_type=jnp.float32)
        hid = (jax.nn.silu(gate) * up).astype(jnp.bfloat16)
        o_ref[...] = jnp.dot(hid, wd_bf[...], preferred_element_type=jnp.float32) * w_ref[...]

    @pl.when(jnp.logical_not(used))
    def _():
        o_ref[...] = jnp.zeros_like(o_ref)


def _moe_ffn(blk_expert, n_used, xs, ws, w_gate, w_up, w_down):
    cap, D = xs.shape
    n_blk = cap // MOE_ROWS
    De = w_gate.shape[-1]
    rows = lambda n: pl.BlockSpec((MOE_ROWS, n), lambda i, be, nu: (i, 0))
    return pl.pallas_call(
        _moe_ffn_body,
        grid_spec=pltpu.PrefetchScalarGridSpec(
            num_scalar_prefetch=2, grid=(n_blk,),
            in_specs=[rows(D), rows(1),
                      pl.BlockSpec((None, D, De), lambda i, be, nu: (be[i], 0, 0)),
                      pl.BlockSpec((None, D, De), lambda i, be, nu: (be[i], 0, 0)),
                      pl.BlockSpec((None, De, D), lambda i, be, nu: (be[i], 0, 0))],
            out_specs=rows(D),
            scratch_shapes=[pltpu.VMEM((D, De), jnp.bfloat16), pltpu.VMEM((D, De), jnp.bfloat16),
                            pltpu.VMEM((De, D), jnp.bfloat16)]),
        out_shape=jax.ShapeDtypeStruct((cap, D), jnp.float32),
        compiler_params=pltpu.CompilerParams(dimension_semantics=("arbitrary",),
                                             vmem_limit_bytes=56 * 1024 * 1024),
        name="moe_ffn",
    )(blk_expert, n_used, xs, ws, w_gate, w_up, w_down)


def kernel(x, ln_mix, w_in, s5_lambda_re, s5_lambda_im, s5_b_re, s5_b_im, s5_c_re, s5_c_im, s5_d,
           s5_log_dt, s5_glu_w, s5_glu_b, s5_out_norm, gdn_conv_w, gdn_a_log, gdn_dt_bias, gdn_norm_w,
           nsa_q_norm, nsa_k_norm, nsa_cmp_pe, nsa_cmp_w1, nsa_cmp_w2, nsa_out_norm, w_out, ln_ffn,
           moe_group_w, moe_group_b, moe_expert_w, moe_expert_b, moe_w_gate, moe_w_up, moe_w_down):
    S = x.shape[1]
    cos, sin = rope_tables(jnp.arange(S), NSA_HEAD_DIM)
    for l in range(DEPTH):
        y = hybrid_mixer(rms_norm(x, ln_mix[l]), w_in[l], s5_lambda_re[l], s5_lambda_im[l], s5_b_re[l],
                         s5_b_im[l], s5_c_re[l], s5_c_im[l], s5_d[l], s5_log_dt[l], s5_glu_w[l], s5_glu_b[l],
                         s5_out_norm[l], gdn_conv_w[l], gdn_a_log[l], gdn_dt_bias[l], gdn_norm_w[l],
                         nsa_q_norm[l], nsa_k_norm[l], nsa_cmp_pe[l], nsa_cmp_w1[l], nsa_cmp_w2[l],
                         nsa_out_norm[l], w_out[l], cos, sin)
        h = x + y.astype(x.dtype)
        x = h + hier_moe(rms_norm(h, ln_ffn[l]), moe_group_w[l], moe_group_b[l], moe_expert_w[l],
                         moe_expert_b[l], moe_w_gate[l], moe_w_up[l], moe_w_down[l])
    return x
```

```python
import functools
import math
import jax, jax.numpy as jnp
from jax import lax
import numpy as np
from jax.experimental import pallas as pl
from jax.experimental.pallas import tpu as pltpu

D_MODEL = 2048
BATCH = 4
SEQ = 4096
DEPTH = 2

EPS = 1e-6
S5_WIDTH = D_MODEL // 4
S5_GROUP = 16
S5_GROUPS = S5_WIDTH // S5_GROUP
S5_STATE = 64
GDN_HEAD_DIM = 128
GDN_WIDTH = D_MODEL // 4
GDN_HEADS = GDN_WIDTH // GDN_HEAD_DIM
GDN_CONV = 4
GDN_CHUNK = 64
NSA_HEAD_DIM = 64
NSA_WIDTH = D_MODEL // 2
NSA_HEADS = NSA_WIDTH // NSA_HEAD_DIM
NSA_KV_HEADS = 4
NSA_KV_WIDTH = NSA_KV_HEADS * NSA_HEAD_DIM
CMP_LEN = 32
CMP_STRIDE = 16
CMP_HIDDEN = 256
SLC_BLOCK = 64
SLC_TOPK = 16
WINDOW = 512
Q_BLOCK = 128
ROPE_THETA = 10000.0
BIG = 1e9
N_GROUPS = 4
EXPERTS_PER_GROUP = 8
N_EXPERTS = N_GROUPS * EXPERTS_PER_GROUP
TOP_K = 2
D_EXPERT = 512
MOE_BLOCK = 128
IN_SIZES = (S5_WIDTH, 3 * GDN_WIDTH, GDN_WIDTH, GDN_HEADS, GDN_HEADS, NSA_WIDTH, 6 * NSA_KV_WIDTH, 3 * NSA_HEADS)
D_IN = sum(IN_SIZES)


PROJ_QKV, PROJ_KV, PROJ_Q, PROJ_U, PROJ_Z, PROJ_SMALL = 0, 1536, 3072, 4096, 4608, 5120
PROJ_SMALL_W = 512
PROJ_W = PROJ_SMALL + PROJ_SMALL_W
PROJ_GATES = PROJ_SMALL + 2 * GDN_HEADS
LANES = 128
IN_ROWS, IN_COLS = 1024, 512
OUT_ROWS = 256
N_ROUTER = N_GROUPS + N_EXPERTS


def _in_proj_weight(w_in):
    u, qkv, z, ga, gb, q, kv, gl = jnp.split(w_in, np.cumsum(IN_SIZES)[:-1].tolist(), axis=-1)
    small = jnp.concatenate([ga, gb, gl], axis=-1)
    small = jnp.pad(small, ((0, 0), (0, PROJ_SMALL_W - small.shape[-1])))
    return jnp.concatenate([qkv, kv, q, u, z, small], axis=-1)


def _in_proj_body(x_ref, g_ref, w_ref, o_ref, a_ref):
    @pl.when(pl.program_id(1) == 0)
    def _():
        x = x_ref[...]
        a_ref[...] = (x * lax.rsqrt(jnp.mean(x * x, axis=-1, keepdims=True) + EPS) * g_ref[...]).astype(jnp.bfloat16)

    o_ref[...] = jnp.dot(a_ref[...], w_ref[...].astype(jnp.bfloat16), preferred_element_type=jnp.float32)


def _in_proj(x, gain, w):
    T, D = x.shape
    N = w.shape[1]
    tm = min(IN_ROWS, T)
    return pl.pallas_call(
        _in_proj_body,
        grid=(T // tm, N // IN_COLS),
        in_specs=[pl.BlockSpec((tm, D), lambda i, j: (i, 0)),
                  pl.BlockSpec((1, D), lambda i, j: (0, 0)),
                  pl.BlockSpec((D, IN_COLS), lambda i, j: (0, j))],
        out_specs=pl.BlockSpec((tm, IN_COLS), lambda i, j: (i, j)),
        out_shape=jax.ShapeDtypeStruct((T, N), jnp.float32),
        scratch_shapes=[pltpu.VMEM((tm, D), jnp.bfloat16)],
        compiler_params=pltpu.CompilerParams(dimension_semantics=("arbitrary", "arbitrary"),
                                             vmem_limit_bytes=48 * 1024 * 1024),
        name="in_proj",
    )(x, gain.astype(jnp.float32).reshape(1, D), w)


def _out_proj_body(x_ref, s5_ref, gdn_ref, nsa_ref, ng_ref, w_ref, fg_ref, rw_ref, rb_ref, h_ref, a_ref, lg_ref):
    f32, bf16 = jnp.float32, jnp.bfloat16
    nsa = nsa_ref[...]
    nsa = nsa * lax.rsqrt(jnp.mean(nsa * nsa, axis=-1, keepdims=True) + EPS) * ng_ref[...]
    k0, k1 = S5_WIDTH, S5_WIDTH + GDN_WIDTH
    y = (jnp.dot(s5_ref[...].astype(bf16), w_ref[0:k0, :], preferred_element_type=f32)
         + jnp.dot(gdn_ref[...].astype(bf16), w_ref[k0:k1, :], preferred_element_type=f32)
         + jnp.dot(nsa.astype(bf16), w_ref[k1:, :], preferred_element_type=f32))
    h = x_ref[...] + y
    h_ref[...] = h
    a = (h * lax.rsqrt(jnp.mean(h * h, axis=-1, keepdims=True) + EPS) * fg_ref[...]).astype(bf16)
    a_ref[...] = a
    lg_ref[...] = jnp.dot(a, rw_ref[...], preferred_element_type=f32) + rb_ref[...]


def _out_proj(x, y_s5, y_gdn, y_nsa, nsa_gain, w_out, ffn_gain, router_w, router_b):
    T, D = x.shape
    tm = min(OUT_ROWS, T)
    f32, bf16 = jnp.float32, jnp.bfloat16
    rw = jnp.pad(router_w.astype(bf16), ((0, 0), (0, LANES - N_ROUTER)))
    rb = jnp.pad(router_b.astype(f32), (0, LANES - N_ROUTER)).reshape(1, LANES)
    rows = lambda n: pl.BlockSpec((tm, n), lambda i: (i, 0))
    full = lambda *shape: pl.BlockSpec(shape, lambda i: (0,) * len(shape))
    return pl.pallas_call(
        _out_proj_body,
        grid=(T // tm,),
        in_specs=[rows(D), rows(S5_WIDTH), rows(GDN_WIDTH), rows(NSA_WIDTH), full(1, NSA_WIDTH), full(D, D),
                  full(1, D), full(D, LANES), full(1, LANES)],
        out_specs=[rows(D), rows(D), rows(LANES)],
        out_shape=[jax.ShapeDtypeStruct((T, D), f32), jax.ShapeDtypeStruct((T, D), bf16),
                   jax.ShapeDtypeStruct((T, LANES), f32)],
        compiler_params=pltpu.CompilerParams(dimension_semantics=("arbitrary",),
                                             vmem_limit_bytes=48 * 1024 * 1024),
        name="out_proj",
    )(x, y_s5, y_gdn, y_nsa, nsa_gain.astype(f32).reshape(1, NSA_WIDTH), w_out.astype(bf16),
      ffn_gain.astype(f32).reshape(1, D), rw, rb)


def rms_norm(x, gain):
    xf = x.astype(jnp.float32)
    y = xf * lax.rsqrt(jnp.mean(xf * xf, axis=-1, keepdims=True) + EPS)
    return (y * gain.astype(jnp.float32)).astype(x.dtype)


def l2_norm(x):
    return x * lax.rsqrt(jnp.sum(x * x, axis=-1, keepdims=True) + EPS)


def rope_tables(pos, dim):
    inv_freq = ROPE_THETA ** (-jnp.arange(0, dim, 2, dtype=jnp.float32) / dim)
    ang = pos.astype(jnp.float32)[:, None] * inv_freq[None, :]
    return jnp.cos(ang)[:, None, :], jnp.sin(ang)[:, None, :]


def rope(x, cos, sin):
    x1, x2 = jnp.split(x, 2, axis=-1)
    return jnp.concatenate([x1 * cos - x2 * sin, x2 * cos + x1 * sin], axis=-1).astype(x.dtype)


def masked_softmax(s, mask):
    s = jnp.where(mask, s, -jnp.inf)
    m = jnp.max(s, axis=-1, keepdims=True)
    m = jnp.where(jnp.isfinite(m), m, 0.0)
    p = jnp.exp(s - m)
    return p / jnp.maximum(jnp.sum(p, axis=-1, keepdims=True), jnp.finfo(jnp.float32).tiny)


S5_OCT = 8
S5_NOCT = S5_GROUPS // S5_OCT
S5_OCT_IN = S5_OCT * S5_GROUP
S5_OCT_ST = S5_OCT * S5_STATE
SUBLANES = 8
S5_SCAN_SHIFTS = (1, 2, 4)


def _s5_constants(lam_re, lam_im, b_re, b_im, c_re, c_im, log_dt):
    f32 = jnp.float32
    lr, li = lam_re.astype(f32), lam_im.astype(f32)
    dt = jnp.exp(log_dt.astype(f32))[:, None]

    def lam_pow(k):
        mag = jnp.exp(k * lr * dt)
        return mag * jnp.cos(k * li * dt), mag * jnp.sin(k * li * dt)

    ar, ai = lam_pow(1.0)
    nr, ni = ar - 1.0, ai
    den = lr * lr + li * li
    qr, qi = (nr * lr + ni * li) / den, (ni * lr - nr * li) / den
    br, bi = b_re.astype(f32), b_im.astype(f32)
    bbr = qr[:, :, None] * br - qi[:, :, None] * bi
    bbi = qr[:, :, None] * bi + qi[:, :, None] * br
    eye = jnp.eye(S5_OCT, dtype=f32)

    def blk_b(t):
        t = t.reshape(S5_NOCT, S5_OCT, S5_STATE, S5_GROUP)
        return jnp.einsum('qgph,gk->qghkp', t, eye).reshape(S5_NOCT, S5_OCT_IN, S5_OCT_ST)

    def blk_c(t):
        t = t.reshape(S5_NOCT, S5_OCT, S5_GROUP, S5_STATE)
        return jnp.einsum('qghp,gk->qgpkh', t, eye).reshape(S5_NOCT, S5_OCT_ST, S5_OCT_IN)

    b_mat = jnp.stack([blk_b(bbr), blk_b(bbi)]).astype(jnp.bfloat16)
    c_mat = jnp.stack([blk_c(c_re.astype(f32)), blk_c(-c_im.astype(f32))]).astype(jnp.bfloat16)
    rows = jnp.arange(SUBLANES, dtype=f32)[:, None, None]
    tiles = []
    for k in S5_SCAN_SHIFTS:
        pr, pi = lam_pow(jnp.full_like(rows, float(k)))
        keep = rows >= k
        tiles.append(jnp.stack([jnp.where(keep, pr, 0.0), jnp.where(keep, pi, 0.0)]))
    pr, pi = lam_pow(rows + 1.0)
    tiles.append(jnp.stack([pr, pi]))
    a_mat = jnp.stack(tiles).reshape(len(tiles), 2, SUBLANES, S5_GROUPS * S5_STATE)
    return b_mat, c_mat, a_mat


def _s5_body(u_ref, b_ref, c_ref, a_ref, d_ref, gw_ref, gb_ref, gain_ref, o_ref, st_ref, xr_ref, xi_ref):
    bf16 = jnp.bfloat16
    n_tiles = u_ref.shape[0] // SUBLANES

    @pl.when(pl.program_id(1) == 0)
    def _():
        st_ref[...] = jnp.zeros_like(st_ref)

    u = u_ref[...]
    ys = []
    for q in range(S5_NOCT):
        lanes = slice(q * S5_OCT_ST, (q + 1) * S5_OCT_ST)
        uq = u[:, q * S5_OCT_IN:(q + 1) * S5_OCT_IN].astype(bf16)
        xr_ref[...] = jnp.dot(uq, b_ref[0, q], preferred_element_type=jnp.float32)
        xi_ref[...] = jnp.dot(uq, b_ref[1, q], preferred_element_type=jnp.float32)

        def tile_step(i, carry, lanes=lanes):
            cr, ci = carry
            r0 = pl.multiple_of(i * SUBLANES, SUBLANES)
            xr = xr_ref[pl.ds(r0, SUBLANES), :]
            xi = xi_ref[pl.ds(r0, SUBLANES), :]
            for lvl, k in enumerate(S5_SCAN_SHIFTS):
                pr, pi = a_ref[lvl, 0, :, lanes], a_ref[lvl, 1, :, lanes]
                sr, si = pltpu.roll(xr, k, 0), pltpu.roll(xi, k, 0)
                xr, xi = xr + (pr * sr - pi * si), xi + (pr * si + pi * sr)
            pr, pi = a_ref[len(S5_SCAN_SHIFTS), 0, :, lanes], a_ref[len(S5_SCAN_SHIFTS), 1, :, lanes]
            cr = jnp.broadcast_to(cr, xr.shape)
            ci = jnp.broadcast_to(ci, xi.shape)
            xr, xi = xr + (pr * cr - pi * ci), xi + (pr * ci + pi * cr)
            xr_ref[pl.ds(r0, SUBLANES), :] = xr
            xi_ref[pl.ds(r0, SUBLANES), :] = xi
            return xr[SUBLANES - 1:, :], xi[SUBLANES - 1:, :]

        cr, ci = lax.fori_loop(0, n_tiles, tile_step, (st_ref[0, :, lanes], st_ref[1, :, lanes]))
        st_ref[0, :, lanes] = cr
        st_ref[1, :, lanes] = ci
        ys.append(jnp.dot(xr_ref[...].astype(bf16), c_ref[0, q], preferred_element_type=jnp.float32)
                  + jnp.dot(xi_ref[...].astype(bf16), c_ref[1, q], preferred_element_type=jnp.float32))
    y = jnp.concatenate(ys, axis=1) + d_ref[...] * u
    y = jax.nn.gelu(y)
    z = jnp.dot(y.astype(bf16), gw_ref[...], preferred_element_type=jnp.float32) + gb_ref[...]
    y = y * jax.nn.sigmoid(z)
    y = y * lax.rsqrt(jnp.mean(y * y, axis=-1, keepdims=True) + EPS)
    o_ref[...] = y * gain_ref[...]


def s5_mixer_normed(proj, lam_re, lam_im, b_re, b_im, c_re, c_im, d_skip, log_dt, glu_w, glu_b, out_norm, chunk=512):
    Bsz, S, _ = proj.shape
    W = S5_WIDTH
    f32 = jnp.float32
    L = min(chunk, S)
    b_mat, c_mat, a_mat = _s5_constants(lam_re, lam_im, b_re, b_im, c_re, c_im, log_dt)
    n_state = S5_GROUPS * S5_STATE
    full = lambda *shape: pl.BlockSpec(shape, lambda b, c: (0,) * len(shape))
    return pl.pallas_call(
        _s5_body,
        grid=(Bsz, S // L),
        in_specs=[pl.BlockSpec((None, L, W), lambda b, c: (b, c, PROJ_U // W)),
                  full(*b_mat.shape), full(*c_mat.shape), full(*a_mat.shape),
                  full(1, W), full(W, W), full(1, W), full(1, W)],
        out_specs=pl.BlockSpec((None, L, W), lambda b, c: (b, c, 0)),
        out_shape=jax.ShapeDtypeStruct((Bsz, S, W), f32),
        scratch_shapes=[pltpu.VMEM((2, 1, n_state), f32),
                        pltpu.VMEM((L, S5_OCT_ST), f32), pltpu.VMEM((L, S5_OCT_ST), f32)],
        compiler_params=pltpu.CompilerParams(dimension_semantics=("arbitrary", "arbitrary"),
                                             vmem_limit_bytes=48 * 1024 * 1024),
        name="s5_mixer",
    )(proj, b_mat, c_mat, a_mat, d_skip.astype(f32).reshape(1, W), glu_w.astype(jnp.bfloat16),
      glu_b.astype(f32).reshape(1, W), out_norm.astype(f32).reshape(1, W))


GDN_ROWS = 256
GDN_PACK = GDN_HEADS * GDN_CHUNK
_HI = lax.Precision.HIGHEST


def _gdn_packed_unit_lower_inverse(a_mat, row, col, blk_diag):
    f32 = jnp.float32

    def mm(x, y):
        y_bd = jnp.where(blk_diag, jnp.concatenate([y] * GDN_HEADS, axis=0), 0.0)
        return jnp.dot(x, y_bd, precision=_HI, preferred_element_type=f32)

    same16 = (row >> 4) == (col >> 4)
    same32 = (row >> 5) == (col >> 5)
    eye = jnp.where(row == col, 1.0, 0.0)
    d1 = jnp.where(same16, a_mat, 0.0)
    d2 = mm(d1, d1)
    d4 = mm(d2, d2)
    d8 = mm(d4, d4)
    t = mm(mm(mm(eye - d1, eye + d2), eye + d4), eye + d8)
    for off in (jnp.where(same32 & jnp.logical_not(same16), a_mat, 0.0), jnp.where(same32, 0.0, a_mat)):
        t = t - mm(mm(t, off), t)
    return t


def _gdn_body(qkv_ref, z_ref, ab_ref, cw_ref, alog_ref, dtb_ref, nw_ref, o_ref, halo_ref, st_ref):
    f32, bf16 = jnp.float32, jnp.bfloat16
    L = qkv_ref.shape[0]
    C, H, dh, W = GDN_CHUNK, GDN_HEADS, GDN_HEAD_DIM, GDN_WIDTH

    @pl.when(pl.program_id(1) == 0)
    def _():
        halo_ref[...] = jnp.zeros_like(halo_ref)
        st_ref[...] = jnp.zeros_like(st_ref)

    x = qkv_ref[...]
    prev = halo_ref[...]
    row8 = lax.broadcasted_iota(jnp.int32, prev.shape, 0)
    acc = cw_ref[GDN_CONV - 1:GDN_CONV, :] * x
    for s in range(1, GDN_CONV):
        xs = pltpu.roll(x, s, 0)
        head = jnp.where(row8 < s, pltpu.roll(prev, s, 0), xs[0:SUBLANES])
        xs = jnp.concatenate([head, xs[SUBLANES:]], axis=0)
        acc = acc + cw_ref[GDN_CONV - 1 - s:GDN_CONV - s, :] * xs
    halo_ref[...] = x[L - SUBLANES:L]
    y = jax.nn.silu(acc)

    a_in, b_in = ab_ref[:, 0:H], ab_ref[:, H:2 * H]
    beta_all = jax.nn.sigmoid(b_in)
    g_all = -jnp.exp(alog_ref[...]) * jax.nn.softplus(a_in + dtb_ref[...])

    row = lax.broadcasted_iota(jnp.int32, (C, GDN_PACK), 0)
    col = lax.broadcasted_iota(jnp.int32, (C, GDN_PACK), 1) & (C - 1)
    incl, strict = row >= col, row > col
    eye_p = jnp.where(row == col, 1.0, 0.0)
    bd_r = lax.broadcasted_iota(jnp.int32, (GDN_PACK, GDN_PACK), 0)
    bd_c = lax.broadcasted_iota(jnp.int32, (GDN_PACK, GDN_PACK), 1)
    blk_diag = (bd_r >> 6) == (bd_c >> 6)
    tri = jnp.where(lax.broadcasted_iota(jnp.int32, (C, C), 0) >= lax.broadcasted_iota(jnp.int32, (C, C), 1), 1.0, 0.0)
    ones_c = jnp.ones((C, C), f32)
    head_of_lane = lax.broadcasted_iota(jnp.int32, (H, GDN_PACK), 1) >> 6
    spread = jnp.where(head_of_lane == lax.broadcasted_iota(jnp.int32, (H, GDN_PACK), 0), 1.0, 0.0)

    def l2n(t):
        return t * lax.rsqrt(jnp.sum(t * t, axis=-1, keepdims=True) + EPS)

    for c in range(L // C):
        rows = slice(c * C, (c + 1) * C)
        beta, g = beta_all[rows], g_all[rows]
        gc = jnp.dot(tri, g, precision=_HI, preferred_element_type=f32)
        gi = jnp.dot(gc, spread, precision=_HI, preferred_element_type=f32)
        gj = jnp.dot(ones_c, gi * eye_p, precision=_HI, preferred_element_type=f32)
        decay = jnp.exp(jnp.where(incl, gi - gj, -jnp.inf))
        egc = jnp.exp(gc)
        qs, ks, vs, kbs, kk, qk = [], [], [], [], [], []
        for h in range(H):
            q = l2n(y[rows, h * dh:(h + 1) * dh]) * dh ** -0.5
            k = l2n(y[rows, W + h * dh:W + (h + 1) * dh])
            v = y[rows, 2 * W + h * dh:2 * W + (h + 1) * dh]
            kb = k * beta[:, h:h + 1]
            qs.append(q), ks.append(k), vs.append(v), kbs.append(kb)
            kk.append(_dot_nt(kb.astype(bf16), k.astype(bf16)))
            qk.append(_dot_nt(q.astype(bf16), k.astype(bf16)))
        a_mat = jnp.where(strict, jnp.concatenate(kk, axis=1) * decay, 0.0)
        qk_p = jnp.concatenate(qk, axis=1) * decay
        t_p = _gdn_packed_unit_lower_inverse(a_mat, row, col, blk_diag)
        outs = []
        for h in range(H):
            t_h = t_p[:, h * C:(h + 1) * C].astype(bf16)
            e_h = egc[:, h:h + 1]
            u = jnp.dot(t_h, (vs[h] * beta[:, h:h + 1]).astype(bf16), preferred_element_type=f32)
            w = jnp.dot(t_h, (kbs[h] * e_h).astype(bf16), preferred_element_type=f32)
            state = st_ref[h]
            sb = state.astype(bf16)
            v_new = u - jnp.dot(w.astype(bf16), sb, preferred_element_type=f32)
            o = (jnp.dot((qs[h] * e_h).astype(bf16), sb, preferred_element_type=f32)
                 + jnp.dot(qk_p[:, h * C:(h + 1) * C].astype(bf16), v_new.astype(bf16), preferred_element_type=f32))
            g_last = gc[C - 1:C, h:h + 1]
            k_dec = (ks[h] * jnp.exp(g_last - gc[:, h:h + 1])).astype(bf16)
            st_ref[h] = state * jnp.exp(g_last) + lax.dot_general(
                k_dec, v_new.astype(bf16), (((0,), (0,)), ((), ())), preferred_element_type=f32)
            o = o * lax.rsqrt(jnp.mean(o * o, axis=-1, keepdims=True) + EPS) * nw_ref[...]
            outs.append(o * jax.nn.silu(z_ref[rows, h * dh:(h + 1) * dh]))
        o_ref[rows, :] = jnp.concatenate(outs, axis=1)


def gdn_mixer(proj, conv_w, a_log, dt_bias, norm_w):
    Bsz, S, _ = proj.shape
    f32 = jnp.float32
    L = min(GDN_ROWS, S)
    H, W, W3 = GDN_HEADS, GDN_WIDTH, 3 * GDN_WIDTH
    full = lambda *shape: pl.BlockSpec(shape, lambda bi, ci: (0,) * len(shape))
    rows = lambda n, col=0: pl.BlockSpec((None, L, n), lambda bi, ci: (bi, ci, col // n))
    return pl.pallas_call(
        _gdn_body,
        grid=(Bsz, S // L),
        in_specs=[rows(W3, PROJ_QKV), rows(W, PROJ_Z), rows(LANES, PROJ_SMALL),
                  full(GDN_CONV, W3), full(1, H), full(1, H), full(1, GDN_HEAD_DIM)],
        out_specs=rows(W),
        out_shape=jax.ShapeDtypeStruct((Bsz, S, W), f32),
        scratch_shapes=[pltpu.VMEM((SUBLANES, W3), f32), pltpu.VMEM((H, GDN_HEAD_DIM, GDN_HEAD_DIM), f32)],
        compiler_params=pltpu.CompilerParams(dimension_semantics=("arbitrary", "arbitrary"),
                                             vmem_limit_bytes=48 * 1024 * 1024),
        name="gdn_mixer",
    )(proj, proj, proj, conv_w.astype(f32), a_log.astype(f32).reshape(1, H),
      dt_bias.astype(f32).reshape(1, H), norm_w.astype(f32).reshape(1, GDN_HEAD_DIM))


def nsa_mixer(q, kv, gate_logits, q_norm, k_norm, cmp_pe, cmp_w1, cmp_w2, cos, sin):
    Bsz, S, _ = q.shape
    f32 = jnp.float32
    G, R, dh = NSA_KV_HEADS, NSA_HEADS // NSA_KV_HEADS, NSA_HEAD_DIM
    scale = dh ** -0.5
    q = rope(rms_norm(q.reshape(Bsz, S, NSA_HEADS, dh), q_norm), cos, sin)
    kc_raw, vc_raw, ks, vs, kw, vw = (t.reshape(Bsz, S, G, dh) for t in jnp.split(kv, 6, axis=-1))
    ks = rope(rms_norm(ks, k_norm[1]), cos, sin)
    kw = rope(rms_norm(kw, k_norm[2]), cos, sin)

    n_cmp = (S - CMP_LEN) // CMP_STRIDE + 1
    win_idx = jnp.arange(n_cmp)[:, None] * CMP_STRIDE + jnp.arange(CMP_LEN)[None, :]
    cmp_end = win_idx[:, -1]

    def compress(t, pe, w1, w2):
        blocks = t[:, win_idx] + pe[:, None, :]
        blocks = jnp.moveaxis(blocks, 3, 2).reshape(Bsz, n_cmp, G, CMP_LEN * dh)
        return jax.nn.gelu(blocks @ w1) @ w2

    cos_c, sin_c = rope_tables(cmp_end, dh)
    k_cmp = rope(rms_norm(compress(kc_raw, cmp_pe[0], cmp_w1[0], cmp_w2[0]), k_norm[0]), cos_c, sin_c)
    v_cmp = compress(vc_raw, cmp_pe[1], cmp_w1[1], cmp_w2[1])

    n_slc = S // SLC_BLOCK
    n_top = min(SLC_TOPK, n_slc)
    c0 = jnp.arange(n_cmp) * CMP_STRIDE
    s0 = jnp.arange(n_slc) * SLC_BLOCK
    overlap = jnp.clip(jnp.minimum(c0[:, None] + CMP_LEN, s0[None, :] + SLC_BLOCK)
                       - jnp.maximum(c0[:, None], s0[None, :]), 0, None).astype(f32) / CMP_LEN
    gates = jax.nn.sigmoid(gate_logits.astype(f32)).reshape(Bsz, S, G, R * 3).transpose(0, 2, 1, 3)

    bf16 = jnp.bfloat16
    n_cmp_pad = S // CMP_STRIDE
    heads = lambda t: t.astype(bf16).transpose(0, 2, 1, 3)
    pad_cmp = lambda t: jnp.pad(heads(t), ((0, 0), (0, 0), (0, n_cmp_pad - n_cmp), (0, 0)))
    q_heads = (q * scale).astype(bf16).reshape(Bsz, S, G, R, dh).transpose(0, 2, 3, 1, 4)
    overlap_t = jnp.pad(overlap.T, ((0, 0), (0, n_cmp_pad - n_cmp))).astype(bf16)
    expand = (jnp.arange(n_slc)[:, None] == (jnp.arange(S) // SLC_BLOCK)[None, :]).astype(bf16)
    return _nsa_attention(q_heads, pad_cmp(k_cmp), pad_cmp(v_cmp), heads(ks), heads(vs), heads(kw), heads(vw),
                          gates, overlap_t, expand, n_top)


NSA_R = NSA_HEADS // NSA_KV_HEADS
SLC_TILE = 512
MASK_NEG = -1e30


def _dot_nt(a, b):
    return lax.dot_general(a, b, (((1,), (1,)), ((), ())), preferred_element_type=jnp.float32)


def _nsa_body(q_ref, kc_ref, vc_ref, ks_ref, vs_ref, kw_ref, vw_ref, gate_ref, ovt_ref, exp_ref, o_ref, bias_ref,
              *, n_top):
    f32, bf16 = jnp.float32, jnp.bfloat16
    R, QB, dh = q_ref.shape
    S = ks_ref.shape[0]
    n_slc, n_cmp_pad = ovt_ref.shape
    M = R * QB
    t0 = pl.program_id(2) * QB
    q = q_ref[...].reshape(M, dh)
    t_rows = t0 + (lax.broadcasted_iota(jnp.int32, (M, 1), 0) & (QB - 1))

    s = _dot_nt(q, kc_ref[...])
    n_idx = lax.broadcasted_iota(jnp.int32, s.shape, 1)
    valid = (n_idx * CMP_STRIDE + (CMP_LEN - 1) <= t_rows) & (n_idx < n_cmp_pad - 1)
    s = jnp.where(valid, s, -jnp.inf)
    m = jnp.max(s, axis=-1, keepdims=True)
    m = jnp.where(jnp.isfinite(m), m, 0.0)
    p = jnp.exp(s - m)
    p = p / jnp.maximum(jnp.sum(p, axis=-1, keepdims=True), jnp.finfo(f32).tiny)
    o_cmp = jnp.dot(p.astype(bf16), vc_ref[...], preferred_element_type=f32)

    p_sum = p[0:QB]
    for r in range(1, R):
        p_sum = p_sum + p[r * QB:(r + 1) * QB]
    imp = jnp.zeros((n_slc, QB), f32)
    rem = p_sum
    for _ in range(3):
        part = rem.astype(bf16)
        imp = imp + _dot_nt(ovt_ref[...], part)
        rem = rem - part.astype(f32)
    j_idx = lax.broadcasted_iota(jnp.int32, (n_slc, QB), 0)
    t_lane = t0 + lax.broadcasted_iota(jnp.int32, (n_slc, QB), 1)
    cur = lax.shift_right_logical(t_lane, 6)
    forced = (j_idx == 0) | (j_idx == cur) | (j_idx == cur - 1)
    started = j_idx * SLC_BLOCK <= t_lane
    imp = jnp.where(forced, BIG, jnp.where(started, imp, -BIG))
    rank = jnp.zeros((n_slc, QB), f32)
    for jp in range(n_slc):
        row = imp[jp:jp + 1, :]
        ahead = (row > imp) | ((row == imp) & (j_idx > jp))
        rank = rank + jnp.where(ahead, 1.0, 0.0)
    sel_t = jnp.where(rank < n_top, 1.0, 0.0)
    sel = jnp.dot(sel_t.T.astype(bf16), exp_ref[...], preferred_element_type=f32)
    k_idx = lax.broadcasted_iota(jnp.int32, (QB, S), 1)
    t_q = t0 + lax.broadcasted_iota(jnp.int32, (QB, S), 0)
    bias_ref[...] = jnp.where((sel > 0.5) & (k_idx <= t_q), 0.0, MASK_NEG)

    def slc_step(kt, carry):
        m, l, acc = carry
        k0 = pl.multiple_of(kt * SLC_TILE, SLC_TILE)
        s = _dot_nt(q, ks_ref[pl.ds(k0, SLC_TILE), :])
        bias = bias_ref[:, pl.ds(k0, SLC_TILE)]
        s = s + jnp.concatenate([bias] * R, axis=0)
        m_new = jnp.maximum(m, jnp.max(s, axis=-1, keepdims=True))
        alpha = jnp.exp(m - m_new)
        p = jnp.exp(s - m_new)
        l = alpha * l + jnp.sum(p, axis=-1, keepdims=True)
        acc = alpha * acc + jnp.dot(p.astype(bf16), vs_ref[pl.ds(k0, SLC_TILE), :], preferred_element_type=f32)
        return m_new, l, acc

    n_tiles = (t0 + QB - 1) // SLC_TILE + 1
    init = (jnp.full((M, 1), MASK_NEG, f32), jnp.zeros((M, 1), f32), jnp.zeros((M, dh), f32))
    _, l, acc = lax.fori_loop(0, n_tiles, slc_step, init)
    o_slc = acc / l

    band = kw_ref.shape[0] if kw_ref.shape[0] < WINDOW + QB else WINDOW + QB
    start = pl.multiple_of(jnp.maximum(t0 + QB - band, 0), QB)
    s = _dot_nt(q, kw_ref[pl.ds(start, band), :])
    diff = t_rows - (start + lax.broadcasted_iota(jnp.int32, s.shape, 1))
    s = jnp.where((diff >= 0) & (diff < WINDOW), s, -jnp.inf)
    p = jnp.exp(s - jnp.max(s, axis=-1, keepdims=True))
    o_win = jnp.dot(p.astype(bf16), vw_ref[pl.ds(start, band), :], preferred_element_type=f32)
    o_win = o_win / jnp.sum(p, axis=-1, keepdims=True)

    gate = gate_ref[...]
    outs = []
    for r in range(R):
        rows = slice(r * QB, (r + 1) * QB)
        outs.append(gate[:, 3 * r:3 * r + 1] * o_cmp[rows] + gate[:, 3 * r + 1:3 * r + 2] * o_slc[rows]
                    + gate[:, 3 * r + 2:3 * r + 3] * o_win[rows])
    o_ref[...] = jnp.concatenate(outs, axis=1)


def _nsa_attention(q, kc, vc, ks, vs, kw, vw, gates, overlap_t, expand, n_top):
    Bsz, G, R, S, dh = q.shape
    n_cmp_pad = kc.shape[2]
    n_slc = overlap_t.shape[0]
    per_head = lambda n: pl.BlockSpec((None, None, n, dh), lambda b, g, i: (b, g, 0, 0))
    const = lambda a: pl.BlockSpec(a.shape, lambda b, g, i: (0, 0))
    return pl.pallas_call(
        functools.partial(_nsa_body, n_top=n_top),
        grid=(Bsz, G, S // Q_BLOCK),
        in_specs=[pl.BlockSpec((None, None, R, Q_BLOCK, dh), lambda b, g, i: (b, g, 0, i, 0)),
                  per_head(n_cmp_pad), per_head(n_cmp_pad), per_head(S), per_head(S), per_head(S), per_head(S),
                  pl.BlockSpec((None, None, Q_BLOCK, 3 * R), lambda b, g, i: (b, g, i, 0)),
                  const(overlap_t), const(expand)],
        out_specs=pl.BlockSpec((None, Q_BLOCK, R * dh), lambda b, g, i: (b, i, g)),
        out_shape=jax.ShapeDtypeStruct((Bsz, S, G * R * dh), jnp.float32),
        scratch_shapes=[pltpu.VMEM((Q_BLOCK, S), jnp.float32)],
        compiler_params=pltpu.CompilerParams(dimension_semantics=("arbitrary", "arbitrary", "arbitrary"),
                                             vmem_limit_bytes=48 * 1024 * 1024),
        name="nsa_attention",
    )(q, kc, vc, ks, vs, kw, vw, gates, overlap_t, expand)


def hybrid_mixer(x, ln_mix, w_in, lam_re, lam_im, b_re, b_im, c_re, c_im, s5_d, s5_log_dt, glu_w, glu_b,
                 s5_out_norm, conv_w, a_log, dt_bias, gdn_norm_w, q_norm, k_norm, cmp_pe, cmp_w1,
                 cmp_w2, cos, sin):
    Bsz, S, D = x.shape
    proj = _in_proj(x.reshape(-1, D), ln_mix, _in_proj_weight(w_in)).reshape(Bsz, S, PROJ_W)
    y_s5 = s5_mixer_normed(proj, lam_re, lam_im, b_re, b_im, c_re, c_im, s5_d, s5_log_dt, glu_w, glu_b, s5_out_norm)
    y_gdn = gdn_mixer(proj, conv_w, a_log, dt_bias, gdn_norm_w)
    y_nsa = nsa_mixer(proj[..., PROJ_Q:PROJ_Q + NSA_WIDTH], proj[..., PROJ_KV:PROJ_KV + 6 * NSA_KV_WIDTH],
                      proj[..., PROJ_GATES:PROJ_GATES + 3 * NSA_HEADS], q_norm, k_norm, cmp_pe, cmp_w1, cmp_w2,
                      cos, sin)
    return y_s5, y_gdn, y_nsa


def hier_moe(x, logits, layer, w_gate, w_up, w_down):
    T, D = x.shape
    f32 = jnp.float32
    g_logits = logits[:, :N_GROUPS]
    grp = jnp.argmax(g_logits, axis=-1)
    p_grp = jnp.take_along_axis(jax.nn.softmax(g_logits, axis=-1), grp[:, None], axis=-1)
    e_logits = logits[:, N_GROUPS:N_ROUTER].reshape(T, N_GROUPS, EXPERTS_PER_GROUP)
    e_logits = jnp.take_along_axis(e_logits, grp[:, None, None], axis=1)[:, 0]
    top_logit, top_local = lax.top_k(e_logits, TOP_K)
    weights = jax.nn.softmax(top_logit, axis=-1) * p_grp
    experts = grp[:, None] * EXPERTS_PER_GROUP + top_local
    n_assign = T * TOP_K
    e_flat = experts.reshape(-1)
    w_flat = weights.reshape(-1)
    tok_flat = jnp.repeat(jnp.arange(T, dtype=jnp.int32), TOP_K)
    order = jnp.argsort(e_flat)
    e_sorted = e_flat[order]
    counts = jnp.bincount(e_flat, length=N_EXPERTS)
    padded = (counts + MOE_ROWS - 1) // MOE_ROWS * MOE_ROWS
    pad_start = jnp.cumsum(padded) - padded
    start = jnp.cumsum(counts) - counts
    dest = (pad_start[e_sorted] + jnp.arange(n_assign) - start[e_sorted]).astype(jnp.int32)
    n_blk = (n_assign + MOE_ROWS - 1) // MOE_ROWS + N_EXPERTS
    cap = n_blk * MOE_ROWS
    slot_tok = jnp.full((cap,), T, jnp.int32).at[dest].set(tok_flat[order])
    slot_w = jnp.zeros((cap,), f32).at[dest].set(w_flat[order])
    blk_expert = jnp.minimum(jnp.searchsorted(jnp.cumsum(padded), jnp.arange(n_blk) * MOE_ROWS, side='right'),
                             N_EXPERTS - 1).astype(jnp.int32)
    n_used = (jnp.sum(padded) // MOE_ROWS).astype(jnp.int32).reshape(1)
    x_pad = jnp.concatenate([x, jnp.zeros((1, D), x.dtype)], axis=0)
    y = _moe_ffn(blk_expert, n_used, x_pad[slot_tok], slot_w.reshape(cap, 1), layer, w_gate.astype(f32),
                 w_up.astype(f32), w_down.astype(f32))
    slot_of = jnp.zeros((n_assign,), jnp.int32).at[order].set(dest).reshape(T, TOP_K)
    out = y[slot_of[:, 0]]
    for k in range(1, TOP_K):
        out = out + y[slot_of[:, k]]
    return out


MOE_ROWS = 256


def _moe_ffn_body(be_ref, nu_ref, x_ref, w_ref, wg_ref, wu_ref, wd_ref, o_ref, wg_bf, wu_bf, wd_bf):
    i = pl.program_id(0)
    used = i < nu_ref[0]

    @pl.when((i == 0) | (be_ref[i] != be_ref[jnp.maximum(i - 1, 0)]))
    def _():
        wg_bf[...] = wg_ref[...].astype(jnp.bfloat16)
        wu_bf[...] = wu_ref[...].astype(jnp.bfloat16)
        wd_bf[...] = wd_ref[...].astype(jnp.bfloat16)

    @pl.when(used)
    def _():
        x = x_ref[...]
        gate = jnp.dot(x, wg_bf[...], preferred_element_type=jnp.float32)
        up = jnp.dot(x, wu_bf[...], preferred_element_type=jnp.float32)
        hid = (jax.nn.silu(gate) * up).astype(jnp.bfloat16)
        o_ref[...] = jnp.dot(hid, wd_bf[...], preferred_element_type=jnp.float32) * w_ref[...]

    @pl.when(jnp.logical_not(used))
    def _():
        o_ref[...] = jnp.zeros_like(o_ref)


def _moe_ffn(blk_expert, n_used, xs, ws, layer, w_gate, w_up, w_down):
    cap, D = xs.shape
    n_blk = cap // MOE_ROWS
    De = w_gate.shape[-1]
    rows = lambda n: pl.BlockSpec((MOE_ROWS, n), lambda i, be, nu: (i, 0))
    return pl.pallas_call(
        _moe_ffn_body,
        grid_spec=pltpu.PrefetchScalarGridSpec(
            num_scalar_prefetch=2, grid=(n_blk,),
            in_specs=[rows(D), rows(1),
                      pl.BlockSpec((None, None, D, De), lambda i, be, nu: (layer, be[i], 0, 0)),
                      pl.BlockSpec((None, None, D, De), lambda i, be, nu: (layer, be[i], 0, 0)),
                      pl.BlockSpec((None, None, De, D), lambda i, be, nu: (layer, be[i], 0, 0))],
            out_specs=rows(D),
            scratch_shapes=[pltpu.VMEM((D, De), jnp.bfloat16), pltpu.VMEM((D, De), jnp.bfloat16),
                            pltpu.VMEM((De, D), jnp.bfloat16)]),
        out_shape=jax.ShapeDtypeStruct((cap, D), jnp.float32),
        compiler_params=pltpu.CompilerParams(dimension_semantics=("arbitrary",),
                                             vmem_limit_bytes=56 * 1024 * 1024),
        name="moe_ffn",
    )(blk_expert, n_used, xs, ws, w_gate, w_up, w_down)


def kernel(x, ln_mix, w_in, s5_lambda_re, s5_lambda_im, s5_b_re, s5_b_im, s5_c_re, s5_c_im, s5_d,
           s5_log_dt, s5_glu_w, s5_glu_b, s5_out_norm, gdn_conv_w, gdn_a_log, gdn_dt_bias, gdn_norm_w,
           nsa_q_norm, nsa_k_norm, nsa_cmp_pe, nsa_cmp_w1, nsa_cmp_w2, nsa_out_norm, w_out, ln_ffn,
           moe_group_w, moe_group_b, moe_expert_w, moe_expert_b, moe_w_gate, moe_w_up, moe_w_down):
    Bsz, S, D = x.shape
    cos, sin = rope_tables(jnp.arange(S), NSA_HEAD_DIM)
    for l in range(DEPTH):
        y_s5, y_gdn, y_nsa = hybrid_mixer(
            x, ln_mix[l], w_in[l], s5_lambda_re[l], s5_lambda_im[l], s5_b_re[l], s5_b_im[l], s5_c_re[l], s5_c_im[l],
            s5_d[l], s5_log_dt[l], s5_glu_w[l], s5_glu_b[l], s5_out_norm[l], gdn_conv_w[l], gdn_a_log[l],
            gdn_dt_bias[l], gdn_norm_w[l], nsa_q_norm[l], nsa_k_norm[l], nsa_cmp_pe[l], nsa_cmp_w1[l], nsa_cmp_w2[l],
            cos, sin)
        router_w = jnp.concatenate([moe_group_w[l], moe_expert_w[l]], axis=-1)
        router_b = jnp.concatenate([moe_group_b[l], moe_expert_b[l]], axis=-1)
        h, a, logits = _out_proj(x.reshape(-1, D), y_s5.reshape(-1, S5_WIDTH), y_gdn.reshape(-1, GDN_WIDTH),
                                 y_nsa.reshape(-1, NSA_WIDTH), nsa_out_norm[l], w_out[l], ln_ffn[l], router_w, router_b)
        x = (h + hier_moe(a, logits, l, moe_w_gate, moe_w_up, moe_w_down)).reshape(Bsz, S, D)
    return x
```

```python
import functools
import math
import jax, jax.numpy as jnp
from jax import lax
import numpy as np
from jax.experimental import pallas as pl
from jax.experimental.pallas import tpu as pltpu

D_MODEL = 2048
BATCH = 4
SEQ = 4096
DEPTH = 2

EPS = 1e-6
S5_WIDTH = D_MODEL // 4
S5_GROUP = 16
S5_GROUPS = S5_WIDTH // S5_GROUP
S5_STATE = 64
GDN_HEAD_DIM = 128
GDN_WIDTH = D_MODEL // 4
GDN_HEADS = GDN_WIDTH // GDN_HEAD_DIM
GDN_CONV = 4
GDN_CHUNK = 64
NSA_HEAD_DIM = 64
NSA_WIDTH = D_MODEL // 2
NSA_HEADS = NSA_WIDTH // NSA_HEAD_DIM
NSA_KV_HEADS = 4
NSA_KV_WIDTH = NSA_KV_HEADS * NSA_HEAD_DIM
CMP_LEN = 32
CMP_STRIDE = 16
CMP_HIDDEN = 256
SLC_BLOCK = 64
SLC_TOPK = 16
WINDOW = 512
Q_BLOCK = 128
ROPE_THETA = 10000.0
BIG = 1e9
N_GROUPS = 4
EXPERTS_PER_GROUP = 8
N_EXPERTS = N_GROUPS * EXPERTS_PER_GROUP
TOP_K = 2
D_EXPERT = 512
MOE_BLOCK = 128
IN_SIZES = (S5_WIDTH, 3 * GDN_WIDTH, GDN_WIDTH, GDN_HEADS, GDN_HEADS, NSA_WIDTH, 6 * NSA_KV_WIDTH, 3 * NSA_HEADS)
D_IN = sum(IN_SIZES)


PROJ_QKV, PROJ_KV, PROJ_Q, PROJ_U, PROJ_Z, PROJ_SMALL = 0, 1536, 3072, 4096, 4608, 5120
PROJ_SMALL_W = 512
PROJ_W = PROJ_SMALL + PROJ_SMALL_W
PROJ_GATES = PROJ_SMALL + 2 * GDN_HEADS
LANES = 128
IN_ROWS, IN_COLS = 1024, 512
OUT_ROWS = 256
N_ROUTER = N_GROUPS + N_EXPERTS


def _in_proj_weight(w_in):
    u, qkv, z, ga, gb, q, kv, gl = jnp.split(w_in, np.cumsum(IN_SIZES)[:-1].tolist(), axis=-1)
    small = jnp.concatenate([ga, gb, gl], axis=-1)
    small = jnp.pad(small, ((0, 0), (0, PROJ_SMALL_W - small.shape[-1])))
    return jnp.concatenate([qkv, kv, q, u, z, small], axis=-1)


def _in_proj_body(x_ref, g_ref, w_ref, o_ref, a_ref):
    @pl.when(pl.program_id(1) == 0)
    def _():
        x = x_ref[...]
        a_ref[...] = (x * lax.rsqrt(jnp.mean(x * x, axis=-1, keepdims=True) + EPS) * g_ref[...]).astype(jnp.bfloat16)

    o_ref[...] = jnp.dot(a_ref[...], w_ref[...].astype(jnp.bfloat16), preferred_element_type=jnp.float32)


def _in_proj(x, gain, w):
    T, D = x.shape
    N = w.shape[1]
    tm = min(IN_ROWS, T)
    return pl.pallas_call(
        _in_proj_body,
        grid=(T // tm, N // IN_COLS),
        in_specs=[pl.BlockSpec((tm, D), lambda i, j: (i, 0)),
                  pl.BlockSpec((1, D), lambda i, j: (0, 0)),
                  pl.BlockSpec((D, IN_COLS), lambda i, j: (0, j))],
        out_specs=pl.BlockSpec((tm, IN_COLS), lambda i, j: (i, j)),
        out_shape=jax.ShapeDtypeStruct((T, N), jnp.float32),
        scratch_shapes=[pltpu.VMEM((tm, D), jnp.bfloat16)],
        compiler_params=pltpu.CompilerParams(dimension_semantics=("arbitrary", "arbitrary"),
                                             vmem_limit_bytes=48 * 1024 * 1024),
        name="in_proj",
    )(x, gain.astype(jnp.float32).reshape(1, D), w)


def _out_proj_body(x_ref, s5_ref, gdn_ref, nsa_ref, ng_ref, w_ref, fg_ref, rw_ref, rb_ref, h_ref, a_ref, lg_ref):
    f32, bf16 = jnp.float32, jnp.bfloat16
    nsa = nsa_ref[...]
    nsa = nsa * lax.rsqrt(jnp.mean(nsa * nsa, axis=-1, keepdims=True) + EPS) * ng_ref[...]
    k0, k1 = S5_WIDTH, S5_WIDTH + GDN_WIDTH
    y = (jnp.dot(s5_ref[...].astype(bf16), w_ref[0:k0, :], preferred_element_type=f32)
         + jnp.dot(gdn_ref[...].astype(bf16), w_ref[k0:k1, :], preferred_element_type=f32)
         + jnp.dot(nsa.astype(bf16), w_ref[k1:, :], preferred_element_type=f32))
    h = x_ref[...] + y
    h_ref[...] = h
    a = (h * lax.rsqrt(jnp.mean(h * h, axis=-1, keepdims=True) + EPS) * fg_ref[...]).astype(bf16)
    a_ref[...] = a
    lg_ref[...] = jnp.dot(a, rw_ref[...], preferred_element_type=f32) + rb_ref[...]


def _out_proj(x, y_s5, y_gdn, y_nsa, nsa_gain, w_out, ffn_gain, router_w, router_b):
    T, D = x.shape
    tm = min(OUT_ROWS, T)
    f32, bf16 = jnp.float32, jnp.bfloat16
    rw = jnp.pad(router_w.astype(bf16), ((0, 0), (0, LANES - N_ROUTER)))
    rb = jnp.pad(router_b.astype(f32), (0, LANES - N_ROUTER)).reshape(1, LANES)
    rows = lambda n: pl.BlockSpec((tm, n), lambda i: (i, 0))
    full = lambda *shape: pl.BlockSpec(shape, lambda i: (0,) * len(shape))
    return pl.pallas_call(
        _out_proj_body,
        grid=(T // tm,),
        in_specs=[rows(D), rows(S5_WIDTH), rows(GDN_WIDTH), rows(NSA_WIDTH), full(1, NSA_WIDTH), full(D, D),
                  full(1, D), full(D, LANES), full(1, LANES)],
        out_specs=[rows(D), rows(D), rows(LANES)],
        out_shape=[jax.ShapeDtypeStruct((T, D), f32), jax.ShapeDtypeStruct((T, D), bf16),
                   jax.ShapeDtypeStruct((T, LANES), f32)],
        compiler_params=pltpu.CompilerParams(dimension_semantics=("arbitrary",),
                                             vmem_limit_bytes=48 * 1024 * 1024),
        name="out_proj",
    )(x, y_s5, y_gdn, y_nsa, nsa_gain.astype(f32).reshape(1, NSA_WIDTH), w_out.astype(bf16),
      ffn_gain.astype(f32).reshape(1, D), rw, rb)


def rms_norm(x, gain):
    xf = x.astype(jnp.float32)
    y = xf * lax.rsqrt(jnp.mean(xf * xf, axis=-1, keepdims=True) + EPS)
    return (y * gain.astype(jnp.float32)).astype(x.dtype)


def l2_norm(x):
    return x * lax.rsqrt(jnp.sum(x * x, axis=-1, keepdims=True) + EPS)


def rope_tables(pos, dim):
    inv_freq = ROPE_THETA ** (-jnp.arange(0, dim, 2, dtype=jnp.float32) / dim)
    ang = pos.astype(jnp.float32)[:, None] * inv_freq[None, :]
    return jnp.cos(ang)[:, None, :], jnp.sin(ang)[:, None, :]


def rope(x, cos, sin):
    x1, x2 = jnp.split(x, 2, axis=-1)
    return jnp.concatenate([x1 * cos - x2 * sin, x2 * cos + x1 * sin], axis=-1).astype(x.dtype)


def masked_softmax(s, mask):
    s = jnp.where(mask, s, -jnp.inf)
    m = jnp.max(s, axis=-1, keepdims=True)
    m = jnp.where(jnp.isfinite(m), m, 0.0)
    p = jnp.exp(s - m)
    return p / jnp.maximum(jnp.sum(p, axis=-1, keepdims=True), jnp.finfo(jnp.float32).tiny)


S5_OCT = 8
S5_NOCT = S5_GROUPS // S5_OCT
S5_OCT_IN = S5_OCT * S5_GROUP
S5_OCT_ST = S5_OCT * S5_STATE
SUBLANES = 8
S5_SCAN_SHIFTS = (1, 2, 4)


def _s5_constants(lam_re, lam_im, b_re, b_im, c_re, c_im, log_dt):
    f32 = jnp.float32
    lr, li = lam_re.astype(f32), lam_im.astype(f32)
    dt = jnp.exp(log_dt.astype(f32))[:, None]

    def lam_pow(k):
        mag = jnp.exp(k * lr * dt)
        return mag * jnp.cos(k * li * dt), mag * jnp.sin(k * li * dt)

    ar, ai = lam_pow(1.0)
    nr, ni = ar - 1.0, ai
    den = lr * lr + li * li
    qr, qi = (nr * lr + ni * li) / den, (ni * lr - nr * li) / den
    br, bi = b_re.astype(f32), b_im.astype(f32)
    bbr = qr[:, :, None] * br - qi[:, :, None] * bi
    bbi = qr[:, :, None] * bi + qi[:, :, None] * br
    eye = jnp.eye(S5_OCT, dtype=f32)

    def blk_b(t):
        t = t.reshape(S5_NOCT, S5_OCT, S5_STATE, S5_GROUP)
        return jnp.einsum('qgph,gk->qghkp', t, eye).reshape(S5_NOCT, S5_OCT_IN, S5_OCT_ST)

    def blk_c(t):
        t = t.reshape(S5_NOCT, S5_OCT, S5_GROUP, S5_STATE)
        return jnp.einsum('qghp,gk->qgpkh', t, eye).reshape(S5_NOCT, S5_OCT_ST, S5_OCT_IN)

    b_mat = jnp.stack([blk_b(bbr), blk_b(bbi)]).astype(jnp.bfloat16)
    c_mat = jnp.stack([blk_c(c_re.astype(f32)), blk_c(-c_im.astype(f32))]).astype(jnp.bfloat16)
    rows = jnp.arange(SUBLANES, dtype=f32)[:, None, None]
    tiles = []
    for k in S5_SCAN_SHIFTS:
        pr, pi = lam_pow(jnp.full_like(rows, float(k)))
        keep = rows >= k
        tiles.append(jnp.stack([jnp.where(keep, pr, 0.0), jnp.where(keep, pi, 0.0)]))
    pr, pi = lam_pow(rows + 1.0)
    tiles.append(jnp.stack([pr, pi]))
    a_mat = jnp.stack(tiles).reshape(len(tiles), 2, SUBLANES, S5_GROUPS * S5_STATE)
    return b_mat, c_mat, a_mat


def _s5_body(u_ref, b_ref, c_ref, a_ref, d_ref, gw_ref, gb_ref, gain_ref, o_ref, st_ref, xr_ref, xi_ref):
    bf16 = jnp.bfloat16
    n_tiles = u_ref.shape[0] // SUBLANES

    @pl.when(pl.program_id(1) == 0)
    def _():
        st_ref[...] = jnp.zeros_like(st_ref)

    u = u_ref[...]
    ys = []
    for q in range(S5_NOCT):
        lanes = slice(q * S5_OCT_ST, (q + 1) * S5_OCT_ST)
        uq = u[:, q * S5_OCT_IN:(q + 1) * S5_OCT_IN].astype(bf16)
        xr_ref[...] = jnp.dot(uq, b_ref[0, q], preferred_element_type=jnp.float32)
        xi_ref[...] = jnp.dot(uq, b_ref[1, q], preferred_element_type=jnp.float32)

        def tile_step(i, carry, lanes=lanes):
            cr, ci = carry
            r0 = pl.multiple_of(i * SUBLANES, SUBLANES)
            xr = xr_ref[pl.ds(r0, SUBLANES), :]
            xi = xi_ref[pl.ds(r0, SUBLANES), :]
            for lvl, k in enumerate(S5_SCAN_SHIFTS):
                pr, pi = a_ref[lvl, 0, :, lanes], a_ref[lvl, 1, :, lanes]
                sr, si = pltpu.roll(xr, k, 0), pltpu.roll(xi, k, 0)
                xr, xi = xr + (pr * sr - pi * si), xi + (pr * si + pi * sr)
            pr, pi = a_ref[len(S5_SCAN_SHIFTS), 0, :, lanes], a_ref[len(S5_SCAN_SHIFTS), 1, :, lanes]
            cr = jnp.broadcast_to(cr, xr.shape)
            ci = jnp.broadcast_to(ci, xi.shape)
            xr, xi = xr + (pr * cr - pi * ci), xi + (pr * ci + pi * cr)
            xr_ref[pl.ds(r0, SUBLANES), :] = xr
            xi_ref[pl.ds(r0, SUBLANES), :] = xi
            return xr[SUBLANES - 1:, :], xi[SUBLANES - 1:, :]

        cr, ci = lax.fori_loop(0, n_tiles, tile_step, (st_ref[0, :, lanes], st_ref[1, :, lanes]))
        st_ref[0, :, lanes] = cr
        st_ref[1, :, lanes] = ci
        ys.append(jnp.dot(xr_ref[...].astype(bf16), c_ref[0, q], preferred_element_type=jnp.float32)
                  + jnp.dot(xi_ref[...].astype(bf16), c_ref[1, q], preferred_element_type=jnp.float32))
    y = jnp.concatenate(ys, axis=1) + d_ref[...] * u
    y = jax.nn.gelu(y)
    z = jnp.dot(y.astype(bf16), gw_ref[...], preferred_element_type=jnp.float32) + gb_ref[...]
    y = y * jax.nn.sigmoid(z)
    y = y * lax.rsqrt(jnp.mean(y * y, axis=-1, keepdims=True) + EPS)
    o_ref[...] = y * gain_ref[...]


def s5_mixer_normed(proj, lam_re, lam_im, b_re, b_im, c_re, c_im, d_skip, log_dt, glu_w, glu_b, out_norm, chunk=512):
    Bsz, S, _ = proj.shape
    W = S5_WIDTH
    f32 = jnp.float32
    L = min(chunk, S)
    b_mat, c_mat, a_mat = _s5_constants(lam_re, lam_im, b_re, b_im, c_re, c_im, log_dt)
    n_state = S5_GROUPS * S5_STATE
    full = lambda *shape: pl.BlockSpec(shape, lambda b, c: (0,) * len(shape))
    return pl.pallas_call(
        _s5_body,
        grid=(Bsz, S // L),
        in_specs=[pl.BlockSpec((None, L, W), lambda b, c: (b, c, PROJ_U // W)),
                  full(*b_mat.shape), full(*c_mat.shape), full(*a_mat.shape),
                  full(1, W), full(W, W), full(1, W), full(1, W)],
        out_specs=pl.BlockSpec((None, L, W), lambda b, c: (b, c, 0)),
        out_shape=jax.ShapeDtypeStruct((Bsz, S, W), f32),
        scratch_shapes=[pltpu.VMEM((2, 1, n_state), f32),
                        pltpu.VMEM((L, S5_OCT_ST), f32), pltpu.VMEM((L, S5_OCT_ST), f32)],
        compiler_params=pltpu.CompilerParams(dimension_semantics=("arbitrary", "arbitrary"),
                                             vmem_limit_bytes=48 * 1024 * 1024),
        name="s5_mixer",
    )(proj, b_mat, c_mat, a_mat, d_skip.astype(f32).reshape(1, W), glu_w.astype(jnp.bfloat16),
      glu_b.astype(f32).reshape(1, W), out_norm.astype(f32).reshape(1, W))


GDN_ROWS = 256
GDN_PACK = GDN_HEADS * GDN_CHUNK


def _bf16_terms(x, n):
    terms, rem = [], x
    for i in range(n):
        t = rem.astype(jnp.bfloat16)
        terms.append(t)
        if i + 1 < n:
            rem = rem - t.astype(jnp.float32)
    return terms


def _dot_exact_lhs(lhs01, x):
    lhs = lhs01.astype(jnp.bfloat16)
    return sum(jnp.dot(lhs, t, preferred_element_type=jnp.float32) for t in _bf16_terms(x, 3))


def _gdn_packed_unit_lower_inverse(a_mats, row, col, blk_diag):
    f32 = jnp.float32

    def mm(x, y):
        xh, xl = _bf16_terms(x, 2)
        yh, yl = (jnp.where(blk_diag, jnp.concatenate([t] * GDN_HEADS, axis=0), 0.0) for t in _bf16_terms(y, 2))
        return (jnp.dot(xh, yh, preferred_element_type=f32)
                + (jnp.dot(xh, yl, preferred_element_type=f32) + jnp.dot(xl, yh, preferred_element_type=f32)))

    same16 = (row >> 4) == (col >> 4)
    same32 = (row >> 5) == (col >> 5)
    eye = jnp.where(row == col, 1.0, 0.0)
    d1 = [jnp.where(same16, a, 0.0) for a in a_mats]
    d2 = [mm(d, d) for d in d1]
    d4 = [mm(d, d) for d in d2]
    d8 = [mm(d, d) for d in d4]
    lo = [mm(eye - a, eye + b) for a, b in zip(d1, d2)]
    hi = [mm(eye + a, eye + b) for a, b in zip(d4, d8)]
    t = [mm(a, b) for a, b in zip(lo, hi)]
    for pick in (lambda a: jnp.where(same32 & jnp.logical_not(same16), a, 0.0), lambda a: jnp.where(same32, 0.0, a)):
        mid = [mm(ti, pick(a)) for ti, a in zip(t, a_mats)]
        t = [ti - mm(m, ti) for ti, m in zip(t, mid)]
    return t


def _gdn_body(qkv_ref, z_ref, ab_ref, cw_ref, alog_ref, dtb_ref, nw_ref, o_ref, halo_ref, st_ref):
    f32, bf16 = jnp.float32, jnp.bfloat16
    L = qkv_ref.shape[0]
    C, H, dh, W = GDN_CHUNK, GDN_HEADS, GDN_HEAD_DIM, GDN_WIDTH

    @pl.when(pl.program_id(1) == 0)
    def _():
        halo_ref[...] = jnp.zeros_like(halo_ref)
        st_ref[...] = jnp.zeros_like(st_ref)

    x = qkv_ref[...]
    prev = halo_ref[...]
    row8 = lax.broadcasted_iota(jnp.int32, prev.shape, 0)
    acc = cw_ref[GDN_CONV - 1:GDN_CONV, :] * x
    for s in range(1, GDN_CONV):
        xs = pltpu.roll(x, s, 0)
        head = jnp.where(row8 < s, pltpu.roll(prev, s, 0), xs[0:SUBLANES])
        xs = jnp.concatenate([head, xs[SUBLANES:]], axis=0)
        acc = acc + cw_ref[GDN_CONV - 1 - s:GDN_CONV - s, :] * xs
    halo_ref[...] = x[L - SUBLANES:L]
    y = jax.nn.silu(acc)

    a_in, b_in = ab_ref[:, 0:H], ab_ref[:, H:2 * H]
    beta_all = jax.nn.sigmoid(b_in)
    g_all = -jnp.exp(alog_ref[...]) * jax.nn.softplus(a_in + dtb_ref[...])

    row = lax.broadcasted_iota(jnp.int32, (C, GDN_PACK), 0)
    col = lax.broadcasted_iota(jnp.int32, (C, GDN_PACK), 1) & (C - 1)
    strict = row > col
    bd_r = lax.broadcasted_iota(jnp.int32, (GDN_PACK, GDN_PACK), 0)
    bd_c = lax.broadcasted_iota(jnp.int32, (GDN_PACK, GDN_PACK), 1)
    blk_diag = (bd_r >> 6) == (bd_c >> 6)

    lr = lax.broadcasted_iota(jnp.int32, (L, L), 0)
    lc = lax.broadcasted_iota(jnp.int32, (L, L), 1)
    same_chunk = (lr >> 6) == (lc >> 6)
    tri = jnp.where(same_chunk & (lr >= lc), 1.0, 0.0)
    head_of_lane = lax.broadcasted_iota(jnp.int32, (H, GDN_PACK), 1) >> 6
    spread = jnp.where(head_of_lane == lax.broadcasted_iota(jnp.int32, (H, GDN_PACK), 0), 1.0, 0.0)
    pr = lax.broadcasted_iota(jnp.int32, (L, GDN_PACK), 0) & (C - 1)
    pc = lax.broadcasted_iota(jnp.int32, (L, GDN_PACK), 1) & (C - 1)
    gc_all = _dot_exact_lhs(tri, g_all)
    gi_all = sum(jnp.dot(t, spread.astype(bf16), preferred_element_type=f32) for t in _bf16_terms(gc_all, 3))
    gj_all = _dot_exact_lhs(jnp.where(same_chunk, 1.0, 0.0), jnp.where(pr == pc, gi_all, 0.0))
    decay_all = jnp.exp(jnp.where(pr >= pc, gi_all - gj_all, -jnp.inf))
    egc_all = jnp.exp(gc_all)

    def l2n(t):
        return t * lax.rsqrt(jnp.sum(t * t, axis=-1, keepdims=True) + EPS)

    n_chunks = L // C
    chunk_rows = [slice(c * C, (c + 1) * C) for c in range(n_chunks)]
    qs, ks, vs, kbs, a_mats, qk_ps = [], [], [], [], [], []
    for rows in chunk_rows:
        kk, qk = [], []
        for h in range(H):
            q = l2n(y[rows, h * dh:(h + 1) * dh]) * dh ** -0.5
            k = l2n(y[rows, W + h * dh:W + (h + 1) * dh])
            kb = k * beta_all[rows, h:h + 1]
            qs.append(q), ks.append(k), vs.append(y[rows, 2 * W + h * dh:2 * W + (h + 1) * dh]), kbs.append(kb)
            kk.append(_dot_nt(kb.astype(bf16), k.astype(bf16)))
            qk.append(_dot_nt(q.astype(bf16), k.astype(bf16)))
        a_mats.append(jnp.where(strict, jnp.concatenate(kk, axis=1) * decay_all[rows], 0.0))
        qk_ps.append(jnp.concatenate(qk, axis=1) * decay_all[rows])
    t_ps = _gdn_packed_unit_lower_inverse(a_mats, row, col, blk_diag)
    for c, rows in enumerate(chunk_rows):
        beta, gc, egc = beta_all[rows], gc_all[rows], egc_all[rows]
        outs = []
        for h in range(H):
            i = c * H + h
            t_h = t_ps[c][:, h * C:(h + 1) * C].astype(bf16)
            e_h = egc[:, h:h + 1]
            u = jnp.dot(t_h, (vs[i] * beta[:, h:h + 1]).astype(bf16), preferred_element_type=f32)
            w = jnp.dot(t_h, (kbs[i] * e_h).astype(bf16), preferred_element_type=f32)
            state = st_ref[h]
            sb = state.astype(bf16)
            v_new = u - jnp.dot(w.astype(bf16), sb, preferred_element_type=f32)
            o = (jnp.dot((qs[i] * e_h).astype(bf16), sb, preferred_element_type=f32)
                 + jnp.dot(qk_ps[c][:, h * C:(h + 1) * C].astype(bf16), v_new.astype(bf16),
                           preferred_element_type=f32))
            g_last = gc[C - 1:C, h:h + 1]
            k_dec = (ks[i] * jnp.exp(g_last - gc[:, h:h + 1])).astype(bf16)
            st_ref[h] = state * jnp.exp(g_last) + lax.dot_general(
                k_dec, v_new.astype(bf16), (((0,), (0,)), ((), ())), preferred_element_type=f32)
            o = o * lax.rsqrt(jnp.mean(o * o, axis=-1, keepdims=True) + EPS) * nw_ref[...]
            outs.append(o * jax.nn.silu(z_ref[rows, h * dh:(h + 1) * dh]))
        o_ref[rows, :] = jnp.concatenate(outs, axis=1)


def gdn_mixer(proj, conv_w, a_log, dt_bias, norm_w):
    Bsz, S, _ = proj.shape
    f32 = jnp.float32
    L = min(GDN_ROWS, S)
    H, W, W3 = GDN_HEADS, GDN_WIDTH, 3 * GDN_WIDTH
    full = lambda *shape: pl.BlockSpec(shape, lambda bi, ci: (0,) * len(shape))
    rows = lambda n, col=0: pl.BlockSpec((None, L, n), lambda bi, ci: (bi, ci, col // n))
    return pl.pallas_call(
        _gdn_body,
        grid=(Bsz, S // L),
        in_specs=[rows(W3, PROJ_QKV), rows(W, PROJ_Z), rows(LANES, PROJ_SMALL),
                  full(GDN_CONV, W3), full(1, H), full(1, H), full(1, GDN_HEAD_DIM)],
        out_specs=rows(W),
        out_shape=jax.ShapeDtypeStruct((Bsz, S, W), f32),
        scratch_shapes=[pltpu.VMEM((SUBLANES, W3), f32), pltpu.VMEM((H, GDN_HEAD_DIM, GDN_HEAD_DIM), f32)],
        compiler_params=pltpu.CompilerParams(dimension_semantics=("arbitrary", "arbitrary"),
                                             vmem_limit_bytes=48 * 1024 * 1024),
        name="gdn_mixer",
    )(proj, proj, proj, conv_w.astype(f32), a_log.astype(f32).reshape(1, H),
      dt_bias.astype(f32).reshape(1, H), norm_w.astype(f32).reshape(1, GDN_HEAD_DIM))


def nsa_mixer(q, kv, gate_logits, q_norm, k_norm, cmp_pe, cmp_w1, cmp_w2, cos, sin):
    Bsz, S, _ = q.shape
    f32 = jnp.float32
    G, R, dh = NSA_KV_HEADS, NSA_HEADS // NSA_KV_HEADS, NSA_HEAD_DIM
    scale = dh ** -0.5
    q = rope(rms_norm(q.reshape(Bsz, S, NSA_HEADS, dh), q_norm), cos, sin)
    kc_raw, vc_raw, ks, vs, kw, vw = (t.reshape(Bsz, S, G, dh) for t in jnp.split(kv, 6, axis=-1))
    ks = rope(rms_norm(ks, k_norm[1]), cos, sin)
    kw = rope(rms_norm(kw, k_norm[2]), cos, sin)

    n_cmp = (S - CMP_LEN) // CMP_STRIDE + 1
    win_idx = jnp.arange(n_cmp)[:, None] * CMP_STRIDE + jnp.arange(CMP_LEN)[None, :]
    cmp_end = win_idx[:, -1]

    def compress(t, pe, w1, w2):
        blocks = t[:, win_idx] + pe[:, None, :]
        blocks = jnp.moveaxis(blocks, 3, 2).reshape(Bsz, n_cmp, G, CMP_LEN * dh)
        return jax.nn.gelu(blocks @ w1) @ w2

    cos_c, sin_c = rope_tables(cmp_end, dh)
    k_cmp = rope(rms_norm(compress(kc_raw, cmp_pe[0], cmp_w1[0], cmp_w2[0]), k_norm[0]), cos_c, sin_c)
    v_cmp = compress(vc_raw, cmp_pe[1], cmp_w1[1], cmp_w2[1])

    n_slc = S // SLC_BLOCK
    n_top = min(SLC_TOPK, n_slc)
    c0 = jnp.arange(n_cmp) * CMP_STRIDE
    s0 = jnp.arange(n_slc) * SLC_BLOCK
    overlap = jnp.clip(jnp.minimum(c0[:, None] + CMP_LEN, s0[None, :] + SLC_BLOCK)
                       - jnp.maximum(c0[:, None], s0[None, :]), 0, None).astype(f32) / CMP_LEN
    gates = jax.nn.sigmoid(gate_logits.astype(f32)).reshape(Bsz, S, G, R * 3).transpose(0, 2, 3, 1)

    bf16 = jnp.bfloat16
    n_cmp_pad = S // CMP_STRIDE
    pad_cmp = lambda t: jnp.pad(t.astype(bf16), ((0, 0), (0, n_cmp_pad - n_cmp), (0, 0), (0, 0)))
    keys = lambda t: t.astype(bf16).transpose(0, 2, 1, 3)
    vals = lambda t: t.astype(bf16).transpose(0, 2, 3, 1)
    q_heads = (q * scale).astype(bf16).reshape(Bsz, S, G, R, dh).transpose(0, 2, 3, 1, 4)
    overlap_t = jnp.pad(overlap.T, ((0, 0), (0, n_cmp_pad - n_cmp))).astype(bf16)
    expand_t = ((jnp.arange(S) // SLC_BLOCK)[:, None] == jnp.arange(n_slc)[None, :]).astype(bf16)
    return _nsa_attention(q_heads, keys(pad_cmp(k_cmp)), vals(pad_cmp(v_cmp)), keys(ks), vals(vs), keys(kw), vals(vw),
                          gates, overlap_t, expand_t, n_top)


NSA_R = NSA_HEADS // NSA_KV_HEADS
SLC_TILE = 1024
MASK_NEG = -1e30


def _dot_nt(a, b):
    return lax.dot_general(a, b, (((1,), (1,)), ((), ())), preferred_element_type=jnp.float32)


def _nsa_body(q_ref, kc_ref, vct_ref, ks_ref, vst_ref, kw_ref, vwt_ref, gate_ref, ovt_ref, expt_ref, o_ref,
              *, n_top):
    f32, bf16 = jnp.float32, jnp.bfloat16
    R, QB, dh = q_ref.shape
    S = ks_ref.shape[0]
    n_slc, n_cmp_pad = ovt_ref.shape
    M = R * QB
    t0 = pl.program_id(2) * QB
    q = q_ref[...].reshape(M, dh)
    t_cols = t0 + (lax.broadcasted_iota(jnp.int32, (1, M), 1) & (QB - 1))

    band = S if S < WINDOW + QB else WINDOW + QB
    start = pl.multiple_of(jnp.maximum(t0 + QB - band, 0), QB)
    s = _dot_nt(kw_ref[pl.ds(start, band), :], q)
    diff = t_cols - (start + lax.broadcasted_iota(jnp.int32, s.shape, 0))
    s = jnp.where((diff >= 0) & (diff < WINDOW), s, -jnp.inf)
    p = jnp.exp(s - jnp.max(s, axis=0, keepdims=True))
    o_win = jnp.dot(vwt_ref[:, pl.ds(start, band)], p.astype(bf16), preferred_element_type=f32)
    o_win = o_win / jnp.sum(p, axis=0, keepdims=True)

    s = _dot_nt(kc_ref[...], q)
    n_idx = lax.broadcasted_iota(jnp.int32, s.shape, 0)
    valid = (n_idx * CMP_STRIDE + (CMP_LEN - 1) <= t_cols) & (n_idx < n_cmp_pad - 1)
    s = jnp.where(valid, s, -jnp.inf)
    m = jnp.max(s, axis=0, keepdims=True)
    m = jnp.where(jnp.isfinite(m), m, 0.0)
    p = jnp.exp(s - m)
    p = p / jnp.maximum(jnp.sum(p, axis=0, keepdims=True), jnp.finfo(f32).tiny)
    o_cmp = jnp.dot(vct_ref[...], p.astype(bf16), preferred_element_type=f32)

    p_sum = p[:, 0:QB]
    for r in range(1, R):
        p_sum = p_sum + p[:, r * QB:(r + 1) * QB]
    imp = sum(jnp.dot(ovt_ref[...], t, preferred_element_type=f32) for t in _bf16_terms(p_sum, 3))
    j_idx = lax.broadcasted_iota(jnp.int32, (n_slc, QB), 0)
    t_lane = t0 + lax.broadcasted_iota(jnp.int32, (n_slc, QB), 1)
    cur = lax.shift_right_logical(t_lane, 6)
    forced = (j_idx == 0) | (j_idx == cur) | (j_idx == cur - 1)
    started = j_idx * SLC_BLOCK <= t_lane
    imp = jnp.where(forced, BIG, jnp.where(started, imp, -BIG))
    rank = jnp.zeros((n_slc, QB), f32)
    for jp in range(n_slc):
        row = imp[jp:jp + 1, :]
        ahead = (row > imp) | ((row == imp) & (j_idx > jp))
        rank = rank + jnp.where(ahead, 1.0, 0.0)
    sel = jnp.where(rank < n_top, 1.0, 0.0).astype(bf16)
    k_rel = lax.broadcasted_iota(jnp.int32, (SLC_TILE, QB), 0)
    t_q = t0 + lax.broadcasted_iota(jnp.int32, (SLC_TILE, QB), 1)

    def slc_step(kt, carry):
        m, l, acc = carry
        k0 = pl.multiple_of(kt * SLC_TILE, SLC_TILE)
        s = _dot_nt(ks_ref[pl.ds(k0, SLC_TILE), :], q)
        sel_keys = jnp.dot(expt_ref[pl.ds(k0, SLC_TILE), :], sel, preferred_element_type=f32)
        bias = jnp.where((sel_keys > 0.5) & (k_rel + k0 <= t_q), 0.0, MASK_NEG)
        s = s + jnp.concatenate([bias] * R, axis=1)
        m_new = jnp.maximum(m, jnp.max(s, axis=0, keepdims=True))
        alpha = jnp.exp(m - m_new)
        p = jnp.exp(s - m_new)
        l = alpha * l + jnp.sum(p, axis=0, keepdims=True)
        acc = alpha * acc + jnp.dot(vst_ref[:, pl.ds(k0, SLC_TILE)], p.astype(bf16), preferred_element_type=f32)
        return m_new, l, acc

    n_tiles = (t0 + QB - 1) // SLC_TILE + 1
    init = (jnp.full((1, M), MASK_NEG, f32), jnp.zeros((1, M), f32), jnp.zeros((dh, M), f32))
    _, l, acc = lax.fori_loop(0, n_tiles, slc_step, init)
    o_slc = acc / l

    gate = gate_ref[...]
    g_cmp, g_slc, g_win = (jnp.concatenate([gate[3 * r + c:3 * r + c + 1, :] for r in range(R)], axis=1)
                           for c in range(3))
    o_t = g_cmp * o_cmp + g_slc * o_slc + g_win * o_win
    o_ref[...] = jnp.concatenate([o_t[:, r * QB:(r + 1) * QB].T for r in range(R)], axis=1)


def _nsa_attention(q, kc, vct, ks, vst, kw, vwt, gates, overlap_t, expand_t, n_top):
    Bsz, G, R, S, dh = q.shape
    n_cmp_pad = kc.shape[2]
    assert S % SLC_TILE == 0 and SLC_TILE % Q_BLOCK == 0, (S, SLC_TILE, Q_BLOCK)
    keys = lambda n: pl.BlockSpec((None, None, n, dh), lambda b, g, i: (b, g, 0, 0))
    vals = lambda n: pl.BlockSpec((None, None, dh, n), lambda b, g, i: (b, g, 0, 0))
    const = lambda a: pl.BlockSpec(a.shape, lambda b, g, i: (0, 0))
    return pl.pallas_call(
        functools.partial(_nsa_body, n_top=n_top),
        grid=(Bsz, G, S // Q_BLOCK),
        in_specs=[pl.BlockSpec((None, None, R, Q_BLOCK, dh), lambda b, g, i: (b, g, 0, i, 0)),
                  keys(n_cmp_pad), vals(n_cmp_pad), keys(S), vals(S), keys(S), vals(S),
                  pl.BlockSpec((None, None, 3 * R, Q_BLOCK), lambda b, g, i: (b, g, 0, i)),
                  const(overlap_t), const(expand_t)],
        out_specs=pl.BlockSpec((None, Q_BLOCK, R * dh), lambda b, g, i: (b, i, g)),
        out_shape=jax.ShapeDtypeStruct((Bsz, S, G * R * dh), jnp.float32),
        compiler_params=pltpu.CompilerParams(dimension_semantics=("arbitrary", "arbitrary", "arbitrary"),
                                             vmem_limit_bytes=48 * 1024 * 1024),
        name="nsa_attention",
    )(q, kc, vct, ks, vst, kw, vwt, gates, overlap_t, expand_t)


def hybrid_mixer(x, ln_mix, w_in, lam_re, lam_im, b_re, b_im, c_re, c_im, s5_d, s5_log_dt, glu_w, glu_b,
                 s5_out_norm, conv_w, a_log, dt_bias, gdn_norm_w, q_norm, k_norm, cmp_pe, cmp_w1,
                 cmp_w2, cos, sin):
    Bsz, S, D = x.shape
    proj = _in_proj(x.reshape(-1, D), ln_mix, _in_proj_weight(w_in)).reshape(Bsz, S, PROJ_W)
    y_s5 = s5_mixer_normed(proj, lam_re, lam_im, b_re, b_im, c_re, c_im, s5_d, s5_log_dt, glu_w, glu_b, s5_out_norm)
    y_gdn = gdn_mixer(proj, conv_w, a_log, dt_bias, gdn_norm_w)
    y_nsa = nsa_mixer(proj[..., PROJ_Q:PROJ_Q + NSA_WIDTH], proj[..., PROJ_KV:PROJ_KV + 6 * NSA_KV_WIDTH],
                      proj[..., PROJ_GATES:PROJ_GATES + 3 * NSA_HEADS], q_norm, k_norm, cmp_pe, cmp_w1, cmp_w2,
                      cos, sin)
    return y_s5, y_gdn, y_nsa


def hier_moe(x, logits, layer, w_gate, w_up, w_down):
    T, D = x.shape
    f32 = jnp.float32
    g_logits = logits[:, :N_GROUPS]
    grp = jnp.argmax(g_logits, axis=-1)
    p_grp = jnp.take_along_axis(jax.nn.softmax(g_logits, axis=-1), grp[:, None], axis=-1)
    e_logits = logits[:, N_GROUPS:N_ROUTER].reshape(T, N_GROUPS, EXPERTS_PER_GROUP)
    e_logits = jnp.take_along_axis(e_logits, grp[:, None, None], axis=1)[:, 0]
    top_logit, top_local = lax.top_k(e_logits, TOP_K)
    weights = jax.nn.softmax(top_logit, axis=-1) * p_grp
    experts = grp[:, None] * EXPERTS_PER_GROUP + top_local
    n_assign = T * TOP_K
    e_flat = experts.reshape(-1)
    w_flat = weights.reshape(-1)
    tok_flat = jnp.repeat(jnp.arange(T, dtype=jnp.int32), TOP_K)
    order = jnp.argsort(e_flat)
    e_sorted = e_flat[order]
    counts = jnp.bincount(e_flat, length=N_EXPERTS)
    padded = (counts + MOE_ROWS - 1) // MOE_ROWS * MOE_ROWS
    pad_start = jnp.cumsum(padded) - padded
    start = jnp.cumsum(counts) - counts
    dest = (pad_start[e_sorted] + jnp.arange(n_assign) - start[e_sorted]).astype(jnp.int32)
    n_blk = (n_assign + MOE_ROWS - 1) // MOE_ROWS + N_EXPERTS
    cap = n_blk * MOE_ROWS
    slot_tok = jnp.full((cap,), T, jnp.int32).at[dest].set(tok_flat[order])
    slot_w = jnp.zeros((cap,), f32).at[dest].set(w_flat[order])
    blk_expert = jnp.minimum(jnp.searchsorted(jnp.cumsum(padded), jnp.arange(n_blk) * MOE_ROWS, side='right'),
                             N_EXPERTS - 1).astype(jnp.int32)
    n_used = (jnp.sum(padded) // MOE_ROWS).astype(jnp.int32).reshape(1)
    x_pad = jnp.concatenate([x, jnp.zeros((1, D), x.dtype)], axis=0)
    y = _moe_ffn(blk_expert, n_used, x_pad[slot_tok], slot_w.reshape(cap, 1), layer, w_gate.astype(f32),
                 w_up.astype(f32), w_down.astype(f32))
    slot_of = jnp.zeros((n_assign,), jnp.int32).at[order].set(dest).reshape(T, TOP_K)
    out = y[slot_of[:, 0]]
    for k in range(1, TOP_K):
        out = out + y[slot_of[:, k]]
    return out


MOE_ROWS = 256


def _moe_ffn_body(be_ref, nu_ref, x_ref, w_ref, wg_ref, wu_ref, wd_ref, o_ref, wg_bf, wu_bf, wd_bf):
    i = pl.program_id(0)
    used = i < nu_ref[0]

    @pl.when((i == 0) | (be_ref[i] != be_ref[jnp.maximum(i - 1, 0)]))
    def _():
        wg_bf[...] = wg_ref[...].astype(jnp.bfloat16)
        wu_bf[...] = wu_ref[...].astype(jnp.bfloat16)
        wd_bf[...] = wd_ref[...].astype(jnp.bfloat16)

    @pl.when(used)
    def _():
        x = x_ref[...]
        gate = jnp.dot(x, wg_bf[...], preferred_element_type=jnp.float32)
        up = jnp.dot(x, wu_bf[...], preferred_element_type=jnp.float32)
        hid = (jax.nn.silu(gate) * up).astype(jnp.bfloat16)
        o_ref[...] = jnp.dot(hid, wd_bf[...], preferred_element_type=jnp.float32) * w_ref[...]

    @pl.when(jnp.logical_not(used))
    def _():
        o_ref[...] = jnp.zeros_like(o_ref)


def _moe_ffn(blk_expert, n_used, xs, ws, layer, w_gate, w_up, w_down):
    cap, D = xs.shape
    n_blk = cap // MOE_ROWS
    De = w_gate.shape[-1]
    rows = lambda n: pl.BlockSpec((MOE_ROWS, n), lambda i, be, nu: (i, 0))
    return pl.pallas_call(
        _moe_ffn_body,
        grid_spec=pltpu.PrefetchScalarGridSpec(
            num_scalar_prefetch=2, grid=(n_blk,),
            in_specs=[rows(D), rows(1),
                      pl.BlockSpec((None, None, D, De), lambda i, be, nu: (layer, be[i], 0, 0)),
                      pl.BlockSpec((None, None, D, De), lambda i, be, nu: (layer, be[i], 0, 0)),
                      pl.BlockSpec((None, None, De, D), lambda i, be, nu: (layer, be[i], 0, 0))],
            out_specs=rows(D),
            scratch_shapes=[pltpu.VMEM((D, De), jnp.bfloat16), pltpu.VMEM((D, De), jnp.bfloat16),
                            pltpu.VMEM((De, D), jnp.bfloat16)]),
        out_shape=jax.ShapeDtypeStruct((cap, D), jnp.float32),
        compiler_params=pltpu.CompilerParams(dimension_semantics=("arbitrary",),
                                             vmem_limit_bytes=56 * 1024 * 1024),
        name="moe_ffn",
    )(blk_expert, n_used, xs, ws, w_gate, w_up, w_down)


def kernel(x, ln_mix, w_in, s5_lambda_re, s5_lambda_im, s5_b_re, s5_b_im, s5_c_re, s5_c_im, s5_d,
           s5_log_dt, s5_glu_w, s5_glu_b, s5_out_norm, gdn_conv_w, gdn_a_log, gdn_dt_bias, gdn_norm_w,
           nsa_q_norm, nsa_k_norm, nsa_cmp_pe, nsa_cmp_w1, nsa_cmp_w2, nsa_out_norm, w_out, ln_ffn,
           moe_group_w, moe_group_b, moe_expert_w, moe_expert_b, moe_w_gate, moe_w_up, moe_w_down):
    Bsz, S, D = x.shape
    cos, sin = rope_tables(jnp.arange(S), NSA_HEAD_DIM)
    for l in range(DEPTH):
        y_s5, y_gdn, y_nsa = hybrid_mixer(
            x, ln_mix[l], w_in[l], s5_lambda_re[l], s5_lambda_im[l], s5_b_re[l], s5_b_im[l], s5_c_re[l], s5_c_im[l],
            s5_d[l], s5_log_dt[l], s5_glu_w[l], s5_glu_b[l], s5_out_norm[l], gdn_conv_w[l], gdn_a_log[l],
            gdn_dt_bias[l], gdn_norm_w[l], nsa_q_norm[l], nsa_k_norm[l], nsa_cmp_pe[l], nsa_cmp_w1[l], nsa_cmp_w2[l],
            cos, sin)
        router_w = jnp.concatenate([moe_group_w[l], moe_expert_w[l]], axis=-1)
        router_b = jnp.concatenate([moe_group_b[l], moe_expert_b[l]], axis=-1)
        h, a, logits = _out_proj(x.reshape(-1, D), y_s5.reshape(-1, S5_WIDTH), y_gdn.reshape(-1, GDN_WIDTH),
                                 y_nsa.reshape(-1, NSA_WIDTH), nsa_out_norm[l], w_out[l], ln_ffn[l], router_w, router_b)
        x = (h + hier_moe(a, logits, l, moe_w_gate, moe_w_up, moe_w_down)).reshape(Bsz, S, D)
    return x
```

```python
import functools
import math
import jax, jax.numpy as jnp
from jax import lax
import numpy as np
from jax.experimental import pallas as pl
from jax.experimental.pallas import tpu as pltpu

D_MODEL = 2048
BATCH = 4
SEQ = 4096
DEPTH = 2

EPS = 1e-6
S5_WIDTH = D_MODEL // 4
S5_GROUP = 16
S5_GROUPS = S5_WIDTH // S5_GROUP
S5_STATE = 64
GDN_HEAD_DIM = 128
GDN_WIDTH = D_MODEL // 4
GDN_HEADS = GDN_WIDTH // GDN_HEAD_DIM
GDN_CONV = 4
GDN_CHUNK = 64
NSA_HEAD_DIM = 64
NSA_WIDTH = D_MODEL // 2
NSA_HEADS = NSA_WIDTH // NSA_HEAD_DIM
NSA_KV_HEADS = 4
NSA_KV_WIDTH = NSA_KV_HEADS * NSA_HEAD_DIM
CMP_LEN = 32
CMP_STRIDE = 16
CMP_HIDDEN = 256
SLC_BLOCK = 64
SLC_TOPK = 16
WINDOW = 512
Q_BLOCK = 128
ROPE_THETA = 10000.0
BIG = 1e9
N_GROUPS = 4
EXPERTS_PER_GROUP = 8
N_EXPERTS = N_GROUPS * EXPERTS_PER_GROUP
TOP_K = 2
D_EXPERT = 512
MOE_BLOCK = 128
IN_SIZES = (S5_WIDTH, 3 * GDN_WIDTH, GDN_WIDTH, GDN_HEADS, GDN_HEADS, NSA_WIDTH, 6 * NSA_KV_WIDTH, 3 * NSA_HEADS)
D_IN = sum(IN_SIZES)


PROJ_QKV, PROJ_KV, PROJ_Q, PROJ_U, PROJ_Z, PROJ_SMALL = 0, 1536, 3072, 4096, 4608, 5120
PROJ_SMALL_W = 512
PROJ_W = PROJ_SMALL + PROJ_SMALL_W
PROJ_GATES = PROJ_SMALL + 2 * GDN_HEADS
LANES = 128
IN_ROWS, IN_COLS = 1024, 512
OUT_ROWS = 256
N_ROUTER = N_GROUPS + N_EXPERTS


def _in_proj_weight(w_in):
    u, qkv, z, ga, gb, q, kv, gl = jnp.split(w_in, np.cumsum(IN_SIZES)[:-1].tolist(), axis=-1)
    small = jnp.concatenate([ga, gb, gl], axis=-1)
    small = jnp.pad(small, ((0, 0), (0, PROJ_SMALL_W - small.shape[-1])))
    return jnp.concatenate([qkv, kv, q, u, z, small], axis=-1)


def _in_proj_body(x_ref, g_ref, w_ref, o_ref, a_ref):
    @pl.when(pl.program_id(1) == 0)
    def _():
        x = x_ref[...]
        a_ref[...] = (x * lax.rsqrt(jnp.mean(x * x, axis=-1, keepdims=True) + EPS) * g_ref[...]).astype(jnp.bfloat16)

    o_ref[...] = jnp.dot(a_ref[...], w_ref[...].astype(jnp.bfloat16), preferred_element_type=jnp.float32)


def _in_proj(x, gain, w):
    T, D = x.shape
    N = w.shape[1]
    tm = min(IN_ROWS, T)
    return pl.pallas_call(
        _in_proj_body,
        grid=(T // tm, N // IN_COLS),
        in_specs=[pl.BlockSpec((tm, D), lambda i, j: (i, 0)),
                  pl.BlockSpec((1, D), lambda i, j: (0, 0)),
                  pl.BlockSpec((D, IN_COLS), lambda i, j: (0, j))],
        out_specs=pl.BlockSpec((tm, IN_COLS), lambda i, j: (i, j)),
        out_shape=jax.ShapeDtypeStruct((T, N), jnp.float32),
        scratch_shapes=[pltpu.VMEM((tm, D), jnp.bfloat16)],
        compiler_params=pltpu.CompilerParams(dimension_semantics=("arbitrary", "arbitrary"),
                                             vmem_limit_bytes=48 * 1024 * 1024),
        name="in_proj",
    )(x, gain.astype(jnp.float32).reshape(1, D), w)


def _out_proj_body(x_ref, s5_ref, gdn_ref, nsa_ref, ng_ref, w_ref, fg_ref, rw_ref, rb_ref, h_ref, a_ref, lg_ref):
    f32, bf16 = jnp.float32, jnp.bfloat16
    nsa = nsa_ref[...]
    nsa = nsa * lax.rsqrt(jnp.mean(nsa * nsa, axis=-1, keepdims=True) + EPS) * ng_ref[...]
    k0, k1 = S5_WIDTH, S5_WIDTH + GDN_WIDTH
    y = (jnp.dot(s5_ref[...].astype(bf16), w_ref[0:k0, :], preferred_element_type=f32)
         + jnp.dot(gdn_ref[...].astype(bf16), w_ref[k0:k1, :], preferred_element_type=f32)
         + jnp.dot(nsa.astype(bf16), w_ref[k1:, :], preferred_element_type=f32))
    h = x_ref[...] + y
    h_ref[...] = h
    a = (h * lax.rsqrt(jnp.mean(h * h, axis=-1, keepdims=True) + EPS) * fg_ref[...]).astype(bf16)
    a_ref[...] = a
    lg_ref[...] = jnp.dot(a, rw_ref[...], preferred_element_type=f32) + rb_ref[...]


def _out_proj(x, y_s5, y_gdn, y_nsa, nsa_gain, w_out, ffn_gain, router_w, router_b):
    T, D = x.shape
    tm = min(OUT_ROWS, T)
    f32, bf16 = jnp.float32, jnp.bfloat16
    rw = jnp.pad(router_w.astype(bf16), ((0, 0), (0, LANES - N_ROUTER)))
    rb = jnp.pad(router_b.astype(f32), (0, LANES - N_ROUTER)).reshape(1, LANES)
    rows = lambda n: pl.BlockSpec((tm, n), lambda i: (i, 0))
    full = lambda *shape: pl.BlockSpec(shape, lambda i: (0,) * len(shape))
    return pl.pallas_call(
        _out_proj_body,
        grid=(T // tm,),
        in_specs=[rows(D), rows(S5_WIDTH), rows(GDN_WIDTH), rows(NSA_WIDTH), full(1, NSA_WIDTH), full(D, D),
                  full(1, D), full(D, LANES), full(1, LANES)],
        out_specs=[rows(D), rows(D), rows(LANES)],
        out_shape=[jax.ShapeDtypeStruct((T, D), f32), jax.ShapeDtypeStruct((T, D), bf16),
                   jax.ShapeDtypeStruct((T, LANES), f32)],
        compiler_params=pltpu.CompilerParams(dimension_semantics=("arbitrary",),
                                             vmem_limit_bytes=48 * 1024 * 1024),
        name="out_proj",
    )(x, y_s5, y_gdn, y_nsa, nsa_gain.astype(f32).reshape(1, NSA_WIDTH), w_out.astype(bf16),
      ffn_gain.astype(f32).reshape(1, D), rw, rb)


def rms_norm(x, gain):
    xf = x.astype(jnp.float32)
    y = xf * lax.rsqrt(jnp.mean(xf * xf, axis=-1, keepdims=True) + EPS)
    return (y * gain.astype(jnp.float32)).astype(x.dtype)


def l2_norm(x):
    return x * lax.rsqrt(jnp.sum(x * x, axis=-1, keepdims=True) + EPS)


def rope_tables(pos, dim):
    inv_freq = ROPE_THETA ** (-jnp.arange(0, dim, 2, dtype=jnp.float32) / dim)
    ang = pos.astype(jnp.float32)[:, None] * inv_freq[None, :]
    return jnp.cos(ang)[:, None, :], jnp.sin(ang)[:, None, :]


def rope(x, cos, sin):
    x1, x2 = jnp.split(x, 2, axis=-1)
    return jnp.concatenate([x1 * cos - x2 * sin, x2 * cos + x1 * sin], axis=-1).astype(x.dtype)


def masked_softmax(s, mask):
    s = jnp.where(mask, s, -jnp.inf)
    m = jnp.max(s, axis=-1, keepdims=True)
    m = jnp.where(jnp.isfinite(m), m, 0.0)
    p = jnp.exp(s - m)
    return p / jnp.maximum(jnp.sum(p, axis=-1, keepdims=True), jnp.finfo(jnp.float32).tiny)


S5_OCT = 8
S5_NOCT = S5_GROUPS // S5_OCT
S5_OCT_IN = S5_OCT * S5_GROUP
S5_OCT_ST = S5_OCT * S5_STATE
SUBLANES = 8
S5_SCAN_SHIFTS = (1, 2, 4)


def _s5_constants(lam_re, lam_im, b_re, b_im, c_re, c_im, log_dt):
    f32 = jnp.float32
    lr, li = lam_re.astype(f32), lam_im.astype(f32)
    dt = jnp.exp(log_dt.astype(f32))[:, None]

    def lam_pow(k):
        mag = jnp.exp(k * lr * dt)
        return mag * jnp.cos(k * li * dt), mag * jnp.sin(k * li * dt)

    ar, ai = lam_pow(1.0)
    nr, ni = ar - 1.0, ai
    den = lr * lr + li * li
    qr, qi = (nr * lr + ni * li) / den, (ni * lr - nr * li) / den
    br, bi = b_re.astype(f32), b_im.astype(f32)
    bbr = qr[:, :, None] * br - qi[:, :, None] * bi
    bbi = qr[:, :, None] * bi + qi[:, :, None] * br
    eye = jnp.eye(S5_OCT, dtype=f32)

    def blk_b(t):
        t = t.reshape(S5_NOCT, S5_OCT, S5_STATE, S5_GROUP)
        return jnp.einsum('qgph,gk->qghkp', t, eye).reshape(S5_NOCT, S5_OCT_IN, S5_OCT_ST)

    def blk_c(t):
        t = t.reshape(S5_NOCT, S5_OCT, S5_GROUP, S5_STATE)
        return jnp.einsum('qghp,gk->qgpkh', t, eye).reshape(S5_NOCT, S5_OCT_ST, S5_OCT_IN)

    b_mat = jnp.stack([blk_b(bbr), blk_b(bbi)]).astype(jnp.bfloat16)
    c_mat = jnp.stack([blk_c(c_re.astype(f32)), blk_c(-c_im.astype(f32))]).astype(jnp.bfloat16)
    rows = jnp.arange(SUBLANES, dtype=f32)[:, None, None]
    tiles = []
    for k in S5_SCAN_SHIFTS:
        pr, pi = lam_pow(jnp.full_like(rows, float(k)))
        keep = rows >= k
        tiles.append(jnp.stack([jnp.where(keep, pr, 0.0), jnp.where(keep, pi, 0.0)]))
    pr, pi = lam_pow(rows + 1.0)
    tiles.append(jnp.stack([pr, pi]))
    a_mat = jnp.stack(tiles).reshape(len(tiles), 2, SUBLANES, S5_GROUPS * S5_STATE)
    return b_mat, c_mat, a_mat


def _s5_body(u_ref, b_ref, c_ref, a_ref, d_ref, gw_ref, gb_ref, gain_ref, o_ref, st_ref, xr_ref, xi_ref):
    bf16 = jnp.bfloat16
    n_tiles = u_ref.shape[0] // SUBLANES

    @pl.when(pl.program_id(1) == 0)
    def _():
        st_ref[...] = jnp.zeros_like(st_ref)

    u = u_ref[...]
    ys = []
    for q in range(S5_NOCT):
        lanes = slice(q * S5_OCT_ST, (q + 1) * S5_OCT_ST)
        uq = u[:, q * S5_OCT_IN:(q + 1) * S5_OCT_IN].astype(bf16)
        xr_ref[...] = jnp.dot(uq, b_ref[0, q], preferred_element_type=jnp.float32)
        xi_ref[...] = jnp.dot(uq, b_ref[1, q], preferred_element_type=jnp.float32)

        def tile_step(i, carry, lanes=lanes):
            cr, ci = carry
            r0 = pl.multiple_of(i * SUBLANES, SUBLANES)
            xr = xr_ref[pl.ds(r0, SUBLANES), :]
            xi = xi_ref[pl.ds(r0, SUBLANES), :]
            for lvl, k in enumerate(S5_SCAN_SHIFTS):
                pr, pi = a_ref[lvl, 0, :, lanes], a_ref[lvl, 1, :, lanes]
                sr, si = pltpu.roll(xr, k, 0), pltpu.roll(xi, k, 0)
                xr, xi = xr + (pr * sr - pi * si), xi + (pr * si + pi * sr)
            pr, pi = a_ref[len(S5_SCAN_SHIFTS), 0, :, lanes], a_ref[len(S5_SCAN_SHIFTS), 1, :, lanes]
            cr = jnp.broadcast_to(cr, xr.shape)
            ci = jnp.broadcast_to(ci, xi.shape)
            xr, xi = xr + (pr * cr - pi * ci), xi + (pr * ci + pi * cr)
            xr_ref[pl.ds(r0, SUBLANES), :] = xr
            xi_ref[pl.ds(r0, SUBLANES), :] = xi
            return xr[SUBLANES - 1:, :], xi[SUBLANES - 1:, :]

        cr, ci = lax.fori_loop(0, n_tiles, tile_step, (st_ref[0, :, lanes], st_ref[1, :, lanes]))
        st_ref[0, :, lanes] = cr
        st_ref[1, :, lanes] = ci
        ys.append(jnp.dot(xr_ref[...].astype(bf16), c_ref[0, q], preferred_element_type=jnp.float32)
                  + jnp.dot(xi_ref[...].astype(bf16), c_ref[1, q], preferred_element_type=jnp.float32))
    y = jnp.concatenate(ys, axis=1) + d_ref[...] * u
    y = jax.nn.gelu(y)
    z = jnp.dot(y.astype(bf16), gw_ref[...], preferred_element_type=jnp.float32) + gb_ref[...]
    y = y * jax.nn.sigmoid(z)
    y = y * lax.rsqrt(jnp.mean(y * y, axis=-1, keepdims=True) + EPS)
    o_ref[...] = y * gain_ref[...]


def s5_mixer_normed(proj, lam_re, lam_im, b_re, b_im, c_re, c_im, d_skip, log_dt, glu_w, glu_b, out_norm, chunk=512):
    Bsz, S, _ = proj.shape
    W = S5_WIDTH
    f32 = jnp.float32
    L = min(chunk, S)
    b_mat, c_mat, a_mat = _s5_constants(lam_re, lam_im, b_re, b_im, c_re, c_im, log_dt)
    n_state = S5_GROUPS * S5_STATE
    full = lambda *shape: pl.BlockSpec(shape, lambda b, c: (0,) * len(shape))
    return pl.pallas_call(
        _s5_body,
        grid=(Bsz, S // L),
        in_specs=[pl.BlockSpec((None, L, W), lambda b, c: (b, c, PROJ_U // W)),
                  full(*b_mat.shape), full(*c_mat.shape), full(*a_mat.shape),
                  full(1, W), full(W, W), full(1, W), full(1, W)],
        out_specs=pl.BlockSpec((None, L, W), lambda b, c: (b, c, 0)),
        out_shape=jax.ShapeDtypeStruct((Bsz, S, W), f32),
        scratch_shapes=[pltpu.VMEM((2, 1, n_state), f32),
                        pltpu.VMEM((L, S5_OCT_ST), f32), pltpu.VMEM((L, S5_OCT_ST), f32)],
        compiler_params=pltpu.CompilerParams(dimension_semantics=("arbitrary", "arbitrary"),
                                             vmem_limit_bytes=48 * 1024 * 1024),
        name="s5_mixer",
    )(proj, b_mat, c_mat, a_mat, d_skip.astype(f32).reshape(1, W), glu_w.astype(jnp.bfloat16),
      glu_b.astype(f32).reshape(1, W), out_norm.astype(f32).reshape(1, W))


GDN_ROWS = 256
GDN_PACK = GDN_HEADS * GDN_CHUNK


def _bf16_terms(x, n):
    terms, rem = [], x
    for i in range(n):
        t = rem.astype(jnp.bfloat16)
        terms.append(t)
        if i + 1 < n:
            rem = rem - t.astype(jnp.float32)
    return terms


def _dot_exact_lhs(lhs01, x):
    lhs = lhs01.astype(jnp.bfloat16)
    return sum(jnp.dot(lhs, t, preferred_element_type=jnp.float32) for t in _bf16_terms(x, 3))


def _gdn_packed_unit_lower_inverse(a_mats, row, col, blk_diag):
    f32 = jnp.float32

    def mm(x, y):
        xh, xl = _bf16_terms(x, 2)
        yh, yl = (jnp.where(blk_diag, jnp.concatenate([t] * GDN_HEADS, axis=0), 0.0) for t in _bf16_terms(y, 2))
        return (jnp.dot(xh, yh, preferred_element_type=f32)
                + (jnp.dot(xh, yl, preferred_element_type=f32) + jnp.dot(xl, yh, preferred_element_type=f32)))

    same16 = (row >> 4) == (col >> 4)
    same32 = (row >> 5) == (col >> 5)
    eye = jnp.where(row == col, 1.0, 0.0)
    d1 = [jnp.where(same16, a, 0.0) for a in a_mats]
    d2 = [mm(d, d) for d in d1]
    d4 = [mm(d, d) for d in d2]
    d8 = [mm(d, d) for d in d4]
    lo = [mm(eye - a, eye + b) for a, b in zip(d1, d2)]
    hi = [mm(eye + a, eye + b) for a, b in zip(d4, d8)]
    t = [mm(a, b) for a, b in zip(lo, hi)]
    for pick in (lambda a: jnp.where(same32 & jnp.logical_not(same16), a, 0.0), lambda a: jnp.where(same32, 0.0, a)):
        mid = [mm(ti, pick(a)) for ti, a in zip(t, a_mats)]
        t = [ti - mm(m, ti) for ti, m in zip(t, mid)]
    return t


def _gdn_body(qkv_ref, z_ref, ab_ref, cw_ref, alog_ref, dtb_ref, nw_ref, o_ref, halo_ref, st_ref):
    f32, bf16 = jnp.float32, jnp.bfloat16
    L = qkv_ref.shape[0]
    C, H, dh, W = GDN_CHUNK, GDN_HEADS, GDN_HEAD_DIM, GDN_WIDTH

    @pl.when(pl.program_id(1) == 0)
    def _():
        halo_ref[...] = jnp.zeros_like(halo_ref)
        st_ref[...] = jnp.zeros_like(st_ref)

    x = qkv_ref[...]
    prev = halo_ref[...]
    row8 = lax.broadcasted_iota(jnp.int32, prev.shape, 0)
    acc = cw_ref[GDN_CONV - 1:GDN_CONV, :] * x
    for s in range(1, GDN_CONV):
        xs = pltpu.roll(x, s, 0)
        head = jnp.where(row8 < s, pltpu.roll(prev, s, 0), xs[0:SUBLANES])
        xs = jnp.concatenate([head, xs[SUBLANES:]], axis=0)
        acc = acc + cw_ref[GDN_CONV - 1 - s:GDN_CONV - s, :] * xs
    halo_ref[...] = x[L - SUBLANES:L]
    y = jax.nn.silu(acc)

    a_in, b_in = ab_ref[:, 0:H], ab_ref[:, H:2 * H]
    beta_all = jax.nn.sigmoid(b_in)
    g_all = -jnp.exp(alog_ref[...]) * jax.nn.softplus(a_in + dtb_ref[...])

    row = lax.broadcasted_iota(jnp.int32, (C, GDN_PACK), 0)
    col = lax.broadcasted_iota(jnp.int32, (C, GDN_PACK), 1) & (C - 1)
    strict = row > col
    bd_r = lax.broadcasted_iota(jnp.int32, (GDN_PACK, GDN_PACK), 0)
    bd_c = lax.broadcasted_iota(jnp.int32, (GDN_PACK, GDN_PACK), 1)
    blk_diag = (bd_r >> 6) == (bd_c >> 6)

    lr = lax.broadcasted_iota(jnp.int32, (L, L), 0)
    lc = lax.broadcasted_iota(jnp.int32, (L, L), 1)
    same_chunk = (lr >> 6) == (lc >> 6)
    tri = jnp.where(same_chunk & (lr >= lc), 1.0, 0.0)
    head_of_lane = lax.broadcasted_iota(jnp.int32, (H, GDN_PACK), 1) >> 6
    spread = jnp.where(head_of_lane == lax.broadcasted_iota(jnp.int32, (H, GDN_PACK), 0), 1.0, 0.0)
    pr = lax.broadcasted_iota(jnp.int32, (L, GDN_PACK), 0) & (C - 1)
    pc = lax.broadcasted_iota(jnp.int32, (L, GDN_PACK), 1) & (C - 1)
    gc_all = _dot_exact_lhs(tri, g_all)
    gi_all = sum(jnp.dot(t, spread.astype(bf16), preferred_element_type=f32) for t in _bf16_terms(gc_all, 3))
    gj_all = _dot_exact_lhs(jnp.where(same_chunk, 1.0, 0.0), jnp.where(pr == pc, gi_all, 0.0))
    decay_all = jnp.exp(jnp.where(pr >= pc, gi_all - gj_all, -jnp.inf))
    egc_all = jnp.exp(gc_all)

    def l2n(t):
        return t * lax.rsqrt(jnp.sum(t * t, axis=-1, keepdims=True) + EPS)

    n_chunks = L // C
    chunk_rows = [slice(c * C, (c + 1) * C) for c in range(n_chunks)]
    qs, ks, vs, kbs, a_mats, qk_ps = [], [], [], [], [], []
    for rows in chunk_rows:
        kk, qk = [], []
        for h in range(H):
            q = l2n(y[rows, h * dh:(h + 1) * dh]) * dh ** -0.5
            k = l2n(y[rows, W + h * dh:W + (h + 1) * dh])
            kb = k * beta_all[rows, h:h + 1]
            qs.append(q), ks.append(k), vs.append(y[rows, 2 * W + h * dh:2 * W + (h + 1) * dh]), kbs.append(kb)
            kk.append(_dot_nt(kb.astype(bf16), k.astype(bf16)))
            qk.append(_dot_nt(q.astype(bf16), k.astype(bf16)))
        a_mats.append(jnp.where(strict, jnp.concatenate(kk, axis=1) * decay_all[rows], 0.0))
        qk_ps.append(jnp.concatenate(qk, axis=1) * decay_all[rows])
    t_ps = _gdn_packed_unit_lower_inverse(a_mats, row, col, blk_diag)
    for c, rows in enumerate(chunk_rows):
        beta, gc, egc = beta_all[rows], gc_all[rows], egc_all[rows]
        outs = []
        for h in range(H):
            i = c * H + h
            t_h = t_ps[c][:, h * C:(h + 1) * C].astype(bf16)
            e_h = egc[:, h:h + 1]
            u = jnp.dot(t_h, (vs[i] * beta[:, h:h + 1]).astype(bf16), preferred_element_type=f32)
            w = jnp.dot(t_h, (kbs[i] * e_h).astype(bf16), preferred_element_type=f32)
            state = st_ref[h]
            sb = state.astype(bf16)
            v_new = u - jnp.dot(w.astype(bf16), sb, preferred_element_type=f32)
            o = (jnp.dot((qs[i] * e_h).astype(bf16), sb, preferred_element_type=f32)
                 + jnp.dot(qk_ps[c][:, h * C:(h + 1) * C].astype(bf16), v_new.astype(bf16),
                           preferred_element_type=f32))
            g_last = gc[C - 1:C, h:h + 1]
            k_dec = (ks[i] * jnp.exp(g_last - gc[:, h:h + 1])).astype(bf16)
            st_ref[h] = state * jnp.exp(g_last) + lax.dot_general(
                k_dec, v_new.astype(bf16), (((0,), (0,)), ((), ())), preferred_element_type=f32)
            o = o * lax.rsqrt(jnp.mean(o * o, axis=-1, keepdims=True) + EPS) * nw_ref[...]
            outs.append(o * jax.nn.silu(z_ref[rows, h * dh:(h + 1) * dh]))
        o_ref[rows, :] = jnp.concatenate(outs, axis=1)


def gdn_mixer(proj, conv_w, a_log, dt_bias, norm_w):
    Bsz, S, _ = proj.shape
    f32 = jnp.float32
    L = min(GDN_ROWS, S)
    H, W, W3 = GDN_HEADS, GDN_WIDTH, 3 * GDN_WIDTH
    full = lambda *shape: pl.BlockSpec(shape, lambda bi, ci: (0,) * len(shape))
    rows = lambda n, col=0: pl.BlockSpec((None, L, n), lambda bi, ci: (bi, ci, col // n))
    return pl.pallas_call(
        _gdn_body,
        grid=(Bsz, S // L),
        in_specs=[rows(W3, PROJ_QKV), rows(W, PROJ_Z), rows(LANES, PROJ_SMALL),
                  full(GDN_CONV, W3), full(1, H), full(1, H), full(1, GDN_HEAD_DIM)],
        out_specs=rows(W),
        out_shape=jax.ShapeDtypeStruct((Bsz, S, W), f32),
        scratch_shapes=[pltpu.VMEM((SUBLANES, W3), f32), pltpu.VMEM((H, GDN_HEAD_DIM, GDN_HEAD_DIM), f32)],
        compiler_params=pltpu.CompilerParams(dimension_semantics=("arbitrary", "arbitrary"),
                                             vmem_limit_bytes=48 * 1024 * 1024),
        name="gdn_mixer",
    )(proj, proj, proj, conv_w.astype(f32), a_log.astype(f32).reshape(1, H),
      dt_bias.astype(f32).reshape(1, H), norm_w.astype(f32).reshape(1, GDN_HEAD_DIM))


def nsa_mixer(q, kv, gate_logits, q_norm, k_norm, cmp_pe, cmp_w1, cmp_w2, cos, sin):
    Bsz, S, _ = q.shape
    f32 = jnp.float32
    G, R, dh = NSA_KV_HEADS, NSA_HEADS // NSA_KV_HEADS, NSA_HEAD_DIM
    scale = dh ** -0.5
    q = rope(rms_norm(q.reshape(Bsz, S, NSA_HEADS, dh), q_norm), cos, sin)
    kc_raw, vc_raw, ks, vs, kw, vw = (t.reshape(Bsz, S, G, dh) for t in jnp.split(kv, 6, axis=-1))
    ks = rope(rms_norm(ks, k_norm[1]), cos, sin)
    kw = rope(rms_norm(kw, k_norm[2]), cos, sin)

    n_cmp = (S - CMP_LEN) // CMP_STRIDE + 1
    win_idx = jnp.arange(n_cmp)[:, None] * CMP_STRIDE + jnp.arange(CMP_LEN)[None, :]
    cmp_end = win_idx[:, -1]

    def compress(t, pe, w1, w2):
        blocks = t[:, win_idx] + pe[:, None, :]
        blocks = jnp.moveaxis(blocks, 3, 2).reshape(Bsz, n_cmp, G, CMP_LEN * dh)
        return jax.nn.gelu(blocks @ w1) @ w2

    cos_c, sin_c = rope_tables(cmp_end, dh)
    k_cmp = rope(rms_norm(compress(kc_raw, cmp_pe[0], cmp_w1[0], cmp_w2[0]), k_norm[0]), cos_c, sin_c)
    v_cmp = compress(vc_raw, cmp_pe[1], cmp_w1[1], cmp_w2[1])

    n_slc = S // SLC_BLOCK
    n_top = min(SLC_TOPK, n_slc)
    c0 = jnp.arange(n_cmp) * CMP_STRIDE
    s0 = jnp.arange(n_slc) * SLC_BLOCK
    overlap = jnp.clip(jnp.minimum(c0[:, None] + CMP_LEN, s0[None, :] + SLC_BLOCK)
                       - jnp.maximum(c0[:, None], s0[None, :]), 0, None).astype(f32) / CMP_LEN
    gates = jax.nn.sigmoid(gate_logits.astype(f32)).reshape(Bsz, S, G, R * 3).transpose(0, 2, 3, 1)

    bf16 = jnp.bfloat16
    n_cmp_pad = S // CMP_STRIDE
    pad_cmp = lambda t: jnp.pad(t.astype(bf16), ((0, 0), (0, n_cmp_pad - n_cmp), (0, 0), (0, 0)))
    keys = lambda t: t.astype(bf16).transpose(0, 2, 1, 3)
    vals = lambda t: t.astype(bf16).transpose(0, 2, 3, 1)
    q_heads = (q * scale).astype(bf16).reshape(Bsz, S, G, R, dh).transpose(0, 2, 3, 1, 4)
    overlap_t = jnp.pad(overlap.T, ((0, 0), (0, n_cmp_pad - n_cmp))).astype(bf16)
    expand_t = ((jnp.arange(S) // SLC_BLOCK)[:, None] == jnp.arange(n_slc)[None, :]).astype(bf16)
    return _nsa_attention(q_heads, keys(pad_cmp(k_cmp)), vals(pad_cmp(v_cmp)), keys(ks), vals(vs), keys(kw), vals(vw),
                          gates, overlap_t, expand_t, n_top)


NSA_R = NSA_HEADS // NSA_KV_HEADS
SLC_TILE = 1024
MASK_NEG = -1e30


def _dot_nt(a, b):
    return lax.dot_general(a, b, (((1,), (1,)), ((), ())), preferred_element_type=jnp.float32)


def _nsa_body(q_ref, kc_ref, vct_ref, ks_ref, vst_ref, kw_ref, vwt_ref, gate_ref, ovt_ref, expt_ref, o_ref,
              *, n_top):
    f32, bf16 = jnp.float32, jnp.bfloat16
    R, QB, dh = q_ref.shape
    S = ks_ref.shape[0]
    n_slc, n_cmp_pad = ovt_ref.shape
    M = R * QB
    t0 = pl.program_id(2) * QB
    q = q_ref[...].reshape(M, dh)
    per_head = lambda t: jnp.concatenate([t] * R, axis=1)

    band = S if S < WINDOW + QB else WINDOW + QB
    start = pl.multiple_of(jnp.maximum(t0 + QB - band, 0), QB)
    s = _dot_nt(kw_ref[pl.ds(start, band), :], q)
    diff = (t0 - start) + (lax.broadcasted_iota(jnp.int32, (band, QB), 1) - lax.broadcasted_iota(jnp.int32, (band, QB), 0))
    s = s + per_head(jnp.where((diff >= 0) & (diff < WINDOW), 0.0, -jnp.inf))
    p = jnp.exp(s - jnp.max(s, axis=0, keepdims=True))
    o_win = jnp.dot(vwt_ref[:, pl.ds(start, band)], p.astype(bf16), preferred_element_type=f32)
    o_win = o_win / jnp.sum(p, axis=0, keepdims=True)

    s = _dot_nt(kc_ref[...], q)
    n_idx = lax.broadcasted_iota(jnp.int32, (n_cmp_pad, QB), 0)
    t_idx = t0 + lax.broadcasted_iota(jnp.int32, (n_cmp_pad, QB), 1)
    valid = (n_idx * CMP_STRIDE + (CMP_LEN - 1) <= t_idx) & (n_idx < n_cmp_pad - 1)
    s = s + per_head(jnp.where(valid, 0.0, -jnp.inf))
    m = jnp.max(s, axis=0, keepdims=True)
    m = jnp.where(jnp.isfinite(m), m, 0.0)
    p = jnp.exp(s - m)
    p = p / jnp.maximum(jnp.sum(p, axis=0, keepdims=True), jnp.finfo(f32).tiny)
    o_cmp = jnp.dot(vct_ref[...], p.astype(bf16), preferred_element_type=f32)

    p_sum = p[:, 0:QB]
    for r in range(1, R):
        p_sum = p_sum + p[:, r * QB:(r + 1) * QB]
    imp = sum(jnp.dot(ovt_ref[...], t, preferred_element_type=f32) for t in _bf16_terms(p_sum, 3))
    j_idx = lax.broadcasted_iota(jnp.int32, (n_slc, QB), 0)
    t_lane = t0 + lax.broadcasted_iota(jnp.int32, (n_slc, QB), 1)
    cur = lax.shift_right_logical(t_lane, 6)
    forced = (j_idx == 0) | (j_idx == cur) | (j_idx == cur - 1)
    started = j_idx * SLC_BLOCK <= t_lane
    imp = jnp.where(forced, BIG, jnp.where(started, imp, -BIG))
    rank = jnp.zeros((n_slc, QB), f32)
    for jp in range(n_slc):
        row = imp[jp:jp + 1, :]
        ahead = (row > imp) | ((row == imp) & (j_idx > jp))
        rank = rank + jnp.where(ahead, 1.0, 0.0)
    sel = jnp.where(rank < n_top, 1.0, 0.0).astype(bf16)
    k_rel = lax.broadcasted_iota(jnp.int32, (SLC_TILE, QB), 0)
    t_q = t0 + lax.broadcasted_iota(jnp.int32, (SLC_TILE, QB), 1)

    def slc_step(kt, carry):
        m, l, acc = carry
        k0 = pl.multiple_of(kt * SLC_TILE, SLC_TILE)
        s = _dot_nt(ks_ref[pl.ds(k0, SLC_TILE), :], q)
        sel_keys = jnp.dot(expt_ref[pl.ds(k0, SLC_TILE), :], sel, preferred_element_type=f32)
        bias = jnp.where((sel_keys > 0.5) & (k_rel + k0 <= t_q), 0.0, MASK_NEG)
        s = s + per_head(bias)
        m_new = jnp.maximum(m, jnp.max(s, axis=0, keepdims=True))
        alpha = jnp.exp(m - m_new)
        p = jnp.exp(s - m_new)
        l = alpha * l + jnp.sum(p, axis=0, keepdims=True)
        acc = alpha * acc + jnp.dot(vst_ref[:, pl.ds(k0, SLC_TILE)], p.astype(bf16), preferred_element_type=f32)
        return m_new, l, acc

    n_tiles = (t0 + QB - 1) // SLC_TILE + 1
    init = (jnp.full((1, M), MASK_NEG, f32), jnp.zeros((1, M), f32), jnp.zeros((dh, M), f32))
    _, l, acc = lax.fori_loop(0, n_tiles, slc_step, init)
    o_slc = acc / l

    gate = gate_ref[...]
    g_cmp, g_slc, g_win = (jnp.concatenate([gate[3 * r + c:3 * r + c + 1, :] for r in range(R)], axis=1)
                           for c in range(3))
    o_t = g_cmp * o_cmp + g_slc * o_slc + g_win * o_win
    o_ref[...] = jnp.concatenate([o_t[:, r * QB:(r + 1) * QB].T for r in range(R)], axis=1)


def _nsa_attention(q, kc, vct, ks, vst, kw, vwt, gates, overlap_t, expand_t, n_top):
    Bsz, G, R, S, dh = q.shape
    n_cmp_pad = kc.shape[2]
    assert S % SLC_TILE == 0 and SLC_TILE % Q_BLOCK == 0, (S, SLC_TILE, Q_BLOCK)
    keys = lambda n: pl.BlockSpec((None, None, n, dh), lambda b, g, i: (b, g, 0, 0))
    vals = lambda n: pl.BlockSpec((None, None, dh, n), lambda b, g, i: (b, g, 0, 0))
    const = lambda a: pl.BlockSpec(a.shape, lambda b, g, i: (0, 0))
    return pl.pallas_call(
        functools.partial(_nsa_body, n_top=n_top),
        grid=(Bsz, G, S // Q_BLOCK),
        in_specs=[pl.BlockSpec((None, None, R, Q_BLOCK, dh), lambda b, g, i: (b, g, 0, i, 0)),
                  keys(n_cmp_pad), vals(n_cmp_pad), keys(S), vals(S), keys(S), vals(S),
                  pl.BlockSpec((None, None, 3 * R, Q_BLOCK), lambda b, g, i: (b, g, 0, i)),
                  const(overlap_t), const(expand_t)],
        out_specs=pl.BlockSpec((None, Q_BLOCK, R * dh), lambda b, g, i: (b, i, g)),
        out_shape=jax.ShapeDtypeStruct((Bsz, S, G * R * dh), jnp.float32),
        compiler_params=pltpu.CompilerParams(dimension_semantics=("arbitrary", "arbitrary", "arbitrary"),
                                             vmem_limit_bytes=48 * 1024 * 1024),
        name="nsa_attention",
    )(q, kc, vct, ks, vst, kw, vwt, gates, overlap_t, expand_t)


def hybrid_mixer(x, ln_mix, w_in, lam_re, lam_im, b_re, b_im, c_re, c_im, s5_d, s5_log_dt, glu_w, glu_b,
                 s5_out_norm, conv_w, a_log, dt_bias, gdn_norm_w, q_norm, k_norm, cmp_pe, cmp_w1,
                 cmp_w2, cos, sin):
    Bsz, S, D = x.shape
    proj = _in_proj(x.reshape(-1, D), ln_mix, _in_proj_weight(w_in)).reshape(Bsz, S, PROJ_W)
    y_s5 = s5_mixer_normed(proj, lam_re, lam_im, b_re, b_im, c_re, c_im, s5_d, s5_log_dt, glu_w, glu_b, s5_out_norm)
    y_gdn = gdn_mixer(proj, conv_w, a_log, dt_bias, gdn_norm_w)
    y_nsa = nsa_mixer(proj[..., PROJ_Q:PROJ_Q + NSA_WIDTH], proj[..., PROJ_KV:PROJ_KV + 6 * NSA_KV_WIDTH],
                      proj[..., PROJ_GATES:PROJ_GATES + 3 * NSA_HEADS], q_norm, k_norm, cmp_pe, cmp_w1, cmp_w2,
                      cos, sin)
    return y_s5, y_gdn, y_nsa


def hier_moe(x, logits, layer, w_gate, w_up, w_down):
    T, D = x.shape
    f32 = jnp.float32
    g_logits = logits[:, :N_GROUPS]
    grp = jnp.argmax(g_logits, axis=-1)
    p_grp = jnp.take_along_axis(jax.nn.softmax(g_logits, axis=-1), grp[:, None], axis=-1)
    e_logits = logits[:, N_GROUPS:N_ROUTER].reshape(T, N_GROUPS, EXPERTS_PER_GROUP)
    e_logits = jnp.take_along_axis(e_logits, grp[:, None, None], axis=1)[:, 0]
    local = jnp.arange(EXPERTS_PER_GROUP, dtype=jnp.int32)[None, :]
    rest, top_logit, top_local = e_logits, [], []
    for _ in range(TOP_K):
        idx = jnp.argmax(rest, axis=-1).astype(jnp.int32)
        top_logit.append(jnp.take_along_axis(rest, idx[:, None], axis=-1)[:, 0])
        top_local.append(idx)
        rest = jnp.where(local == idx[:, None], -jnp.inf, rest)
    top_logit, top_local = jnp.stack(top_logit, axis=-1), jnp.stack(top_local, axis=-1)
    weights = jax.nn.softmax(top_logit, axis=-1) * p_grp
    experts = grp[:, None].astype(jnp.int32) * EXPERTS_PER_GROUP + top_local
    n_assign = T * TOP_K
    e_flat = experts.reshape(-1)
    tok_flat = jnp.repeat(jnp.arange(T, dtype=jnp.int32), TOP_K)
    onehot = (e_flat[:, None] == jnp.arange(N_EXPERTS, dtype=jnp.int32)[None, :]).astype(jnp.int32)
    running = jnp.cumsum(onehot, axis=0)
    counts = running[-1]
    padded = (counts + MOE_ROWS - 1) // MOE_ROWS * MOE_ROWS
    pad_start = jnp.cumsum(padded) - padded
    dest = (pad_start[e_flat] + jnp.take_along_axis(running, e_flat[:, None], axis=1)[:, 0] - 1).astype(jnp.int32)
    n_blk = (n_assign + MOE_ROWS - 1) // MOE_ROWS + N_EXPERTS
    cap = n_blk * MOE_ROWS
    slot_tok = jnp.full((cap,), T, jnp.int32).at[dest].set(tok_flat)
    blk_expert = jnp.minimum(jnp.searchsorted(jnp.cumsum(padded), jnp.arange(n_blk) * MOE_ROWS, side='right'),
                             N_EXPERTS - 1).astype(jnp.int32)
    n_used = (jnp.sum(padded) // MOE_ROWS).astype(jnp.int32).reshape(1)
    x_pad = jnp.concatenate([x, jnp.zeros((1, D), x.dtype)], axis=0)
    y = _moe_ffn(blk_expert, n_used, x_pad[slot_tok], layer, w_gate.astype(f32), w_up.astype(f32),
                 w_down.astype(f32))
    dest = dest.reshape(T, TOP_K)
    out = weights[:, 0:1] * y[dest[:, 0]]
    for k in range(1, TOP_K):
        out = out + weights[:, k:k + 1] * y[dest[:, k]]
    return out


MOE_ROWS = 256


def _moe_ffn_body(be_ref, nu_ref, x_ref, wg_ref, wu_ref, wd_ref, o_ref, wg_bf, wu_bf, wd_bf):
    i = pl.program_id(0)
    used = i < nu_ref[0]

    @pl.when((i == 0) | (be_ref[i] != be_ref[jnp.maximum(i - 1, 0)]))
    def _():
        wg_bf[...] = wg_ref[...].astype(jnp.bfloat16)
        wu_bf[...] = wu_ref[...].astype(jnp.bfloat16)
        wd_bf[...] = wd_ref[...].astype(jnp.bfloat16)

    @pl.when(used)
    def _():
        x = x_ref[...]
        gate = jnp.dot(x, wg_bf[...], preferred_element_type=jnp.float32)
        up = jnp.dot(x, wu_bf[...], preferred_element_type=jnp.float32)
        hid = (jax.nn.silu(gate) * up).astype(jnp.bfloat16)
        o_ref[...] = jnp.dot(hid, wd_bf[...], preferred_element_type=jnp.float32)

    @pl.when(jnp.logical_not(used))
    def _():
        o_ref[...] = jnp.zeros_like(o_ref)


def _moe_ffn(blk_expert, n_used, xs, layer, w_gate, w_up, w_down):
    cap, D = xs.shape
    n_blk = cap // MOE_ROWS
    De = w_gate.shape[-1]
    rows = lambda n: pl.BlockSpec((MOE_ROWS, n), lambda i, be, nu: (i, 0))
    return pl.pallas_call(
        _moe_ffn_body,
        grid_spec=pltpu.PrefetchScalarGridSpec(
            num_scalar_prefetch=2, grid=(n_blk,),
            in_specs=[rows(D),
                      pl.BlockSpec((None, None, D, De), lambda i, be, nu: (layer, be[i], 0, 0)),
                      pl.BlockSpec((None, None, D, De), lambda i, be, nu: (layer, be[i], 0, 0)),
                      pl.BlockSpec((None, None, De, D), lambda i, be, nu: (layer, be[i], 0, 0))],
            out_specs=rows(D),
            scratch_shapes=[pltpu.VMEM((D, De), jnp.bfloat16), pltpu.VMEM((D, De), jnp.bfloat16),
                            pltpu.VMEM((De, D), jnp.bfloat16)]),
        out_shape=jax.ShapeDtypeStruct((cap, D), jnp.float32),
        compiler_params=pltpu.CompilerParams(dimension_semantics=("arbitrary",),
                                             vmem_limit_bytes=56 * 1024 * 1024),
        name="moe_ffn",
    )(blk_expert, n_used, xs, w_gate, w_up, w_down)


def kernel(x, ln_mix, w_in, s5_lambda_re, s5_lambda_im, s5_b_re, s5_b_im, s5_c_re, s5_c_im, s5_d,
           s5_log_dt, s5_glu_w, s5_glu_b, s5_out_norm, gdn_conv_w, gdn_a_log, gdn_dt_bias, gdn_norm_w,
           nsa_q_norm, nsa_k_norm, nsa_cmp_pe, nsa_cmp_w1, nsa_cmp_w2, nsa_out_norm, w_out, ln_ffn,
           moe_group_w, moe_group_b, moe_expert_w, moe_expert_b, moe_w_gate, moe_w_up, moe_w_down):
    Bsz, S, D = x.shape
    cos, sin = rope_tables(jnp.arange(S), NSA_HEAD_DIM)
    for l in range(DEPTH):
        y_s5, y_gdn, y_nsa = hybrid_mixer(
            x, ln_mix[l], w_in[l], s5_lambda_re[l], s5_lambda_im[l], s5_b_re[l], s5_b_im[l], s5_c_re[l], s5_c_im[l],
            s5_d[l], s5_log_dt[l], s5_glu_w[l], s5_glu_b[l], s5_out_norm[l], gdn_conv_w[l], gdn_a_log[l],
            gdn_dt_bias[l], gdn_norm_w[l], nsa_q_norm[l], nsa_k_norm[l], nsa_cmp_pe[l], nsa_cmp_w1[l], nsa_cmp_w2[l],
            cos, sin)
        router_w = jnp.concatenate([moe_group_w[l], moe_expert_w[l]], axis=-1)
        router_b = jnp.concatenate([moe_group_b[l], moe_expert_b[l]], axis=-1)
        h, a, logits = _out_proj(x.reshape(-1, D), y_s5.reshape(-1, S5_WIDTH), y_gdn.reshape(-1, GDN_WIDTH),
                                 y_nsa.reshape(-1, NSA_WIDTH), nsa_out_norm[l], w_out[l], ln_ffn[l], router_w, router_b)
        x = (h + hier_moe(a, logits, l, moe_w_gate, moe_w_up, moe_w_down)).reshape(Bsz, S, D)
    return x
```

```python
import functools
import math
import jax, jax.numpy as jnp
from jax import lax
import numpy as np
from jax.experimental import pallas as pl
from jax.experimental.pallas import tpu as pltpu

D_MODEL = 2048
BATCH = 4
SEQ = 4096
DEPTH = 2

EPS = 1e-6
S5_WIDTH = D_MODEL // 4
S5_GROUP = 16
S5_GROUPS = S5_WIDTH // S5_GROUP
S5_STATE = 64
GDN_HEAD_DIM = 128
GDN_WIDTH = D_MODEL // 4
GDN_HEADS = GDN_WIDTH // GDN_HEAD_DIM
GDN_CONV = 4
GDN_CHUNK = 64
NSA_HEAD_DIM = 64
NSA_WIDTH = D_MODEL // 2
NSA_HEADS = NSA_WIDTH // NSA_HEAD_DIM
NSA_KV_HEADS = 4
NSA_KV_WIDTH = NSA_KV_HEADS * NSA_HEAD_DIM
CMP_LEN = 32
CMP_STRIDE = 16
CMP_HIDDEN = 256
SLC_BLOCK = 64
SLC_TOPK = 16
WINDOW = 512
Q_BLOCK = 128
ROPE_THETA = 10000.0
BIG = 1e9
N_GROUPS = 4
EXPERTS_PER_GROUP = 8
N_EXPERTS = N_GROUPS * EXPERTS_PER_GROUP
TOP_K = 2
D_EXPERT = 512
MOE_BLOCK = 128
IN_SIZES = (S5_WIDTH, 3 * GDN_WIDTH, GDN_WIDTH, GDN_HEADS, GDN_HEADS, NSA_WIDTH, 6 * NSA_KV_WIDTH, 3 * NSA_HEADS)
D_IN = sum(IN_SIZES)


PROJ_QKV, PROJ_KV, PROJ_Q, PROJ_U, PROJ_Z, PROJ_SMALL = 0, 1536, 3072, 4096, 4608, 5120
PROJ_SMALL_W = 512
PROJ_W = PROJ_SMALL + PROJ_SMALL_W
PROJ_GATES = PROJ_SMALL + 2 * GDN_HEADS
LANES = 128
IN_ROWS, IN_COLS = 1024, 512
OUT_ROWS = 256
N_ROUTER = N_GROUPS + N_EXPERTS


def _in_proj_weight(w_in):
    u, qkv, z, ga, gb, q, kv, gl = jnp.split(w_in, np.cumsum(IN_SIZES)[:-1].tolist(), axis=-1)
    small = jnp.concatenate([ga, gb, gl], axis=-1)
    small = jnp.pad(small, ((0, 0), (0, PROJ_SMALL_W - small.shape[-1])))
    return jnp.concatenate([qkv, kv, q, u, z, small], axis=-1)


def _in_proj_body(x_ref, g_ref, w_ref, o_ref, a_ref):
    @pl.when(pl.program_id(1) == 0)
    def _():
        x = x_ref[...]
        a_ref[...] = (x * lax.rsqrt(jnp.mean(x * x, axis=-1, keepdims=True) + EPS) * g_ref[...]).astype(jnp.bfloat16)

    o_ref[...] = jnp.dot(a_ref[...], w_ref[...].astype(jnp.bfloat16), preferred_element_type=jnp.float32)


def _in_proj(x, gain, w):
    T, D = x.shape
    N = w.shape[1]
    tm = min(IN_ROWS, T)
    return pl.pallas_call(
        _in_proj_body,
        grid=(T // tm, N // IN_COLS),
        in_specs=[pl.BlockSpec((tm, D), lambda i, j: (i, 0)),
                  pl.BlockSpec((1, D), lambda i, j: (0, 0)),
                  pl.BlockSpec((D, IN_COLS), lambda i, j: (0, j))],
        out_specs=pl.BlockSpec((tm, IN_COLS), lambda i, j: (i, j)),
        out_shape=jax.ShapeDtypeStruct((T, N), jnp.float32),
        scratch_shapes=[pltpu.VMEM((tm, D), jnp.bfloat16)],
        compiler_params=pltpu.CompilerParams(dimension_semantics=("arbitrary", "arbitrary"),
                                             vmem_limit_bytes=48 * 1024 * 1024),
        name="in_proj",
    )(x, gain.astype(jnp.float32).reshape(1, D), w)


def _out_proj_body(x_ref, s5_ref, gdn_ref, nsa_ref, ng_ref, w_ref, fg_ref, rw_ref, rb_ref, h_ref, a_ref, lg_ref):
    f32, bf16 = jnp.float32, jnp.bfloat16
    nsa = nsa_ref[...]
    nsa = nsa * lax.rsqrt(jnp.mean(nsa * nsa, axis=-1, keepdims=True) + EPS) * ng_ref[...]
    k0, k1 = S5_WIDTH, S5_WIDTH + GDN_WIDTH
    y = (jnp.dot(s5_ref[...].astype(bf16), w_ref[0:k0, :], preferred_element_type=f32)
         + jnp.dot(gdn_ref[...].astype(bf16), w_ref[k0:k1, :], preferred_element_type=f32)
         + jnp.dot(nsa.astype(bf16), w_ref[k1:, :], preferred_element_type=f32))
    h = x_ref[...] + y
    h_ref[...] = h
    a = (h * lax.rsqrt(jnp.mean(h * h, axis=-1, keepdims=True) + EPS) * fg_ref[...]).astype(bf16)
    a_ref[...] = a
    lg_ref[...] = jnp.dot(a, rw_ref[...], preferred_element_type=f32) + rb_ref[...]


def _out_proj(x, y_s5, y_gdn, y_nsa, nsa_gain, w_out, ffn_gain, router_w, router_b):
    T, D = x.shape
    tm = min(OUT_ROWS, T)
    f32, bf16 = jnp.float32, jnp.bfloat16
    rw = jnp.pad(router_w.astype(bf16), ((0, 0), (0, LANES - N_ROUTER)))
    rb = jnp.pad(router_b.astype(f32), (0, LANES - N_ROUTER)).reshape(1, LANES)
    rows = lambda n: pl.BlockSpec((tm, n), lambda i: (i, 0))
    full = lambda *shape: pl.BlockSpec(shape, lambda i: (0,) * len(shape))
    return pl.pallas_call(
        _out_proj_body,
        grid=(T // tm,),
        in_specs=[rows(D), rows(S5_WIDTH), rows(GDN_WIDTH), rows(NSA_WIDTH), full(1, NSA_WIDTH), full(D, D),
                  full(1, D), full(D, LANES), full(1, LANES)],
        out_specs=[rows(D), rows(D), rows(LANES)],
        out_shape=[jax.ShapeDtypeStruct((T, D), f32), jax.ShapeDtypeStruct((T, D), bf16),
                   jax.ShapeDtypeStruct((T, LANES), f32)],
        compiler_params=pltpu.CompilerParams(dimension_semantics=("arbitrary",),
                                             vmem_limit_bytes=48 * 1024 * 1024),
        name="out_proj",
    )(x, y_s5, y_gdn, y_nsa, nsa_gain.astype(f32).reshape(1, NSA_WIDTH), w_out.astype(bf16),
      ffn_gain.astype(f32).reshape(1, D), rw, rb)


def rms_norm(x, gain):
    xf = x.astype(jnp.float32)
    y = xf * lax.rsqrt(jnp.mean(xf * xf, axis=-1, keepdims=True) + EPS)
    return (y * gain.astype(jnp.float32)).astype(x.dtype)


def l2_norm(x):
    return x * lax.rsqrt(jnp.sum(x * x, axis=-1, keepdims=True) + EPS)


def rope_tables(pos, dim):
    inv_freq = ROPE_THETA ** (-jnp.arange(0, dim, 2, dtype=jnp.float32) / dim)
    ang = pos.astype(jnp.float32)[:, None] * inv_freq[None, :]
    return jnp.cos(ang)[:, None, :], jnp.sin(ang)[:, None, :]


def rope(x, cos, sin):
    x1, x2 = jnp.split(x, 2, axis=-1)
    return jnp.concatenate([x1 * cos - x2 * sin, x2 * cos + x1 * sin], axis=-1).astype(x.dtype)


def masked_softmax(s, mask):
    s = jnp.where(mask, s, -jnp.inf)
    m = jnp.max(s, axis=-1, keepdims=True)
    m = jnp.where(jnp.isfinite(m), m, 0.0)
    p = jnp.exp(s - m)
    return p / jnp.maximum(jnp.sum(p, axis=-1, keepdims=True), jnp.finfo(jnp.float32).tiny)


S5_OCT = 8
S5_NOCT = S5_GROUPS // S5_OCT
S5_OCT_IN = S5_OCT * S5_GROUP
S5_OCT_ST = S5_OCT * S5_STATE
SUBLANES = 8
S5_SCAN_SHIFTS = (1, 2, 4)


def _s5_constants(lam_re, lam_im, b_re, b_im, c_re, c_im, log_dt):
    f32 = jnp.float32
    lr, li = lam_re.astype(f32), lam_im.astype(f32)
    dt = jnp.exp(log_dt.astype(f32))[:, None]

    def lam_pow(k):
        mag = jnp.exp(k * lr * dt)
        return mag * jnp.cos(k * li * dt), mag * jnp.sin(k * li * dt)

    ar, ai = lam_pow(1.0)
    nr, ni = ar - 1.0, ai
    den = lr * lr + li * li
    qr, qi = (nr * lr + ni * li) / den, (ni * lr - nr * li) / den
    br, bi = b_re.astype(f32), b_im.astype(f32)
    bbr = qr[:, :, None] * br - qi[:, :, None] * bi
    bbi = qr[:, :, None] * bi + qi[:, :, None] * br
    eye = jnp.eye(S5_OCT, dtype=f32)

    def blk_b(t):
        t = t.reshape(S5_NOCT, S5_OCT, S5_STATE, S5_GROUP)
        return jnp.einsum('qgph,gk->qghkp', t, eye).reshape(S5_NOCT, S5_OCT_IN, S5_OCT_ST)

    def blk_c(t):
        t = t.reshape(S5_NOCT, S5_OCT, S5_GROUP, S5_STATE)
        return jnp.einsum('qghp,gk->qgpkh', t, eye).reshape(S5_NOCT, S5_OCT_ST, S5_OCT_IN)

    b_mat = jnp.stack([blk_b(bbr), blk_b(bbi)]).astype(jnp.bfloat16)
    c_mat = jnp.stack([blk_c(c_re.astype(f32)), blk_c(-c_im.astype(f32))]).astype(jnp.bfloat16)
    rows = jnp.arange(SUBLANES, dtype=f32)[:, None, None]
    tiles = []
    for k in S5_SCAN_SHIFTS:
        pr, pi = lam_pow(jnp.full_like(rows, float(k)))
        keep = rows >= k
        tiles.append(jnp.stack([jnp.where(keep, pr, 0.0), jnp.where(keep, pi, 0.0)]))
    pr, pi = lam_pow(rows + 1.0)
    tiles.append(jnp.stack([pr, pi]))
    a_mat = jnp.stack(tiles).reshape(len(tiles), 2, SUBLANES, S5_GROUPS * S5_STATE)
    return b_mat, c_mat, a_mat


def _s5_body(u_ref, b_ref, c_ref, a_ref, d_ref, gw_ref, gb_ref, gain_ref, o_ref, st_ref, xr_ref, xi_ref):
    bf16 = jnp.bfloat16
    n_tiles = u_ref.shape[0] // SUBLANES

    @pl.when(pl.program_id(1) == 0)
    def _():
        st_ref[...] = jnp.zeros_like(st_ref)

    u = u_ref[...]
    ys = []
    for q in range(S5_NOCT):
        lanes = slice(q * S5_OCT_ST, (q + 1) * S5_OCT_ST)
        uq = u[:, q * S5_OCT_IN:(q + 1) * S5_OCT_IN].astype(bf16)
        xr_ref[...] = jnp.dot(uq, b_ref[0, q], preferred_element_type=jnp.float32)
        xi_ref[...] = jnp.dot(uq, b_ref[1, q], preferred_element_type=jnp.float32)

        def tile_step(i, carry, lanes=lanes):
            cr, ci = carry
            r0 = pl.multiple_of(i * SUBLANES, SUBLANES)
            xr = xr_ref[pl.ds(r0, SUBLANES), :]
            xi = xi_ref[pl.ds(r0, SUBLANES), :]
            for lvl, k in enumerate(S5_SCAN_SHIFTS):
                pr, pi = a_ref[lvl, 0, :, lanes], a_ref[lvl, 1, :, lanes]
                sr, si = pltpu.roll(xr, k, 0), pltpu.roll(xi, k, 0)
                xr, xi = xr + (pr * sr - pi * si), xi + (pr * si + pi * sr)
            pr, pi = a_ref[len(S5_SCAN_SHIFTS), 0, :, lanes], a_ref[len(S5_SCAN_SHIFTS), 1, :, lanes]
            cr = jnp.broadcast_to(cr, xr.shape)
            ci = jnp.broadcast_to(ci, xi.shape)
            xr, xi = xr + (pr * cr - pi * ci), xi + (pr * ci + pi * cr)
            xr_ref[pl.ds(r0, SUBLANES), :] = xr
            xi_ref[pl.ds(r0, SUBLANES), :] = xi
            return xr[SUBLANES - 1:, :], xi[SUBLANES - 1:, :]

        cr, ci = lax.fori_loop(0, n_tiles, tile_step, (st_ref[0, :, lanes], st_ref[1, :, lanes]))
        st_ref[0, :, lanes] = cr
        st_ref[1, :, lanes] = ci
        ys.append(jnp.dot(xr_ref[...].astype(bf16), c_ref[0, q], preferred_element_type=jnp.float32)
                  + jnp.dot(xi_ref[...].astype(bf16), c_ref[1, q], preferred_element_type=jnp.float32))
    y = jnp.concatenate(ys, axis=1) + d_ref[...] * u
    y = jax.nn.gelu(y)
    z = jnp.dot(y.astype(bf16), gw_ref[...], preferred_element_type=jnp.float32) + gb_ref[...]
    y = y * jax.nn.sigmoid(z)
    y = y * lax.rsqrt(jnp.mean(y * y, axis=-1, keepdims=True) + EPS)
    o_ref[...] = y * gain_ref[...]


def s5_mixer_normed(proj, lam_re, lam_im, b_re, b_im, c_re, c_im, d_skip, log_dt, glu_w, glu_b, out_norm, chunk=512):
    Bsz, S, _ = proj.shape
    W = S5_WIDTH
    f32 = jnp.float32
    L = min(chunk, S)
    b_mat, c_mat, a_mat = _s5_constants(lam_re, lam_im, b_re, b_im, c_re, c_im, log_dt)
    n_state = S5_GROUPS * S5_STATE
    full = lambda *shape: pl.BlockSpec(shape, lambda b, c: (0,) * len(shape))
    return pl.pallas_call(
        _s5_body,
        grid=(Bsz, S // L),
        in_specs=[pl.BlockSpec((None, L, W), lambda b, c: (b, c, PROJ_U // W)),
                  full(*b_mat.shape), full(*c_mat.shape), full(*a_mat.shape),
                  full(1, W), full(W, W), full(1, W), full(1, W)],
        out_specs=pl.BlockSpec((None, L, W), lambda b, c: (b, c, 0)),
        out_shape=jax.ShapeDtypeStruct((Bsz, S, W), f32),
        scratch_shapes=[pltpu.VMEM((2, 1, n_state), f32),
                        pltpu.VMEM((L, S5_OCT_ST), f32), pltpu.VMEM((L, S5_OCT_ST), f32)],
        compiler_params=pltpu.CompilerParams(dimension_semantics=("arbitrary", "arbitrary"),
                                             vmem_limit_bytes=48 * 1024 * 1024),
        name="s5_mixer",
    )(proj, b_mat, c_mat, a_mat, d_skip.astype(f32).reshape(1, W), glu_w.astype(jnp.bfloat16),
      glu_b.astype(f32).reshape(1, W), out_norm.astype(f32).reshape(1, W))


GDN_ROWS = 256
GDN_PACK = GDN_HEADS * GDN_CHUNK


def _bf16_terms(x, n):
    terms, rem = [], x
    for i in range(n):
        t = rem.astype(jnp.bfloat16)
        terms.append(t)
        if i + 1 < n:
            rem = rem - t.astype(jnp.float32)
    return terms


def _dot_exact_lhs(lhs01, x):
    lhs = lhs01.astype(jnp.bfloat16)
    return sum(jnp.dot(lhs, t, preferred_element_type=jnp.float32) for t in _bf16_terms(x, 3))


def _gdn_packed_unit_lower_inverse(a_mats, row, col, blk_diag):
    f32 = jnp.float32

    def mm(x, y):
        xh, xl = _bf16_terms(x, 2)
        yh, yl = (jnp.where(blk_diag, jnp.concatenate([t] * GDN_HEADS, axis=0), 0.0) for t in _bf16_terms(y, 2))
        return (jnp.dot(xh, yh, preferred_element_type=f32)
                + (jnp.dot(xh, yl, preferred_element_type=f32) + jnp.dot(xl, yh, preferred_element_type=f32)))

    same16 = (row >> 4) == (col >> 4)
    same32 = (row >> 5) == (col >> 5)
    eye = jnp.where(row == col, 1.0, 0.0)
    d1 = [jnp.where(same16, a, 0.0) for a in a_mats]
    d2 = [mm(d, d) for d in d1]
    d4 = [mm(d, d) for d in d2]
    d8 = [mm(d, d) for d in d4]
    lo = [mm(eye - a, eye + b) for a, b in zip(d1, d2)]
    hi = [mm(eye + a, eye + b) for a, b in zip(d4, d8)]
    t = [mm(a, b) for a, b in zip(lo, hi)]
    for pick in (lambda a: jnp.where(same32 & jnp.logical_not(same16), a, 0.0), lambda a: jnp.where(same32, 0.0, a)):
        mid = [mm(ti, pick(a)) for ti, a in zip(t, a_mats)]
        t = [ti - mm(m, ti) for ti, m in zip(t, mid)]
    return t


def _gdn_body(qkv_ref, z_ref, ab_ref, cw_ref, alog_ref, dtb_ref, nw_ref, o_ref, halo_ref, st_ref):
    f32, bf16 = jnp.float32, jnp.bfloat16
    L = qkv_ref.shape[0]
    C, H, dh, W = GDN_CHUNK, GDN_HEADS, GDN_HEAD_DIM, GDN_WIDTH

    @pl.when(pl.program_id(1) == 0)
    def _():
        halo_ref[...] = jnp.zeros_like(halo_ref)
        st_ref[...] = jnp.zeros_like(st_ref)

    x = qkv_ref[...]
    prev = halo_ref[...]
    row8 = lax.broadcasted_iota(jnp.int32, prev.shape, 0)
    acc = cw_ref[GDN_CONV - 1:GDN_CONV, :] * x
    for s in range(1, GDN_CONV):
        xs = pltpu.roll(x, s, 0)
        head = jnp.where(row8 < s, pltpu.roll(prev, s, 0), xs[0:SUBLANES])
        xs = jnp.concatenate([head, xs[SUBLANES:]], axis=0)
        acc = acc + cw_ref[GDN_CONV - 1 - s:GDN_CONV - s, :] * xs
    halo_ref[...] = x[L - SUBLANES:L]
    y = jax.nn.silu(acc)

    a_in, b_in = ab_ref[:, 0:H], ab_ref[:, H:2 * H]
    beta_all = jax.nn.sigmoid(b_in)
    g_all = -jnp.exp(alog_ref[...]) * jax.nn.softplus(a_in + dtb_ref[...])

    row = lax.broadcasted_iota(jnp.int32, (C, GDN_PACK), 0)
    col = lax.broadcasted_iota(jnp.int32, (C, GDN_PACK), 1) & (C - 1)
    strict = row > col
    bd_r = lax.broadcasted_iota(jnp.int32, (GDN_PACK, GDN_PACK), 0)
    bd_c = lax.broadcasted_iota(jnp.int32, (GDN_PACK, GDN_PACK), 1)
    blk_diag = (bd_r >> 6) == (bd_c >> 6)

    lr = lax.broadcasted_iota(jnp.int32, (L, L), 0)
    lc = lax.broadcasted_iota(jnp.int32, (L, L), 1)
    same_chunk = (lr >> 6) == (lc >> 6)
    tri = jnp.where(same_chunk & (lr >= lc), 1.0, 0.0)
    head_of_lane = lax.broadcasted_iota(jnp.int32, (H, GDN_PACK), 1) >> 6
    spread = jnp.where(head_of_lane == lax.broadcasted_iota(jnp.int32, (H, GDN_PACK), 0), 1.0, 0.0)
    pr = lax.broadcasted_iota(jnp.int32, (L, GDN_PACK), 0) & (C - 1)
    pc = lax.broadcasted_iota(jnp.int32, (L, GDN_PACK), 1) & (C - 1)
    gc_all = _dot_exact_lhs(tri, g_all)
    gi_all = sum(jnp.dot(t, spread.astype(bf16), preferred_element_type=f32) for t in _bf16_terms(gc_all, 3))
    gj_all = _dot_exact_lhs(jnp.where(same_chunk, 1.0, 0.0), jnp.where(pr == pc, gi_all, 0.0))
    decay_all = jnp.exp(jnp.where(pr >= pc, gi_all - gj_all, -jnp.inf))
    egc_all = jnp.exp(gc_all)

    def l2n(t):
        return t * lax.rsqrt(jnp.sum(t * t, axis=-1, keepdims=True) + EPS)

    n_chunks = L // C
    chunk_rows = [slice(c * C, (c + 1) * C) for c in range(n_chunks)]
    qs, ks, vs, kbs, a_mats, qk_ps = [], [], [], [], [], []
    for rows in chunk_rows:
        kk, qk = [], []
        for h in range(H):
            q = l2n(y[rows, h * dh:(h + 1) * dh]) * dh ** -0.5
            k = l2n(y[rows, W + h * dh:W + (h + 1) * dh])
            kb = k * beta_all[rows, h:h + 1]
            qs.append(q), ks.append(k), vs.append(y[rows, 2 * W + h * dh:2 * W + (h + 1) * dh]), kbs.append(kb)
            kk.append(_dot_nt(kb.astype(bf16), k.astype(bf16)))
            qk.append(_dot_nt(q.astype(bf16), k.astype(bf16)))
        a_mats.append(jnp.where(strict, jnp.concatenate(kk, axis=1) * decay_all[rows], 0.0))
        qk_ps.append(jnp.concatenate(qk, axis=1) * decay_all[rows])
    t_ps = _gdn_packed_unit_lower_inverse(a_mats, row, col, blk_diag)
    for c, rows in enumerate(chunk_rows):
        beta, gc, egc = beta_all[rows], gc_all[rows], egc_all[rows]
        outs = []
        for h in range(H):
            i = c * H + h
            t_h = t_ps[c][:, h * C:(h + 1) * C].astype(bf16)
            e_h = egc[:, h:h + 1]
            u = jnp.dot(t_h, (vs[i] * beta[:, h:h + 1]).astype(bf16), preferred_element_type=f32)
            w = jnp.dot(t_h, (kbs[i] * e_h).astype(bf16), preferred_element_type=f32)
            state = st_ref[h]
            sb = state.astype(bf16)
            v_new = u - jnp.dot(w.astype(bf16), sb, preferred_element_type=f32)
            o = (jnp.dot((qs[i] * e_h).astype(bf16), sb, preferred_element_type=f32)
                 + jnp.dot(qk_ps[c][:, h * C:(h + 1) * C].astype(bf16), v_new.astype(bf16),
                           preferred_element_type=f32))
            g_last = gc[C - 1:C, h:h + 1]
            k_dec = (ks[i] * jnp.exp(g_last - gc[:, h:h + 1])).astype(bf16)
            st_ref[h] = state * jnp.exp(g_last) + lax.dot_general(
                k_dec, v_new.astype(bf16), (((0,), (0,)), ((), ())), preferred_element_type=f32)
            o = o * lax.rsqrt(jnp.mean(o * o, axis=-1, keepdims=True) + EPS) * nw_ref[...]
            outs.append(o * jax.nn.silu(z_ref[rows, h * dh:(h + 1) * dh]))
        o_ref[rows, :] = jnp.concatenate(outs, axis=1)


def gdn_mixer(proj, conv_w, a_log, dt_bias, norm_w):
    Bsz, S, _ = proj.shape
    f32 = jnp.float32
    L = min(GDN_ROWS, S)
    H, W, W3 = GDN_HEADS, GDN_WIDTH, 3 * GDN_WIDTH
    full = lambda *shape: pl.BlockSpec(shape, lambda bi, ci: (0,) * len(shape))
    rows = lambda n, col=0: pl.BlockSpec((None, L, n), lambda bi, ci: (bi, ci, col // n))
    return pl.pallas_call(
        _gdn_body,
        grid=(Bsz, S // L),
        in_specs=[rows(W3, PROJ_QKV), rows(W, PROJ_Z), rows(LANES, PROJ_SMALL),
                  full(GDN_CONV, W3), full(1, H), full(1, H), full(1, GDN_HEAD_DIM)],
        out_specs=rows(W),
        out_shape=jax.ShapeDtypeStruct((Bsz, S, W), f32),
        scratch_shapes=[pltpu.VMEM((SUBLANES, W3), f32), pltpu.VMEM((H, GDN_HEAD_DIM, GDN_HEAD_DIM), f32)],
        compiler_params=pltpu.CompilerParams(dimension_semantics=("arbitrary", "arbitrary"),
                                             vmem_limit_bytes=48 * 1024 * 1024),
        name="gdn_mixer",
    )(proj, proj, proj, conv_w.astype(f32), a_log.astype(f32).reshape(1, H),
      dt_bias.astype(f32).reshape(1, H), norm_w.astype(f32).reshape(1, GDN_HEAD_DIM))


def nsa_mixer(q, kv, gate_logits, q_norm, k_norm, cmp_pe, cmp_w1, cmp_w2, cos, sin):
    Bsz, S, _ = q.shape
    f32 = jnp.float32
    G, R, dh = NSA_KV_HEADS, NSA_HEADS // NSA_KV_HEADS, NSA_HEAD_DIM
    scale = dh ** -0.5
    q = rope(rms_norm(q.reshape(Bsz, S, NSA_HEADS, dh), q_norm), cos, sin)
    kc_raw, vc_raw, ks, vs, kw, vw = (t.reshape(Bsz, S, G, dh) for t in jnp.split(kv, 6, axis=-1))
    ks = rope(rms_norm(ks, k_norm[1]), cos, sin)
    kw = rope(rms_norm(kw, k_norm[2]), cos, sin)

    n_cmp = (S - CMP_LEN) // CMP_STRIDE + 1
    win_idx = jnp.arange(n_cmp)[:, None] * CMP_STRIDE + jnp.arange(CMP_LEN)[None, :]
    cmp_end = win_idx[:, -1]

    def compress(t, pe, w1, w2):
        blocks = t[:, win_idx] + pe[:, None, :]
        blocks = jnp.moveaxis(blocks, 3, 2).reshape(Bsz, n_cmp, G, CMP_LEN * dh)
        return jax.nn.gelu(blocks @ w1) @ w2

    cos_c, sin_c = rope_tables(cmp_end, dh)
    k_cmp = rope(rms_norm(compress(kc_raw, cmp_pe[0], cmp_w1[0], cmp_w2[0]), k_norm[0]), cos_c, sin_c)
    v_cmp = compress(vc_raw, cmp_pe[1], cmp_w1[1], cmp_w2[1])

    n_slc = S // SLC_BLOCK
    n_top = min(SLC_TOPK, n_slc)
    c0 = jnp.arange(n_cmp) * CMP_STRIDE
    s0 = jnp.arange(n_slc) * SLC_BLOCK
    overlap = jnp.clip(jnp.minimum(c0[:, None] + CMP_LEN, s0[None, :] + SLC_BLOCK)
                       - jnp.maximum(c0[:, None], s0[None, :]), 0, None).astype(f32) / CMP_LEN
    gates = jax.nn.sigmoid(gate_logits.astype(f32)).reshape(Bsz, S, G, R * 3).transpose(0, 2, 3, 1)

    bf16 = jnp.bfloat16
    n_cmp_pad = S // CMP_STRIDE
    pad_cmp = lambda t: jnp.pad(t.astype(bf16), ((0, 0), (0, n_cmp_pad - n_cmp), (0, 0), (0, 0)))
    keys = lambda t: t.astype(bf16).transpose(0, 2, 1, 3)
    vals = lambda t: t.astype(bf16).transpose(0, 2, 3, 1)
    q_heads = (q * scale).astype(bf16).reshape(Bsz, S, G, R, dh).transpose(0, 2, 3, 1, 4)
    overlap_t = jnp.pad(overlap.T, ((0, 0), (0, n_cmp_pad - n_cmp))).astype(bf16)
    expand_t = ((jnp.arange(S) // SLC_BLOCK)[:, None] == jnp.arange(n_slc)[None, :]).astype(bf16)
    return _nsa_attention(q_heads, keys(pad_cmp(k_cmp)), vals(pad_cmp(v_cmp)), keys(ks), vals(vs), keys(kw), vals(vw),
                          gates, overlap_t, expand_t, n_top)


NSA_R = NSA_HEADS // NSA_KV_HEADS
SLC_TILE = 1024
MASK_NEG = -1e30


def _dot_nt(a, b):
    return lax.dot_general(a, b, (((1,), (1,)), ((), ())), preferred_element_type=jnp.float32)


def _nsa_body(q_ref, kc_ref, vct_ref, ks_ref, vst_ref, kw_ref, vwt_ref, gate_ref, ovt_ref, expt_ref, o_ref,
              *, n_top):
    f32, bf16 = jnp.float32, jnp.bfloat16
    R, QB, dh = q_ref.shape
    S = ks_ref.shape[0]
    n_slc, n_cmp_pad = ovt_ref.shape
    M = R * QB
    t0 = pl.program_id(2) * QB
    q = q_ref[...].reshape(M, dh)
    per_head = lambda t: jnp.concatenate([t] * R, axis=1)

    band = S if S < WINDOW + QB else WINDOW + QB
    start = pl.multiple_of(jnp.maximum(t0 + QB - band, 0), QB)
    s = _dot_nt(kw_ref[pl.ds(start, band), :], q)
    diff = (t0 - start) + (lax.broadcasted_iota(jnp.int32, (band, QB), 1) - lax.broadcasted_iota(jnp.int32, (band, QB), 0))
    s = s + per_head(jnp.where((diff >= 0) & (diff < WINDOW), 0.0, -jnp.inf))
    p = jnp.exp(s - jnp.max(s, axis=0, keepdims=True))
    o_win = jnp.dot(vwt_ref[:, pl.ds(start, band)], p.astype(bf16), preferred_element_type=f32)
    o_win = o_win / jnp.sum(p, axis=0, keepdims=True)

    s = _dot_nt(kc_ref[...], q)
    n_idx = lax.broadcasted_iota(jnp.int32, (n_cmp_pad, QB), 0)
    t_idx = t0 + lax.broadcasted_iota(jnp.int32, (n_cmp_pad, QB), 1)
    valid = (n_idx * CMP_STRIDE + (CMP_LEN - 1) <= t_idx) & (n_idx < n_cmp_pad - 1)
    s = s + per_head(jnp.where(valid, 0.0, -jnp.inf))
    m = jnp.max(s, axis=0, keepdims=True)
    m = jnp.where(jnp.isfinite(m), m, 0.0)
    p = jnp.exp(s - m)
    p = p / jnp.maximum(jnp.sum(p, axis=0, keepdims=True), jnp.finfo(f32).tiny)
    o_cmp = jnp.dot(vct_ref[...], p.astype(bf16), preferred_element_type=f32)

    p_sum = p[:, 0:QB]
    for r in range(1, R):
        p_sum = p_sum + p[:, r * QB:(r + 1) * QB]
    imp = sum(jnp.dot(ovt_ref[...], t, preferred_element_type=f32) for t in _bf16_terms(p_sum, 3))
    j_idx = lax.broadcasted_iota(jnp.int32, (n_slc, QB), 0)
    t_lane = t0 + lax.broadcasted_iota(jnp.int32, (n_slc, QB), 1)
    cur = lax.shift_right_logical(t_lane, 6)
    forced = (j_idx == 0) | (j_idx == cur) | (j_idx == cur - 1)
    started = j_idx * SLC_BLOCK <= t_lane
    imp = jnp.where(forced, BIG, jnp.where(started, imp, -BIG))
    rank = jnp.zeros((n_slc, QB), f32)
    for jp in range(n_slc):
        row = imp[jp:jp + 1, :]
        ahead = (row > imp) | ((row == imp) & (j_idx > jp))
        rank = rank + jnp.where(ahead, 1.0, 0.0)
    sel = jnp.where(rank < n_top, 1.0, 0.0).astype(bf16)
    k_rel = lax.broadcasted_iota(jnp.int32, (SLC_TILE, QB), 0)
    t_q = t0 + lax.broadcasted_iota(jnp.int32, (SLC_TILE, QB), 1)

    def slc_step(kt, carry):
        m, l, acc = carry
        k0 = pl.multiple_of(kt * SLC_TILE, SLC_TILE)
        s = _dot_nt(ks_ref[pl.ds(k0, SLC_TILE), :], q)
        sel_keys = jnp.dot(expt_ref[pl.ds(k0, SLC_TILE), :], sel, preferred_element_type=f32)
        bias = jnp.where((sel_keys > 0.5) & (k_rel + k0 <= t_q), 0.0, MASK_NEG)
        s = s + per_head(bias)
        m_new = jnp.maximum(m, jnp.max(s, axis=0, keepdims=True))
        alpha = jnp.exp(m - m_new)
        p = jnp.exp(s - m_new)
        l = alpha * l + jnp.sum(p, axis=0, keepdims=True)
        acc = alpha * acc + jnp.dot(vst_ref[:, pl.ds(k0, SLC_TILE)], p.astype(bf16), preferred_element_type=f32)
        return m_new, l, acc

    n_tiles = (t0 + QB - 1) // SLC_TILE + 1
    init = (jnp.full((1, M), MASK_NEG, f32), jnp.zeros((1, M), f32), jnp.zeros((dh, M), f32))
    _, l, acc = lax.fori_loop(0, n_tiles, slc_step, init)
    o_slc = acc / l

    gate = gate_ref[...]
    g_cmp, g_slc, g_win = (jnp.concatenate([gate[3 * r + c:3 * r + c + 1, :] for r in range(R)], axis=1)
                           for c in range(3))
    o_t = g_cmp * o_cmp + g_slc * o_slc + g_win * o_win
    o_ref[...] = jnp.concatenate([o_t[:, r * QB:(r + 1) * QB].T for r in range(R)], axis=1)


def _nsa_attention(q, kc, vct, ks, vst, kw, vwt, gates, overlap_t, expand_t, n_top):
    Bsz, G, R, S, dh = q.shape
    n_cmp_pad = kc.shape[2]
    assert S % SLC_TILE == 0 and SLC_TILE % Q_BLOCK == 0, (S, SLC_TILE, Q_BLOCK)
    keys = lambda n: pl.BlockSpec((None, None, n, dh), lambda b, g, i: (b, g, 0, 0))
    vals = lambda n: pl.BlockSpec((None, None, dh, n), lambda b, g, i: (b, g, 0, 0))
    const = lambda a: pl.BlockSpec(a.shape, lambda b, g, i: (0, 0))
    return pl.pallas_call(
        functools.partial(_nsa_body, n_top=n_top),
        grid=(Bsz, G, S // Q_BLOCK),
        in_specs=[pl.BlockSpec((None, None, R, Q_BLOCK, dh), lambda b, g, i: (b, g, 0, i, 0)),
                  keys(n_cmp_pad), vals(n_cmp_pad), keys(S), vals(S), keys(S), vals(S),
                  pl.BlockSpec((None, None, 3 * R, Q_BLOCK), lambda b, g, i: (b, g, 0, i)),
                  const(overlap_t), const(expand_t)],
        out_specs=pl.BlockSpec((None, Q_BLOCK, R * dh), lambda b, g, i: (b, i, g)),
        out_shape=jax.ShapeDtypeStruct((Bsz, S, G * R * dh), jnp.float32),
        compiler_params=pltpu.CompilerParams(dimension_semantics=("arbitrary", "arbitrary", "arbitrary"),
                                             vmem_limit_bytes=48 * 1024 * 1024),
        name="nsa_attention",
    )(q, kc, vct, ks, vst, kw, vwt, gates, overlap_t, expand_t)


def hybrid_mixer(x, ln_mix, w_in, lam_re, lam_im, b_re, b_im, c_re, c_im, s5_d, s5_log_dt, glu_w, glu_b,
                 s5_out_norm, conv_w, a_log, dt_bias, gdn_norm_w, q_norm, k_norm, cmp_pe, cmp_w1,
                 cmp_w2, cos, sin):
    Bsz, S, D = x.shape
    proj = _in_proj(x.reshape(-1, D), ln_mix, _in_proj_weight(w_in)).reshape(Bsz, S, PROJ_W)
    y_s5 = s5_mixer_normed(proj, lam_re, lam_im, b_re, b_im, c_re, c_im, s5_d, s5_log_dt, glu_w, glu_b, s5_out_norm)
    y_gdn = gdn_mixer(proj, conv_w, a_log, dt_bias, gdn_norm_w)
    y_nsa = nsa_mixer(proj[..., PROJ_Q:PROJ_Q + NSA_WIDTH], proj[..., PROJ_KV:PROJ_KV + 6 * NSA_KV_WIDTH],
                      proj[..., PROJ_GATES:PROJ_GATES + 3 * NSA_HEADS], q_norm, k_norm, cmp_pe, cmp_w1, cmp_w2,
                      cos, sin)
    return y_s5, y_gdn, y_nsa


def hier_moe(x, h, logits, layer, w_gate, w_up, w_down):
    T, D = x.shape
    f32 = jnp.float32
    g_logits = logits[:, :N_GROUPS]
    grp = jnp.argmax(g_logits, axis=-1)
    p_grp = jnp.take_along_axis(jax.nn.softmax(g_logits, axis=-1), grp[:, None], axis=-1)
    e_logits = logits[:, N_GROUPS:N_ROUTER].reshape(T, N_GROUPS, EXPERTS_PER_GROUP)
    e_logits = jnp.take_along_axis(e_logits, grp[:, None, None], axis=1)[:, 0]
    local = jnp.arange(EXPERTS_PER_GROUP, dtype=jnp.int32)[None, :]
    rest, top_logit, top_local = e_logits, [], []
    for _ in range(TOP_K):
        idx = jnp.argmax(rest, axis=-1).astype(jnp.int32)
        top_logit.append(jnp.take_along_axis(rest, idx[:, None], axis=-1)[:, 0])
        top_local.append(idx)
        rest = jnp.where(local == idx[:, None], -jnp.inf, rest)
    top_logit, top_local = jnp.stack(top_logit, axis=-1), jnp.stack(top_local, axis=-1)
    weights = jax.nn.softmax(top_logit, axis=-1) * p_grp
    experts = grp[:, None].astype(jnp.int32) * EXPERTS_PER_GROUP + top_local
    n_assign = T * TOP_K
    e_flat = experts.reshape(-1)
    tok_flat = jnp.repeat(jnp.arange(T, dtype=jnp.int32), TOP_K)
    onehot = (e_flat[:, None] == jnp.arange(N_EXPERTS, dtype=jnp.int32)[None, :]).astype(jnp.int32)
    running = jnp.cumsum(onehot, axis=0)
    counts = running[-1]
    padded = (counts + MOE_ROWS - 1) // MOE_ROWS * MOE_ROWS
    pad_start = jnp.cumsum(padded) - padded
    dest = (pad_start[e_flat] + jnp.take_along_axis(running, e_flat[:, None], axis=1)[:, 0] - 1).astype(jnp.int32)
    n_blk = (n_assign + MOE_ROWS - 1) // MOE_ROWS + N_EXPERTS
    cap = n_blk * MOE_ROWS
    slot_tok = jnp.full((cap,), T, jnp.int32).at[dest].set(tok_flat)
    blk_expert = jnp.minimum(jnp.searchsorted(jnp.cumsum(padded), jnp.arange(n_blk) * MOE_ROWS, side='right'),
                             N_EXPERTS - 1).astype(jnp.int32)
    n_used = (jnp.sum(padded) // MOE_ROWS).astype(jnp.int32).reshape(1)
    x_pad = jnp.concatenate([x, jnp.zeros((1, D), x.dtype)], axis=0)
    y = _moe_ffn(blk_expert, n_used, x_pad[slot_tok], layer, w_gate.astype(f32), w_up.astype(f32),
                 w_down.astype(f32))
    return _moe_combine(dest, h, weights, y)


CMB_ROWS = 128


def _moe_combine_body(d_ref, h_ref, w_ref, y_hbm, o_ref, buf, sem):
    i = pl.program_id(0)
    n = pl.num_programs(0)
    rows = h_ref.shape[0]

    def row_copy(step, r, k, slot):
        src = y_hbm.at[pl.ds(d_ref[(step * rows + r) * TOP_K + k], 1), :]
        return pltpu.make_async_copy(src, buf.at[slot, k, pl.ds(r, 1), :], sem.at[slot])

    def start_step(step, slot):
        def body(r, c):
            for k in range(TOP_K):
                row_copy(step, r, k, slot).start()
            return c
        lax.fori_loop(0, rows, body, 0)

    @pl.when(i == 0)
    def _():
        start_step(0, 0)

    @pl.when(i + 1 < n)
    def _():
        start_step(i + 1, (i + 1) & 1)

    slot = i & 1

    def wait_body(r, c):
        for k in range(TOP_K):
            row_copy(i, r, k, slot).wait()
        return c
    lax.fori_loop(0, rows, wait_body, 0)

    out = h_ref[...]
    for k in range(TOP_K):
        out = out + w_ref[:, k:k + 1] * buf[slot, k]
    o_ref[...] = out


def _moe_combine(dest, h, weights, y):
    T, D = h.shape
    rows = min(CMB_ROWS, T)
    return pl.pallas_call(
        _moe_combine_body,
        grid_spec=pltpu.PrefetchScalarGridSpec(
            num_scalar_prefetch=1, grid=(T // rows,),
            in_specs=[pl.BlockSpec((rows, D), lambda i, d: (i, 0)),
                      pl.BlockSpec((rows, TOP_K), lambda i, d: (i, 0)),
                      pl.BlockSpec(memory_space=pl.ANY)],
            out_specs=pl.BlockSpec((rows, D), lambda i, d: (i, 0)),
            scratch_shapes=[pltpu.VMEM((2, TOP_K, rows, D), jnp.float32), pltpu.SemaphoreType.DMA((2,))]),
        out_shape=jax.ShapeDtypeStruct((T, D), jnp.float32),
        compiler_params=pltpu.CompilerParams(dimension_semantics=("arbitrary",),
                                             vmem_limit_bytes=32 * 1024 * 1024),
        name="moe_combine",
    )(dest, h, weights, y)


MOE_ROWS = 256


def _moe_ffn_body(be_ref, nu_ref, x_ref, wg_ref, wu_ref, wd_ref, o_ref, wg_bf, wu_bf, wd_bf):
    i = pl.program_id(0)
    used = i < nu_ref[0]

    @pl.when((i == 0) | (be_ref[i] != be_ref[jnp.maximum(i - 1, 0)]))
    def _():
        wg_bf[...] = wg_ref[...].astype(jnp.bfloat16)
        wu_bf[...] = wu_ref[...].astype(jnp.bfloat16)
        wd_bf[...] = wd_ref[...].astype(jnp.bfloat16)

    @pl.when(used)
    def _():
        x = x_ref[...]
        gate = jnp.dot(x, wg_bf[...], preferred_element_type=jnp.float32)
        up = jnp.dot(x, wu_bf[...], preferred_element_type=jnp.float32)
        hid = (jax.nn.silu(gate) * up).astype(jnp.bfloat16)
        o_ref[...] = jnp.dot(hid, wd_bf[...], preferred_element_type=jnp.float32)

    @pl.when(jnp.logical_not(used))
    def _():
        o_ref[...] = jnp.zeros_like(o_ref)


def _moe_ffn(blk_expert, n_used, xs, layer, w_gate, w_up, w_down):
    cap, D = xs.shape
    n_blk = cap // MOE_ROWS
    De = w_gate.shape[-1]
    rows = lambda n: pl.BlockSpec((MOE_ROWS, n), lambda i, be, nu: (i, 0))
    return pl.pallas_call(
        _moe_ffn_body,
        grid_spec=pltpu.PrefetchScalarGridSpec(
            num_scalar_prefetch=2, grid=(n_blk,),
            in_specs=[rows(D),
                      pl.BlockSpec((None, None, D, De), lambda i, be, nu: (layer, be[i], 0, 0)),
                      pl.BlockSpec((None, None, D, De), lambda i, be, nu: (layer, be[i], 0, 0)),
                      pl.BlockSpec((None, None, De, D), lambda i, be, nu: (layer, be[i], 0, 0))],
            out_specs=rows(D),
            scratch_shapes=[pltpu.VMEM((D, De), jnp.bfloat16), pltpu.VMEM((D, De), jnp.bfloat16),
                            pltpu.VMEM((De, D), jnp.bfloat16)]),
        out_shape=jax.ShapeDtypeStruct((cap, D), jnp.float32),
        compiler_params=pltpu.CompilerParams(dimension_semantics=("arbitrary",),
                                             vmem_limit_bytes=56 * 1024 * 1024),
        name="moe_ffn",
    )(blk_expert, n_used, xs, w_gate, w_up, w_down)


def kernel(x, ln_mix, w_in, s5_lambda_re, s5_lambda_im, s5_b_re, s5_b_im, s5_c_re, s5_c_im, s5_d,
           s5_log_dt, s5_glu_w, s5_glu_b, s5_out_norm, gdn_conv_w, gdn_a_log, gdn_dt_bias, gdn_norm_w,
           nsa_q_norm, nsa_k_norm, nsa_cmp_pe, nsa_cmp_w1, nsa_cmp_w2, nsa_out_norm, w_out, ln_ffn,
           moe_group_w, moe_group_b, moe_expert_w, moe_expert_b, moe_w_gate, moe_w_up, moe_w_down):
    Bsz, S, D = x.shape
    cos, sin = rope_tables(jnp.arange(S), NSA_HEAD_DIM)
    for l in range(DEPTH):
        y_s5, y_gdn, y_nsa = hybrid_mixer(
            x, ln_mix[l], w_in[l], s5_lambda_re[l], s5_lambda_im[l], s5_b_re[l], s5_b_im[l], s5_c_re[l], s5_c_im[l],
            s5_d[l], s5_log_dt[l], s5_glu_w[l], s5_glu_b[l], s5_out_norm[l], gdn_conv_w[l], gdn_a_log[l],
            gdn_dt_bias[l], gdn_norm_w[l], nsa_q_norm[l], nsa_k_norm[l], nsa_cmp_pe[l], nsa_cmp_w1[l], nsa_cmp_w2[l],
            cos, sin)
        router_w = jnp.concatenate([moe_group_w[l], moe_expert_w[l]], axis=-1)
        router_b = jnp.concatenate([moe_group_b[l], moe_expert_b[l]], axis=-1)
        h, a, logits = _out_proj(x.reshape(-1, D), y_s5.reshape(-1, S5_WIDTH), y_gdn.reshape(-1, GDN_WIDTH),
                                 y_nsa.reshape(-1, NSA_WIDTH), nsa_out_norm[l], w_out[l], ln_ffn[l], router_w, router_b)
        x = hier_moe(a, h, logits, l, moe_w_gate, moe_w_up, moe_w_down).reshape(Bsz, S, D)
    return x
```

```python
import functools
import math
import jax, jax.numpy as jnp
from jax import lax
import numpy as np
from jax.experimental import pallas as pl
from jax.experimental.pallas import tpu as pltpu

D_MODEL = 2048
BATCH = 4
SEQ = 4096
DEPTH = 2

EPS = 1e-6
S5_WIDTH = D_MODEL // 4
S5_GROUP = 16
S5_GROUPS = S5_WIDTH // S5_GROUP
S5_STATE = 64
GDN_HEAD_DIM = 128
GDN_WIDTH = D_MODEL // 4
GDN_HEADS = GDN_WIDTH // GDN_HEAD_DIM
GDN_CONV = 4
GDN_CHUNK = 64
NSA_HEAD_DIM = 64
NSA_WIDTH = D_MODEL // 2
NSA_HEADS = NSA_WIDTH // NSA_HEAD_DIM
NSA_KV_HEADS = 4
NSA_KV_WIDTH = NSA_KV_HEADS * NSA_HEAD_DIM
CMP_LEN = 32
CMP_STRIDE = 16
CMP_HIDDEN = 256
SLC_BLOCK = 64
SLC_TOPK = 16
WINDOW = 512
Q_BLOCK = 128
ROPE_THETA = 10000.0
BIG = 1e9
N_GROUPS = 4
EXPERTS_PER_GROUP = 8
N_EXPERTS = N_GROUPS * EXPERTS_PER_GROUP
TOP_K = 2
D_EXPERT = 512
MOE_BLOCK = 128
IN_SIZES = (S5_WIDTH, 3 * GDN_WIDTH, GDN_WIDTH, GDN_HEADS, GDN_HEADS, NSA_WIDTH, 6 * NSA_KV_WIDTH, 3 * NSA_HEADS)
D_IN = sum(IN_SIZES)


PROJ_QKV, PROJ_KV, PROJ_Q, PROJ_U, PROJ_Z, PROJ_SMALL = 0, 1536, 3072, 4096, 4608, 5120
PROJ_SMALL_W = 512
PROJ_W = PROJ_SMALL + PROJ_SMALL_W
PROJ_GATES = PROJ_SMALL + 2 * GDN_HEADS
LANES = 128
IN_ROWS, IN_COLS = 1024, 512
OUT_ROWS = 256
N_ROUTER = N_GROUPS + N_EXPERTS


def _in_proj_weight(w_in):
    u, qkv, z, ga, gb, q, kv, gl = jnp.split(w_in, np.cumsum(IN_SIZES)[:-1].tolist(), axis=-1)
    small = jnp.concatenate([ga, gb, gl], axis=-1)
    small = jnp.pad(small, ((0, 0), (0, PROJ_SMALL_W - small.shape[-1])))
    return jnp.concatenate([qkv, kv, q, u, z, small], axis=-1)


def _in_proj_body(x_ref, g_ref, w_ref, o_ref, a_ref):
    @pl.when(pl.program_id(1) == 0)
    def _():
        x = x_ref[...]
        a_ref[...] = (x * lax.rsqrt(jnp.mean(x * x, axis=-1, keepdims=True) + EPS) * g_ref[...]).astype(jnp.bfloat16)

    o_ref[...] = jnp.dot(a_ref[...], w_ref[...].astype(jnp.bfloat16), preferred_element_type=jnp.float32)


def _in_proj(x, gain, w):
    T, D = x.shape
    N = w.shape[1]
    tm = min(IN_ROWS, T)
    return pl.pallas_call(
        _in_proj_body,
        grid=(T // tm, N // IN_COLS),
        in_specs=[pl.BlockSpec((tm, D), lambda i, j: (i, 0)),
                  pl.BlockSpec((1, D), lambda i, j: (0, 0)),
                  pl.BlockSpec((D, IN_COLS), lambda i, j: (0, j))],
        out_specs=pl.BlockSpec((tm, IN_COLS), lambda i, j: (i, j)),
        out_shape=jax.ShapeDtypeStruct((T, N), jnp.float32),
        scratch_shapes=[pltpu.VMEM((tm, D), jnp.bfloat16)],
        compiler_params=pltpu.CompilerParams(dimension_semantics=("arbitrary", "arbitrary"),
                                             vmem_limit_bytes=48 * 1024 * 1024),
        name="in_proj",
    )(x, gain.astype(jnp.float32).reshape(1, D), w)


def _out_proj_body(x_ref, s5_ref, gdn_ref, nsa_ref, ng_ref, w_ref, fg_ref, rw_ref, rb_ref, h_ref, a_ref, lg_ref):
    f32, bf16 = jnp.float32, jnp.bfloat16
    nsa = nsa_ref[...]
    nsa = nsa * lax.rsqrt(jnp.mean(nsa * nsa, axis=-1, keepdims=True) + EPS) * ng_ref[...]
    k0, k1 = S5_WIDTH, S5_WIDTH + GDN_WIDTH
    y = (jnp.dot(s5_ref[...].astype(bf16), w_ref[0:k0, :], preferred_element_type=f32)
         + jnp.dot(gdn_ref[...].astype(bf16), w_ref[k0:k1, :], preferred_element_type=f32)
         + jnp.dot(nsa.astype(bf16), w_ref[k1:, :], preferred_element_type=f32))
    h = x_ref[...] + y
    h_ref[...] = h
    a = h * lax.rsqrt(jnp.mean(h * h, axis=-1, keepdims=True) + EPS) * fg_ref[...]
    a_ref[...] = a
    lg_ref[...] = jnp.dot(a.astype(bf16), rw_ref[...], preferred_element_type=f32) + rb_ref[...]


def _out_proj(x, y_s5, y_gdn, y_nsa, nsa_gain, w_out, ffn_gain, router_w, router_b):
    T, D = x.shape
    tm = min(OUT_ROWS, T)
    f32, bf16 = jnp.float32, jnp.bfloat16
    rw = jnp.pad(router_w.astype(bf16), ((0, 0), (0, LANES - N_ROUTER)))
    rb = jnp.pad(router_b.astype(f32), (0, LANES - N_ROUTER)).reshape(1, LANES)
    rows = lambda n: pl.BlockSpec((tm, n), lambda i: (i, 0))
    full = lambda *shape: pl.BlockSpec(shape, lambda i: (0,) * len(shape))
    return pl.pallas_call(
        _out_proj_body,
        grid=(T // tm,),
        in_specs=[rows(D), rows(S5_WIDTH), rows(GDN_WIDTH), rows(NSA_WIDTH), full(1, NSA_WIDTH), full(D, D),
                  full(1, D), full(D, LANES), full(1, LANES)],
        out_specs=[rows(D), rows(D), rows(LANES)],
        out_shape=[jax.ShapeDtypeStruct((T, D), f32), jax.ShapeDtypeStruct((T, D), f32),
                   jax.ShapeDtypeStruct((T, LANES), f32)],
        compiler_params=pltpu.CompilerParams(dimension_semantics=("arbitrary",),
                                             vmem_limit_bytes=48 * 1024 * 1024),
        name="out_proj",
    )(x, y_s5, y_gdn, y_nsa, nsa_gain.astype(f32).reshape(1, NSA_WIDTH), w_out.astype(bf16),
      ffn_gain.astype(f32).reshape(1, D), rw, rb)


def rms_norm(x, gain):
    xf = x.astype(jnp.float32)
    y = xf * lax.rsqrt(jnp.mean(xf * xf, axis=-1, keepdims=True) + EPS)
    return (y * gain.astype(jnp.float32)).astype(x.dtype)


def l2_norm(x):
    return x * lax.rsqrt(jnp.sum(x * x, axis=-1, keepdims=True) + EPS)


def rope_tables(pos, dim):
    inv_freq = ROPE_THETA ** (-jnp.arange(0, dim, 2, dtype=jnp.float32) / dim)
    ang = pos.astype(jnp.float32)[:, None] * inv_freq[None, :]
    return jnp.cos(ang)[:, None, :], jnp.sin(ang)[:, None, :]


def rope(x, cos, sin):
    x1, x2 = jnp.split(x, 2, axis=-1)
    return jnp.concatenate([x1 * cos - x2 * sin, x2 * cos + x1 * sin], axis=-1).astype(x.dtype)


def masked_softmax(s, mask):
    s = jnp.where(mask, s, -jnp.inf)
    m = jnp.max(s, axis=-1, keepdims=True)
    m = jnp.where(jnp.isfinite(m), m, 0.0)
    p = jnp.exp(s - m)
    return p / jnp.maximum(jnp.sum(p, axis=-1, keepdims=True), jnp.finfo(jnp.float32).tiny)


S5_OCT = 8
S5_NOCT = S5_GROUPS // S5_OCT
S5_OCT_IN = S5_OCT * S5_GROUP
S5_OCT_ST = S5_OCT * S5_STATE
SUBLANES = 8
S5_SCAN_SHIFTS = (1, 2, 4)


def _s5_constants(lam_re, lam_im, b_re, b_im, c_re, c_im, log_dt):
    f32 = jnp.float32
    lr, li = lam_re.astype(f32), lam_im.astype(f32)
    dt = jnp.exp(log_dt.astype(f32))[:, None]

    def lam_pow(k):
        mag = jnp.exp(k * lr * dt)
        return mag * jnp.cos(k * li * dt), mag * jnp.sin(k * li * dt)

    ar, ai = lam_pow(1.0)
    nr, ni = ar - 1.0, ai
    den = lr * lr + li * li
    qr, qi = (nr * lr + ni * li) / den, (ni * lr - nr * li) / den
    br, bi = b_re.astype(f32), b_im.astype(f32)
    bbr = qr[:, :, None] * br - qi[:, :, None] * bi
    bbi = qr[:, :, None] * bi + qi[:, :, None] * br
    eye = jnp.eye(S5_OCT, dtype=f32)

    def blk_b(t):
        t = t.reshape(S5_NOCT, S5_OCT, S5_STATE, S5_GROUP)
        return jnp.einsum('qgph,gk->qghkp', t, eye).reshape(S5_NOCT, S5_OCT_IN, S5_OCT_ST)

    def blk_c(t):
        t = t.reshape(S5_NOCT, S5_OCT, S5_GROUP, S5_STATE)
        return jnp.einsum('qghp,gk->qgpkh', t, eye).reshape(S5_NOCT, S5_OCT_ST, S5_OCT_IN)

    b_mat = jnp.stack([blk_b(bbr), blk_b(bbi)]).astype(jnp.bfloat16)
    c_mat = jnp.stack([blk_c(c_re.astype(f32)), blk_c(-c_im.astype(f32))]).astype(jnp.bfloat16)
    rows = jnp.arange(SUBLANES, dtype=f32)[:, None, None]
    tiles = []
    for k in S5_SCAN_SHIFTS:
        pr, pi = lam_pow(jnp.full_like(rows, float(k)))
        keep = rows >= k
        tiles.append(jnp.stack([jnp.where(keep, pr, 0.0), jnp.where(keep, pi, 0.0)]))
    pr, pi = lam_pow(rows + 1.0)
    tiles.append(jnp.stack([pr, pi]))
    a_mat = jnp.stack(tiles).reshape(len(tiles), 2, SUBLANES, S5_GROUPS * S5_STATE)
    return b_mat, c_mat, a_mat


def _s5_body(u_ref, b_ref, c_ref, a_ref, d_ref, gw_ref, gb_ref, gain_ref, o_ref, st_ref, xr_ref, xi_ref):
    bf16 = jnp.bfloat16
    n_tiles = u_ref.shape[0] // SUBLANES

    @pl.when(pl.program_id(1) == 0)
    def _():
        st_ref[...] = jnp.zeros_like(st_ref)

    u = u_ref[...]
    ys = []
    for q in range(S5_NOCT):
        lanes = slice(q * S5_OCT_ST, (q + 1) * S5_OCT_ST)
        uq = u[:, q * S5_OCT_IN:(q + 1) * S5_OCT_IN].astype(bf16)
        xr_ref[...] = jnp.dot(uq, b_ref[0, q], preferred_element_type=jnp.float32)
        xi_ref[...] = jnp.dot(uq, b_ref[1, q], preferred_element_type=jnp.float32)

        def tile_step(i, carry, lanes=lanes):
            cr, ci = carry
            r0 = pl.multiple_of(i * SUBLANES, SUBLANES)
            xr = xr_ref[pl.ds(r0, SUBLANES), :]
            xi = xi_ref[pl.ds(r0, SUBLANES), :]
            for lvl, k in enumerate(S5_SCAN_SHIFTS):
                pr, pi = a_ref[lvl, 0, :, lanes], a_ref[lvl, 1, :, lanes]
                sr, si = pltpu.roll(xr, k, 0), pltpu.roll(xi, k, 0)
                xr, xi = xr + (pr * sr - pi * si), xi + (pr * si + pi * sr)
            pr, pi = a_ref[len(S5_SCAN_SHIFTS), 0, :, lanes], a_ref[len(S5_SCAN_SHIFTS), 1, :, lanes]
            cr = jnp.broadcast_to(cr, xr.shape)
            ci = jnp.broadcast_to(ci, xi.shape)
            xr, xi = xr + (pr * cr - pi * ci), xi + (pr * ci + pi * cr)
            xr_ref[pl.ds(r0, SUBLANES), :] = xr
            xi_ref[pl.ds(r0, SUBLANES), :] = xi
            return xr[SUBLANES - 1:, :], xi[SUBLANES - 1:, :]

        cr, ci = lax.fori_loop(0, n_tiles, tile_step, (st_ref[0, :, lanes], st_ref[1, :, lanes]))
        st_ref[0, :, lanes] = cr
        st_ref[1, :, lanes] = ci
        ys.append(jnp.dot(xr_ref[...].astype(bf16), c_ref[0, q], preferred_element_type=jnp.float32)
                  + jnp.dot(xi_ref[...].astype(bf16), c_ref[1, q], preferred_element_type=jnp.float32))
    y = jnp.concatenate(ys, axis=1) + d_ref[...] * u
    y = jax.nn.gelu(y)
    z = jnp.dot(y.astype(bf16), gw_ref[...], preferred_element_type=jnp.float32) + gb_ref[...]
    y = y * jax.nn.sigmoid(z)
    y = y * lax.rsqrt(jnp.mean(y * y, axis=-1, keepdims=True) + EPS)
    o_ref[...] = y * gain_ref[...]


def s5_mixer_normed(proj, lam_re, lam_im, b_re, b_im, c_re, c_im, d_skip, log_dt, glu_w, glu_b, out_norm, chunk=512):
    Bsz, S, _ = proj.shape
    W = S5_WIDTH
    f32 = jnp.float32
    L = min(chunk, S)
    b_mat, c_mat, a_mat = _s5_constants(lam_re, lam_im, b_re, b_im, c_re, c_im, log_dt)
    n_state = S5_GROUPS * S5_STATE
    full = lambda *shape: pl.BlockSpec(shape, lambda b, c: (0,) * len(shape))
    return pl.pallas_call(
        _s5_body,
        grid=(Bsz, S // L),
        in_specs=[pl.BlockSpec((None, L, W), lambda b, c: (b, c, PROJ_U // W)),
                  full(*b_mat.shape), full(*c_mat.shape), full(*a_mat.shape),
                  full(1, W), full(W, W), full(1, W), full(1, W)],
        out_specs=pl.BlockSpec((None, L, W), lambda b, c: (b, c, 0)),
        out_shape=jax.ShapeDtypeStruct((Bsz, S, W), f32),
        scratch_shapes=[pltpu.VMEM((2, 1, n_state), f32),
                        pltpu.VMEM((L, S5_OCT_ST), f32), pltpu.VMEM((L, S5_OCT_ST), f32)],
        compiler_params=pltpu.CompilerParams(dimension_semantics=("arbitrary", "arbitrary"),
                                             vmem_limit_bytes=48 * 1024 * 1024),
        name="s5_mixer",
    )(proj, b_mat, c_mat, a_mat, d_skip.astype(f32).reshape(1, W), glu_w.astype(jnp.bfloat16),
      glu_b.astype(f32).reshape(1, W), out_norm.astype(f32).reshape(1, W))


GDN_ROWS = 256
GDN_PACK = GDN_HEADS * GDN_CHUNK


def _bf16_terms(x, n):
    terms, rem = [], x
    for i in range(n):
        t = rem.astype(jnp.bfloat16)
        terms.append(t)
        if i + 1 < n:
            rem = rem - t.astype(jnp.float32)
    return terms


def _dot_exact_lhs(lhs01, x):
    lhs = lhs01.astype(jnp.bfloat16)
    return sum(jnp.dot(lhs, t, preferred_element_type=jnp.float32) for t in _bf16_terms(x, 3))


def _gdn_packed_unit_lower_inverse(a_mats, row, col, blk_diag):
    f32 = jnp.float32

    def mm(x, y):
        xh, xl = _bf16_terms(x, 2)
        yh, yl = (jnp.where(blk_diag, jnp.concatenate([t] * GDN_HEADS, axis=0), 0.0) for t in _bf16_terms(y, 2))
        return (jnp.dot(xh, yh, preferred_element_type=f32)
                + (jnp.dot(xh, yl, preferred_element_type=f32) + jnp.dot(xl, yh, preferred_element_type=f32)))

    same16 = (row >> 4) == (col >> 4)
    same32 = (row >> 5) == (col >> 5)
    eye = jnp.where(row == col, 1.0, 0.0)
    d1 = [jnp.where(same16, a, 0.0) for a in a_mats]
    d2 = [mm(d, d) for d in d1]
    d4 = [mm(d, d) for d in d2]
    d8 = [mm(d, d) for d in d4]
    lo = [mm(eye - a, eye + b) for a, b in zip(d1, d2)]
    hi = [mm(eye + a, eye + b) for a, b in zip(d4, d8)]
    t = [mm(a, b) for a, b in zip(lo, hi)]
    for pick in (lambda a: jnp.where(same32 & jnp.logical_not(same16), a, 0.0), lambda a: jnp.where(same32, 0.0, a)):
        mid = [mm(ti, pick(a)) for ti, a in zip(t, a_mats)]
        t = [ti - mm(m, ti) for ti, m in zip(t, mid)]
    return t


def _gdn_body(qkv_ref, z_ref, ab_ref, cw_ref, alog_ref, dtb_ref, nw_ref, o_ref, halo_ref, st_ref):
    f32, bf16 = jnp.float32, jnp.bfloat16
    L = qkv_ref.shape[0]
    C, H, dh, W = GDN_CHUNK, GDN_HEADS, GDN_HEAD_DIM, GDN_WIDTH

    @pl.when(pl.program_id(1) == 0)
    def _():
        halo_ref[...] = jnp.zeros_like(halo_ref)
        st_ref[...] = jnp.zeros_like(st_ref)

    x = qkv_ref[...]
    prev = halo_ref[...]
    row8 = lax.broadcasted_iota(jnp.int32, prev.shape, 0)
    acc = cw_ref[GDN_CONV - 1:GDN_CONV, :] * x
    for s in range(1, GDN_CONV):
        xs = pltpu.roll(x, s, 0)
        head = jnp.where(row8 < s, pltpu.roll(prev, s, 0), xs[0:SUBLANES])
        xs = jnp.concatenate([head, xs[SUBLANES:]], axis=0)
        acc = acc + cw_ref[GDN_CONV - 1 - s:GDN_CONV - s, :] * xs
    halo_ref[...] = x[L - SUBLANES:L]
    y = jax.nn.silu(acc)

    a_in, b_in = ab_ref[:, 0:H], ab_ref[:, H:2 * H]
    beta_all = jax.nn.sigmoid(b_in)
    g_all = -jnp.exp(alog_ref[...]) * jax.nn.softplus(a_in + dtb_ref[...])

    row = lax.broadcasted_iota(jnp.int32, (C, GDN_PACK), 0)
    col = lax.broadcasted_iota(jnp.int32, (C, GDN_PACK), 1) & (C - 1)
    strict = row > col
    bd_r = lax.broadcasted_iota(jnp.int32, (GDN_PACK, GDN_PACK), 0)
    bd_c = lax.broadcasted_iota(jnp.int32, (GDN_PACK, GDN_PACK), 1)
    blk_diag = (bd_r >> 6) == (bd_c >> 6)

    lr = lax.broadcasted_iota(jnp.int32, (L, L), 0)
    lc = lax.broadcasted_iota(jnp.int32, (L, L), 1)
    same_chunk = (lr >> 6) == (lc >> 6)
    tri = jnp.where(same_chunk & (lr >= lc), 1.0, 0.0)
    head_of_lane = lax.broadcasted_iota(jnp.int32, (H, GDN_PACK), 1) >> 6
    spread = jnp.where(head_of_lane == lax.broadcasted_iota(jnp.int32, (H, GDN_PACK), 0), 1.0, 0.0)
    pr = lax.broadcasted_iota(jnp.int32, (L, GDN_PACK), 0) & (C - 1)
    pc = lax.broadcasted_iota(jnp.int32, (L, GDN_PACK), 1) & (C - 1)
    gc_all = _dot_exact_lhs(tri, g_all)
    gi_all = sum(jnp.dot(t, spread.astype(bf16), preferred_element_type=f32) for t in _bf16_terms(gc_all, 3))
    gj_all = _dot_exact_lhs(jnp.where(same_chunk, 1.0, 0.0), jnp.where(pr == pc, gi_all, 0.0))
    decay_all = jnp.exp(jnp.where(pr >= pc, gi_all - gj_all, -jnp.inf))
    egc_all = jnp.exp(gc_all)

    def l2n(t):
        return t * lax.rsqrt(jnp.sum(t * t, axis=-1, keepdims=True) + EPS)

    n_chunks = L // C
    chunk_rows = [slice(c * C, (c + 1) * C) for c in range(n_chunks)]
    qs, ks, vs, kbs, a_mats, qk_ps = [], [], [], [], [], []
    for rows in chunk_rows:
        kk, qk = [], []
        for h in range(H):
            q = l2n(y[rows, h * dh:(h + 1) * dh]) * dh ** -0.5
            k = l2n(y[rows, W + h * dh:W + (h + 1) * dh])
            kb = k * beta_all[rows, h:h + 1]
            qs.append(q), ks.append(k), vs.append(y[rows, 2 * W + h * dh:2 * W + (h + 1) * dh]), kbs.append(kb)
            kk.append(_dot_nt(kb.astype(bf16), k.astype(bf16)))
            qk.append(_dot_nt(q.astype(bf16), k.astype(bf16)))
        a_mats.append(jnp.where(strict, jnp.concatenate(kk, axis=1) * decay_all[rows], 0.0))
        qk_ps.append(jnp.concatenate(qk, axis=1) * decay_all[rows])
    t_ps = _gdn_packed_unit_lower_inverse(a_mats, row, col, blk_diag)
    for c, rows in enumerate(chunk_rows):
        beta, gc, egc = beta_all[rows], gc_all[rows], egc_all[rows]
        outs = []
        for h in range(H):
            i = c * H + h
            t_h = t_ps[c][:, h * C:(h + 1) * C].astype(bf16)
            e_h = egc[:, h:h + 1]
            u = jnp.dot(t_h, (vs[i] * beta[:, h:h + 1]).astype(bf16), preferred_element_type=f32)
            w = jnp.dot(t_h, (kbs[i] * e_h).astype(bf16), preferred_element_type=f32)
            state = st_ref[h]
            sb = state.astype(bf16)
            v_new = u - jnp.dot(w.astype(bf16), sb, preferred_element_type=f32)
            o = (jnp.dot((qs[i] * e_h).astype(bf16), sb, preferred_element_type=f32)
                 + jnp.dot(qk_ps[c][:, h * C:(h + 1) * C].astype(bf16), v_new.astype(bf16),
                           preferred_element_type=f32))
            g_last = gc[C - 1:C, h:h + 1]
            k_dec = (ks[i] * jnp.exp(g_last - gc[:, h:h + 1])).astype(bf16)
            st_ref[h] = state * jnp.exp(g_last) + lax.dot_general(
                k_dec, v_new.astype(bf16), (((0,), (0,)), ((), ())), preferred_element_type=f32)
            o = o * lax.rsqrt(jnp.mean(o * o, axis=-1, keepdims=True) + EPS) * nw_ref[...]
            outs.append(o * jax.nn.silu(z_ref[rows, h * dh:(h + 1) * dh]))
        o_ref[rows, :] = jnp.concatenate(outs, axis=1)


def gdn_mixer(proj, conv_w, a_log, dt_bias, norm_w):
    Bsz, S, _ = proj.shape
    f32 = jnp.float32
    L = min(GDN_ROWS, S)
    H, W, W3 = GDN_HEADS, GDN_WIDTH, 3 * GDN_WIDTH
    full = lambda *shape: pl.BlockSpec(shape, lambda bi, ci: (0,) * len(shape))
    rows = lambda n, col=0: pl.BlockSpec((None, L, n), lambda bi, ci: (bi, ci, col // n))
    return pl.pallas_call(
        _gdn_body,
        grid=(Bsz, S // L),
        in_specs=[rows(W3, PROJ_QKV), rows(W, PROJ_Z), rows(LANES, PROJ_SMALL),
                  full(GDN_CONV, W3), full(1, H), full(1, H), full(1, GDN_HEAD_DIM)],
        out_specs=rows(W),
        out_shape=jax.ShapeDtypeStruct((Bsz, S, W), f32),
        scratch_shapes=[pltpu.VMEM((SUBLANES, W3), f32), pltpu.VMEM((H, GDN_HEAD_DIM, GDN_HEAD_DIM), f32)],
        compiler_params=pltpu.CompilerParams(dimension_semantics=("arbitrary", "arbitrary"),
                                             vmem_limit_bytes=48 * 1024 * 1024),
        name="gdn_mixer",
    )(proj, proj, proj, conv_w.astype(f32), a_log.astype(f32).reshape(1, H),
      dt_bias.astype(f32).reshape(1, H), norm_w.astype(f32).reshape(1, GDN_HEAD_DIM))


def nsa_mixer(q, kv, gate_logits, q_norm, k_norm, cmp_pe, cmp_w1, cmp_w2, cos, sin):
    Bsz, S, _ = q.shape
    f32 = jnp.float32
    G, R, dh = NSA_KV_HEADS, NSA_HEADS // NSA_KV_HEADS, NSA_HEAD_DIM
    scale = dh ** -0.5
    q = rope(rms_norm(q.reshape(Bsz, S, NSA_HEADS, dh), q_norm), cos, sin)
    kc_raw, vc_raw, ks, vs, kw, vw = (t.reshape(Bsz, S, G, dh) for t in jnp.split(kv, 6, axis=-1))
    ks = rope(rms_norm(ks, k_norm[1]), cos, sin)
    kw = rope(rms_norm(kw, k_norm[2]), cos, sin)

    n_cmp = (S - CMP_LEN) // CMP_STRIDE + 1
    win_idx = jnp.arange(n_cmp)[:, None] * CMP_STRIDE + jnp.arange(CMP_LEN)[None, :]
    cmp_end = win_idx[:, -1]

    def compress(t, pe, w1, w2):
        blocks = t[:, win_idx] + pe[:, None, :]
        blocks = jnp.moveaxis(blocks, 3, 2).reshape(Bsz, n_cmp, G, CMP_LEN * dh)
        return jax.nn.gelu(blocks @ w1) @ w2

    cos_c, sin_c = rope_tables(cmp_end, dh)
    k_cmp = rope(rms_norm(compress(kc_raw, cmp_pe[0], cmp_w1[0], cmp_w2[0]), k_norm[0]), cos_c, sin_c)
    v_cmp = compress(vc_raw, cmp_pe[1], cmp_w1[1], cmp_w2[1])

    n_slc = S // SLC_BLOCK
    n_top = min(SLC_TOPK, n_slc)
    c0 = jnp.arange(n_cmp) * CMP_STRIDE
    s0 = jnp.arange(n_slc) * SLC_BLOCK
    overlap = jnp.clip(jnp.minimum(c0[:, None] + CMP_LEN, s0[None, :] + SLC_BLOCK)
                       - jnp.maximum(c0[:, None], s0[None, :]), 0, None).astype(f32) / CMP_LEN
    gates = jax.nn.sigmoid(gate_logits.astype(f32)).reshape(Bsz, S, G, R * 3).transpose(0, 2, 3, 1)

    bf16 = jnp.bfloat16
    n_cmp_pad = S // CMP_STRIDE
    pad_cmp = lambda t: jnp.pad(t.astype(bf16), ((0, 0), (0, n_cmp_pad - n_cmp), (0, 0), (0, 0)))
    keys = lambda t: t.astype(bf16).transpose(0, 2, 1, 3)
    vals = lambda t: t.astype(bf16).transpose(0, 2, 3, 1)
    q_heads = (q * scale).astype(bf16).reshape(Bsz, S, G, R, dh).transpose(0, 2, 3, 1, 4)
    overlap_t = jnp.pad(overlap.T, ((0, 0), (0, n_cmp_pad - n_cmp))).astype(bf16)
    expand_t = ((jnp.arange(S) // SLC_BLOCK)[:, None] == jnp.arange(n_slc)[None, :]).astype(bf16)
    return _nsa_attention(q_heads, keys(pad_cmp(k_cmp)), vals(pad_cmp(v_cmp)), keys(ks), vals(vs), keys(kw), vals(vw),
                          gates, overlap_t, expand_t, n_top)


NSA_R = NSA_HEADS // NSA_KV_HEADS
SLC_TILE = 1024
MASK_NEG = -1e30


def _dot_nt(a, b):
    return lax.dot_general(a, b, (((1,), (1,)), ((), ())), preferred_element_type=jnp.float32)


def _nsa_body(q_ref, kc_ref, vct_ref, ks_ref, vst_ref, kw_ref, vwt_ref, gate_ref, ovt_ref, expt_ref, o_ref,
              *, n_top):
    f32, bf16 = jnp.float32, jnp.bfloat16
    R, QB, dh = q_ref.shape
    S = ks_ref.shape[0]
    n_slc, n_cmp_pad = ovt_ref.shape
    M = R * QB
    t0 = pl.program_id(2) * QB
    q = q_ref[...].reshape(M, dh)
    per_head = lambda t: jnp.concatenate([t] * R, axis=1)

    band = S if S < WINDOW + QB else WINDOW + QB
    start = pl.multiple_of(jnp.maximum(t0 + QB - band, 0), QB)
    s = _dot_nt(kw_ref[pl.ds(start, band), :], q)
    diff = (t0 - start) + (lax.broadcasted_iota(jnp.int32, (band, QB), 1) - lax.broadcasted_iota(jnp.int32, (band, QB), 0))
    s = s + per_head(jnp.where((diff >= 0) & (diff < WINDOW), 0.0, -jnp.inf))
    p = jnp.exp(s - jnp.max(s, axis=0, keepdims=True))
    o_win = jnp.dot(vwt_ref[:, pl.ds(start, band)], p.astype(bf16), preferred_element_type=f32)
    o_win = o_win / jnp.sum(p, axis=0, keepdims=True)

    s = _dot_nt(kc_ref[...], q)
    n_idx = lax.broadcasted_iota(jnp.int32, (n_cmp_pad, QB), 0)
    t_idx = t0 + lax.broadcasted_iota(jnp.int32, (n_cmp_pad, QB), 1)
    valid = (n_idx * CMP_STRIDE + (CMP_LEN - 1) <= t_idx) & (n_idx < n_cmp_pad - 1)
    s = s + per_head(jnp.where(valid, 0.0, -jnp.inf))
    m = jnp.max(s, axis=0, keepdims=True)
    m = jnp.where(jnp.isfinite(m), m, 0.0)
    p = jnp.exp(s - m)
    p = p / jnp.maximum(jnp.sum(p, axis=0, keepdims=True), jnp.finfo(f32).tiny)
    o_cmp = jnp.dot(vct_ref[...], p.astype(bf16), preferred_element_type=f32)

    p_sum = p[:, 0:QB]
    for r in range(1, R):
        p_sum = p_sum + p[:, r * QB:(r + 1) * QB]
    imp = sum(jnp.dot(ovt_ref[...], t, preferred_element_type=f32) for t in _bf16_terms(p_sum, 3))
    j_idx = lax.broadcasted_iota(jnp.int32, (n_slc, QB), 0)
    t_lane = t0 + lax.broadcasted_iota(jnp.int32, (n_slc, QB), 1)
    cur = lax.shift_right_logical(t_lane, 6)
    forced = (j_idx == 0) | (j_idx == cur) | (j_idx == cur - 1)
    started = j_idx * SLC_BLOCK <= t_lane
    imp = jnp.where(forced, BIG, jnp.where(started, imp, -BIG))
    rank = jnp.zeros((n_slc, QB), f32)
    for jp in range(n_slc):
        row = imp[jp:jp + 1, :]
        ahead = (row > imp) | ((row == imp) & (j_idx > jp))
        rank = rank + jnp.where(ahead, 1.0, 0.0)
    sel = jnp.where(rank < n_top, 1.0, 0.0).astype(bf16)
    k_rel = lax.broadcasted_iota(jnp.int32, (SLC_TILE, QB), 0)
    t_q = t0 + lax.broadcasted_iota(jnp.int32, (SLC_TILE, QB), 1)

    def slc_step(kt, carry):
        m, l, acc = carry
        k0 = pl.multiple_of(kt * SLC_TILE, SLC_TILE)
        s = _dot_nt(ks_ref[pl.ds(k0, SLC_TILE), :], q)
        sel_keys = jnp.dot(expt_ref[pl.ds(k0, SLC_TILE), :], sel, preferred_element_type=f32)
        bias = jnp.where((sel_keys > 0.5) & (k_rel + k0 <= t_q), 0.0, MASK_NEG)
        s = s + per_head(bias)
        m_new = jnp.maximum(m, jnp.max(s, axis=0, keepdims=True))
        alpha = jnp.exp(m - m_new)
        p = jnp.exp(s - m_new)
        l = alpha * l + jnp.sum(p, axis=0, keepdims=True)
        acc = alpha * acc + jnp.dot(vst_ref[:, pl.ds(k0, SLC_TILE)], p.astype(bf16), preferred_element_type=f32)
        return m_new, l, acc

    n_tiles = (t0 + QB - 1) // SLC_TILE + 1
    init = (jnp.full((1, M), MASK_NEG, f32), jnp.zeros((1, M), f32), jnp.zeros((dh, M), f32))
    _, l, acc = lax.fori_loop(0, n_tiles, slc_step, init)
    o_slc = acc / l

    gate = gate_ref[...]
    g_cmp, g_slc, g_win = (jnp.concatenate([gate[3 * r + c:3 * r + c + 1, :] for r in range(R)], axis=1)
                           for c in range(3))
    o_t = g_cmp * o_cmp + g_slc * o_slc + g_win * o_win
    o_ref[...] = jnp.concatenate([o_t[:, r * QB:(r + 1) * QB].T for r in range(R)], axis=1)


def _nsa_attention(q, kc, vct, ks, vst, kw, vwt, gates, overlap_t, expand_t, n_top):
    Bsz, G, R, S, dh = q.shape
    n_cmp_pad = kc.shape[2]
    assert S % SLC_TILE == 0 and SLC_TILE % Q_BLOCK == 0, (S, SLC_TILE, Q_BLOCK)
    keys = lambda n: pl.BlockSpec((None, None, n, dh), lambda b, g, i: (b, g, 0, 0))
    vals = lambda n: pl.BlockSpec((None, None, dh, n), lambda b, g, i: (b, g, 0, 0))
    const = lambda a: pl.BlockSpec(a.shape, lambda b, g, i: (0, 0))
    return pl.pallas_call(
        functools.partial(_nsa_body, n_top=n_top),
        grid=(Bsz, G, S // Q_BLOCK),
        in_specs=[pl.BlockSpec((None, None, R, Q_BLOCK, dh), lambda b, g, i: (b, g, 0, i, 0)),
                  keys(n_cmp_pad), vals(n_cmp_pad), keys(S), vals(S), keys(S), vals(S),
                  pl.BlockSpec((None, None, 3 * R, Q_BLOCK), lambda b, g, i: (b, g, 0, i)),
                  const(overlap_t), const(expand_t)],
        out_specs=pl.BlockSpec((None, Q_BLOCK, R * dh), lambda b, g, i: (b, i, g)),
        out_shape=jax.ShapeDtypeStruct((Bsz, S, G * R * dh), jnp.float32),
        compiler_params=pltpu.CompilerParams(dimension_semantics=("arbitrary", "arbitrary", "arbitrary"),
                                             vmem_limit_bytes=48 * 1024 * 1024),
        name="nsa_attention",
    )(q, kc, vct, ks, vst, kw, vwt, gates, overlap_t, expand_t)


def hybrid_mixer(x, ln_mix, w_in, lam_re, lam_im, b_re, b_im, c_re, c_im, s5_d, s5_log_dt, glu_w, glu_b,
                 s5_out_norm, conv_w, a_log, dt_bias, gdn_norm_w, q_norm, k_norm, cmp_pe, cmp_w1,
                 cmp_w2, cos, sin):
    Bsz, S, D = x.shape
    proj = _in_proj(x.reshape(-1, D), ln_mix, _in_proj_weight(w_in)).reshape(Bsz, S, PROJ_W)
    y_s5 = s5_mixer_normed(proj, lam_re, lam_im, b_re, b_im, c_re, c_im, s5_d, s5_log_dt, glu_w, glu_b, s5_out_norm)
    y_gdn = gdn_mixer(proj, conv_w, a_log, dt_bias, gdn_norm_w)
    y_nsa = nsa_mixer(proj[..., PROJ_Q:PROJ_Q + NSA_WIDTH], proj[..., PROJ_KV:PROJ_KV + 6 * NSA_KV_WIDTH],
                      proj[..., PROJ_GATES:PROJ_GATES + 3 * NSA_HEADS], q_norm, k_norm, cmp_pe, cmp_w1, cmp_w2,
                      cos, sin)
    return y_s5, y_gdn, y_nsa


def hier_moe(x, h, logits, layer, w_gate, w_up, w_down):
    T, D = x.shape
    f32 = jnp.float32
    g_logits = logits[:, :N_GROUPS]
    grp = jnp.argmax(g_logits, axis=-1)
    p_grp = jnp.take_along_axis(jax.nn.softmax(g_logits, axis=-1), grp[:, None], axis=-1)
    e_logits = logits[:, N_GROUPS:N_ROUTER].reshape(T, N_GROUPS, EXPERTS_PER_GROUP)
    e_logits = jnp.take_along_axis(e_logits, grp[:, None, None], axis=1)[:, 0]
    local = jnp.arange(EXPERTS_PER_GROUP, dtype=jnp.int32)[None, :]
    rest, top_logit, top_local = e_logits, [], []
    for _ in range(TOP_K):
        idx = jnp.argmax(rest, axis=-1).astype(jnp.int32)
        top_logit.append(jnp.take_along_axis(rest, idx[:, None], axis=-1)[:, 0])
        top_local.append(idx)
        rest = jnp.where(local == idx[:, None], -jnp.inf, rest)
    top_logit, top_local = jnp.stack(top_logit, axis=-1), jnp.stack(top_local, axis=-1)
    weights = jax.nn.softmax(top_logit, axis=-1) * p_grp
    experts = grp[:, None].astype(jnp.int32) * EXPERTS_PER_GROUP + top_local
    n_assign = T * TOP_K
    e_flat = experts.reshape(-1)
    tok_flat = jnp.repeat(jnp.arange(T, dtype=jnp.int32), TOP_K)
    onehot = (e_flat[:, None] == jnp.arange(N_EXPERTS, dtype=jnp.int32)[None, :]).astype(jnp.int32)
    running = jnp.cumsum(onehot, axis=0)
    counts = running[-1]
    padded = (counts + MOE_ROWS - 1) // MOE_ROWS * MOE_ROWS
    pad_start = jnp.cumsum(padded) - padded
    dest = (pad_start[e_flat] + jnp.take_along_axis(running, e_flat[:, None], axis=1)[:, 0] - 1).astype(jnp.int32)
    n_blk = (n_assign + MOE_ROWS - 1) // MOE_ROWS + N_EXPERTS
    cap = n_blk * MOE_ROWS
    slot_tok = jnp.full((cap,), T, jnp.int32).at[dest].set(tok_flat)
    blk_expert = jnp.minimum(jnp.searchsorted(jnp.cumsum(padded), jnp.arange(n_blk) * MOE_ROWS, side='right'),
                             N_EXPERTS - 1).astype(jnp.int32)
    x_pad = jnp.concatenate([x, jnp.zeros((1, D), x.dtype)], axis=0)
    y = _moe_ffn(blk_expert, slot_tok, x_pad, layer, w_gate.astype(f32), w_up.astype(f32),
                 w_down.astype(f32))
    return _moe_combine(dest, h, weights, y)


CMB_ROWS = 128


def _moe_combine_body(d_ref, h_ref, w_ref, y_hbm, o_ref, buf, sem):
    i = pl.program_id(0)
    n = pl.num_programs(0)
    rows = h_ref.shape[0]
    slot = i & 1

    def row_copy(step, r, k, to):
        src = y_hbm.at[pl.ds(d_ref[(step * rows + r) * TOP_K + k], 1), :]
        return pltpu.make_async_copy(src, buf.at[to, k, pl.ds(r, 1), :], sem.at[to])

    def start_step(step, to):
        for r in range(rows):
            for k in range(TOP_K):
                row_copy(step, r, k, to).start()

    def finish_step():
        for r in range(rows):
            for k in range(TOP_K):
                row_copy(i, r, k, slot).wait()
        out = h_ref[...]
        for k in range(TOP_K):
            out = out + w_ref[:, k:k + 1] * buf[slot, k]
        o_ref[...] = out

    @pl.when(i == 0)
    def _():
        start_step(0, 0)

    @pl.when(i + 1 < n)
    def _():
        start_step(i + 1, 1 - slot)
        finish_step()

    @pl.when(i + 1 == n)
    def _():
        finish_step()


def _moe_combine(dest, h, weights, y):
    T, D = h.shape
    rows = min(CMB_ROWS, T)
    return pl.pallas_call(
        _moe_combine_body,
        grid_spec=pltpu.PrefetchScalarGridSpec(
            num_scalar_prefetch=1, grid=(T // rows,),
            in_specs=[pl.BlockSpec((rows, D), lambda i, d: (i, 0)),
                      pl.BlockSpec((rows, TOP_K), lambda i, d: (i, 0)),
                      pl.BlockSpec(memory_space=pl.ANY)],
            out_specs=pl.BlockSpec((rows, D), lambda i, d: (i, 0)),
            scratch_shapes=[pltpu.VMEM((2, TOP_K, rows, D), jnp.float32), pltpu.SemaphoreType.DMA((2,))]),
        out_shape=jax.ShapeDtypeStruct((T, D), jnp.float32),
        compiler_params=pltpu.CompilerParams(dimension_semantics=("arbitrary",),
                                             vmem_limit_bytes=32 * 1024 * 1024),
        name="moe_combine",
    )(dest, h, weights, y)


MOE_ROWS = 256


def _moe_ffn_body(be_ref, tok_ref, x_hbm, wg_ref, wu_ref, wd_ref, o_ref, xbuf, sem, wg_bf, wu_bf, wd_bf):
    i = pl.program_id(0)
    n = pl.num_programs(0)
    slot = i & 1

    def row_copy(step, r, to):
        src = x_hbm.at[pl.ds(tok_ref[step * MOE_ROWS + r], 1), :]
        return pltpu.make_async_copy(src, xbuf.at[to, pl.ds(r, 1), :], sem.at[to])

    def start_step(step, to):
        for r in range(MOE_ROWS):
            row_copy(step, r, to).start()

    def finish_step():
        for r in range(MOE_ROWS):
            row_copy(i, r, slot).wait()
        x = xbuf[slot].astype(jnp.bfloat16)
        gate = jnp.dot(x, wg_bf[...], preferred_element_type=jnp.float32)
        up = jnp.dot(x, wu_bf[...], preferred_element_type=jnp.float32)
        hid = (jax.nn.silu(gate) * up).astype(jnp.bfloat16)
        o_ref[...] = jnp.dot(hid, wd_bf[...], preferred_element_type=jnp.float32)

    @pl.when(i == 0)
    def _():
        start_step(0, 0)

    @pl.when((i == 0) | (be_ref[i] != be_ref[jnp.maximum(i - 1, 0)]))
    def _():
        wg_bf[...] = wg_ref[...].astype(jnp.bfloat16)
        wu_bf[...] = wu_ref[...].astype(jnp.bfloat16)
        wd_bf[...] = wd_ref[...].astype(jnp.bfloat16)

    @pl.when(i + 1 < n)
    def _():
        start_step(i + 1, 1 - slot)
        finish_step()

    @pl.when(i + 1 == n)
    def _():
        finish_step()


def _moe_ffn(blk_expert, slot_tok, x, layer, w_gate, w_up, w_down):
    cap = slot_tok.shape[0]
    D = x.shape[1]
    n_blk = cap // MOE_ROWS
    De = w_gate.shape[-1]
    return pl.pallas_call(
        _moe_ffn_body,
        grid_spec=pltpu.PrefetchScalarGridSpec(
            num_scalar_prefetch=2, grid=(n_blk,),
            in_specs=[pl.BlockSpec(memory_space=pl.ANY),
                      pl.BlockSpec((None, None, D, De), lambda i, be, tok: (layer, be[i], 0, 0)),
                      pl.BlockSpec((None, None, D, De), lambda i, be, tok: (layer, be[i], 0, 0)),
                      pl.BlockSpec((None, None, De, D), lambda i, be, tok: (layer, be[i], 0, 0))],
            out_specs=pl.BlockSpec((MOE_ROWS, D), lambda i, be, tok: (i, 0)),
            scratch_shapes=[pltpu.VMEM((2, MOE_ROWS, D), jnp.float32), pltpu.SemaphoreType.DMA((2,)),
                            pltpu.VMEM((D, De), jnp.bfloat16), pltpu.VMEM((D, De), jnp.bfloat16),
                            pltpu.VMEM((De, D), jnp.bfloat16)]),
        out_shape=jax.ShapeDtypeStruct((cap, D), jnp.float32),
        compiler_params=pltpu.CompilerParams(dimension_semantics=("arbitrary",),
                                             vmem_limit_bytes=56 * 1024 * 1024),
        name="moe_ffn",
    )(blk_expert, slot_tok, x, w_gate, w_up, w_down)


def kernel(x, ln_mix, w_in, s5_lambda_re, s5_lambda_im, s5_b_re, s5_b_im, s5_c_re, s5_c_im, s5_d,
           s5_log_dt, s5_glu_w, s5_glu_b, s5_out_norm, gdn_conv_w, gdn_a_log, gdn_dt_bias, gdn_norm_w,
           nsa_q_norm, nsa_k_norm, nsa_cmp_pe, nsa_cmp_w1, nsa_cmp_w2, nsa_out_norm, w_out, ln_ffn,
           moe_group_w, moe_group_b, moe_expert_w, moe_expert_b, moe_w_gate, moe_w_up, moe_w_down):
    Bsz, S, D = x.shape
    cos, sin = rope_tables(jnp.arange(S), NSA_HEAD_DIM)
    for l in range(DEPTH):
        y_s5, y_gdn, y_nsa = hybrid_mixer(
            x, ln_mix[l], w_in[l], s5_lambda_re[l], s5_lambda_im[l], s5_b_re[l], s5_b_im[l], s5_c_re[l], s5_c_im[l],
            s5_d[l], s5_log_dt[l], s5_glu_w[l], s5_glu_b[l], s5_out_norm[l], gdn_conv_w[l], gdn_a_log[l],
            gdn_dt_bias[l], gdn_norm_w[l], nsa_q_norm[l], nsa_k_norm[l], nsa_cmp_pe[l], nsa_cmp_w1[l], nsa_cmp_w2[l],
            cos, sin)
        router_w = jnp.concatenate([moe_group_w[l], moe_expert_w[l]], axis=-1)
        router_b = jnp.concatenate([moe_group_b[l], moe_expert_b[l]], axis=-1)
        h, a, logits = _out_proj(x.reshape(-1, D), y_s5.reshape(-1, S5_WIDTH), y_gdn.reshape(-1, GDN_WIDTH),
                                 y_nsa.reshape(-1, NSA_WIDTH), nsa_out_norm[l], w_out[l], ln_ffn[l], router_w, router_b)
        x = hier_moe(a, h, logits, l, moe_w_gate, moe_w_up, moe_w_down).reshape(Bsz, S, D)
    return x
```

```python
import functools
import math
import jax, jax.numpy as jnp
from jax import lax
import numpy as np
from jax.experimental import pallas as pl
from jax.experimental.pallas import tpu as pltpu

D_MODEL = 2048
BATCH = 4
SEQ = 4096
DEPTH = 2

EPS = 1e-6
S5_WIDTH = D_MODEL // 4
S5_GROUP = 16
S5_GROUPS = S5_WIDTH // S5_GROUP
S5_STATE = 64
GDN_HEAD_DIM = 128
GDN_WIDTH = D_MODEL // 4
GDN_HEADS = GDN_WIDTH // GDN_HEAD_DIM
GDN_CONV = 4
GDN_CHUNK = 64
NSA_HEAD_DIM = 64
NSA_WIDTH = D_MODEL // 2
NSA_HEADS = NSA_WIDTH // NSA_HEAD_DIM
NSA_KV_HEADS = 4
NSA_KV_WIDTH = NSA_KV_HEADS * NSA_HEAD_DIM
CMP_LEN = 32
CMP_STRIDE = 16
CMP_HIDDEN = 256
SLC_BLOCK = 64
SLC_TOPK = 16
WINDOW = 512
Q_BLOCK = 128
ROPE_THETA = 10000.0
BIG = 1e9
N_GROUPS = 4
EXPERTS_PER_GROUP = 8
N_EXPERTS = N_GROUPS * EXPERTS_PER_GROUP
TOP_K = 2
D_EXPERT = 512
MOE_BLOCK = 128
IN_SIZES = (S5_WIDTH, 3 * GDN_WIDTH, GDN_WIDTH, GDN_HEADS, GDN_HEADS, NSA_WIDTH, 6 * NSA_KV_WIDTH, 3 * NSA_HEADS)
D_IN = sum(IN_SIZES)


PROJ_QKV, PROJ_KV, PROJ_Q, PROJ_U, PROJ_Z, PROJ_SMALL = 0, 1536, 3072, 4096, 4608, 5120
PROJ_SMALL_W = 512
PROJ_W = PROJ_SMALL + PROJ_SMALL_W
PROJ_GATES = PROJ_SMALL + 2 * GDN_HEADS
LANES = 128
IN_ROWS, IN_COLS = 1024, 512
OUT_ROWS = 256
N_ROUTER = N_GROUPS + N_EXPERTS


def _in_proj_weight(w_in):
    u, qkv, z, ga, gb, q, kv, gl = jnp.split(w_in, np.cumsum(IN_SIZES)[:-1].tolist(), axis=-1)
    small = jnp.concatenate([ga, gb, gl], axis=-1)
    small = jnp.pad(small, ((0, 0), (0, PROJ_SMALL_W - small.shape[-1])))
    return jnp.concatenate([qkv, kv, q, u, z, small], axis=-1)


def _in_proj_body(x_ref, g_ref, w_ref, o_ref, a_ref):
    @pl.when(pl.program_id(1) == 0)
    def _():
        x = x_ref[...]
        a_ref[...] = (x * lax.rsqrt(jnp.mean(x * x, axis=-1, keepdims=True) + EPS) * g_ref[...]).astype(jnp.bfloat16)

    o_ref[...] = jnp.dot(a_ref[...], w_ref[...].astype(jnp.bfloat16), preferred_element_type=jnp.float32)


def _in_proj(x, gain, w):
    T, D = x.shape
    N = w.shape[1]
    tm = min(IN_ROWS, T)
    return pl.pallas_call(
        _in_proj_body,
        grid=(T // tm, N // IN_COLS),
        in_specs=[pl.BlockSpec((tm, D), lambda i, j: (i, 0)),
                  pl.BlockSpec((1, D), lambda i, j: (0, 0)),
                  pl.BlockSpec((D, IN_COLS), lambda i, j: (0, j))],
        out_specs=pl.BlockSpec((tm, IN_COLS), lambda i, j: (i, j)),
        out_shape=jax.ShapeDtypeStruct((T, N), jnp.float32),
        scratch_shapes=[pltpu.VMEM((tm, D), jnp.bfloat16)],
        compiler_params=pltpu.CompilerParams(dimension_semantics=("arbitrary", "arbitrary"),
                                             vmem_limit_bytes=48 * 1024 * 1024),
        name="in_proj",
    )(x, gain.astype(jnp.float32).reshape(1, D), w)


def _out_proj_body(x_ref, s5_ref, gdn_ref, nsa_ref, ng_ref, w_ref, fg_ref, rw_ref, rb_ref, h_ref, a_ref, lg_ref):
    f32, bf16 = jnp.float32, jnp.bfloat16
    nsa = nsa_ref[...]
    nsa = nsa * lax.rsqrt(jnp.mean(nsa * nsa, axis=-1, keepdims=True) + EPS) * ng_ref[...]
    k0, k1 = S5_WIDTH, S5_WIDTH + GDN_WIDTH
    y = (jnp.dot(s5_ref[...].astype(bf16), w_ref[0:k0, :], preferred_element_type=f32)
         + jnp.dot(gdn_ref[...].astype(bf16), w_ref[k0:k1, :], preferred_element_type=f32)
         + jnp.dot(nsa.astype(bf16), w_ref[k1:, :], preferred_element_type=f32))
    h = x_ref[...] + y
    h_ref[...] = h
    a = h * lax.rsqrt(jnp.mean(h * h, axis=-1, keepdims=True) + EPS) * fg_ref[...]
    a_ref[...] = a
    lg_ref[...] = jnp.dot(a.astype(bf16), rw_ref[...], preferred_element_type=f32) + rb_ref[...]


def _out_proj(x, y_s5, y_gdn, y_nsa, nsa_gain, w_out, ffn_gain, router_w, router_b):
    T, D = x.shape
    tm = min(OUT_ROWS, T)
    f32, bf16 = jnp.float32, jnp.bfloat16
    rw = jnp.pad(router_w.astype(bf16), ((0, 0), (0, LANES - N_ROUTER)))
    rb = jnp.pad(router_b.astype(f32), (0, LANES - N_ROUTER)).reshape(1, LANES)
    rows = lambda n: pl.BlockSpec((tm, n), lambda i: (i, 0))
    full = lambda *shape: pl.BlockSpec(shape, lambda i: (0,) * len(shape))
    return pl.pallas_call(
        _out_proj_body,
        grid=(T // tm,),
        in_specs=[rows(D), rows(S5_WIDTH), rows(GDN_WIDTH), rows(NSA_WIDTH), full(1, NSA_WIDTH), full(D, D),
                  full(1, D), full(D, LANES), full(1, LANES)],
        out_specs=[rows(D), rows(D), rows(LANES)],
        out_shape=[jax.ShapeDtypeStruct((T, D), f32), jax.ShapeDtypeStruct((T, D), f32),
                   jax.ShapeDtypeStruct((T, LANES), f32)],
        compiler_params=pltpu.CompilerParams(dimension_semantics=("arbitrary",),
                                             vmem_limit_bytes=48 * 1024 * 1024),
        name="out_proj",
    )(x, y_s5, y_gdn, y_nsa, nsa_gain.astype(f32).reshape(1, NSA_WIDTH), w_out.astype(bf16),
      ffn_gain.astype(f32).reshape(1, D), rw, rb)


def rms_norm(x, gain):
    xf = x.astype(jnp.float32)
    y = xf * lax.rsqrt(jnp.mean(xf * xf, axis=-1, keepdims=True) + EPS)
    return (y * gain.astype(jnp.float32)).astype(x.dtype)


def l2_norm(x):
    return x * lax.rsqrt(jnp.sum(x * x, axis=-1, keepdims=True) + EPS)


def rope_tables(pos, dim):
    inv_freq = ROPE_THETA ** (-jnp.arange(0, dim, 2, dtype=jnp.float32) / dim)
    ang = pos.astype(jnp.float32)[:, None] * inv_freq[None, :]
    return jnp.cos(ang)[:, None, :], jnp.sin(ang)[:, None, :]


def rope(x, cos, sin):
    x1, x2 = jnp.split(x, 2, axis=-1)
    return jnp.concatenate([x1 * cos - x2 * sin, x2 * cos + x1 * sin], axis=-1).astype(x.dtype)


def masked_softmax(s, mask):
    s = jnp.where(mask, s, -jnp.inf)
    m = jnp.max(s, axis=-1, keepdims=True)
    m = jnp.where(jnp.isfinite(m), m, 0.0)
    p = jnp.exp(s - m)
    return p / jnp.maximum(jnp.sum(p, axis=-1, keepdims=True), jnp.finfo(jnp.float32).tiny)


S5_OCT = 8
S5_NOCT = S5_GROUPS // S5_OCT
S5_OCT_IN = S5_OCT * S5_GROUP
S5_OCT_ST = S5_OCT * S5_STATE
SUBLANES = 8
S5_SCAN_SHIFTS = (1, 2, 4)


def _s5_constants(lam_re, lam_im, b_re, b_im, c_re, c_im, log_dt):
    f32 = jnp.float32
    lr, li = lam_re.astype(f32), lam_im.astype(f32)
    dt = jnp.exp(log_dt.astype(f32))[:, None]

    def lam_pow(k):
        mag = jnp.exp(k * lr * dt)
        return mag * jnp.cos(k * li * dt), mag * jnp.sin(k * li * dt)

    ar, ai = lam_pow(1.0)
    nr, ni = ar - 1.0, ai
    den = lr * lr + li * li
    qr, qi = (nr * lr + ni * li) / den, (ni * lr - nr * li) / den
    br, bi = b_re.astype(f32), b_im.astype(f32)
    bbr = qr[:, :, None] * br - qi[:, :, None] * bi
    bbi = qr[:, :, None] * bi + qi[:, :, None] * br
    eye = jnp.eye(S5_OCT, dtype=f32)

    def blk_b(t):
        t = t.reshape(S5_NOCT, S5_OCT, S5_STATE, S5_GROUP)
        return jnp.einsum('qgph,gk->qghkp', t, eye).reshape(S5_NOCT, S5_OCT_IN, S5_OCT_ST)

    def blk_c(t):
        t = t.reshape(S5_NOCT, S5_OCT, S5_GROUP, S5_STATE)
        return jnp.einsum('qghp,gk->qgpkh', t, eye).reshape(S5_NOCT, S5_OCT_ST, S5_OCT_IN)

    b_mat = jnp.stack([blk_b(bbr), blk_b(bbi)]).astype(jnp.bfloat16)
    c_mat = jnp.stack([blk_c(c_re.astype(f32)), blk_c(-c_im.astype(f32))]).astype(jnp.bfloat16)
    rows = jnp.arange(SUBLANES, dtype=f32)[:, None, None]
    tiles = []
    for k in S5_SCAN_SHIFTS:
        pr, pi = lam_pow(jnp.full_like(rows, float(k)))
        keep = rows >= k
        tiles.append(jnp.stack([jnp.where(keep, pr, 0.0), jnp.where(keep, pi, 0.0)]))
    pr, pi = lam_pow(rows + 1.0)
    tiles.append(jnp.stack([pr, pi]))
    a_mat = jnp.stack(tiles).reshape(len(tiles), 2, SUBLANES, S5_GROUPS * S5_STATE)
    return b_mat, c_mat, a_mat


def _s5_body(u_ref, b_ref, c_ref, a_ref, d_ref, gw_ref, gb_ref, gain_ref, o_ref, st_ref, xr_ref, xi_ref):
    bf16 = jnp.bfloat16
    n_tiles = u_ref.shape[0] // SUBLANES

    @pl.when(pl.program_id(1) == 0)
    def _():
        st_ref[...] = jnp.zeros_like(st_ref)

    u = u_ref[...]
    ys = []
    for q in range(S5_NOCT):
        lanes = slice(q * S5_OCT_ST, (q + 1) * S5_OCT_ST)
        uq = u[:, q * S5_OCT_IN:(q + 1) * S5_OCT_IN].astype(bf16)
        xr_ref[...] = jnp.dot(uq, b_ref[0, q], preferred_element_type=jnp.float32)
        xi_ref[...] = jnp.dot(uq, b_ref[1, q], preferred_element_type=jnp.float32)

        def tile_step(i, carry, lanes=lanes):
            cr, ci = carry
            r0 = pl.multiple_of(i * SUBLANES, SUBLANES)
            xr = xr_ref[pl.ds(r0, SUBLANES), :]
            xi = xi_ref[pl.ds(r0, SUBLANES), :]
            for lvl, k in enumerate(S5_SCAN_SHIFTS):
                pr, pi = a_ref[lvl, 0, :, lanes], a_ref[lvl, 1, :, lanes]
                sr, si = pltpu.roll(xr, k, 0), pltpu.roll(xi, k, 0)
                xr, xi = xr + (pr * sr - pi * si), xi + (pr * si + pi * sr)
            pr, pi = a_ref[len(S5_SCAN_SHIFTS), 0, :, lanes], a_ref[len(S5_SCAN_SHIFTS), 1, :, lanes]
            cr = jnp.broadcast_to(cr, xr.shape)
            ci = jnp.broadcast_to(ci, xi.shape)
            xr, xi = xr + (pr * cr - pi * ci), xi + (pr * ci + pi * cr)
            xr_ref[pl.ds(r0, SUBLANES), :] = xr
            xi_ref[pl.ds(r0, SUBLANES), :] = xi
            return xr[SUBLANES - 1:, :], xi[SUBLANES - 1:, :]

        cr, ci = lax.fori_loop(0, n_tiles, tile_step, (st_ref[0, :, lanes], st_ref[1, :, lanes]))
        st_ref[0, :, lanes] = cr
        st_ref[1, :, lanes] = ci
        ys.append(jnp.dot(xr_ref[...].astype(bf16), c_ref[0, q], preferred_element_type=jnp.float32)
                  + jnp.dot(xi_ref[...].astype(bf16), c_ref[1, q], preferred_element_type=jnp.float32))
    y = jnp.concatenate(ys, axis=1) + d_ref[...] * u
    y = jax.nn.gelu(y)
    z = jnp.dot(y.astype(bf16), gw_ref[...], preferred_element_type=jnp.float32) + gb_ref[...]
    y = y * jax.nn.sigmoid(z)
    y = y * lax.rsqrt(jnp.mean(y * y, axis=-1, keepdims=True) + EPS)
    o_ref[...] = y * gain_ref[...]


def s5_mixer_normed(proj, lam_re, lam_im, b_re, b_im, c_re, c_im, d_skip, log_dt, glu_w, glu_b, out_norm, chunk=512):
    Bsz, S, _ = proj.shape
    W = S5_WIDTH
    f32 = jnp.float32
    L = min(chunk, S)
    b_mat, c_mat, a_mat = _s5_constants(lam_re, lam_im, b_re, b_im, c_re, c_im, log_dt)
    n_state = S5_GROUPS * S5_STATE
    full = lambda *shape: pl.BlockSpec(shape, lambda b, c: (0,) * len(shape))
    return pl.pallas_call(
        _s5_body,
        grid=(Bsz, S // L),
        in_specs=[pl.BlockSpec((None, L, W), lambda b, c: (b, c, PROJ_U // W)),
                  full(*b_mat.shape), full(*c_mat.shape), full(*a_mat.shape),
                  full(1, W), full(W, W), full(1, W), full(1, W)],
        out_specs=pl.BlockSpec((None, L, W), lambda b, c: (b, c, 0)),
        out_shape=jax.ShapeDtypeStruct((Bsz, S, W), f32),
        scratch_shapes=[pltpu.VMEM((2, 1, n_state), f32),
                        pltpu.VMEM((L, S5_OCT_ST), f32), pltpu.VMEM((L, S5_OCT_ST), f32)],
        compiler_params=pltpu.CompilerParams(dimension_semantics=("arbitrary", "arbitrary"),
                                             vmem_limit_bytes=48 * 1024 * 1024),
        name="s5_mixer",
    )(proj, b_mat, c_mat, a_mat, d_skip.astype(f32).reshape(1, W), glu_w.astype(jnp.bfloat16),
      glu_b.astype(f32).reshape(1, W), out_norm.astype(f32).reshape(1, W))


GDN_ROWS = 256
GDN_PACK = GDN_HEADS * GDN_CHUNK


def _bf16_terms(x, n):
    terms, rem = [], x
    for i in range(n):
        t = rem.astype(jnp.bfloat16)
        terms.append(t)
        if i + 1 < n:
            rem = rem - t.astype(jnp.float32)
    return terms


def _dot_exact_lhs(lhs01, x):
    lhs = lhs01.astype(jnp.bfloat16)
    return sum(jnp.dot(lhs, t, preferred_element_type=jnp.float32) for t in _bf16_terms(x, 3))


def _gdn_packed_unit_lower_inverse(a_mats, row, col, blk_diag):
    f32 = jnp.float32

    def mm(x, y):
        xh, xl = _bf16_terms(x, 2)
        yh, yl = (jnp.where(blk_diag, jnp.concatenate([t] * GDN_HEADS, axis=0), 0.0) for t in _bf16_terms(y, 2))
        return (jnp.dot(xh, yh, preferred_element_type=f32)
                + (jnp.dot(xh, yl, preferred_element_type=f32) + jnp.dot(xl, yh, preferred_element_type=f32)))

    same16 = (row >> 4) == (col >> 4)
    same32 = (row >> 5) == (col >> 5)
    eye = jnp.where(row == col, 1.0, 0.0)
    d1 = [jnp.where(same16, a, 0.0) for a in a_mats]
    d2 = [mm(d, d) for d in d1]
    d4 = [mm(d, d) for d in d2]
    d8 = [mm(d, d) for d in d4]
    lo = [mm(eye - a, eye + b) for a, b in zip(d1, d2)]
    hi = [mm(eye + a, eye + b) for a, b in zip(d4, d8)]
    t = [mm(a, b) for a, b in zip(lo, hi)]
    for pick in (lambda a: jnp.where(same32 & jnp.logical_not(same16), a, 0.0), lambda a: jnp.where(same32, 0.0, a)):
        mid = [mm(ti, pick(a)) for ti, a in zip(t, a_mats)]
        t = [ti - mm(m, ti) for ti, m in zip(t, mid)]
    return t


def _gdn_body(qkv_ref, z_ref, ab_ref, cw_ref, alog_ref, dtb_ref, nw_ref, o_ref, halo_ref, st_ref):
    f32, bf16 = jnp.float32, jnp.bfloat16
    L = qkv_ref.shape[0]
    C, H, dh, W = GDN_CHUNK, GDN_HEADS, GDN_HEAD_DIM, GDN_WIDTH

    @pl.when(pl.program_id(1) == 0)
    def _():
        halo_ref[...] = jnp.zeros_like(halo_ref)
        st_ref[...] = jnp.zeros_like(st_ref)

    x = qkv_ref[...]
    prev = halo_ref[...]
    row8 = lax.broadcasted_iota(jnp.int32, prev.shape, 0)
    acc = cw_ref[GDN_CONV - 1:GDN_CONV, :] * x
    for s in range(1, GDN_CONV):
        xs = pltpu.roll(x, s, 0)
        head = jnp.where(row8 < s, pltpu.roll(prev, s, 0), xs[0:SUBLANES])
        xs = jnp.concatenate([head, xs[SUBLANES:]], axis=0)
        acc = acc + cw_ref[GDN_CONV - 1 - s:GDN_CONV - s, :] * xs
    halo_ref[...] = x[L - SUBLANES:L]
    y = jax.nn.silu(acc)

    a_in, b_in = ab_ref[:, 0:H], ab_ref[:, H:2 * H]
    beta_all = jax.nn.sigmoid(b_in)
    g_all = -jnp.exp(alog_ref[...]) * jax.nn.softplus(a_in + dtb_ref[...])

    row = lax.broadcasted_iota(jnp.int32, (C, GDN_PACK), 0)
    col = lax.broadcasted_iota(jnp.int32, (C, GDN_PACK), 1) & (C - 1)
    strict = row > col
    bd_r = lax.broadcasted_iota(jnp.int32, (GDN_PACK, GDN_PACK), 0)
    bd_c = lax.broadcasted_iota(jnp.int32, (GDN_PACK, GDN_PACK), 1)
    blk_diag = (bd_r >> 6) == (bd_c >> 6)

    lr = lax.broadcasted_iota(jnp.int32, (L, L), 0)
    lc = lax.broadcasted_iota(jnp.int32, (L, L), 1)
    same_chunk = (lr >> 6) == (lc >> 6)
    tri = jnp.where(same_chunk & (lr >= lc), 1.0, 0.0)
    head_of_lane = lax.broadcasted_iota(jnp.int32, (H, GDN_PACK), 1) >> 6
    spread = jnp.where(head_of_lane == lax.broadcasted_iota(jnp.int32, (H, GDN_PACK), 0), 1.0, 0.0)
    pr = lax.broadcasted_iota(jnp.int32, (L, GDN_PACK), 0) & (C - 1)
    pc = lax.broadcasted_iota(jnp.int32, (L, GDN_PACK), 1) & (C - 1)
    gc_all = _dot_exact_lhs(tri, g_all)
    gi_all = sum(jnp.dot(t, spread.astype(bf16), preferred_element_type=f32) for t in _bf16_terms(gc_all, 3))
    gj_all = _dot_exact_lhs(jnp.where(same_chunk, 1.0, 0.0), jnp.where(pr == pc, gi_all, 0.0))
    decay_all = jnp.exp(jnp.where(pr >= pc, gi_all - gj_all, -jnp.inf))
    egc_all = jnp.exp(gc_all)

    def l2n(t):
        return t * lax.rsqrt(jnp.sum(t * t, axis=-1, keepdims=True) + EPS)

    n_chunks = L // C
    chunk_rows = [slice(c * C, (c + 1) * C) for c in range(n_chunks)]
    qs, ks, vs, kbs, a_mats, qk_ps = [], [], [], [], [], []
    for rows in chunk_rows:
        kk, qk = [], []
        for h in range(H):
            q = l2n(y[rows, h * dh:(h + 1) * dh]) * dh ** -0.5
            k = l2n(y[rows, W + h * dh:W + (h + 1) * dh])
            kb = k * beta_all[rows, h:h + 1]
            qs.append(q), ks.append(k), vs.append(y[rows, 2 * W + h * dh:2 * W + (h + 1) * dh]), kbs.append(kb)
            kk.append(_dot_nt(kb.astype(bf16), k.astype(bf16)))
            qk.append(_dot_nt(q.astype(bf16), k.astype(bf16)))
        a_mats.append(jnp.where(strict, jnp.concatenate(kk, axis=1) * decay_all[rows], 0.0))
        qk_ps.append(jnp.concatenate(qk, axis=1) * decay_all[rows])
    t_ps = _gdn_packed_unit_lower_inverse(a_mats, row, col, blk_diag)
    for c, rows in enumerate(chunk_rows):
        beta, gc, egc = beta_all[rows], gc_all[rows], egc_all[rows]
        outs = []
        for h in range(H):
            i = c * H + h
            t_h = t_ps[c][:, h * C:(h + 1) * C].astype(bf16)
            e_h = egc[:, h:h + 1]
            u = jnp.dot(t_h, (vs[i] * beta[:, h:h + 1]).astype(bf16), preferred_element_type=f32)
            w = jnp.dot(t_h, (kbs[i] * e_h).astype(bf16), preferred_element_type=f32)
            state = st_ref[h]
            sb = state.astype(bf16)
            v_new = u - jnp.dot(w.astype(bf16), sb, preferred_element_type=f32)
            o = (jnp.dot((qs[i] * e_h).astype(bf16), sb, preferred_element_type=f32)
                 + jnp.dot(qk_ps[c][:, h * C:(h + 1) * C].astype(bf16), v_new.astype(bf16),
                           preferred_element_type=f32))
            g_last = gc[C - 1:C, h:h + 1]
            k_dec = (ks[i] * jnp.exp(g_last - gc[:, h:h + 1])).astype(bf16)
            st_ref[h] = state * jnp.exp(g_last) + lax.dot_general(
                k_dec, v_new.astype(bf16), (((0,), (0,)), ((), ())), preferred_element_type=f32)
            o = o * lax.rsqrt(jnp.mean(o * o, axis=-1, keepdims=True) + EPS) * nw_ref[...]
            outs.append(o * jax.nn.silu(z_ref[rows, h * dh:(h + 1) * dh]))
        o_ref[rows, :] = jnp.concatenate(outs, axis=1)


def gdn_mixer(proj, conv_w, a_log, dt_bias, norm_w):
    Bsz, S, _ = proj.shape
    f32 = jnp.float32
    L = min(GDN_ROWS, S)
    H, W, W3 = GDN_HEADS, GDN_WIDTH, 3 * GDN_WIDTH
    full = lambda *shape: pl.BlockSpec(shape, lambda bi, ci: (0,) * len(shape))
    rows = lambda n, col=0: pl.BlockSpec((None, L, n), lambda bi, ci: (bi, ci, col // n))
    return pl.pallas_call(
        _gdn_body,
        grid=(Bsz, S // L),
        in_specs=[rows(W3, PROJ_QKV), rows(W, PROJ_Z), rows(LANES, PROJ_SMALL),
                  full(GDN_CONV, W3), full(1, H), full(1, H), full(1, GDN_HEAD_DIM)],
        out_specs=rows(W),
        out_shape=jax.ShapeDtypeStruct((Bsz, S, W), f32),
        scratch_shapes=[pltpu.VMEM((SUBLANES, W3), f32), pltpu.VMEM((H, GDN_HEAD_DIM, GDN_HEAD_DIM), f32)],
        compiler_params=pltpu.CompilerParams(dimension_semantics=("arbitrary", "arbitrary"),
                                             vmem_limit_bytes=48 * 1024 * 1024),
        name="gdn_mixer",
    )(proj, proj, proj, conv_w.astype(f32), a_log.astype(f32).reshape(1, H),
      dt_bias.astype(f32).reshape(1, H), norm_w.astype(f32).reshape(1, GDN_HEAD_DIM))


def nsa_mixer(q, kv, gate_logits, q_norm, k_norm, cmp_pe, cmp_w1, cmp_w2, cos, sin):
    Bsz, S, _ = q.shape
    f32 = jnp.float32
    G, R, dh = NSA_KV_HEADS, NSA_HEADS // NSA_KV_HEADS, NSA_HEAD_DIM
    scale = dh ** -0.5
    q = rope(rms_norm(q.reshape(Bsz, S, NSA_HEADS, dh), q_norm), cos, sin)
    kc_raw, vc_raw, ks, vs, kw, vw = (t.reshape(Bsz, S, G, dh) for t in jnp.split(kv, 6, axis=-1))
    ks = rope(rms_norm(ks, k_norm[1]), cos, sin)
    kw = rope(rms_norm(kw, k_norm[2]), cos, sin)

    n_cmp = (S - CMP_LEN) // CMP_STRIDE + 1
    win_idx = jnp.arange(n_cmp)[:, None] * CMP_STRIDE + jnp.arange(CMP_LEN)[None, :]
    cmp_end = win_idx[:, -1]

    def compress(t, pe, w1, w2):
        blocks = t[:, win_idx] + pe[:, None, :]
        blocks = jnp.moveaxis(blocks, 3, 2).reshape(Bsz, n_cmp, G, CMP_LEN * dh)
        return jax.nn.gelu(blocks @ w1) @ w2

    cos_c, sin_c = rope_tables(cmp_end, dh)
    k_cmp = rope(rms_norm(compress(kc_raw, cmp_pe[0], cmp_w1[0], cmp_w2[0]), k_norm[0]), cos_c, sin_c)
    v_cmp = compress(vc_raw, cmp_pe[1], cmp_w1[1], cmp_w2[1])

    n_slc = S // SLC_BLOCK
    n_top = min(SLC_TOPK, n_slc)
    c0 = jnp.arange(n_cmp) * CMP_STRIDE
    s0 = jnp.arange(n_slc) * SLC_BLOCK
    overlap = jnp.clip(jnp.minimum(c0[:, None] + CMP_LEN, s0[None, :] + SLC_BLOCK)
                       - jnp.maximum(c0[:, None], s0[None, :]), 0, None).astype(f32) / CMP_LEN
    gates = jax.nn.sigmoid(gate_logits.astype(f32)).reshape(Bsz, S, G, R * 3).transpose(0, 2, 3, 1)

    bf16 = jnp.bfloat16
    n_cmp_pad = S // CMP_STRIDE
    pad_cmp = lambda t: jnp.pad(t.astype(bf16), ((0, 0), (0, n_cmp_pad - n_cmp), (0, 0), (0, 0)))
    keys = lambda t: t.astype(bf16).transpose(0, 2, 1, 3)
    vals = lambda t: t.astype(bf16).transpose(0, 2, 3, 1)
    q_heads = (q * scale).astype(bf16).reshape(Bsz, S, G, R, dh).transpose(0, 2, 3, 1, 4)
    overlap_t = jnp.pad(overlap.T, ((0, 0), (0, n_cmp_pad - n_cmp))).astype(bf16)
    expand_t = ((jnp.arange(S) // SLC_BLOCK)[:, None] == jnp.arange(n_slc)[None, :]).astype(bf16)
    return _nsa_attention(q_heads, keys(pad_cmp(k_cmp)), vals(pad_cmp(v_cmp)), keys(ks), vals(vs), keys(kw), vals(vw),
                          gates, overlap_t, expand_t, n_top)


NSA_R = NSA_HEADS // NSA_KV_HEADS
SLC_TILE = 1024
MASK_NEG = -1e30


def _dot_nt(a, b):
    return lax.dot_general(a, b, (((1,), (1,)), ((), ())), preferred_element_type=jnp.float32)


def _nsa_body(q_ref, kc_ref, vct_ref, ks_ref, vst_ref, kw_ref, vwt_ref, gate_ref, ovt_ref, expt_ref, o_ref,
              *, n_top):
    f32, bf16 = jnp.float32, jnp.bfloat16
    R, QB, dh = q_ref.shape
    S = ks_ref.shape[0]
    n_slc, n_cmp_pad = ovt_ref.shape
    M = R * QB
    t0 = pl.program_id(2) * QB
    q = q_ref[...].reshape(M, dh)
    per_head = lambda t: jnp.concatenate([t] * R, axis=1)

    band = S if S < WINDOW + QB else WINDOW + QB
    start = pl.multiple_of(jnp.maximum(t0 + QB - band, 0), QB)
    s = _dot_nt(kw_ref[pl.ds(start, band), :], q)
    diff = (t0 - start) + (lax.broadcasted_iota(jnp.int32, (band, QB), 1) - lax.broadcasted_iota(jnp.int32, (band, QB), 0))
    s = s + per_head(jnp.where((diff >= 0) & (diff < WINDOW), 0.0, -jnp.inf))
    p = jnp.exp(s - jnp.max(s, axis=0, keepdims=True))
    o_win = jnp.dot(vwt_ref[:, pl.ds(start, band)], p.astype(bf16), preferred_element_type=f32)
    o_win = o_win / jnp.sum(p, axis=0, keepdims=True)

    s = _dot_nt(kc_ref[...], q)
    n_idx = lax.broadcasted_iota(jnp.int32, (n_cmp_pad, QB), 0)
    t_idx = t0 + lax.broadcasted_iota(jnp.int32, (n_cmp_pad, QB), 1)
    valid = (n_idx * CMP_STRIDE + (CMP_LEN - 1) <= t_idx) & (n_idx < n_cmp_pad - 1)
    s = s + per_head(jnp.where(valid, 0.0, -jnp.inf))
    m = jnp.max(s, axis=0, keepdims=True)
    m = jnp.where(jnp.isfinite(m), m, 0.0)
    p = jnp.exp(s - m)
    p = p / jnp.maximum(jnp.sum(p, axis=0, keepdims=True), jnp.finfo(f32).tiny)
    o_cmp = jnp.dot(vct_ref[...], p.astype(bf16), preferred_element_type=f32)

    p_sum = p[:, 0:QB]
    for r in range(1, R):
        p_sum = p_sum + p[:, r * QB:(r + 1) * QB]
    imp = sum(jnp.dot(ovt_ref[...], t, preferred_element_type=f32) for t in _bf16_terms(p_sum, 3))
    j_idx = lax.broadcasted_iota(jnp.int32, (n_slc, QB), 0)
    t_lane = t0 + lax.broadcasted_iota(jnp.int32, (n_slc, QB), 1)
    cur = lax.shift_right_logical(t_lane, 6)
    forced = (j_idx == 0) | (j_idx == cur) | (j_idx == cur - 1)
    started = j_idx * SLC_BLOCK <= t_lane
    imp = jnp.where(forced, BIG, jnp.where(started, imp, -BIG))
    rank = jnp.zeros((n_slc, QB), f32)
    for jp in range(n_slc):
        row = imp[jp:jp + 1, :]
        ahead = (row > imp) | ((row == imp) & (j_idx > jp))
        rank = rank + jnp.where(ahead, 1.0, 0.0)
    sel = jnp.where(rank < n_top, 1.0, 0.0).astype(bf16)
    k_rel = lax.broadcasted_iota(jnp.int32, (SLC_TILE, QB), 0)
    t_q = t0 + lax.broadcasted_iota(jnp.int32, (SLC_TILE, QB), 1)

    def slc_step(kt, carry):
        m, l, acc = carry
        k0 = pl.multiple_of(kt * SLC_TILE, SLC_TILE)
        s = _dot_nt(ks_ref[pl.ds(k0, SLC_TILE), :], q)
        sel_keys = jnp.dot(expt_ref[pl.ds(k0, SLC_TILE), :], sel, preferred_element_type=f32)
        bias = jnp.where((sel_keys > 0.5) & (k_rel + k0 <= t_q), 0.0, MASK_NEG)
        s = s + per_head(bias)
        m_new = jnp.maximum(m, jnp.max(s, axis=0, keepdims=True))
        alpha = jnp.exp(m - m_new)
        p = jnp.exp(s - m_new)
        l = alpha * l + jnp.sum(p, axis=0, keepdims=True)
        acc = alpha * acc + jnp.dot(vst_ref[:, pl.ds(k0, SLC_TILE)], p.astype(bf16), preferred_element_type=f32)
        return m_new, l, acc

    n_tiles = (t0 + QB - 1) // SLC_TILE + 1
    init = (jnp.full((1, M), MASK_NEG, f32), jnp.zeros((1, M), f32), jnp.zeros((dh, M), f32))
    _, l, acc = lax.fori_loop(0, n_tiles, slc_step, init)
    o_slc = acc / l

    gate = gate_ref[...]
    g_cmp, g_slc, g_win = (jnp.concatenate([gate[3 * r + c:3 * r + c + 1, :] for r in range(R)], axis=1)
                           for c in range(3))
    o_t = g_cmp * o_cmp + g_slc * o_slc + g_win * o_win
    o_ref[...] = jnp.concatenate([o_t[:, r * QB:(r + 1) * QB].T for r in range(R)], axis=1)


def _nsa_attention(q, kc, vct, ks, vst, kw, vwt, gates, overlap_t, expand_t, n_top):
    Bsz, G, R, S, dh = q.shape
    n_cmp_pad = kc.shape[2]
    assert S % SLC_TILE == 0 and SLC_TILE % Q_BLOCK == 0, (S, SLC_TILE, Q_BLOCK)
    keys = lambda n: pl.BlockSpec((None, None, n, dh), lambda b, g, i: (b, g, 0, 0))
    vals = lambda n: pl.BlockSpec((None, None, dh, n), lambda b, g, i: (b, g, 0, 0))
    const = lambda a: pl.BlockSpec(a.shape, lambda b, g, i: (0, 0))
    return pl.pallas_call(
        functools.partial(_nsa_body, n_top=n_top),
        grid=(Bsz, G, S // Q_BLOCK),
        in_specs=[pl.BlockSpec((None, None, R, Q_BLOCK, dh), lambda b, g, i: (b, g, 0, i, 0)),
                  keys(n_cmp_pad), vals(n_cmp_pad), keys(S), vals(S), keys(S), vals(S),
                  pl.BlockSpec((None, None, 3 * R, Q_BLOCK), lambda b, g, i: (b, g, 0, i)),
                  const(overlap_t), const(expand_t)],
        out_specs=pl.BlockSpec((None, Q_BLOCK, R * dh), lambda b, g, i: (b, i, g)),
        out_shape=jax.ShapeDtypeStruct((Bsz, S, G * R * dh), jnp.float32),
        compiler_params=pltpu.CompilerParams(dimension_semantics=("arbitrary", "arbitrary", "arbitrary"),
                                             vmem_limit_bytes=48 * 1024 * 1024),
        name="nsa_attention",
    )(q, kc, vct, ks, vst, kw, vwt, gates, overlap_t, expand_t)


def hybrid_mixer(x, ln_mix, w_in, lam_re, lam_im, b_re, b_im, c_re, c_im, s5_d, s5_log_dt, glu_w, glu_b,
                 s5_out_norm, conv_w, a_log, dt_bias, gdn_norm_w, q_norm, k_norm, cmp_pe, cmp_w1,
                 cmp_w2, cos, sin):
    Bsz, S, D = x.shape
    proj = _in_proj(x.reshape(-1, D), ln_mix, _in_proj_weight(w_in)).reshape(Bsz, S, PROJ_W)
    y_s5 = s5_mixer_normed(proj, lam_re, lam_im, b_re, b_im, c_re, c_im, s5_d, s5_log_dt, glu_w, glu_b, s5_out_norm)
    y_gdn = gdn_mixer(proj, conv_w, a_log, dt_bias, gdn_norm_w)
    y_nsa = nsa_mixer(proj[..., PROJ_Q:PROJ_Q + NSA_WIDTH], proj[..., PROJ_KV:PROJ_KV + 6 * NSA_KV_WIDTH],
                      proj[..., PROJ_GATES:PROJ_GATES + 3 * NSA_HEADS], q_norm, k_norm, cmp_pe, cmp_w1, cmp_w2,
                      cos, sin)
    return y_s5, y_gdn, y_nsa


def hier_moe(x, h, logits, layer, w_gate, w_up, w_down):
    T, D = x.shape
    f32 = jnp.float32
    g_logits = logits[:, :N_GROUPS]
    grp = jnp.argmax(g_logits, axis=-1)
    p_grp = jnp.take_along_axis(jax.nn.softmax(g_logits, axis=-1), grp[:, None], axis=-1)
    e_logits = logits[:, N_GROUPS:N_ROUTER].reshape(T, N_GROUPS, EXPERTS_PER_GROUP)
    e_logits = jnp.take_along_axis(e_logits, grp[:, None, None], axis=1)[:, 0]
    local = jnp.arange(EXPERTS_PER_GROUP, dtype=jnp.int32)[None, :]
    rest, top_logit, top_local = e_logits, [], []
    for _ in range(TOP_K):
        idx = jnp.argmax(rest, axis=-1).astype(jnp.int32)
        top_logit.append(jnp.take_along_axis(rest, idx[:, None], axis=-1)[:, 0])
        top_local.append(idx)
        rest = jnp.where(local == idx[:, None], -jnp.inf, rest)
    top_logit, top_local = jnp.stack(top_logit, axis=-1), jnp.stack(top_local, axis=-1)
    weights = jax.nn.softmax(top_logit, axis=-1) * p_grp
    experts = grp[:, None].astype(jnp.int32) * EXPERTS_PER_GROUP + top_local
    n_assign = T * TOP_K
    e_flat = experts.reshape(-1)
    tok_flat = jnp.repeat(jnp.arange(T, dtype=jnp.int32), TOP_K)
    onehot = (e_flat[:, None] == jnp.arange(N_EXPERTS, dtype=jnp.int32)[None, :]).astype(jnp.int32)
    running = jnp.cumsum(onehot, axis=0)
    counts = running[-1]
    padded = (counts + MOE_ROWS - 1) // MOE_ROWS * MOE_ROWS
    pad_start = jnp.cumsum(padded) - padded
    dest = (pad_start[e_flat] + jnp.take_along_axis(running, e_flat[:, None], axis=1)[:, 0] - 1).astype(jnp.int32)
    n_blk = (n_assign + MOE_ROWS - 1) // MOE_ROWS + N_EXPERTS
    cap = n_blk * MOE_ROWS
    slot_tok = jnp.full((cap,), T, jnp.int32).at[dest].set(tok_flat)
    blk_expert = jnp.minimum(jnp.searchsorted(jnp.cumsum(padded), jnp.arange(n_blk) * MOE_ROWS, side='right'),
                             N_EXPERTS - 1).astype(jnp.int32)
    x_pad = jnp.concatenate([x, jnp.zeros((1, D), x.dtype)], axis=0)
    y = _moe_ffn(blk_expert, slot_tok, x_pad, layer, w_gate.astype(f32), w_up.astype(f32),
                 w_down.astype(f32))
    return _moe_combine(dest, h, weights, y)


CMB_ROWS = 128


def _moe_combine_body(d_ref, h_ref, w_ref, y_hbm, o_ref, buf, sem):
    i = pl.program_id(0)
    n = pl.num_programs(0)
    rows = h_ref.shape[0]
    slot = i & 1

    def row_copy(step, r, k, to):
        src = y_hbm.at[pl.ds(d_ref[(step * rows + r) * TOP_K + k], 1), :]
        return pltpu.make_async_copy(src, buf.at[to, k, pl.ds(r, 1), :], sem.at[to])

    def start_step(step, to):
        for r in range(rows):
            for k in range(TOP_K):
                row_copy(step, r, k, to).start(priority=(r * TOP_K + k) % 2)

    def finish_step():
        for r in range(rows):
            for k in range(TOP_K):
                row_copy(i, r, k, slot).wait()
        out = h_ref[...]
        for k in range(TOP_K):
            out = out + w_ref[:, k:k + 1] * buf[slot, k]
        o_ref[...] = out

    @pl.when(i == 0)
    def _():
        start_step(0, 0)

    @pl.when(i + 1 < n)
    def _():
        start_step(i + 1, 1 - slot)
        finish_step()

    @pl.when(i + 1 == n)
    def _():
        finish_step()


def _moe_combine(dest, h, weights, y):
    T, D = h.shape
    rows = min(CMB_ROWS, T)
    return pl.pallas_call(
        _moe_combine_body,
        grid_spec=pltpu.PrefetchScalarGridSpec(
            num_scalar_prefetch=1, grid=(T // rows,),
            in_specs=[pl.BlockSpec((rows, D), lambda i, d: (i, 0)),
                      pl.BlockSpec((rows, TOP_K), lambda i, d: (i, 0)),
                      pl.BlockSpec(memory_space=pl.ANY)],
            out_specs=pl.BlockSpec((rows, D), lambda i, d: (i, 0)),
            scratch_shapes=[pltpu.VMEM((2, TOP_K, rows, D), jnp.float32), pltpu.SemaphoreType.DMA((2,))]),
        out_shape=jax.ShapeDtypeStruct((T, D), jnp.float32),
        compiler_params=pltpu.CompilerParams(dimension_semantics=("arbitrary",),
                                             vmem_limit_bytes=32 * 1024 * 1024),
        name="moe_combine",
    )(dest, h, weights, y)


MOE_ROWS = 256


def _moe_ffn_body(be_ref, tok_ref, x_hbm, wg_ref, wu_ref, wd_ref, o_ref, xbuf, sem, wg_bf, wu_bf, wd_bf):
    i = pl.program_id(0)
    n = pl.num_programs(0)
    slot = i & 1

    def row_copy(step, r, to):
        src = x_hbm.at[pl.ds(tok_ref[step * MOE_ROWS + r], 1), :]
        return pltpu.make_async_copy(src, xbuf.at[to, pl.ds(r, 1), :], sem.at[to])

    def start_step(step, to):
        for r in range(MOE_ROWS):
            row_copy(step, r, to).start(priority=r % 2)

    def finish_step():
        for r in range(MOE_ROWS):
            row_copy(i, r, slot).wait()
        x = xbuf[slot].astype(jnp.bfloat16)
        gate = jnp.dot(x, wg_bf[...], preferred_element_type=jnp.float32)
        up = jnp.dot(x, wu_bf[...], preferred_element_type=jnp.float32)
        hid = (jax.nn.silu(gate) * up).astype(jnp.bfloat16)
        o_ref[...] = jnp.dot(hid, wd_bf[...], preferred_element_type=jnp.float32)

    @pl.when(i == 0)
    def _():
        start_step(0, 0)

    @pl.when((i == 0) | (be_ref[i] != be_ref[jnp.maximum(i - 1, 0)]))
    def _():
        wg_bf[...] = wg_ref[...].astype(jnp.bfloat16)
        wu_bf[...] = wu_ref[...].astype(jnp.bfloat16)
        wd_bf[...] = wd_ref[...].astype(jnp.bfloat16)

    @pl.when(i + 1 < n)
    def _():
        start_step(i + 1, 1 - slot)
        finish_step()

    @pl.when(i + 1 == n)
    def _():
        finish_step()


def _moe_ffn(blk_expert, slot_tok, x, layer, w_gate, w_up, w_down):
    cap = slot_tok.shape[0]
    D = x.shape[1]
    n_blk = cap // MOE_ROWS
    De = w_gate.shape[-1]
    return pl.pallas_call(
        _moe_ffn_body,
        grid_spec=pltpu.PrefetchScalarGridSpec(
            num_scalar_prefetch=2, grid=(n_blk,),
            in_specs=[pl.BlockSpec(memory_space=pl.ANY),
                      pl.BlockSpec((None, None, D, De), lambda i, be, tok: (layer, be[i], 0, 0)),
                      pl.BlockSpec((None, None, D, De), lambda i, be, tok: (layer, be[i], 0, 0)),
                      pl.BlockSpec((None, None, De, D), lambda i, be, tok: (layer, be[i], 0, 0))],
            out_specs=pl.BlockSpec((MOE_ROWS, D), lambda i, be, tok: (i, 0)),
            scratch_shapes=[pltpu.VMEM((2, MOE_ROWS, D), jnp.float32), pltpu.SemaphoreType.DMA((2,)),
                            pltpu.VMEM((D, De), jnp.bfloat16), pltpu.VMEM((D, De), jnp.bfloat16),
                            pltpu.VMEM((De, D), jnp.bfloat16)]),
        out_shape=jax.ShapeDtypeStruct((cap, D), jnp.float32),
        compiler_params=pltpu.CompilerParams(dimension_semantics=("arbitrary",),
                                             vmem_limit_bytes=56 * 1024 * 1024),
        name="moe_ffn",
    )(blk_expert, slot_tok, x, w_gate, w_up, w_down)


def kernel(x, ln_mix, w_in, s5_lambda_re, s5_lambda_im, s5_b_re, s5_b_im, s5_c_re, s5_c_im, s5_d,
           s5_log_dt, s5_glu_w, s5_glu_b, s5_out_norm, gdn_conv_w, gdn_a_log, gdn_dt_bias, gdn_norm_w,
           nsa_q_norm, nsa_k_norm, nsa_cmp_pe, nsa_cmp_w1, nsa_cmp_w2, nsa_out_norm, w_out, ln_ffn,
           moe_group_w, moe_group_b, moe_expert_w, moe_expert_b, moe_w_gate, moe_w_up, moe_w_down):
    Bsz, S, D = x.shape
    cos, sin = rope_tables(jnp.arange(S), NSA_HEAD_DIM)
    for l in range(DEPTH):
        y_s5, y_gdn, y_nsa = hybrid_mixer(
            x, ln_mix[l], w_in[l], s5_lambda_re[l], s5_lambda_im[l], s5_b_re[l], s5_b_im[l], s5_c_re[l], s5_c_im[l],
            s5_d[l], s5_log_dt[l], s5_glu_w[l], s5_glu_b[l], s5_out_norm[l], gdn_conv_w[l], gdn_a_log[l],
            gdn_dt_bias[l], gdn_norm_w[l], nsa_q_norm[l], nsa_k_norm[l], nsa_cmp_pe[l], nsa_cmp_w1[l], nsa_cmp_w2[l],
            cos, sin)
        router_w = jnp.concatenate([moe_group_w[l], moe_expert_w[l]], axis=-1)
        router_b = jnp.concatenate([moe_group_b[l], moe_expert_b[l]], axis=-1)
        h, a, logits = _out_proj(x.reshape(-1, D), y_s5.reshape(-1, S5_WIDTH), y_gdn.reshape(-1, GDN_WIDTH),
                                 y_nsa.reshape(-1, NSA_WIDTH), nsa_out_norm[l], w_out[l], ln_ffn[l], router_w, router_b)
        x = hier_moe(a, h, logits, l, moe_w_gate, moe_w_up, moe_w_down).reshape(Bsz, S, D)
    return x
```

```python
import functools
import math
import jax, jax.numpy as jnp
from jax import lax
import numpy as np
from jax.experimental import pallas as pl
from jax.experimental.pallas import tpu as pltpu

D_MODEL = 2048
BATCH = 4
SEQ = 4096
DEPTH = 2

EPS = 1e-6
S5_WIDTH = D_MODEL // 4
S5_GROUP = 16
S5_GROUPS = S5_WIDTH // S5_GROUP
S5_STATE = 64
GDN_HEAD_DIM = 128
GDN_WIDTH = D_MODEL // 4
GDN_HEADS = GDN_WIDTH // GDN_HEAD_DIM
GDN_CONV = 4
GDN_CHUNK = 64
NSA_HEAD_DIM = 64
NSA_WIDTH = D_MODEL // 2
NSA_HEADS = NSA_WIDTH // NSA_HEAD_DIM
NSA_KV_HEADS = 4
NSA_KV_WIDTH = NSA_KV_HEADS * NSA_HEAD_DIM
CMP_LEN = 32
CMP_STRIDE = 16
CMP_HIDDEN = 256
SLC_BLOCK = 64
SLC_TOPK = 16
WINDOW = 512
Q_BLOCK = 128
ROPE_THETA = 10000.0
BIG = 1e9
N_GROUPS = 4
EXPERTS_PER_GROUP = 8
N_EXPERTS = N_GROUPS * EXPERTS_PER_GROUP
TOP_K = 2
D_EXPERT = 512
MOE_BLOCK = 128
IN_SIZES = (S5_WIDTH, 3 * GDN_WIDTH, GDN_WIDTH, GDN_HEADS, GDN_HEADS, NSA_WIDTH, 6 * NSA_KV_WIDTH, 3 * NSA_HEADS)
D_IN = sum(IN_SIZES)


PROJ_QKV, PROJ_KV, PROJ_Q, PROJ_U, PROJ_Z, PROJ_SMALL = 0, 1536, 3072, 4096, 4608, 5120
PROJ_SMALL_W = 512
PROJ_W = PROJ_SMALL + PROJ_SMALL_W
PROJ_GATES = PROJ_SMALL + 2 * GDN_HEADS
LANES = 128
IN_ROWS, IN_COLS = 1024, 512
OUT_ROWS = 256
N_ROUTER = N_GROUPS + N_EXPERTS


def _in_proj_weight(w_in):
    u, qkv, z, ga, gb, q, kv, gl = jnp.split(w_in, np.cumsum(IN_SIZES)[:-1].tolist(), axis=-1)
    small = jnp.concatenate([ga, gb, gl], axis=-1)
    small = jnp.pad(small, ((0, 0), (0, PROJ_SMALL_W - small.shape[-1])))
    return jnp.concatenate([qkv, kv, q, u, z, small], axis=-1)


def _in_proj_body(x_ref, g_ref, w_ref, o_ref, a_ref):
    @pl.when(pl.program_id(1) == 0)
    def _():
        x = x_ref[...]
        a_ref[...] = (x * lax.rsqrt(jnp.mean(x * x, axis=-1, keepdims=True) + EPS) * g_ref[...]).astype(jnp.bfloat16)

    o_ref[...] = jnp.dot(a_ref[...], w_ref[...].astype(jnp.bfloat16), preferred_element_type=jnp.float32)


def _in_proj(x, gain, w):
    T, D = x.shape
    N = w.shape[1]
    tm = min(IN_ROWS, T)
    return pl.pallas_call(
        _in_proj_body,
        grid=(T // tm, N // IN_COLS),
        in_specs=[pl.BlockSpec((tm, D), lambda i, j: (i, 0)),
                  pl.BlockSpec((1, D), lambda i, j: (0, 0)),
                  pl.BlockSpec((D, IN_COLS), lambda i, j: (0, j))],
        out_specs=pl.BlockSpec((tm, IN_COLS), lambda i, j: (i, j)),
        out_shape=jax.ShapeDtypeStruct((T, N), jnp.float32),
        scratch_shapes=[pltpu.VMEM((tm, D), jnp.bfloat16)],
        compiler_params=pltpu.CompilerParams(dimension_semantics=("arbitrary", "arbitrary"),
                                             vmem_limit_bytes=48 * 1024 * 1024),
        name="in_proj",
    )(x, gain.astype(jnp.float32).reshape(1, D), w)


def _out_proj_body(x_ref, s5_ref, gdn_ref, nsa_ref, ng_ref, w_ref, fg_ref, rw_ref, rb_ref, h_ref, a_ref, lg_ref):
    f32, bf16 = jnp.float32, jnp.bfloat16
    nsa = nsa_ref[...]
    nsa = nsa * lax.rsqrt(jnp.mean(nsa * nsa, axis=-1, keepdims=True) + EPS) * ng_ref[...]
    k0, k1 = S5_WIDTH, S5_WIDTH + GDN_WIDTH
    y = (jnp.dot(s5_ref[...].astype(bf16), w_ref[0:k0, :], preferred_element_type=f32)
         + jnp.dot(gdn_ref[...].astype(bf16), w_ref[k0:k1, :], preferred_element_type=f32)
         + jnp.dot(nsa.astype(bf16), w_ref[k1:, :], preferred_element_type=f32))
    h = x_ref[...] + y
    h_ref[...] = h
    a = h * lax.rsqrt(jnp.mean(h * h, axis=-1, keepdims=True) + EPS) * fg_ref[...]
    a_ref[...] = a
    lg_ref[...] = jnp.dot(a.astype(bf16), rw_ref[...], preferred_element_type=f32) + rb_ref[...]


def _out_proj(x, y_s5, y_gdn, y_nsa, nsa_gain, w_out, ffn_gain, router_w, router_b):
    T, D = x.shape
    tm = min(OUT_ROWS, T)
    f32, bf16 = jnp.float32, jnp.bfloat16
    rw = jnp.pad(router_w.astype(bf16), ((0, 0), (0, LANES - N_ROUTER)))
    rb = jnp.pad(router_b.astype(f32), (0, LANES - N_ROUTER)).reshape(1, LANES)
    rows = lambda n: pl.BlockSpec((tm, n), lambda i: (i, 0))
    full = lambda *shape: pl.BlockSpec(shape, lambda i: (0,) * len(shape))
    return pl.pallas_call(
        _out_proj_body,
        grid=(T // tm,),
        in_specs=[rows(D), rows(S5_WIDTH), rows(GDN_WIDTH), rows(NSA_WIDTH), full(1, NSA_WIDTH), full(D, D),
                  full(1, D), full(D, LANES), full(1, LANES)],
        out_specs=[rows(D), rows(D), rows(LANES)],
        out_shape=[jax.ShapeDtypeStruct((T, D), f32), jax.ShapeDtypeStruct((T, D), f32),
                   jax.ShapeDtypeStruct((T, LANES), f32)],
        compiler_params=pltpu.CompilerParams(dimension_semantics=("arbitrary",),
                                             vmem_limit_bytes=48 * 1024 * 1024),
        name="out_proj",
    )(x, y_s5, y_gdn, y_nsa, nsa_gain.astype(f32).reshape(1, NSA_WIDTH), w_out.astype(bf16),
      ffn_gain.astype(f32).reshape(1, D), rw, rb)


def rms_norm(x, gain):
    xf = x.astype(jnp.float32)
    y = xf * lax.rsqrt(jnp.mean(xf * xf, axis=-1, keepdims=True) + EPS)
    return (y * gain.astype(jnp.float32)).astype(x.dtype)


def l2_norm(x):
    return x * lax.rsqrt(jnp.sum(x * x, axis=-1, keepdims=True) + EPS)


def rope_tables(pos, dim):
    inv_freq = ROPE_THETA ** (-jnp.arange(0, dim, 2, dtype=jnp.float32) / dim)
    ang = pos.astype(jnp.float32)[:, None] * inv_freq[None, :]
    return jnp.cos(ang)[:, None, :], jnp.sin(ang)[:, None, :]


def rope(x, cos, sin):
    x1, x2 = jnp.split(x, 2, axis=-1)
    return jnp.concatenate([x1 * cos - x2 * sin, x2 * cos + x1 * sin], axis=-1).astype(x.dtype)


def masked_softmax(s, mask):
    s = jnp.where(mask, s, -jnp.inf)
    m = jnp.max(s, axis=-1, keepdims=True)
    m = jnp.where(jnp.isfinite(m), m, 0.0)
    p = jnp.exp(s - m)
    return p / jnp.maximum(jnp.sum(p, axis=-1, keepdims=True), jnp.finfo(jnp.float32).tiny)


S5_OCT = 8
S5_NOCT = S5_GROUPS // S5_OCT
S5_OCT_IN = S5_OCT * S5_GROUP
S5_OCT_ST = S5_OCT * S5_STATE
SUBLANES = 8
S5_SCAN_SHIFTS = (1, 2, 4)


def _s5_constants(lam_re, lam_im, b_re, b_im, c_re, c_im, log_dt):
    f32 = jnp.float32
    lr, li = lam_re.astype(f32), lam_im.astype(f32)
    dt = jnp.exp(log_dt.astype(f32))[:, None]

    def lam_pow(k):
        mag = jnp.exp(k * lr * dt)
        return mag * jnp.cos(k * li * dt), mag * jnp.sin(k * li * dt)

    ar, ai = lam_pow(1.0)
    nr, ni = ar - 1.0, ai
    den = lr * lr + li * li
    qr, qi = (nr * lr + ni * li) / den, (ni * lr - nr * li) / den
    br, bi = b_re.astype(f32), b_im.astype(f32)
    bbr = qr[:, :, None] * br - qi[:, :, None] * bi
    bbi = qr[:, :, None] * bi + qi[:, :, None] * br
    eye = jnp.eye(S5_OCT, dtype=f32)

    def blk_b(t):
        t = t.reshape(S5_NOCT, S5_OCT, S5_STATE, S5_GROUP)
        return jnp.einsum('qgph,gk->qghkp', t, eye).reshape(S5_NOCT, S5_OCT_IN, S5_OCT_ST)

    def blk_c(t):
        t = t.reshape(S5_NOCT, S5_OCT, S5_GROUP, S5_STATE)
        return jnp.einsum('qghp,gk->qgpkh', t, eye).reshape(S5_NOCT, S5_OCT_ST, S5_OCT_IN)

    b_mat = jnp.stack([blk_b(bbr), blk_b(bbi)]).astype(jnp.bfloat16)
    c_mat = jnp.stack([blk_c(c_re.astype(f32)), blk_c(-c_im.astype(f32))]).astype(jnp.bfloat16)
    rows = jnp.arange(SUBLANES, dtype=f32)[:, None, None]
    tiles = []
    for k in S5_SCAN_SHIFTS:
        pr, pi = lam_pow(jnp.full_like(rows, float(k)))
        keep = rows >= k
        tiles.append(jnp.stack([jnp.where(keep, pr, 0.0), jnp.where(keep, pi, 0.0)]))
    pr, pi = lam_pow(rows + 1.0)
    tiles.append(jnp.stack([pr, pi]))
    a_mat = jnp.stack(tiles).reshape(len(tiles), 2, SUBLANES, S5_GROUPS * S5_STATE)
    return b_mat, c_mat, a_mat


def _s5_body(u_ref, b_ref, c_ref, a_ref, d_ref, gw_ref, gb_ref, gain_ref, o_ref, st_ref, xr_ref, xi_ref):
    bf16 = jnp.bfloat16
    n_tiles = u_ref.shape[0] // SUBLANES

    @pl.when(pl.program_id(1) == 0)
    def _():
        st_ref[...] = jnp.zeros_like(st_ref)

    u = u_ref[...]
    ys = []
    for q in range(S5_NOCT):
        lanes = slice(q * S5_OCT_ST, (q + 1) * S5_OCT_ST)
        uq = u[:, q * S5_OCT_IN:(q + 1) * S5_OCT_IN].astype(bf16)
        xr_ref[...] = jnp.dot(uq, b_ref[0, q], preferred_element_type=jnp.float32)
        xi_ref[...] = jnp.dot(uq, b_ref[1, q], preferred_element_type=jnp.float32)

        def tile_step(i, carry, lanes=lanes):
            cr, ci = carry
            r0 = pl.multiple_of(i * SUBLANES, SUBLANES)
            xr = xr_ref[pl.ds(r0, SUBLANES), :]
            xi = xi_ref[pl.ds(r0, SUBLANES), :]
            for lvl, k in enumerate(S5_SCAN_SHIFTS):
                pr, pi = a_ref[lvl, 0, :, lanes], a_ref[lvl, 1, :, lanes]
                sr, si = pltpu.roll(xr, k, 0), pltpu.roll(xi, k, 0)
                xr, xi = xr + (pr * sr - pi * si), xi + (pr * si + pi * sr)
            pr, pi = a_ref[len(S5_SCAN_SHIFTS), 0, :, lanes], a_ref[len(S5_SCAN_SHIFTS), 1, :, lanes]
            cr = jnp.broadcast_to(cr, xr.shape)
            ci = jnp.broadcast_to(ci, xi.shape)
            xr, xi = xr + (pr * cr - pi * ci), xi + (pr * ci + pi * cr)
            xr_ref[pl.ds(r0, SUBLANES), :] = xr
            xi_ref[pl.ds(r0, SUBLANES), :] = xi
            return xr[SUBLANES - 1:, :], xi[SUBLANES - 1:, :]

        cr, ci = lax.fori_loop(0, n_tiles, tile_step, (st_ref[0, :, lanes], st_ref[1, :, lanes]))
        st_ref[0, :, lanes] = cr
        st_ref[1, :, lanes] = ci
        ys.append(jnp.dot(xr_ref[...].astype(bf16), c_ref[0, q], preferred_element_type=jnp.float32)
                  + jnp.dot(xi_ref[...].astype(bf16), c_ref[1, q], preferred_element_type=jnp.float32))
    y = jnp.concatenate(ys, axis=1) + d_ref[...] * u
    y = jax.nn.gelu(y)
    z = jnp.dot(y.astype(bf16), gw_ref[...], preferred_element_type=jnp.float32) + gb_ref[...]
    y = y * jax.nn.sigmoid(z)
    y = y * lax.rsqrt(jnp.mean(y * y, axis=-1, keepdims=True) + EPS)
    o_ref[...] = y * gain_ref[...]


def s5_mixer_normed(proj, lam_re, lam_im, b_re, b_im, c_re, c_im, d_skip, log_dt, glu_w, glu_b, out_norm, chunk=512):
    Bsz, S, _ = proj.shape
    W = S5_WIDTH
    f32 = jnp.float32
    L = min(chunk, S)
    b_mat, c_mat, a_mat = _s5_constants(lam_re, lam_im, b_re, b_im, c_re, c_im, log_dt)
    n_state = S5_GROUPS * S5_STATE
    full = lambda *shape: pl.BlockSpec(shape, lambda b, c: (0,) * len(shape))
    return pl.pallas_call(
        _s5_body,
        grid=(Bsz, S // L),
        in_specs=[pl.BlockSpec((None, L, W), lambda b, c: (b, c, PROJ_U // W)),
                  full(*b_mat.shape), full(*c_mat.shape), full(*a_mat.shape),
                  full(1, W), full(W, W), full(1, W), full(1, W)],
        out_specs=pl.BlockSpec((None, L, W), lambda b, c: (b, c, 0)),
        out_shape=jax.ShapeDtypeStruct((Bsz, S, W), f32),
        scratch_shapes=[pltpu.VMEM((2, 1, n_state), f32),
                        pltpu.VMEM((L, S5_OCT_ST), f32), pltpu.VMEM((L, S5_OCT_ST), f32)],
        compiler_params=pltpu.CompilerParams(dimension_semantics=("arbitrary", "arbitrary"),
                                             vmem_limit_bytes=48 * 1024 * 1024),
        name="s5_mixer",
    )(proj, b_mat, c_mat, a_mat, d_skip.astype(f32).reshape(1, W), glu_w.astype(jnp.bfloat16),
      glu_b.astype(f32).reshape(1, W), out_norm.astype(f32).reshape(1, W))


GDN_ROWS = 256
GDN_PACK = GDN_HEADS * GDN_CHUNK


def _bf16_terms(x, n):
    terms, rem = [], x
    for i in range(n):
        t = rem.astype(jnp.bfloat16)
        terms.append(t)
        if i + 1 < n:
            rem = rem - t.astype(jnp.float32)
    return terms


def _dot_exact_lhs(lhs01, x):
    lhs = lhs01.astype(jnp.bfloat16)
    return sum(jnp.dot(lhs, t, preferred_element_type=jnp.float32) for t in _bf16_terms(x, 3))


def _gdn_packed_unit_lower_inverse(a_mats, row, col, blk_diag):
    f32 = jnp.float32

    def mm(x, y):
        xh, xl = _bf16_terms(x, 2)
        yh, yl = (jnp.where(blk_diag, jnp.concatenate([t] * GDN_HEADS, axis=0), 0.0) for t in _bf16_terms(y, 2))
        return (jnp.dot(xh, yh, preferred_element_type=f32)
                + (jnp.dot(xh, yl, preferred_element_type=f32) + jnp.dot(xl, yh, preferred_element_type=f32)))

    same16 = (row >> 4) == (col >> 4)
    same32 = (row >> 5) == (col >> 5)
    eye = jnp.where(row == col, 1.0, 0.0)
    d1 = [jnp.where(same16, a, 0.0) for a in a_mats]
    d2 = [mm(d, d) for d in d1]
    d4 = [mm(d, d) for d in d2]
    d8 = [mm(d, d) for d in d4]
    lo = [mm(eye - a, eye + b) for a, b in zip(d1, d2)]
    hi = [mm(eye + a, eye + b) for a, b in zip(d4, d8)]
    t = [mm(a, b) for a, b in zip(lo, hi)]
    for pick in (lambda a: jnp.where(same32 & jnp.logical_not(same16), a, 0.0), lambda a: jnp.where(same32, 0.0, a)):
        mid = [mm(ti, pick(a)) for ti, a in zip(t, a_mats)]
        t = [ti - mm(m, ti) for ti, m in zip(t, mid)]
    return t


def _gdn_body(qkv_ref, z_ref, ab_ref, cw_ref, alog_ref, dtb_ref, nw_ref, o_ref, halo_ref, st_ref):
    f32, bf16 = jnp.float32, jnp.bfloat16
    L = qkv_ref.shape[0]
    C, H, dh, W = GDN_CHUNK, GDN_HEADS, GDN_HEAD_DIM, GDN_WIDTH

    @pl.when(pl.program_id(1) == 0)
    def _():
        halo_ref[...] = jnp.zeros_like(halo_ref)
        st_ref[...] = jnp.zeros_like(st_ref)

    x = qkv_ref[...]
    prev = halo_ref[...]
    row8 = lax.broadcasted_iota(jnp.int32, prev.shape, 0)
    acc = cw_ref[GDN_CONV - 1:GDN_CONV, :] * x
    for s in range(1, GDN_CONV):
        xs = pltpu.roll(x, s, 0)
        head = jnp.where(row8 < s, pltpu.roll(prev, s, 0), xs[0:SUBLANES])
        xs = jnp.concatenate([head, xs[SUBLANES:]], axis=0)
        acc = acc + cw_ref[GDN_CONV - 1 - s:GDN_CONV - s, :] * xs
    halo_ref[...] = x[L - SUBLANES:L]
    y = jax.nn.silu(acc)

    a_in, b_in = ab_ref[:, 0:H], ab_ref[:, H:2 * H]
    beta_all = jax.nn.sigmoid(b_in)
    g_all = -jnp.exp(alog_ref[...]) * jax.nn.softplus(a_in + dtb_ref[...])

    row = lax.broadcasted_iota(jnp.int32, (C, GDN_PACK), 0)
    col = lax.broadcasted_iota(jnp.int32, (C, GDN_PACK), 1) & (C - 1)
    strict = row > col
    bd_r = lax.broadcasted_iota(jnp.int32, (GDN_PACK, GDN_PACK), 0)
    bd_c = lax.broadcasted_iota(jnp.int32, (GDN_PACK, GDN_PACK), 1)
    blk_diag = (bd_r >> 6) == (bd_c >> 6)

    lr = lax.broadcasted_iota(jnp.int32, (L, L), 0)
    lc = lax.broadcasted_iota(jnp.int32, (L, L), 1)
    same_chunk = (lr >> 6) == (lc >> 6)
    tri = jnp.where(same_chunk & (lr >= lc), 1.0, 0.0)
    head_of_lane = lax.broadcasted_iota(jnp.int32, (H, GDN_PACK), 1) >> 6
    spread = jnp.where(head_of_lane == lax.broadcasted_iota(jnp.int32, (H, GDN_PACK), 0), 1.0, 0.0)
    pr = lax.broadcasted_iota(jnp.int32, (L, GDN_PACK), 0) & (C - 1)
    pc = lax.broadcasted_iota(jnp.int32, (L, GDN_PACK), 1) & (C - 1)
    gc_all = _dot_exact_lhs(tri, g_all)
    gi_all = sum(jnp.dot(t, spread.astype(bf16), preferred_element_type=f32) for t in _bf16_terms(gc_all, 3))
    gj_all = _dot_exact_lhs(jnp.where(same_chunk, 1.0, 0.0), jnp.where(pr == pc, gi_all, 0.0))
    decay_all = jnp.exp(jnp.where(pr >= pc, gi_all - gj_all, -jnp.inf))
    egc_all = jnp.exp(gc_all)

    def l2n(t):
        return t * lax.rsqrt(jnp.sum(t * t, axis=-1, keepdims=True) + EPS)

    n_chunks = L // C
    chunk_rows = [slice(c * C, (c + 1) * C) for c in range(n_chunks)]
    qs, ks, vs, kbs, a_mats, qk_ps = [], [], [], [], [], []
    for rows in chunk_rows:
        kk, qk = [], []
        for h in range(H):
            q = l2n(y[rows, h * dh:(h + 1) * dh]) * dh ** -0.5
            k = l2n(y[rows, W + h * dh:W + (h + 1) * dh])
            kb = k * beta_all[rows, h:h + 1]
            qs.append(q), ks.append(k), vs.append(y[rows, 2 * W + h * dh:2 * W + (h + 1) * dh]), kbs.append(kb)
            kk.append(_dot_nt(kb.astype(bf16), k.astype(bf16)))
            qk.append(_dot_nt(q.astype(bf16), k.astype(bf16)))
        a_mats.append(jnp.where(strict, jnp.concatenate(kk, axis=1) * decay_all[rows], 0.0))
        qk_ps.append(jnp.concatenate(qk, axis=1) * decay_all[rows])
    t_ps = _gdn_packed_unit_lower_inverse(a_mats, row, col, blk_diag)
    for c, rows in enumerate(chunk_rows):
        beta, gc, egc = beta_all[rows], gc_all[rows], egc_all[rows]
        outs = []
        for h in range(H):
            i = c * H + h
            t_h = t_ps[c][:, h * C:(h + 1) * C].astype(bf16)
            e_h = egc[:, h:h + 1]
            u = jnp.dot(t_h, (vs[i] * beta[:, h:h + 1]).astype(bf16), preferred_element_type=f32)
            w = jnp.dot(t_h, (kbs[i] * e_h).astype(bf16), preferred_element_type=f32)
            state = st_ref[h]
            sb = state.astype(bf16)
            v_new = u - jnp.dot(w.astype(bf16), sb, preferred_element_type=f32)
            o = (jnp.dot((qs[i] * e_h).astype(bf16), sb, preferred_element_type=f32)
                 + jnp.dot(qk_ps[c][:, h * C:(h + 1) * C].astype(bf16), v_new.astype(bf16),
                           preferred_element_type=f32))
            g_last = gc[C - 1:C, h:h + 1]
            k_dec = (ks[i] * jnp.exp(g_last - gc[:, h:h + 1])).astype(bf16)
            st_ref[h] = state * jnp.exp(g_last) + lax.dot_general(
                k_dec, v_new.astype(bf16), (((0,), (0,)), ((), ())), preferred_element_type=f32)
            o = o * lax.rsqrt(jnp.mean(o * o, axis=-1, keepdims=True) + EPS) * nw_ref[...]
            outs.append(o * jax.nn.silu(z_ref[rows, h * dh:(h + 1) * dh]))
        o_ref[rows, :] = jnp.concatenate(outs, axis=1)


def gdn_mixer(proj, conv_w, a_log, dt_bias, norm_w):
    Bsz, S, _ = proj.shape
    f32 = jnp.float32
    L = min(GDN_ROWS, S)
    H, W, W3 = GDN_HEADS, GDN_WIDTH, 3 * GDN_WIDTH
    full = lambda *shape: pl.BlockSpec(shape, lambda bi, ci: (0,) * len(shape))
    rows = lambda n, col=0: pl.BlockSpec((None, L, n), lambda bi, ci: (bi, ci, col // n))
    return pl.pallas_call(
        _gdn_body,
        grid=(Bsz, S // L),
        in_specs=[rows(W3, PROJ_QKV), rows(W, PROJ_Z), rows(LANES, PROJ_SMALL),
                  full(GDN_CONV, W3), full(1, H), full(1, H), full(1, GDN_HEAD_DIM)],
        out_specs=rows(W),
        out_shape=jax.ShapeDtypeStruct((Bsz, S, W), f32),
        scratch_shapes=[pltpu.VMEM((SUBLANES, W3), f32), pltpu.VMEM((H, GDN_HEAD_DIM, GDN_HEAD_DIM), f32)],
        compiler_params=pltpu.CompilerParams(dimension_semantics=("arbitrary", "arbitrary"),
                                             vmem_limit_bytes=48 * 1024 * 1024),
        name="gdn_mixer",
    )(proj, proj, proj, conv_w.astype(f32), a_log.astype(f32).reshape(1, H),
      dt_bias.astype(f32).reshape(1, H), norm_w.astype(f32).reshape(1, GDN_HEAD_DIM))


def nsa_mixer(q, kv, gate_logits, q_norm, k_norm, cmp_pe, cmp_w1, cmp_w2, cos, sin):
    Bsz, S, _ = q.shape
    f32 = jnp.float32
    G, R, dh = NSA_KV_HEADS, NSA_HEADS // NSA_KV_HEADS, NSA_HEAD_DIM
    scale = dh ** -0.5
    q = rope(rms_norm(q.reshape(Bsz, S, NSA_HEADS, dh), q_norm), cos, sin)
    kc_raw, vc_raw, ks, vs, kw, vw = (t.reshape(Bsz, S, G, dh) for t in jnp.split(kv, 6, axis=-1))
    ks = rope(rms_norm(ks, k_norm[1]), cos, sin)
    kw = rope(rms_norm(kw, k_norm[2]), cos, sin)

    assert CMP_LEN == 2 * CMP_STRIDE and S % CMP_STRIDE == 0
    n_cmp = (S - CMP_LEN) // CMP_STRIDE + 1
    win_idx = jnp.arange(n_cmp)[:, None] * CMP_STRIDE + jnp.arange(CMP_LEN)[None, :]
    cmp_end = win_idx[:, -1]

    def compress(t, pe, w1, w2):
        grp = t.reshape(Bsz, S // CMP_STRIDE, CMP_STRIDE, G, dh)
        blocks = jnp.concatenate([grp[:, :-1], grp[:, 1:]], axis=2) + pe[:, None, :]
        blocks = jnp.moveaxis(blocks, 3, 2).reshape(Bsz, n_cmp, G, CMP_LEN * dh)
        return jax.nn.gelu(blocks @ w1) @ w2

    cos_c, sin_c = rope_tables(cmp_end, dh)
    k_cmp = rope(rms_norm(compress(kc_raw, cmp_pe[0], cmp_w1[0], cmp_w2[0]), k_norm[0]), cos_c, sin_c)
    v_cmp = compress(vc_raw, cmp_pe[1], cmp_w1[1], cmp_w2[1])

    n_slc = S // SLC_BLOCK
    n_top = min(SLC_TOPK, n_slc)
    c0 = jnp.arange(n_cmp) * CMP_STRIDE
    s0 = jnp.arange(n_slc) * SLC_BLOCK
    overlap = jnp.clip(jnp.minimum(c0[:, None] + CMP_LEN, s0[None, :] + SLC_BLOCK)
                       - jnp.maximum(c0[:, None], s0[None, :]), 0, None).astype(f32) / CMP_LEN
    gates = jax.nn.sigmoid(gate_logits.astype(f32)).reshape(Bsz, S, G, R * 3).transpose(0, 2, 3, 1)

    bf16 = jnp.bfloat16
    n_cmp_pad = S // CMP_STRIDE
    pad_cmp = lambda t: jnp.pad(t.astype(bf16), ((0, 0), (0, n_cmp_pad - n_cmp), (0, 0), (0, 0)))
    keys = lambda t: t.astype(bf16).transpose(0, 2, 1, 3)
    vals = lambda t: t.astype(bf16).transpose(0, 2, 3, 1)
    q_heads = (q * scale).astype(bf16).reshape(Bsz, S, G, R, dh).transpose(0, 2, 3, 1, 4)
    overlap_t = jnp.pad(overlap.T, ((0, 0), (0, n_cmp_pad - n_cmp))).astype(bf16)
    expand_t = ((jnp.arange(S) // SLC_BLOCK)[:, None] == jnp.arange(n_slc)[None, :]).astype(bf16)
    return _nsa_attention(q_heads, keys(pad_cmp(k_cmp)), vals(pad_cmp(v_cmp)), keys(ks), vals(vs), keys(kw), vals(vw),
                          gates, overlap_t, expand_t, n_top)


NSA_R = NSA_HEADS // NSA_KV_HEADS
SLC_TILE = 1024
MASK_NEG = -1e30


def _dot_nt(a, b):
    return lax.dot_general(a, b, (((1,), (1,)), ((), ())), preferred_element_type=jnp.float32)


def _nsa_body(q_ref, kc_ref, vct_ref, ks_ref, vst_ref, kw_ref, vwt_ref, gate_ref, ovt_ref, expt_ref, o_ref,
              *, n_top):
    f32, bf16 = jnp.float32, jnp.bfloat16
    R, QB, dh = q_ref.shape
    S = ks_ref.shape[0]
    n_slc, n_cmp_pad = ovt_ref.shape
    M = R * QB
    t0 = pl.program_id(2) * QB
    q = q_ref[...].reshape(M, dh)
    per_head = lambda t: jnp.concatenate([t] * R, axis=1)

    band = S if S < WINDOW + QB else WINDOW + QB
    start = pl.multiple_of(jnp.maximum(t0 + QB - band, 0), QB)
    s = _dot_nt(kw_ref[pl.ds(start, band), :], q)
    diff = (t0 - start) + (lax.broadcasted_iota(jnp.int32, (band, QB), 1) - lax.broadcasted_iota(jnp.int32, (band, QB), 0))
    s = s + per_head(jnp.where((diff >= 0) & (diff < WINDOW), 0.0, -jnp.inf))
    p = jnp.exp(s - jnp.max(s, axis=0, keepdims=True))
    o_win = jnp.dot(vwt_ref[:, pl.ds(start, band)], p.astype(bf16), preferred_element_type=f32)
    o_win = o_win / jnp.sum(p, axis=0, keepdims=True)

    s = _dot_nt(kc_ref[...], q)
    n_idx = lax.broadcasted_iota(jnp.int32, (n_cmp_pad, QB), 0)
    t_idx = t0 + lax.broadcasted_iota(jnp.int32, (n_cmp_pad, QB), 1)
    valid = (n_idx * CMP_STRIDE + (CMP_LEN - 1) <= t_idx) & (n_idx < n_cmp_pad - 1)
    s = s + per_head(jnp.where(valid, 0.0, -jnp.inf))
    m = jnp.max(s, axis=0, keepdims=True)
    m = jnp.where(jnp.isfinite(m), m, 0.0)
    p = jnp.exp(s - m)
    p = p / jnp.maximum(jnp.sum(p, axis=0, keepdims=True), jnp.finfo(f32).tiny)
    o_cmp = jnp.dot(vct_ref[...], p.astype(bf16), preferred_element_type=f32)

    p_sum = p[:, 0:QB]
    for r in range(1, R):
        p_sum = p_sum + p[:, r * QB:(r + 1) * QB]
    imp = sum(jnp.dot(ovt_ref[...], t, preferred_element_type=f32) for t in _bf16_terms(p_sum, 3))
    j_idx = lax.broadcasted_iota(jnp.int32, (n_slc, QB), 0)
    t_lane = t0 + lax.broadcasted_iota(jnp.int32, (n_slc, QB), 1)
    cur = lax.shift_right_logical(t_lane, 6)
    forced = (j_idx == 0) | (j_idx == cur) | (j_idx == cur - 1)
    started = j_idx * SLC_BLOCK <= t_lane
    imp = jnp.where(forced, BIG, jnp.where(started, imp, -BIG))
    rank = jnp.zeros((n_slc, QB), f32)
    for jp in range(n_slc):
        row = imp[jp:jp + 1, :]
        ahead = (row > imp) | ((row == imp) & (j_idx > jp))
        rank = rank + jnp.where(ahead, 1.0, 0.0)
    sel = jnp.where(rank < n_top, 1.0, 0.0).astype(bf16)
    k_rel = lax.broadcasted_iota(jnp.int32, (SLC_TILE, QB), 0)
    t_q = t0 + lax.broadcasted_iota(jnp.int32, (SLC_TILE, QB), 1)

    def slc_step(kt, carry):
        m, l, acc = carry
        k0 = pl.multiple_of(kt * SLC_TILE, SLC_TILE)
        s = _dot_nt(ks_ref[pl.ds(k0, SLC_TILE), :], q)
        sel_keys = jnp.dot(expt_ref[pl.ds(k0, SLC_TILE), :], sel, preferred_element_type=f32)
        bias = jnp.where((sel_keys > 0.5) & (k_rel + k0 <= t_q), 0.0, MASK_NEG)
        s = s + per_head(bias)
        m_new = jnp.maximum(m, jnp.max(s, axis=0, keepdims=True))
        alpha = jnp.exp(m - m_new)
        p = jnp.exp(s - m_new)
        l = alpha * l + jnp.sum(p, axis=0, keepdims=True)
        acc = alpha * acc + jnp.dot(vst_ref[:, pl.ds(k0, SLC_TILE)], p.astype(bf16), preferred_element_type=f32)
        return m_new, l, acc

    n_tiles = (t0 + QB - 1) // SLC_TILE + 1
    init = (jnp.full((1, M), MASK_NEG, f32), jnp.zeros((1, M), f32), jnp.zeros((dh, M), f32))
    _, l, acc = lax.fori_loop(0, n_tiles, slc_step, init)
    o_slc = acc / l

    gate = gate_ref[...]
    g_cmp, g_slc, g_win = (jnp.concatenate([gate[3 * r + c:3 * r + c + 1, :] for r in range(R)], axis=1)
                           for c in range(3))
    o_t = g_cmp * o_cmp + g_slc * o_slc + g_win * o_win
    o_ref[...] = jnp.concatenate([o_t[:, r * QB:(r + 1) * QB].T for r in range(R)], axis=1)


def _nsa_attention(q, kc, vct, ks, vst, kw, vwt, gates, overlap_t, expand_t, n_top):
    Bsz, G, R, S, dh = q.shape
    n_cmp_pad = kc.shape[2]
    assert S % SLC_TILE == 0 and SLC_TILE % Q_BLOCK == 0, (S, SLC_TILE, Q_BLOCK)
    keys = lambda n: pl.BlockSpec((None, None, n, dh), lambda b, g, i: (b, g, 0, 0))
    vals = lambda n: pl.BlockSpec((None, None, dh, n), lambda b, g, i: (b, g, 0, 0))
    const = lambda a: pl.BlockSpec(a.shape, lambda b, g, i: (0, 0))
    return pl.pallas_call(
        functools.partial(_nsa_body, n_top=n_top),
        grid=(Bsz, G, S // Q_BLOCK),
        in_specs=[pl.BlockSpec((None, None, R, Q_BLOCK, dh), lambda b, g, i: (b, g, 0, i, 0)),
                  keys(n_cmp_pad), vals(n_cmp_pad), keys(S), vals(S), keys(S), vals(S),
                  pl.BlockSpec((None, None, 3 * R, Q_BLOCK), lambda b, g, i: (b, g, 0, i)),
                  const(overlap_t), const(expand_t)],
        out_specs=pl.BlockSpec((None, Q_BLOCK, R * dh), lambda b, g, i: (b, i, g)),
        out_shape=jax.ShapeDtypeStruct((Bsz, S, G * R * dh), jnp.float32),
        compiler_params=pltpu.CompilerParams(dimension_semantics=("arbitrary", "arbitrary", "arbitrary"),
                                             vmem_limit_bytes=48 * 1024 * 1024),
        name="nsa_attention",
    )(q, kc, vct, ks, vst, kw, vwt, gates, overlap_t, expand_t)


def hybrid_mixer(x, ln_mix, w_in, lam_re, lam_im, b_re, b_im, c_re, c_im, s5_d, s5_log_dt, glu_w, glu_b,
                 s5_out_norm, conv_w, a_log, dt_bias, gdn_norm_w, q_norm, k_norm, cmp_pe, cmp_w1,
                 cmp_w2, cos, sin):
    Bsz, S, D = x.shape
    proj = _in_proj(x.reshape(-1, D), ln_mix, _in_proj_weight(w_in)).reshape(Bsz, S, PROJ_W)
    y_s5 = s5_mixer_normed(proj, lam_re, lam_im, b_re, b_im, c_re, c_im, s5_d, s5_log_dt, glu_w, glu_b, s5_out_norm)
    y_gdn = gdn_mixer(proj, conv_w, a_log, dt_bias, gdn_norm_w)
    y_nsa = nsa_mixer(proj[..., PROJ_Q:PROJ_Q + NSA_WIDTH], proj[..., PROJ_KV:PROJ_KV + 6 * NSA_KV_WIDTH],
                      proj[..., PROJ_GATES:PROJ_GATES + 3 * NSA_HEADS], q_norm, k_norm, cmp_pe, cmp_w1, cmp_w2,
                      cos, sin)
    return y_s5, y_gdn, y_nsa


def hier_moe(x, h, logits, layer, w_gate, w_up, w_down):
    T, D = x.shape
    f32 = jnp.float32
    g_logits = logits[:, :N_GROUPS]
    grp = jnp.argmax(g_logits, axis=-1)
    p_grp = jnp.take_along_axis(jax.nn.softmax(g_logits, axis=-1), grp[:, None], axis=-1)
    e_logits = logits[:, N_GROUPS:N_ROUTER].reshape(T, N_GROUPS, EXPERTS_PER_GROUP)
    e_logits = jnp.take_along_axis(e_logits, grp[:, None, None], axis=1)[:, 0]
    local = jnp.arange(EXPERTS_PER_GROUP, dtype=jnp.int32)[None, :]
    rest, top_logit, top_local = e_logits, [], []
    for _ in range(TOP_K):
        idx = jnp.argmax(rest, axis=-1).astype(jnp.int32)
        top_logit.append(jnp.take_along_axis(rest, idx[:, None], axis=-1)[:, 0])
        top_local.append(idx)
        rest = jnp.where(local == idx[:, None], -jnp.inf, rest)
    top_logit, top_local = jnp.stack(top_logit, axis=-1), jnp.stack(top_local, axis=-1)
    weights = jax.nn.softmax(top_logit, axis=-1) * p_grp
    experts = grp[:, None].astype(jnp.int32) * EXPERTS_PER_GROUP + top_local
    n_assign = T * TOP_K
    e_flat = experts.reshape(-1)
    tok_flat = jnp.repeat(jnp.arange(T, dtype=jnp.int32), TOP_K)
    onehot = (e_flat[:, None] == jnp.arange(N_EXPERTS, dtype=jnp.int32)[None, :]).astype(f32)
    blocks = onehot.reshape(n_assign // COUNT_BLOCK, COUNT_BLOCK, N_EXPERTS)
    tri = jnp.tril(jnp.ones((COUNT_BLOCK, COUNT_BLOCK), f32))
    inside = jnp.einsum('ij,bje->bie', tri, blocks)
    before = jnp.cumsum(inside[:, -1, :], axis=0) - inside[:, -1, :]
    running = (inside + before[:, None, :]).reshape(n_assign, N_EXPERTS).astype(jnp.int32)
    counts = running[-1]
    padded = (counts + MOE_ROWS - 1) // MOE_ROWS * MOE_ROWS
    pad_start = jnp.cumsum(padded) - padded
    dest = (pad_start[e_flat] + jnp.take_along_axis(running, e_flat[:, None], axis=1)[:, 0] - 1).astype(jnp.int32)
    n_blk = (n_assign + MOE_ROWS - 1) // MOE_ROWS + N_EXPERTS
    cap = n_blk * MOE_ROWS
    slot_tok = jnp.zeros((cap,), jnp.int32).at[dest].set(tok_flat)
    blk_expert = jnp.minimum(jnp.searchsorted(jnp.cumsum(padded), jnp.arange(n_blk) * MOE_ROWS, side='right'),
                             N_EXPERTS - 1).astype(jnp.int32)
    y = _moe_ffn(blk_expert, slot_tok, x, layer, w_gate.astype(f32), w_up.astype(f32),
                 w_down.astype(f32))
    return _moe_combine(dest, h, weights, y)


CMB_ROWS = 128


def _moe_combine_body(d_ref, h_ref, w_ref, y_hbm, o_ref, buf, sem):
    i = pl.program_id(0)
    n = pl.num_programs(0)
    rows = h_ref.shape[0]
    slot = i & 1

    def row_copy(step, r, k, to):
        src = y_hbm.at[pl.ds(d_ref[(step * rows + r) * TOP_K + k], 1), :]
        return pltpu.make_async_copy(src, buf.at[to, k, pl.ds(r, 1), :], sem.at[to])

    def start_step(step, to):
        for r in range(rows):
            for k in range(TOP_K):
                row_copy(step, r, k, to).start(priority=(r * TOP_K + k) % 2)

    def finish_step():
        for r in range(rows):
            for k in range(TOP_K):
                row_copy(i, r, k, slot).wait()
        out = h_ref[...]
        for k in range(TOP_K):
            out = out + w_ref[:, k:k + 1] * buf[slot, k]
        o_ref[...] = out

    @pl.when(i == 0)
    def _():
        start_step(0, 0)

    @pl.when(i + 1 < n)
    def _():
        start_step(i + 1, 1 - slot)
        finish_step()

    @pl.when(i + 1 == n)
    def _():
        finish_step()


def _moe_combine(dest, h, weights, y):
    T, D = h.shape
    rows = min(CMB_ROWS, T)
    return pl.pallas_call(
        _moe_combine_body,
        grid_spec=pltpu.PrefetchScalarGridSpec(
            num_scalar_prefetch=1, grid=(T // rows,),
            in_specs=[pl.BlockSpec((rows, D), lambda i, d: (i, 0)),
                      pl.BlockSpec((rows, TOP_K), lambda i, d: (i, 0)),
                      pl.BlockSpec(memory_space=pl.ANY)],
            out_specs=pl.BlockSpec((rows, D), lambda i, d: (i, 0)),
            scratch_shapes=[pltpu.VMEM((2, TOP_K, rows, D), jnp.float32), pltpu.SemaphoreType.DMA((2,))]),
        out_shape=jax.ShapeDtypeStruct((T, D), jnp.float32),
        compiler_params=pltpu.CompilerParams(dimension_semantics=("arbitrary",),
                                             vmem_limit_bytes=32 * 1024 * 1024),
        name="moe_combine",
    )(dest, h, weights, y)


MOE_ROWS = 256
COUNT_BLOCK = 128


def _moe_ffn_body(be_ref, tok_ref, x_hbm, wg_ref, wu_ref, wd_ref, o_ref, xbuf, sem, wg_bf, wu_bf, wd_bf):
    i = pl.program_id(0)
    n = pl.num_programs(0)
    slot = i & 1

    def row_copy(step, r, to):
        src = x_hbm.at[pl.ds(tok_ref[step * MOE_ROWS + r], 1), :]
        return pltpu.make_async_copy(src, xbuf.at[to, pl.ds(r, 1), :], sem.at[to])

    def start_step(step, to):
        for r in range(MOE_ROWS):
            row_copy(step, r, to).start(priority=r % 2)

    def finish_step():
        for r in range(MOE_ROWS):
            row_copy(i, r, slot).wait()
        x = xbuf[slot].astype(jnp.bfloat16)
        gate = jnp.dot(x, wg_bf[...], preferred_element_type=jnp.float32)
        up = jnp.dot(x, wu_bf[...], preferred_element_type=jnp.float32)
        hid = (jax.nn.silu(gate) * up).astype(jnp.bfloat16)
        o_ref[...] = jnp.dot(hid, wd_bf[...], preferred_element_type=jnp.float32)

    @pl.when(i == 0)
    def _():
        start_step(0, 0)

    @pl.when((i == 0) | (be_ref[i] != be_ref[jnp.maximum(i - 1, 0)]))
    def _():
        wg_bf[...] = wg_ref[...].astype(jnp.bfloat16)
        wu_bf[...] = wu_ref[...].astype(jnp.bfloat16)
        wd_bf[...] = wd_ref[...].astype(jnp.bfloat16)

    @pl.when(i + 1 < n)
    def _():
        start_step(i + 1, 1 - slot)
        finish_step()

    @pl.when(i + 1 == n)
    def _():
        finish_step()


def _moe_ffn(blk_expert, slot_tok, x, layer, w_gate, w_up, w_down):
    cap = slot_tok.shape[0]
    D = x.shape[1]
    n_blk = cap // MOE_ROWS
    De = w_gate.shape[-1]
    return pl.pallas_call(
        _moe_ffn_body,
        grid_spec=pltpu.PrefetchScalarGridSpec(
            num_scalar_prefetch=2, grid=(n_blk,),
            in_specs=[pl.BlockSpec(memory_space=pl.ANY),
                      pl.BlockSpec((None, None, D, De), lambda i, be, tok: (layer, be[i], 0, 0)),
                      pl.BlockSpec((None, None, D, De), lambda i, be, tok: (layer, be[i], 0, 0)),
                      pl.BlockSpec((None, None, De, D), lambda i, be, tok: (layer, be[i], 0, 0))],
            out_specs=pl.BlockSpec((MOE_ROWS, D), lambda i, be, tok: (i, 0)),
            scratch_shapes=[pltpu.VMEM((2, MOE_ROWS, D), jnp.float32), pltpu.SemaphoreType.DMA((2,)),
                            pltpu.VMEM((D, De), jnp.bfloat16), pltpu.VMEM((D, De), jnp.bfloat16),
                            pltpu.VMEM((De, D), jnp.bfloat16)]),
        out_shape=jax.ShapeDtypeStruct((cap, D), jnp.float32),
        compiler_params=pltpu.CompilerParams(dimension_semantics=("arbitrary",),
                                             vmem_limit_bytes=56 * 1024 * 1024),
        name="moe_ffn",
    )(blk_expert, slot_tok, x, w_gate, w_up, w_down)


def kernel(x, ln_mix, w_in, s5_lambda_re, s5_lambda_im, s5_b_re, s5_b_im, s5_c_re, s5_c_im, s5_d,
           s5_log_dt, s5_glu_w, s5_glu_b, s5_out_norm, gdn_conv_w, gdn_a_log, gdn_dt_bias, gdn_norm_w,
           nsa_q_norm, nsa_k_norm, nsa_cmp_pe, nsa_cmp_w1, nsa_cmp_w2, nsa_out_norm, w_out, ln_ffn,
           moe_group_w, moe_group_b, moe_expert_w, moe_expert_b, moe_w_gate, moe_w_up, moe_w_down):
    Bsz, S, D = x.shape
    cos, sin = rope_tables(jnp.arange(S), NSA_HEAD_DIM)
    for l in range(DEPTH):
        y_s5, y_gdn, y_nsa = hybrid_mixer(
            x, ln_mix[l], w_in[l], s5_lambda_re[l], s5_lambda_im[l], s5_b_re[l], s5_b_im[l], s5_c_re[l], s5_c_im[l],
            s5_d[l], s5_log_dt[l], s5_glu_w[l], s5_glu_b[l], s5_out_norm[l], gdn_conv_w[l], gdn_a_log[l],
            gdn_dt_bias[l], gdn_norm_w[l], nsa_q_norm[l], nsa_k_norm[l], nsa_cmp_pe[l], nsa_cmp_w1[l], nsa_cmp_w2[l],
            cos, sin)
        router_w = jnp.concatenate([moe_group_w[l], moe_expert_w[l]], axis=-1)
        router_b = jnp.concatenate([moe_group_b[l], moe_expert_b[l]], axis=-1)
        h, a, logits = _out_proj(x.reshape(-1, D), y_s5.reshape(-1, S5_WIDTH), y_gdn.reshape(-1, GDN_WIDTH),
                                 y_nsa.reshape(-1, NSA_WIDTH), nsa_out_norm[l], w_out[l], ln_ffn[l], router_w, router_b)
        x = hier_moe(a, h, logits, l, moe_w_gate, moe_w_up, moe_w_down).reshape(Bsz, S, D)
    return x
```

```python
import functools
import math
import jax, jax.numpy as jnp
from jax import lax
import numpy as np
from jax.experimental import pallas as pl
from jax.experimental.pallas import tpu as pltpu

D_MODEL = 2048
BATCH = 4
SEQ = 4096
DEPTH = 2

EPS = 1e-6
S5_WIDTH = D_MODEL // 4
S5_GROUP = 16
S5_GROUPS = S5_WIDTH // S5_GROUP
S5_STATE = 64
GDN_HEAD_DIM = 128
GDN_WIDTH = D_MODEL // 4
GDN_HEADS = GDN_WIDTH // GDN_HEAD_DIM
GDN_CONV = 4
GDN_CHUNK = 64
NSA_HEAD_DIM = 64
NSA_WIDTH = D_MODEL // 2
NSA_HEADS = NSA_WIDTH // NSA_HEAD_DIM
NSA_KV_HEADS = 4
NSA_KV_WIDTH = NSA_KV_HEADS * NSA_HEAD_DIM
CMP_LEN = 32
CMP_STRIDE = 16
CMP_HIDDEN = 256
SLC_BLOCK = 64
SLC_TOPK = 16
WINDOW = 512
Q_BLOCK = 128
ROPE_THETA = 10000.0
BIG = 1e9
N_GROUPS = 4
EXPERTS_PER_GROUP = 8
N_EXPERTS = N_GROUPS * EXPERTS_PER_GROUP
TOP_K = 2
D_EXPERT = 512
MOE_BLOCK = 128
IN_SIZES = (S5_WIDTH, 3 * GDN_WIDTH, GDN_WIDTH, GDN_HEADS, GDN_HEADS, NSA_WIDTH, 6 * NSA_KV_WIDTH, 3 * NSA_HEADS)
D_IN = sum(IN_SIZES)


PROJ_QKV, PROJ_KV, PROJ_Q, PROJ_U, PROJ_Z, PROJ_SMALL = 0, 1536, 3072, 4096, 4608, 5120
PROJ_SMALL_W = 512
PROJ_W = PROJ_SMALL + PROJ_SMALL_W
PROJ_GATES = PROJ_SMALL + 2 * GDN_HEADS
LANES = 128
IN_ROWS, IN_COLS = 1024, 512
OUT_ROWS = 256
N_ROUTER = N_GROUPS + N_EXPERTS


def _in_proj_weight(w_in):
    u, qkv, z, ga, gb, q, kv, gl = jnp.split(w_in, np.cumsum(IN_SIZES)[:-1].tolist(), axis=-1)
    small = jnp.concatenate([ga, gb, gl], axis=-1)
    small = jnp.pad(small, ((0, 0), (0, PROJ_SMALL_W - small.shape[-1])))
    return jnp.concatenate([qkv, kv, q, u, z, small], axis=-1)


def _in_proj_body(x_ref, g_ref, w_ref, o_ref, a_ref):
    @pl.when(pl.program_id(1) == 0)
    def _():
        x = x_ref[...]
        a_ref[...] = (x * lax.rsqrt(jnp.mean(x * x, axis=-1, keepdims=True) + EPS) * g_ref[...]).astype(jnp.bfloat16)

    o_ref[...] = jnp.dot(a_ref[...], w_ref[...].astype(jnp.bfloat16), preferred_element_type=jnp.float32)


def _in_proj(x, gain, w):
    T, D = x.shape
    N = w.shape[1]
    tm = min(IN_ROWS, T)
    return pl.pallas_call(
        _in_proj_body,
        grid=(T // tm, N // IN_COLS),
        in_specs=[pl.BlockSpec((tm, D), lambda i, j: (i, 0)),
                  pl.BlockSpec((1, D), lambda i, j: (0, 0)),
                  pl.BlockSpec((D, IN_COLS), lambda i, j: (0, j))],
        out_specs=pl.BlockSpec((tm, IN_COLS), lambda i, j: (i, j)),
        out_shape=jax.ShapeDtypeStruct((T, N), jnp.float32),
        scratch_shapes=[pltpu.VMEM((tm, D), jnp.bfloat16)],
        compiler_params=pltpu.CompilerParams(dimension_semantics=("arbitrary", "arbitrary"),
                                             vmem_limit_bytes=48 * 1024 * 1024),
        name="in_proj",
    )(x, gain.astype(jnp.float32).reshape(1, D), w)


def _out_proj_body(x_ref, s5_ref, gdn_ref, nsa_ref, ng_ref, w_ref, fg_ref, rw_ref, rb_ref, h_ref, a_ref, lg_ref):
    f32, bf16 = jnp.float32, jnp.bfloat16
    nsa = nsa_ref[...]
    nsa = nsa * lax.rsqrt(jnp.mean(nsa * nsa, axis=-1, keepdims=True) + EPS) * ng_ref[...]
    k0, k1 = S5_WIDTH, S5_WIDTH + GDN_WIDTH
    y = (jnp.dot(s5_ref[...].astype(bf16), w_ref[0:k0, :], preferred_element_type=f32)
         + jnp.dot(gdn_ref[...].astype(bf16), w_ref[k0:k1, :], preferred_element_type=f32)
         + jnp.dot(nsa.astype(bf16), w_ref[k1:, :], preferred_element_type=f32))
    h = x_ref[...] + y
    h_ref[...] = h
    a = h * lax.rsqrt(jnp.mean(h * h, axis=-1, keepdims=True) + EPS) * fg_ref[...]
    a_ref[...] = a
    lg_ref[...] = jnp.dot(a.astype(bf16), rw_ref[...], preferred_element_type=f32) + rb_ref[...]


def _out_proj(x, y_s5, y_gdn, y_nsa, nsa_gain, w_out, ffn_gain, router_w, router_b):
    T, D = x.shape
    tm = min(OUT_ROWS, T)
    f32, bf16 = jnp.float32, jnp.bfloat16
    rw = jnp.pad(router_w.astype(bf16), ((0, 0), (0, LANES - N_ROUTER)))
    rb = jnp.pad(router_b.astype(f32), (0, LANES - N_ROUTER)).reshape(1, LANES)
    rows = lambda n: pl.BlockSpec((tm, n), lambda i: (i, 0))
    full = lambda *shape: pl.BlockSpec(shape, lambda i: (0,) * len(shape))
    return pl.pallas_call(
        _out_proj_body,
        grid=(T // tm,),
        in_specs=[rows(D), rows(S5_WIDTH), rows(GDN_WIDTH), rows(NSA_WIDTH), full(1, NSA_WIDTH), full(D, D),
                  full(1, D), full(D, LANES), full(1, LANES)],
        out_specs=[rows(D), rows(D), rows(LANES)],
        out_shape=[jax.ShapeDtypeStruct((T, D), f32), jax.ShapeDtypeStruct((T, D), f32),
                   jax.ShapeDtypeStruct((T, LANES), f32)],
        compiler_params=pltpu.CompilerParams(dimension_semantics=("arbitrary",),
                                             vmem_limit_bytes=48 * 1024 * 1024),
        name="out_proj",
    )(x, y_s5, y_gdn, y_nsa, nsa_gain.astype(f32).reshape(1, NSA_WIDTH), w_out.astype(bf16),
      ffn_gain.astype(f32).reshape(1, D), rw, rb)


def rms_norm(x, gain):
    xf = x.astype(jnp.float32)
    y = xf * lax.rsqrt(jnp.mean(xf * xf, axis=-1, keepdims=True) + EPS)
    return (y * gain.astype(jnp.float32)).astype(x.dtype)


def l2_norm(x):
    return x * lax.rsqrt(jnp.sum(x * x, axis=-1, keepdims=True) + EPS)


def rope_tables(pos, dim):
    inv_freq = ROPE_THETA ** (-jnp.arange(0, dim, 2, dtype=jnp.float32) / dim)
    ang = pos.astype(jnp.float32)[:, None] * inv_freq[None, :]
    return jnp.cos(ang)[:, None, :], jnp.sin(ang)[:, None, :]


def rope(x, cos, sin):
    x1, x2 = jnp.split(x, 2, axis=-1)
    return jnp.concatenate([x1 * cos - x2 * sin, x2 * cos + x1 * sin], axis=-1).astype(x.dtype)


def masked_softmax(s, mask):
    s = jnp.where(mask, s, -jnp.inf)
    m = jnp.max(s, axis=-1, keepdims=True)
    m = jnp.where(jnp.isfinite(m), m, 0.0)
    p = jnp.exp(s - m)
    return p / jnp.maximum(jnp.sum(p, axis=-1, keepdims=True), jnp.finfo(jnp.float32).tiny)


S5_OCT = 8
S5_NOCT = S5_GROUPS // S5_OCT
S5_OCT_IN = S5_OCT * S5_GROUP
S5_OCT_ST = S5_OCT * S5_STATE
SUBLANES = 8
S5_SCAN_SHIFTS = (1, 2, 4)


def _s5_constants(lam_re, lam_im, b_re, b_im, c_re, c_im, log_dt):
    f32 = jnp.float32
    lr, li = lam_re.astype(f32), lam_im.astype(f32)
    dt = jnp.exp(log_dt.astype(f32))[:, None]

    def lam_pow(k):
        mag = jnp.exp(k * lr * dt)
        return mag * jnp.cos(k * li * dt), mag * jnp.sin(k * li * dt)

    ar, ai = lam_pow(1.0)
    nr, ni = ar - 1.0, ai
    den = lr * lr + li * li
    qr, qi = (nr * lr + ni * li) / den, (ni * lr - nr * li) / den
    br, bi = b_re.astype(f32), b_im.astype(f32)
    bbr = qr[:, :, None] * br - qi[:, :, None] * bi
    bbi = qr[:, :, None] * bi + qi[:, :, None] * br
    eye = jnp.eye(S5_OCT, dtype=f32)

    def blk_b(t):
        t = t.reshape(S5_NOCT, S5_OCT, S5_STATE, S5_GROUP)
        return jnp.einsum('qgph,gk->qghkp', t, eye).reshape(S5_NOCT, S5_OCT_IN, S5_OCT_ST)

    def blk_c(t):
        t = t.reshape(S5_NOCT, S5_OCT, S5_GROUP, S5_STATE)
        return jnp.einsum('qghp,gk->qgpkh', t, eye).reshape(S5_NOCT, S5_OCT_ST, S5_OCT_IN)

    b_mat = jnp.stack([blk_b(bbr), blk_b(bbi)]).astype(jnp.bfloat16)
    c_mat = jnp.stack([blk_c(c_re.astype(f32)), blk_c(-c_im.astype(f32))]).astype(jnp.bfloat16)
    rows = jnp.arange(SUBLANES, dtype=f32)[:, None, None]
    tiles = []
    for k in S5_SCAN_SHIFTS:
        pr, pi = lam_pow(jnp.full_like(rows, float(k)))
        keep = rows >= k
        tiles.append(jnp.stack([jnp.where(keep, pr, 0.0), jnp.where(keep, pi, 0.0)]))
    pr, pi = lam_pow(rows + 1.0)
    tiles.append(jnp.stack([pr, pi]))
    a_mat = jnp.stack(tiles).reshape(len(tiles), 2, SUBLANES, S5_GROUPS * S5_STATE)
    return b_mat, c_mat, a_mat


def _s5_body(u_ref, b_ref, c_ref, a_ref, d_ref, gw_ref, gb_ref, gain_ref, o_ref, st_ref, xr_ref, xi_ref):
    bf16 = jnp.bfloat16
    n_tiles = u_ref.shape[0] // SUBLANES

    @pl.when(pl.program_id(1) == 0)
    def _():
        st_ref[...] = jnp.zeros_like(st_ref)

    u = u_ref[...]
    ys = []
    for q in range(S5_NOCT):
        lanes = slice(q * S5_OCT_ST, (q + 1) * S5_OCT_ST)
        uq = u[:, q * S5_OCT_IN:(q + 1) * S5_OCT_IN].astype(bf16)
        xr_ref[...] = jnp.dot(uq, b_ref[0, q], preferred_element_type=jnp.float32)
        xi_ref[...] = jnp.dot(uq, b_ref[1, q], preferred_element_type=jnp.float32)

        def tile_step(i, carry, lanes=lanes):
            cr, ci = carry
            r0 = pl.multiple_of(i * SUBLANES, SUBLANES)
            xr = xr_ref[pl.ds(r0, SUBLANES), :]
            xi = xi_ref[pl.ds(r0, SUBLANES), :]
            for lvl, k in enumerate(S5_SCAN_SHIFTS):
                pr, pi = a_ref[lvl, 0, :, lanes], a_ref[lvl, 1, :, lanes]
                sr, si = pltpu.roll(xr, k, 0), pltpu.roll(xi, k, 0)
                xr, xi = xr + (pr * sr - pi * si), xi + (pr * si + pi * sr)
            pr, pi = a_ref[len(S5_SCAN_SHIFTS), 0, :, lanes], a_ref[len(S5_SCAN_SHIFTS), 1, :, lanes]
            cr = jnp.broadcast_to(cr, xr.shape)
            ci = jnp.broadcast_to(ci, xi.shape)
            xr, xi = xr + (pr * cr - pi * ci), xi + (pr * ci + pi * cr)
            xr_ref[pl.ds(r0, SUBLANES), :] = xr
            xi_ref[pl.ds(r0, SUBLANES), :] = xi
            return xr[SUBLANES - 1:, :], xi[SUBLANES - 1:, :]

        cr, ci = lax.fori_loop(0, n_tiles, tile_step, (st_ref[0, :, lanes], st_ref[1, :, lanes]))
        st_ref[0, :, lanes] = cr
        st_ref[1, :, lanes] = ci
        ys.append(jnp.dot(xr_ref[...].astype(bf16), c_ref[0, q], preferred_element_type=jnp.float32)
                  + jnp.dot(xi_ref[...].astype(bf16), c_ref[1, q], preferred_element_type=jnp.float32))
    y = jnp.concatenate(ys, axis=1) + d_ref[...] * u
    y = jax.nn.gelu(y)
    z = jnp.dot(y.astype(bf16), gw_ref[...], preferred_element_type=jnp.float32) + gb_ref[...]
    y = y * jax.nn.sigmoid(z)
    y = y * lax.rsqrt(jnp.mean(y * y, axis=-1, keepdims=True) + EPS)
    o_ref[...] = y * gain_ref[...]


def s5_mixer_normed(proj, lam_re, lam_im, b_re, b_im, c_re, c_im, d_skip, log_dt, glu_w, glu_b, out_norm, chunk=512):
    Bsz, S, _ = proj.shape
    W = S5_WIDTH
    f32 = jnp.float32
    L = min(chunk, S)
    b_mat, c_mat, a_mat = _s5_constants(lam_re, lam_im, b_re, b_im, c_re, c_im, log_dt)
    n_state = S5_GROUPS * S5_STATE
    full = lambda *shape: pl.BlockSpec(shape, lambda b, c: (0,) * len(shape))
    return pl.pallas_call(
        _s5_body,
        grid=(Bsz, S // L),
        in_specs=[pl.BlockSpec((None, L, W), lambda b, c: (b, c, PROJ_U // W)),
                  full(*b_mat.shape), full(*c_mat.shape), full(*a_mat.shape),
                  full(1, W), full(W, W), full(1, W), full(1, W)],
        out_specs=pl.BlockSpec((None, L, W), lambda b, c: (b, c, 0)),
        out_shape=jax.ShapeDtypeStruct((Bsz, S, W), f32),
        scratch_shapes=[pltpu.VMEM((2, 1, n_state), f32),
                        pltpu.VMEM((L, S5_OCT_ST), f32), pltpu.VMEM((L, S5_OCT_ST), f32)],
        compiler_params=pltpu.CompilerParams(dimension_semantics=("arbitrary", "arbitrary"),
                                             vmem_limit_bytes=48 * 1024 * 1024),
        name="s5_mixer",
    )(proj, b_mat, c_mat, a_mat, d_skip.astype(f32).reshape(1, W), glu_w.astype(jnp.bfloat16),
      glu_b.astype(f32).reshape(1, W), out_norm.astype(f32).reshape(1, W))


GDN_ROWS = 256
GDN_PACK = GDN_HEADS * GDN_CHUNK


def _bf16_terms(x, n):
    terms, rem = [], x
    for i in range(n):
        t = rem.astype(jnp.bfloat16)
        terms.append(t)
        if i + 1 < n:
            rem = rem - t.astype(jnp.float32)
    return terms


def _dot_exact_lhs(lhs01, x):
    lhs = lhs01.astype(jnp.bfloat16)
    return sum(jnp.dot(lhs, t, preferred_element_type=jnp.float32) for t in _bf16_terms(x, 3))


def _gdn_packed_unit_lower_inverse(a_mats, row, col, blk_diag):
    f32 = jnp.float32

    def mm(x, y):
        xh, xl = _bf16_terms(x, 2)
        yh, yl = (jnp.where(blk_diag, jnp.concatenate([t] * GDN_HEADS, axis=0), 0.0) for t in _bf16_terms(y, 2))
        return (jnp.dot(xh, yh, preferred_element_type=f32)
                + (jnp.dot(xh, yl, preferred_element_type=f32) + jnp.dot(xl, yh, preferred_element_type=f32)))

    same16 = (row >> 4) == (col >> 4)
    same32 = (row >> 5) == (col >> 5)
    eye = jnp.where(row == col, 1.0, 0.0)
    d1 = [jnp.where(same16, a, 0.0) for a in a_mats]
    d2 = [mm(d, d) for d in d1]
    d4 = [mm(d, d) for d in d2]
    d8 = [mm(d, d) for d in d4]
    lo = [mm(eye - a, eye + b) for a, b in zip(d1, d2)]
    hi = [mm(eye + a, eye + b) for a, b in zip(d4, d8)]
    t = [mm(a, b) for a, b in zip(lo, hi)]
    for pick in (lambda a: jnp.where(same32 & jnp.logical_not(same16), a, 0.0), lambda a: jnp.where(same32, 0.0, a)):
        mid = [mm(ti, pick(a)) for ti, a in zip(t, a_mats)]
        t = [ti - mm(m, ti) for ti, m in zip(t, mid)]
    return t


def _gdn_body(qkv_ref, z_ref, ab_ref, cw_ref, alog_ref, dtb_ref, nw_ref, o_ref, halo_ref, st_ref):
    f32, bf16 = jnp.float32, jnp.bfloat16
    L = qkv_ref.shape[0]
    C, H, dh, W = GDN_CHUNK, GDN_HEADS, GDN_HEAD_DIM, GDN_WIDTH

    @pl.when(pl.program_id(1) == 0)
    def _():
        halo_ref[...] = jnp.zeros_like(halo_ref)
        st_ref[...] = jnp.zeros_like(st_ref)

    x = qkv_ref[...]
    prev = halo_ref[...]
    row8 = lax.broadcasted_iota(jnp.int32, prev.shape, 0)
    acc = cw_ref[GDN_CONV - 1:GDN_CONV, :] * x
    for s in range(1, GDN_CONV):
        xs = pltpu.roll(x, s, 0)
        head = jnp.where(row8 < s, pltpu.roll(prev, s, 0), xs[0:SUBLANES])
        xs = jnp.concatenate([head, xs[SUBLANES:]], axis=0)
        acc = acc + cw_ref[GDN_CONV - 1 - s:GDN_CONV - s, :] * xs
    halo_ref[...] = x[L - SUBLANES:L]
    y = jax.nn.silu(acc)

    a_in, b_in = ab_ref[:, 0:H], ab_ref[:, H:2 * H]
    beta_all = jax.nn.sigmoid(b_in)
    g_all = -jnp.exp(alog_ref[...]) * jax.nn.softplus(a_in + dtb_ref[...])

    row = lax.broadcasted_iota(jnp.int32, (C, GDN_PACK), 0)
    col = lax.broadcasted_iota(jnp.int32, (C, GDN_PACK), 1) & (C - 1)
    strict = row > col
    bd_r = lax.broadcasted_iota(jnp.int32, (GDN_PACK, GDN_PACK), 0)
    bd_c = lax.broadcasted_iota(jnp.int32, (GDN_PACK, GDN_PACK), 1)
    blk_diag = (bd_r >> 6) == (bd_c >> 6)

    lr = lax.broadcasted_iota(jnp.int32, (L, L), 0)
    lc = lax.broadcasted_iota(jnp.int32, (L, L), 1)
    same_chunk = (lr >> 6) == (lc >> 6)
    tri = jnp.where(same_chunk & (lr >= lc), 1.0, 0.0)
    head_of_lane = lax.broadcasted_iota(jnp.int32, (H, GDN_PACK), 1) >> 6
    spread = jnp.where(head_of_lane == lax.broadcasted_iota(jnp.int32, (H, GDN_PACK), 0), 1.0, 0.0)
    pr = lax.broadcasted_iota(jnp.int32, (L, GDN_PACK), 0) & (C - 1)
    pc = lax.broadcasted_iota(jnp.int32, (L, GDN_PACK), 1) & (C - 1)
    gc_all = _dot_exact_lhs(tri, g_all)
    gi_all = sum(jnp.dot(t, spread.astype(bf16), preferred_element_type=f32) for t in _bf16_terms(gc_all, 3))
    gj_all = _dot_exact_lhs(jnp.where(same_chunk, 1.0, 0.0), jnp.where(pr == pc, gi_all, 0.0))
    decay_all = jnp.exp(jnp.where(pr >= pc, gi_all - gj_all, -jnp.inf))
    egc_all = jnp.exp(gc_all)

    def l2n(t):
        return t * lax.rsqrt(jnp.sum(t * t, axis=-1, keepdims=True) + EPS)

    n_chunks = L // C
    chunk_rows = [slice(c * C, (c + 1) * C) for c in range(n_chunks)]
    qs, ks, vs, kbs, a_mats, qk_ps = [], [], [], [], [], []
    for rows in chunk_rows:
        kk, qk = [], []
        for h in range(H):
            q = l2n(y[rows, h * dh:(h + 1) * dh]) * dh ** -0.5
            k = l2n(y[rows, W + h * dh:W + (h + 1) * dh])
            kb = k * beta_all[rows, h:h + 1]
            qs.append(q), ks.append(k), vs.append(y[rows, 2 * W + h * dh:2 * W + (h + 1) * dh]), kbs.append(kb)
            kk.append(_dot_nt(kb.astype(bf16), k.astype(bf16)))
            qk.append(_dot_nt(q.astype(bf16), k.astype(bf16)))
        a_mats.append(jnp.where(strict, jnp.concatenate(kk, axis=1) * decay_all[rows], 0.0))
        qk_ps.append(jnp.concatenate(qk, axis=1) * decay_all[rows])
    t_ps = _gdn_packed_unit_lower_inverse(a_mats, row, col, blk_diag)
    for c, rows in enumerate(chunk_rows):
        beta, gc, egc = beta_all[rows], gc_all[rows], egc_all[rows]
        outs = []
        for h in range(H):
            i = c * H + h
            t_h = t_ps[c][:, h * C:(h + 1) * C].astype(bf16)
            e_h = egc[:, h:h + 1]
            u = jnp.dot(t_h, (vs[i] * beta[:, h:h + 1]).astype(bf16), preferred_element_type=f32)
            w = jnp.dot(t_h, (kbs[i] * e_h).astype(bf16), preferred_element_type=f32)
            state = st_ref[h]
            sb = state.astype(bf16)
            v_new = u - jnp.dot(w.astype(bf16), sb, preferred_element_type=f32)
            o = (jnp.dot((qs[i] * e_h).astype(bf16), sb, preferred_element_type=f32)
                 + jnp.dot(qk_ps[c][:, h * C:(h + 1) * C].astype(bf16), v_new.astype(bf16),
                           preferred_element_type=f32))
            g_last = gc[C - 1:C, h:h + 1]
            k_dec = (ks[i] * jnp.exp(g_last - gc[:, h:h + 1])).astype(bf16)
            st_ref[h] = state * jnp.exp(g_last) + lax.dot_general(
                k_dec, v_new.astype(bf16), (((0,), (0,)), ((), ())), preferred_element_type=f32)
            o = o * lax.rsqrt(jnp.mean(o * o, axis=-1, keepdims=True) + EPS) * nw_ref[...]
            outs.append(o * jax.nn.silu(z_ref[rows, h * dh:(h + 1) * dh]))
        o_ref[rows, :] = jnp.concatenate(outs, axis=1)


def gdn_mixer(proj, conv_w, a_log, dt_bias, norm_w):
    Bsz, S, _ = proj.shape
    f32 = jnp.float32
    L = min(GDN_ROWS, S)
    H, W, W3 = GDN_HEADS, GDN_WIDTH, 3 * GDN_WIDTH
    full = lambda *shape: pl.BlockSpec(shape, lambda bi, ci: (0,) * len(shape))
    rows = lambda n, col=0: pl.BlockSpec((None, L, n), lambda bi, ci: (bi, ci, col // n))
    return pl.pallas_call(
        _gdn_body,
        grid=(Bsz, S // L),
        in_specs=[rows(W3, PROJ_QKV), rows(W, PROJ_Z), rows(LANES, PROJ_SMALL),
                  full(GDN_CONV, W3), full(1, H), full(1, H), full(1, GDN_HEAD_DIM)],
        out_specs=rows(W),
        out_shape=jax.ShapeDtypeStruct((Bsz, S, W), f32),
        scratch_shapes=[pltpu.VMEM((SUBLANES, W3), f32), pltpu.VMEM((H, GDN_HEAD_DIM, GDN_HEAD_DIM), f32)],
        compiler_params=pltpu.CompilerParams(dimension_semantics=("arbitrary", "arbitrary"),
                                             vmem_limit_bytes=48 * 1024 * 1024),
        name="gdn_mixer",
    )(proj, proj, proj, conv_w.astype(f32), a_log.astype(f32).reshape(1, H),
      dt_bias.astype(f32).reshape(1, H), norm_w.astype(f32).reshape(1, GDN_HEAD_DIM))


def nsa_mixer(q, kv, gate_logits, q_norm, k_norm, cmp_pe, cmp_w1, cmp_w2, cos, sin):
    Bsz, S, _ = q.shape
    f32 = jnp.float32
    G, R, dh = NSA_KV_HEADS, NSA_HEADS // NSA_KV_HEADS, NSA_HEAD_DIM
    scale = dh ** -0.5
    q = rope(rms_norm(q.reshape(Bsz, S, NSA_HEADS, dh), q_norm), cos, sin)
    kc_raw, vc_raw, ks, vs, kw, vw = (t.reshape(Bsz, S, G, dh) for t in jnp.split(kv, 6, axis=-1))
    vs, vw = lax.optimization_barrier((vs, vw))
    ks = rope(rms_norm(ks, k_norm[1]), cos, sin)
    kw = rope(rms_norm(kw, k_norm[2]), cos, sin)

    assert CMP_LEN == 2 * CMP_STRIDE and S % CMP_STRIDE == 0
    n_cmp = (S - CMP_LEN) // CMP_STRIDE + 1
    win_idx = jnp.arange(n_cmp)[:, None] * CMP_STRIDE + jnp.arange(CMP_LEN)[None, :]
    cmp_end = win_idx[:, -1]

    def compress(t, pe, w1, w2):
        grp = t.reshape(Bsz, S // CMP_STRIDE, CMP_STRIDE, G, dh)
        blocks = jnp.concatenate([grp[:, :-1], grp[:, 1:]], axis=2) + pe[:, None, :]
        blocks = jnp.moveaxis(blocks, 3, 2).reshape(Bsz, n_cmp, G, CMP_LEN * dh)
        return jax.nn.gelu(blocks @ w1) @ w2

    cos_c, sin_c = rope_tables(cmp_end, dh)
    k_cmp = rope(rms_norm(compress(kc_raw, cmp_pe[0], cmp_w1[0], cmp_w2[0]), k_norm[0]), cos_c, sin_c)
    v_cmp = compress(vc_raw, cmp_pe[1], cmp_w1[1], cmp_w2[1])

    n_slc = S // SLC_BLOCK
    n_top = min(SLC_TOPK, n_slc)
    c0 = jnp.arange(n_cmp) * CMP_STRIDE
    s0 = jnp.arange(n_slc) * SLC_BLOCK
    overlap = jnp.clip(jnp.minimum(c0[:, None] + CMP_LEN, s0[None, :] + SLC_BLOCK)
                       - jnp.maximum(c0[:, None], s0[None, :]), 0, None).astype(f32) / CMP_LEN
    gates = jax.nn.sigmoid(gate_logits.astype(f32)).reshape(Bsz, S, G, R * 3).transpose(0, 2, 3, 1)

    bf16 = jnp.bfloat16
    n_cmp_pad = S // CMP_STRIDE
    pad_cmp = lambda t: jnp.pad(t.astype(bf16), ((0, 0), (0, n_cmp_pad - n_cmp), (0, 0), (0, 0)))
    keys = lambda t: t.astype(bf16).transpose(0, 2, 1, 3)
    vals = lambda t: t.astype(bf16).transpose(0, 2, 3, 1)
    q_heads = (q * scale).astype(bf16).reshape(Bsz, S, G, R, dh).transpose(0, 2, 3, 1, 4)
    overlap_t = jnp.pad(overlap.T, ((0, 0), (0, n_cmp_pad - n_cmp))).astype(bf16)
    expand_t = ((jnp.arange(S) // SLC_BLOCK)[:, None] == jnp.arange(n_slc)[None, :]).astype(bf16)
    return _nsa_attention(q_heads, keys(pad_cmp(k_cmp)), vals(pad_cmp(v_cmp)), keys(ks), vals(vs), keys(kw), vals(vw),
                          gates, overlap_t, expand_t, n_top)


NSA_R = NSA_HEADS // NSA_KV_HEADS
SLC_TILE = 1024
MASK_NEG = -1e30


def _dot_nt(a, b):
    return lax.dot_general(a, b, (((1,), (1,)), ((), ())), preferred_element_type=jnp.float32)


def _nsa_body(q_ref, kc_ref, vct_ref, ks_ref, vst_ref, kw_ref, vwt_ref, gate_ref, ovt_ref, expt_ref, o_ref,
              *, n_top):
    f32, bf16 = jnp.float32, jnp.bfloat16
    R, QB, dh = q_ref.shape
    S = ks_ref.shape[0]
    n_slc, n_cmp_pad = ovt_ref.shape
    M = R * QB
    t0 = pl.program_id(2) * QB
    q = q_ref[...].reshape(M, dh)
    per_head = lambda t: jnp.concatenate([t] * R, axis=1)

    band = S if S < WINDOW + QB else WINDOW + QB
    start = pl.multiple_of(jnp.maximum(t0 + QB - band, 0), QB)
    s = _dot_nt(kw_ref[pl.ds(start, band), :], q)
    diff = (t0 - start) + (lax.broadcasted_iota(jnp.int32, (band, QB), 1) - lax.broadcasted_iota(jnp.int32, (band, QB), 0))
    s = s + per_head(jnp.where((diff >= 0) & (diff < WINDOW), 0.0, -jnp.inf))
    p = jnp.exp(s - jnp.max(s, axis=0, keepdims=True))
    o_win = jnp.dot(vwt_ref[:, pl.ds(start, band)], p.astype(bf16), preferred_element_type=f32)
    o_win = o_win / jnp.sum(p, axis=0, keepdims=True)

    s = _dot_nt(kc_ref[...], q)
    n_idx = lax.broadcasted_iota(jnp.int32, (n_cmp_pad, QB), 0)
    t_idx = t0 + lax.broadcasted_iota(jnp.int32, (n_cmp_pad, QB), 1)
    valid = (n_idx * CMP_STRIDE + (CMP_LEN - 1) <= t_idx) & (n_idx < n_cmp_pad - 1)
    s = s + per_head(jnp.where(valid, 0.0, -jnp.inf))
    m = jnp.max(s, axis=0, keepdims=True)
    m = jnp.where(jnp.isfinite(m), m, 0.0)
    p = jnp.exp(s - m)
    p = p / jnp.maximum(jnp.sum(p, axis=0, keepdims=True), jnp.finfo(f32).tiny)
    o_cmp = jnp.dot(vct_ref[...], p.astype(bf16), preferred_element_type=f32)

    p_sum = p[:, 0:QB]
    for r in range(1, R):
        p_sum = p_sum + p[:, r * QB:(r + 1) * QB]
    imp = sum(jnp.dot(ovt_ref[...], t, preferred_element_type=f32) for t in _bf16_terms(p_sum, 3))
    j_idx = lax.broadcasted_iota(jnp.int32, (n_slc, QB), 0)
    t_lane = t0 + lax.broadcasted_iota(jnp.int32, (n_slc, QB), 1)
    cur = lax.shift_right_logical(t_lane, 6)
    forced = (j_idx == 0) | (j_idx == cur) | (j_idx == cur - 1)
    started = j_idx * SLC_BLOCK <= t_lane
    imp = jnp.where(forced, BIG, jnp.where(started, imp, -BIG))
    rank = jnp.zeros((n_slc, QB), f32)
    for jp in range(n_slc):
        row = imp[jp:jp + 1, :]
        ahead = (row > imp) | ((row == imp) & (j_idx > jp))
        rank = rank + jnp.where(ahead, 1.0, 0.0)
    sel = jnp.where(rank < n_top, 1.0, 0.0).astype(bf16)
    k_rel = lax.broadcasted_iota(jnp.int32, (SLC_TILE, QB), 0)
    t_q = t0 + lax.broadcasted_iota(jnp.int32, (SLC_TILE, QB), 1)

    def slc_step(kt, carry):
        m, l, acc = carry
        k0 = pl.multiple_of(kt * SLC_TILE, SLC_TILE)
        s = _dot_nt(ks_ref[pl.ds(k0, SLC_TILE), :], q)
        sel_keys = jnp.dot(expt_ref[pl.ds(k0, SLC_TILE), :], sel, preferred_element_type=f32)
        bias = jnp.where((sel_keys > 0.5) & (k_rel + k0 <= t_q), 0.0, MASK_NEG)
        s = s + per_head(bias)
        m_new = jnp.maximum(m, jnp.max(s, axis=0, keepdims=True))
        alpha = jnp.exp(m - m_new)
        p = jnp.exp(s - m_new)
        l = alpha * l + jnp.sum(p, axis=0, keepdims=True)
        acc = alpha * acc + jnp.dot(vst_ref[:, pl.ds(k0, SLC_TILE)], p.astype(bf16), preferred_element_type=f32)
        return m_new, l, acc

    n_tiles = (t0 + QB - 1) // SLC_TILE + 1
    init = (jnp.full((1, M), MASK_NEG, f32), jnp.zeros((1, M), f32), jnp.zeros((dh, M), f32))
    _, l, acc = lax.fori_loop(0, n_tiles, slc_step, init)
    o_slc = acc / l

    gate = gate_ref[...]
    g_cmp, g_slc, g_win = (jnp.concatenate([gate[3 * r + c:3 * r + c + 1, :] for r in range(R)], axis=1)
                           for c in range(3))
    o_t = g_cmp * o_cmp + g_slc * o_slc + g_win * o_win
    o_ref[...] = jnp.concatenate([o_t[:, r * QB:(r + 1) * QB].T for r in range(R)], axis=1)


def _nsa_attention(q, kc, vct, ks, vst, kw, vwt, gates, overlap_t, expand_t, n_top):
    Bsz, G, R, S, dh = q.shape
    n_cmp_pad = kc.shape[2]
    assert S % SLC_TILE == 0 and SLC_TILE % Q_BLOCK == 0, (S, SLC_TILE, Q_BLOCK)
    keys = lambda n: pl.BlockSpec((None, None, n, dh), lambda b, g, i: (b, g, 0, 0))
    vals = lambda n: pl.BlockSpec((None, None, dh, n), lambda b, g, i: (b, g, 0, 0))
    const = lambda a: pl.BlockSpec(a.shape, lambda b, g, i: (0, 0))
    return pl.pallas_call(
        functools.partial(_nsa_body, n_top=n_top),
        grid=(Bsz, G, S // Q_BLOCK),
        in_specs=[pl.BlockSpec((None, None, R, Q_BLOCK, dh), lambda b, g, i: (b, g, 0, i, 0)),
                  keys(n_cmp_pad), vals(n_cmp_pad), keys(S), vals(S), keys(S), vals(S),
                  pl.BlockSpec((None, None, 3 * R, Q_BLOCK), lambda b, g, i: (b, g, 0, i)),
                  const(overlap_t), const(expand_t)],
        out_specs=pl.BlockSpec((None, Q_BLOCK, R * dh), lambda b, g, i: (b, i, g)),
        out_shape=jax.ShapeDtypeStruct((Bsz, S, G * R * dh), jnp.float32),
        compiler_params=pltpu.CompilerParams(dimension_semantics=("arbitrary", "arbitrary", "arbitrary"),
                                             vmem_limit_bytes=48 * 1024 * 1024),
        name="nsa_attention",
    )(q, kc, vct, ks, vst, kw, vwt, gates, overlap_t, expand_t)


def hybrid_mixer(x, ln_mix, w_in, lam_re, lam_im, b_re, b_im, c_re, c_im, s5_d, s5_log_dt, glu_w, glu_b,
                 s5_out_norm, conv_w, a_log, dt_bias, gdn_norm_w, q_norm, k_norm, cmp_pe, cmp_w1,
                 cmp_w2, cos, sin):
    Bsz, S, D = x.shape
    proj = _in_proj(x.reshape(-1, D), ln_mix, _in_proj_weight(w_in)).reshape(Bsz, S, PROJ_W)
    y_s5 = s5_mixer_normed(proj, lam_re, lam_im, b_re, b_im, c_re, c_im, s5_d, s5_log_dt, glu_w, glu_b, s5_out_norm)
    y_gdn = gdn_mixer(proj, conv_w, a_log, dt_bias, gdn_norm_w)
    y_nsa = nsa_mixer(proj[..., PROJ_Q:PROJ_Q + NSA_WIDTH], proj[..., PROJ_KV:PROJ_KV + 6 * NSA_KV_WIDTH],
                      proj[..., PROJ_GATES:PROJ_GATES + 3 * NSA_HEADS], q_norm, k_norm, cmp_pe, cmp_w1, cmp_w2,
                      cos, sin)
    return y_s5, y_gdn, y_nsa


def hier_moe(x, h, logits, layer, w_gate, w_up, w_down):
    T, D = x.shape
    f32 = jnp.float32
    g_logits = logits[:, :N_GROUPS]
    grp = jnp.argmax(g_logits, axis=-1)
    p_grp = jnp.take_along_axis(jax.nn.softmax(g_logits, axis=-1), grp[:, None], axis=-1)
    e_logits = logits[:, N_GROUPS:N_ROUTER].reshape(T, N_GROUPS, EXPERTS_PER_GROUP)
    e_logits = jnp.take_along_axis(e_logits, grp[:, None, None], axis=1)[:, 0]
    local = jnp.arange(EXPERTS_PER_GROUP, dtype=jnp.int32)[None, :]
    rest, top_logit, top_local = e_logits, [], []
    for _ in range(TOP_K):
        idx = jnp.argmax(rest, axis=-1).astype(jnp.int32)
        top_logit.append(jnp.take_along_axis(rest, idx[:, None], axis=-1)[:, 0])
        top_local.append(idx)
        rest = jnp.where(local == idx[:, None], -jnp.inf, rest)
    top_logit, top_local = jnp.stack(top_logit, axis=-1), jnp.stack(top_local, axis=-1)
    weights = jax.nn.softmax(top_logit, axis=-1) * p_grp
    experts = grp[:, None].astype(jnp.int32) * EXPERTS_PER_GROUP + top_local
    n_assign = T * TOP_K
    e_flat = experts.reshape(-1)
    tok_flat = jnp.repeat(jnp.arange(T, dtype=jnp.int32), TOP_K)
    onehot = (e_flat[:, None] == jnp.arange(N_EXPERTS, dtype=jnp.int32)[None, :]).astype(f32)
    blocks = onehot.reshape(n_assign // COUNT_BLOCK, COUNT_BLOCK, N_EXPERTS)
    tri = jnp.tril(jnp.ones((COUNT_BLOCK, COUNT_BLOCK), f32))
    inside = jnp.einsum('ij,bje->bie', tri, blocks)
    before = jnp.cumsum(inside[:, -1, :], axis=0) - inside[:, -1, :]
    running = (inside + before[:, None, :]).reshape(n_assign, N_EXPERTS).astype(jnp.int32)
    counts = running[-1]
    padded = (counts + MOE_ROWS - 1) // MOE_ROWS * MOE_ROWS
    pad_start = jnp.cumsum(padded) - padded
    dest = (pad_start[e_flat] + jnp.take_along_axis(running, e_flat[:, None], axis=1)[:, 0] - 1).astype(jnp.int32)
    n_blk = (n_assign + MOE_ROWS - 1) // MOE_ROWS + N_EXPERTS
    cap = n_blk * MOE_ROWS
    slot_tok = jnp.zeros((cap,), jnp.int32).at[dest].set(tok_flat)
    seg_end = jnp.cumsum(padded)
    blk_start = jnp.arange(n_blk, dtype=seg_end.dtype) * MOE_ROWS
    blk_expert = jnp.minimum(jnp.sum((seg_end[None, :] <= blk_start[:, None]).astype(jnp.int32), axis=1),
                             N_EXPERTS - 1).astype(jnp.int32)
    y = _moe_ffn(blk_expert, slot_tok, x, layer, w_gate.astype(f32), w_up.astype(f32),
                 w_down.astype(f32))
    return _moe_combine(dest, h, weights, y)


CMB_ROWS = 128


def _moe_combine_body(d_ref, h_ref, w_ref, y_hbm, o_ref, buf, sem):
    i = pl.program_id(0)
    n = pl.num_programs(0)
    rows = h_ref.shape[0]
    slot = i & 1

    def row_copy(step, r, k, to):
        src = y_hbm.at[pl.ds(d_ref[(step * rows + r) * TOP_K + k], 1), :]
        return pltpu.make_async_copy(src, buf.at[to, k, pl.ds(r, 1), :], sem.at[to])

    def start_step(step, to):
        for r in range(rows):
            for k in range(TOP_K):
                row_copy(step, r, k, to).start(priority=(r * TOP_K + k) % 2)

    def finish_step():
        for r in range(rows):
            for k in range(TOP_K):
                row_copy(i, r, k, slot).wait()
        out = h_ref[...]
        for k in range(TOP_K):
            out = out + w_ref[:, k:k + 1] * buf[slot, k]
        o_ref[...] = out

    @pl.when(i == 0)
    def _():
        start_step(0, 0)

    @pl.when(i + 1 < n)
    def _():
        start_step(i + 1, 1 - slot)
        finish_step()

    @pl.when(i + 1 == n)
    def _():
        finish_step()


def _moe_combine(dest, h, weights, y):
    T, D = h.shape
    rows = min(CMB_ROWS, T)
    return pl.pallas_call(
        _moe_combine_body,
        grid_spec=pltpu.PrefetchScalarGridSpec(
            num_scalar_prefetch=1, grid=(T // rows,),
            in_specs=[pl.BlockSpec((rows, D), lambda i, d: (i, 0)),
                      pl.BlockSpec((rows, TOP_K), lambda i, d: (i, 0)),
                      pl.BlockSpec(memory_space=pl.ANY)],
            out_specs=pl.BlockSpec((rows, D), lambda i, d: (i, 0)),
            scratch_shapes=[pltpu.VMEM((2, TOP_K, rows, D), jnp.float32), pltpu.SemaphoreType.DMA((2,))]),
        out_shape=jax.ShapeDtypeStruct((T, D), jnp.float32),
        compiler_params=pltpu.CompilerParams(dimension_semantics=("arbitrary",),
                                             vmem_limit_bytes=32 * 1024 * 1024),
        name="moe_combine",
    )(dest, h, weights, y)


MOE_ROWS = 256
COUNT_BLOCK = 128


def _moe_ffn_body(be_ref, tok_ref, x_hbm, wg_ref, wu_ref, wd_ref, o_ref, xbuf, sem, wg_bf, wu_bf, wd_bf):
    i = pl.program_id(0)
    n = pl.num_programs(0)
    slot = i & 1

    def row_copy(step, r, to):
        src = x_hbm.at[pl.ds(tok_ref[step * MOE_ROWS + r], 1), :]
        return pltpu.make_async_copy(src, xbuf.at[to, pl.ds(r, 1), :], sem.at[to])

    def start_step(step, to):
        for r in range(MOE_ROWS):
            row_copy(step, r, to).start(priority=r % 2)

    def finish_step():
        for r in range(MOE_ROWS):
            row_copy(i, r, slot).wait()
        x = xbuf[slot].astype(jnp.bfloat16)
        gate = jnp.dot(x, wg_bf[...], preferred_element_type=jnp.float32)
        up = jnp.dot(x, wu_bf[...], preferred_element_type=jnp.float32)
        hid = (jax.nn.silu(gate) * up).astype(jnp.bfloat16)
        o_ref[...] = jnp.dot(hid, wd_bf[...], preferred_element_type=jnp.float32)

    @pl.when(i == 0)
    def _():
        start_step(0, 0)

    @pl.when((i == 0) | (be_ref[i] != be_ref[jnp.maximum(i - 1, 0)]))
    def _():
        wg_bf[...] = wg_ref[...].astype(jnp.bfloat16)
        wu_bf[...] = wu_ref[...].astype(jnp.bfloat16)
        wd_bf[...] = wd_ref[...].astype(jnp.bfloat16)

    @pl.when(i + 1 < n)
    def _():
        start_step(i + 1, 1 - slot)
        finish_step()

    @pl.when(i + 1 == n)
    def _():
        finish_step()


def _moe_ffn(blk_expert, slot_tok, x, layer, w_gate, w_up, w_down):
    cap = slot_tok.shape[0]
    D = x.shape[1]
    n_blk = cap // MOE_ROWS
    De = w_gate.shape[-1]
    return pl.pallas_call(
        _moe_ffn_body,
        grid_spec=pltpu.PrefetchScalarGridSpec(
            num_scalar_prefetch=2, grid=(n_blk,),
            in_specs=[pl.BlockSpec(memory_space=pl.ANY),
                      pl.BlockSpec((None, None, D, De), lambda i, be, tok: (layer, be[i], 0, 0)),
                      pl.BlockSpec((None, None, D, De), lambda i, be, tok: (layer, be[i], 0, 0)),
                      pl.BlockSpec((None, None, De, D), lambda i, be, tok: (layer, be[i], 0, 0))],
            out_specs=pl.BlockSpec((MOE_ROWS, D), lambda i, be, tok: (i, 0)),
            scratch_shapes=[pltpu.VMEM((2, MOE_ROWS, D), jnp.float32), pltpu.SemaphoreType.DMA((2,)),
                            pltpu.VMEM((D, De), jnp.bfloat16), pltpu.VMEM((D, De), jnp.bfloat16),
                            pltpu.VMEM((De, D), jnp.bfloat16)]),
        out_shape=jax.ShapeDtypeStruct((cap, D), jnp.float32),
        compiler_params=pltpu.CompilerParams(dimension_semantics=("arbitrary",),
                                             vmem_limit_bytes=56 * 1024 * 1024),
        name="moe_ffn",
    )(blk_expert, slot_tok, x, w_gate, w_up, w_down)


def kernel(x, ln_mix, w_in, s5_lambda_re, s5_lambda_im, s5_b_re, s5_b_im, s5_c_re, s5_c_im, s5_d,
           s5_log_dt, s5_glu_w, s5_glu_b, s5_out_norm, gdn_conv_w, gdn_a_log, gdn_dt_bias, gdn_norm_w,
           nsa_q_norm, nsa_k_norm, nsa_cmp_pe, nsa_cmp_w1, nsa_cmp_w2, nsa_out_norm, w_out, ln_ffn,
           moe_group_w, moe_group_b, moe_expert_w, moe_expert_b, moe_w_gate, moe_w_up, moe_w_down):
    Bsz, S, D = x.shape
    cos, sin = rope_tables(jnp.arange(S), NSA_HEAD_DIM)
    for l in range(DEPTH):
        y_s5, y_gdn, y_nsa = hybrid_mixer(
            x, ln_mix[l], w_in[l], s5_lambda_re[l], s5_lambda_im[l], s5_b_re[l], s5_b_im[l], s5_c_re[l], s5_c_im[l],
            s5_d[l], s5_log_dt[l], s5_glu_w[l], s5_glu_b[l], s5_out_norm[l], gdn_conv_w[l], gdn_a_log[l],
            gdn_dt_bias[l], gdn_norm_w[l], nsa_q_norm[l], nsa_k_norm[l], nsa_cmp_pe[l], nsa_cmp_w1[l], nsa_cmp_w2[l],
            cos, sin)
        router_w = jnp.concatenate([moe_group_w[l], moe_expert_w[l]], axis=-1)
        router_b = jnp.concatenate([moe_group_b[l], moe_expert_b[l]], axis=-1)
        h, a, logits = _out_proj(x.reshape(-1, D), y_s5.reshape(-1, S5_WIDTH), y_gdn.reshape(-1, GDN_WIDTH),
                                 y_nsa.reshape(-1, NSA_WIDTH), nsa_out_norm[l], w_out[l], ln_ffn[l], router_w, router_b)
        x = hier_moe(a, h, logits, l, moe_w_gate, moe_w_up, moe_w_down).reshape(Bsz, S, D)
    return x
```

```python
import functools
import math
import jax, jax.numpy as jnp
from jax import lax
import numpy as np
from jax.experimental import pallas as pl
from jax.experimental.pallas import tpu as pltpu

D_MODEL = 2048
BATCH = 4
SEQ = 4096
DEPTH = 2

EPS = 1e-6
S5_WIDTH = D_MODEL // 4
S5_GROUP = 16
S5_GROUPS = S5_WIDTH // S5_GROUP
S5_STATE = 64
GDN_HEAD_DIM = 128
GDN_WIDTH = D_MODEL // 4
GDN_HEADS = GDN_WIDTH // GDN_HEAD_DIM
GDN_CONV = 4
GDN_CHUNK = 64
NSA_HEAD_DIM = 64
NSA_WIDTH = D_MODEL // 2
NSA_HEADS = NSA_WIDTH // NSA_HEAD_DIM
NSA_KV_HEADS = 4
NSA_KV_WIDTH = NSA_KV_HEADS * NSA_HEAD_DIM
CMP_LEN = 32
CMP_STRIDE = 16
CMP_HIDDEN = 256
SLC_BLOCK = 64
SLC_TOPK = 16
WINDOW = 512
Q_BLOCK = 128
ROPE_THETA = 10000.0
BIG = 1e9
N_GROUPS = 4
EXPERTS_PER_GROUP = 8
N_EXPERTS = N_GROUPS * EXPERTS_PER_GROUP
TOP_K = 2
D_EXPERT = 512
MOE_BLOCK = 128
IN_SIZES = (S5_WIDTH, 3 * GDN_WIDTH, GDN_WIDTH, GDN_HEADS, GDN_HEADS, NSA_WIDTH, 6 * NSA_KV_WIDTH, 3 * NSA_HEADS)
D_IN = sum(IN_SIZES)


PROJ_QKV, PROJ_KV, PROJ_Q, PROJ_U, PROJ_Z, PROJ_SMALL = 0, 1536, 3072, 4096, 4608, 5120
PROJ_SMALL_W = 512
PROJ_W = PROJ_SMALL + PROJ_SMALL_W
PROJ_GATES = PROJ_SMALL + 2 * GDN_HEADS
LANES = 128
IN_ROWS, IN_COLS = 1024, 512
OUT_ROWS = 256
N_ROUTER = N_GROUPS + N_EXPERTS


def _in_proj_weight(w_in):
    u, qkv, z, ga, gb, q, kv, gl = jnp.split(w_in, np.cumsum(IN_SIZES)[:-1].tolist(), axis=-1)
    small = jnp.concatenate([ga, gb, gl], axis=-1)
    small = jnp.pad(small, ((0, 0), (0, PROJ_SMALL_W - small.shape[-1])))
    return jnp.concatenate([qkv, kv, q, u, z, small], axis=-1)


def _in_proj_body(x_ref, g_ref, w_ref, o_ref, a_ref):
    @pl.when(pl.program_id(1) == 0)
    def _():
        x = x_ref[...]
        a_ref[...] = (x * lax.rsqrt(jnp.mean(x * x, axis=-1, keepdims=True) + EPS) * g_ref[...]).astype(jnp.bfloat16)

    o_ref[...] = jnp.dot(a_ref[...], w_ref[...].astype(jnp.bfloat16), preferred_element_type=jnp.float32)


def _in_proj(x, gain, w):
    T, D = x.shape
    N = w.shape[1]
    tm = min(IN_ROWS, T)
    return pl.pallas_call(
        _in_proj_body,
        grid=(T // tm, N // IN_COLS),
        in_specs=[pl.BlockSpec((tm, D), lambda i, j: (i, 0)),
                  pl.BlockSpec((1, D), lambda i, j: (0, 0)),
                  pl.BlockSpec((D, IN_COLS), lambda i, j: (0, j))],
        out_specs=pl.BlockSpec((tm, IN_COLS), lambda i, j: (i, j)),
        out_shape=jax.ShapeDtypeStruct((T, N), jnp.float32),
        scratch_shapes=[pltpu.VMEM((tm, D), jnp.bfloat16)],
        compiler_params=pltpu.CompilerParams(dimension_semantics=("arbitrary", "arbitrary"),
                                             vmem_limit_bytes=48 * 1024 * 1024),
        name="in_proj",
    )(x, gain.astype(jnp.float32).reshape(1, D), w)


def _out_proj_body(x_ref, s5_ref, gdn_ref, nsa_ref, ng_ref, w_ref, fg_ref, rw_ref, rb_ref, h_ref, a_ref, lg_ref):
    f32, bf16 = jnp.float32, jnp.bfloat16
    nsa = nsa_ref[...]
    nsa = nsa * lax.rsqrt(jnp.mean(nsa * nsa, axis=-1, keepdims=True) + EPS) * ng_ref[...]
    k0, k1 = S5_WIDTH, S5_WIDTH + GDN_WIDTH
    y = (jnp.dot(s5_ref[...].astype(bf16), w_ref[0:k0, :], preferred_element_type=f32)
         + jnp.dot(gdn_ref[...].astype(bf16), w_ref[k0:k1, :], preferred_element_type=f32)
         + jnp.dot(nsa.astype(bf16), w_ref[k1:, :], preferred_element_type=f32))
    h = x_ref[...] + y
    h_ref[...] = h
    a = h * lax.rsqrt(jnp.mean(h * h, axis=-1, keepdims=True) + EPS) * fg_ref[...]
    a_ref[...] = a
    lg_ref[...] = jnp.dot(a.astype(bf16), rw_ref[...], preferred_element_type=f32) + rb_ref[...]


def _out_proj(x, y_s5, y_gdn, y_nsa, nsa_gain, w_out, ffn_gain, router_w, router_b):
    T, D = x.shape
    tm = min(OUT_ROWS, T)
    f32, bf16 = jnp.float32, jnp.bfloat16
    rw = jnp.pad(router_w.astype(bf16), ((0, 0), (0, LANES - N_ROUTER)))
    rb = jnp.pad(router_b.astype(f32), (0, LANES - N_ROUTER)).reshape(1, LANES)
    rows = lambda n: pl.BlockSpec((tm, n), lambda i: (i, 0))
    full = lambda *shape: pl.BlockSpec(shape, lambda i: (0,) * len(shape))
    return pl.pallas_call(
        _out_proj_body,
        grid=(T // tm,),
        in_specs=[rows(D), rows(S5_WIDTH), rows(GDN_WIDTH), rows(NSA_WIDTH), full(1, NSA_WIDTH), full(D, D),
                  full(1, D), full(D, LANES), full(1, LANES)],
        out_specs=[rows(D), rows(D), rows(LANES)],
        out_shape=[jax.ShapeDtypeStruct((T, D), f32), jax.ShapeDtypeStruct((T, D), f32),
                   jax.ShapeDtypeStruct((T, LANES), f32)],
        compiler_params=pltpu.CompilerParams(dimension_semantics=("arbitrary",),
                                             vmem_limit_bytes=48 * 1024 * 1024),
        name="out_proj",
    )(x, y_s5, y_gdn, y_nsa, nsa_gain.astype(f32).reshape(1, NSA_WIDTH), w_out.astype(bf16),
      ffn_gain.astype(f32).reshape(1, D), rw, rb)


def rms_norm(x, gain):
    xf = x.astype(jnp.float32)
    y = xf * lax.rsqrt(jnp.mean(xf * xf, axis=-1, keepdims=True) + EPS)
    return (y * gain.astype(jnp.float32)).astype(x.dtype)


def l2_norm(x):
    return x * lax.rsqrt(jnp.sum(x * x, axis=-1, keepdims=True) + EPS)


def rope_tables(pos, dim):
    inv_freq = ROPE_THETA ** (-jnp.arange(0, dim, 2, dtype=jnp.float32) / dim)
    ang = pos.astype(jnp.float32)[:, None] * inv_freq[None, :]
    return jnp.cos(ang)[:, None, :], jnp.sin(ang)[:, None, :]


def rope(x, cos, sin):
    x1, x2 = jnp.split(x, 2, axis=-1)
    return jnp.concatenate([x1 * cos - x2 * sin, x2 * cos + x1 * sin], axis=-1).astype(x.dtype)


def masked_softmax(s, mask):
    s = jnp.where(mask, s, -jnp.inf)
    m = jnp.max(s, axis=-1, keepdims=True)
    m = jnp.where(jnp.isfinite(m), m, 0.0)
    p = jnp.exp(s - m)
    return p / jnp.maximum(jnp.sum(p, axis=-1, keepdims=True), jnp.finfo(jnp.float32).tiny)


S5_OCT = 8
S5_NOCT = S5_GROUPS // S5_OCT
S5_OCT_IN = S5_OCT * S5_GROUP
S5_OCT_ST = S5_OCT * S5_STATE
SUBLANES = 8
S5_SCAN_SHIFTS = (1, 2, 4)


def _s5_constants(lam_re, lam_im, b_re, b_im, c_re, c_im, log_dt):
    f32 = jnp.float32
    lr, li = lam_re.astype(f32), lam_im.astype(f32)
    dt = jnp.exp(log_dt.astype(f32))[:, None]

    def lam_pow(k):
        mag = jnp.exp(k * lr * dt)
        return mag * jnp.cos(k * li * dt), mag * jnp.sin(k * li * dt)

    ar, ai = lam_pow(1.0)
    nr, ni = ar - 1.0, ai
    den = lr * lr + li * li
    qr, qi = (nr * lr + ni * li) / den, (ni * lr - nr * li) / den
    br, bi = b_re.astype(f32), b_im.astype(f32)
    bbr = qr[:, :, None] * br - qi[:, :, None] * bi
    bbi = qr[:, :, None] * bi + qi[:, :, None] * br
    eye = jnp.eye(S5_OCT, dtype=f32)

    def blk_b(t):
        t = t.reshape(S5_NOCT, S5_OCT, S5_STATE, S5_GROUP)
        return jnp.einsum('qgph,gk->qghkp', t, eye).reshape(S5_NOCT, S5_OCT_IN, S5_OCT_ST)

    def blk_c(t):
        t = t.reshape(S5_NOCT, S5_OCT, S5_GROUP, S5_STATE)
        return jnp.einsum('qghp,gk->qgpkh', t, eye).reshape(S5_NOCT, S5_OCT_ST, S5_OCT_IN)

    b_mat = jnp.stack([blk_b(bbr), blk_b(bbi)]).astype(jnp.bfloat16)
    c_mat = jnp.stack([blk_c(c_re.astype(f32)), blk_c(-c_im.astype(f32))]).astype(jnp.bfloat16)
    rows = jnp.arange(SUBLANES, dtype=f32)[:, None, None]
    tiles = []
    for k in S5_SCAN_SHIFTS:
        pr, pi = lam_pow(jnp.full_like(rows, float(k)))
        keep = rows >= k
        tiles.append(jnp.stack([jnp.where(keep, pr, 0.0), jnp.where(keep, pi, 0.0)]))
    pr, pi = lam_pow(rows + 1.0)
    tiles.append(jnp.stack([pr, pi]))
    a_mat = jnp.stack(tiles).reshape(len(tiles), 2, SUBLANES, S5_GROUPS * S5_STATE)
    return b_mat, c_mat, a_mat


def _s5_body(u_ref, b_ref, c_ref, a_ref, d_ref, gw_ref, gb_ref, gain_ref, o_ref, st_ref, xr_ref, xi_ref):
    bf16 = jnp.bfloat16
    n_tiles = u_ref.shape[0] // SUBLANES

    @pl.when(pl.program_id(1) == 0)
    def _():
        st_ref[...] = jnp.zeros_like(st_ref)

    u = u_ref[...]
    ys = []
    for q in range(S5_NOCT):
        lanes = slice(q * S5_OCT_ST, (q + 1) * S5_OCT_ST)
        uq = u[:, q * S5_OCT_IN:(q + 1) * S5_OCT_IN].astype(bf16)
        xr_ref[...] = jnp.dot(uq, b_ref[0, q], preferred_element_type=jnp.float32)
        xi_ref[...] = jnp.dot(uq, b_ref[1, q], preferred_element_type=jnp.float32)

        def tile_step(i, carry, lanes=lanes):
            cr, ci = carry
            r0 = pl.multiple_of(i * SUBLANES, SUBLANES)
            xr = xr_ref[pl.ds(r0, SUBLANES), :]
            xi = xi_ref[pl.ds(r0, SUBLANES), :]
            for lvl, k in enumerate(S5_SCAN_SHIFTS):
                pr, pi = a_ref[lvl, 0, :, lanes], a_ref[lvl, 1, :, lanes]
                sr, si = pltpu.roll(xr, k, 0), pltpu.roll(xi, k, 0)
                xr, xi = xr + (pr * sr - pi * si), xi + (pr * si + pi * sr)
            pr, pi = a_ref[len(S5_SCAN_SHIFTS), 0, :, lanes], a_ref[len(S5_SCAN_SHIFTS), 1, :, lanes]
            cr = jnp.broadcast_to(cr, xr.shape)
            ci = jnp.broadcast_to(ci, xi.shape)
            xr, xi = xr + (pr * cr - pi * ci), xi + (pr * ci + pi * cr)
            xr_ref[pl.ds(r0, SUBLANES), :] = xr
            xi_ref[pl.ds(r0, SUBLANES), :] = xi
            return xr[SUBLANES - 1:, :], xi[SUBLANES - 1:, :]

        cr, ci = lax.fori_loop(0, n_tiles, tile_step, (st_ref[0, :, lanes], st_ref[1, :, lanes]))
        st_ref[0, :, lanes] = cr
        st_ref[1, :, lanes] = ci
        ys.append(jnp.dot(xr_ref[...].astype(bf16), c_ref[0, q], preferred_element_type=jnp.float32)
                  + jnp.dot(xi_ref[...].astype(bf16), c_ref[1, q], preferred_element_type=jnp.float32))
    y = jnp.concatenate(ys, axis=1) + d_ref[...] * u
    y = jax.nn.gelu(y)
    z = jnp.dot(y.astype(bf16), gw_ref[...], preferred_element_type=jnp.float32) + gb_ref[...]
    y = y * jax.nn.sigmoid(z)
    y = y * lax.rsqrt(jnp.mean(y * y, axis=-1, keepdims=True) + EPS)
    o_ref[...] = y * gain_ref[...]


def s5_mixer_normed(proj, lam_re, lam_im, b_re, b_im, c_re, c_im, d_skip, log_dt, glu_w, glu_b, out_norm, chunk=512):
    Bsz, S, _ = proj.shape
    W = S5_WIDTH
    f32 = jnp.float32
    L = min(chunk, S)
    b_mat, c_mat, a_mat = _s5_constants(lam_re, lam_im, b_re, b_im, c_re, c_im, log_dt)
    n_state = S5_GROUPS * S5_STATE
    full = lambda *shape: pl.BlockSpec(shape, lambda b, c: (0,) * len(shape))
    return pl.pallas_call(
        _s5_body,
        grid=(Bsz, S // L),
        in_specs=[pl.BlockSpec((None, L, W), lambda b, c: (b, c, PROJ_U // W)),
                  full(*b_mat.shape), full(*c_mat.shape), full(*a_mat.shape),
                  full(1, W), full(W, W), full(1, W), full(1, W)],
        out_specs=pl.BlockSpec((None, L, W), lambda b, c: (b, c, 0)),
        out_shape=jax.ShapeDtypeStruct((Bsz, S, W), f32),
        scratch_shapes=[pltpu.VMEM((2, 1, n_state), f32),
                        pltpu.VMEM((L, S5_OCT_ST), f32), pltpu.VMEM((L, S5_OCT_ST), f32)],
        compiler_params=pltpu.CompilerParams(dimension_semantics=("arbitrary", "arbitrary"),
                                             vmem_limit_bytes=48 * 1024 * 1024),
        name="s5_mixer",
    )(proj, b_mat, c_mat, a_mat, d_skip.astype(f32).reshape(1, W), glu_w.astype(jnp.bfloat16),
      glu_b.astype(f32).reshape(1, W), out_norm.astype(f32).reshape(1, W))


GDN_ROWS = 256
GDN_PACK = GDN_HEADS * GDN_CHUNK


def _bf16_terms(x, n):
    terms, rem = [], x
    for i in range(n):
        t = rem.astype(jnp.bfloat16)
        terms.append(t)
        if i + 1 < n:
            rem = rem - t.astype(jnp.float32)
    return terms


def _dot_exact_lhs(lhs01, x):
    lhs = lhs01.astype(jnp.bfloat16)
    return sum(jnp.dot(lhs, t, preferred_element_type=jnp.float32) for t in _bf16_terms(x, 3))


def _gdn_packed_unit_lower_inverse(a_mats, row, col, blk_diag):
    f32 = jnp.float32

    def mm(x, y):
        xh, xl = _bf16_terms(x, 2)
        yh, yl = (jnp.where(blk_diag, jnp.concatenate([t] * GDN_HEADS, axis=0), 0.0) for t in _bf16_terms(y, 2))
        return (jnp.dot(xh, yh, preferred_element_type=f32)
                + (jnp.dot(xh, yl, preferred_element_type=f32) + jnp.dot(xl, yh, preferred_element_type=f32)))

    same16 = (row >> 4) == (col >> 4)
    same32 = (row >> 5) == (col >> 5)
    eye = jnp.where(row == col, 1.0, 0.0)
    d1 = [jnp.where(same16, a, 0.0) for a in a_mats]
    d2 = [mm(d, d) for d in d1]
    d4 = [mm(d, d) for d in d2]
    d8 = [mm(d, d) for d in d4]
    lo = [mm(eye - a, eye + b) for a, b in zip(d1, d2)]
    hi = [mm(eye + a, eye + b) for a, b in zip(d4, d8)]
    t = [mm(a, b) for a, b in zip(lo, hi)]
    for pick in (lambda a: jnp.where(same32 & jnp.logical_not(same16), a, 0.0), lambda a: jnp.where(same32, 0.0, a)):
        mid = [mm(ti, pick(a)) for ti, a in zip(t, a_mats)]
        t = [ti - mm(m, ti) for ti, m in zip(t, mid)]
    return t


def _gdn_body(qkv_ref, z_ref, ab_ref, cw_ref, alog_ref, dtb_ref, nw_ref, o_ref, halo_ref, st_ref):
    f32, bf16 = jnp.float32, jnp.bfloat16
    L = qkv_ref.shape[0]
    C, H, dh, W = GDN_CHUNK, GDN_HEADS, GDN_HEAD_DIM, GDN_WIDTH

    @pl.when(pl.program_id(1) == 0)
    def _():
        halo_ref[...] = jnp.zeros_like(halo_ref)
        st_ref[...] = jnp.zeros_like(st_ref)

    x = qkv_ref[...]
    prev = halo_ref[...]
    row8 = lax.broadcasted_iota(jnp.int32, prev.shape, 0)
    acc = cw_ref[GDN_CONV - 1:GDN_CONV, :] * x
    for s in range(1, GDN_CONV):
        xs = pltpu.roll(x, s, 0)
        head = jnp.where(row8 < s, pltpu.roll(prev, s, 0), xs[0:SUBLANES])
        xs = jnp.concatenate([head, xs[SUBLANES:]], axis=0)
        acc = acc + cw_ref[GDN_CONV - 1 - s:GDN_CONV - s, :] * xs
    halo_ref[...] = x[L - SUBLANES:L]
    y = jax.nn.silu(acc)

    a_in, b_in = ab_ref[:, 0:H], ab_ref[:, H:2 * H]
    beta_all = jax.nn.sigmoid(b_in)
    g_all = -jnp.exp(alog_ref[...]) * jax.nn.softplus(a_in + dtb_ref[...])

    row = lax.broadcasted_iota(jnp.int32, (C, GDN_PACK), 0)
    col = lax.broadcasted_iota(jnp.int32, (C, GDN_PACK), 1) & (C - 1)
    strict = row > col
    bd_r = lax.broadcasted_iota(jnp.int32, (GDN_PACK, GDN_PACK), 0)
    bd_c = lax.broadcasted_iota(jnp.int32, (GDN_PACK, GDN_PACK), 1)
    blk_diag = (bd_r >> 6) == (bd_c >> 6)

    lr = lax.broadcasted_iota(jnp.int32, (L, L), 0)
    lc = lax.broadcasted_iota(jnp.int32, (L, L), 1)
    same_chunk = (lr >> 6) == (lc >> 6)
    tri = jnp.where(same_chunk & (lr >= lc), 1.0, 0.0)
    head_of_lane = lax.broadcasted_iota(jnp.int32, (H, GDN_PACK), 1) >> 6
    spread = jnp.where(head_of_lane == lax.broadcasted_iota(jnp.int32, (H, GDN_PACK), 0), 1.0, 0.0)
    pr = lax.broadcasted_iota(jnp.int32, (L, GDN_PACK), 0) & (C - 1)
    pc = lax.broadcasted_iota(jnp.int32, (L, GDN_PACK), 1) & (C - 1)
    gc_all = _dot_exact_lhs(tri, g_all)
    gi_all = sum(jnp.dot(t, spread.astype(bf16), preferred_element_type=f32) for t in _bf16_terms(gc_all, 3))
    gj_all = _dot_exact_lhs(jnp.where(same_chunk, 1.0, 0.0), jnp.where(pr == pc, gi_all, 0.0))
    decay_all = jnp.exp(jnp.where(pr >= pc, gi_all - gj_all, -jnp.inf))
    egc_all = jnp.exp(gc_all)

    def l2n(t):
        return t * lax.rsqrt(jnp.sum(t * t, axis=-1, keepdims=True) + EPS)

    n_chunks = L // C
    chunk_rows = [slice(c * C, (c + 1) * C) for c in range(n_chunks)]
    qs, ks, vs, kbs, a_mats, qk_ps = [], [], [], [], [], []
    for rows in chunk_rows:
        kk, qk = [], []
        for h in range(H):
            q = l2n(y[rows, h * dh:(h + 1) * dh]) * dh ** -0.5
            k = l2n(y[rows, W + h * dh:W + (h + 1) * dh])
            kb = k * beta_all[rows, h:h + 1]
            qs.append(q), ks.append(k), vs.append(y[rows, 2 * W + h * dh:2 * W + (h + 1) * dh]), kbs.append(kb)
            kk.append(_dot_nt(kb.astype(bf16), k.astype(bf16)))
            qk.append(_dot_nt(q.astype(bf16), k.astype(bf16)))
        a_mats.append(jnp.where(strict, jnp.concatenate(kk, axis=1) * decay_all[rows], 0.0))
        qk_ps.append(jnp.concatenate(qk, axis=1) * decay_all[rows])
    t_ps = _gdn_packed_unit_lower_inverse(a_mats, row, col, blk_diag)
    for c, rows in enumerate(chunk_rows):
        beta, gc, egc = beta_all[rows], gc_all[rows], egc_all[rows]
        outs = []
        for h in range(H):
            i = c * H + h
            t_h = t_ps[c][:, h * C:(h + 1) * C].astype(bf16)
            e_h = egc[:, h:h + 1]
            u = jnp.dot(t_h, (vs[i] * beta[:, h:h + 1]).astype(bf16), preferred_element_type=f32)
            w = jnp.dot(t_h, (kbs[i] * e_h).astype(bf16), preferred_element_type=f32)
            state = st_ref[h]
            sb = state.astype(bf16)
            v_new = u - jnp.dot(w.astype(bf16), sb, preferred_element_type=f32)
            o = (jnp.dot((qs[i] * e_h).astype(bf16), sb, preferred_element_type=f32)
                 + jnp.dot(qk_ps[c][:, h * C:(h + 1) * C].astype(bf16), v_new.astype(bf16),
                           preferred_element_type=f32))
            g_last = gc[C - 1:C, h:h + 1]
            k_dec = (ks[i] * jnp.exp(g_last - gc[:, h:h + 1])).astype(bf16)
            st_ref[h] = state * jnp.exp(g_last) + lax.dot_general(
                k_dec, v_new.astype(bf16), (((0,), (0,)), ((), ())), preferred_element_type=f32)
            o = o * lax.rsqrt(jnp.mean(o * o, axis=-1, keepdims=True) + EPS) * nw_ref[...]
            outs.append(o * jax.nn.silu(z_ref[rows, h * dh:(h + 1) * dh]))
        o_ref[rows, :] = jnp.concatenate(outs, axis=1)


def gdn_mixer(proj, conv_w, a_log, dt_bias, norm_w):
    Bsz, S, _ = proj.shape
    f32 = jnp.float32
    L = min(GDN_ROWS, S)
    H, W, W3 = GDN_HEADS, GDN_WIDTH, 3 * GDN_WIDTH
    full = lambda *shape: pl.BlockSpec(shape, lambda bi, ci: (0,) * len(shape))
    rows = lambda n, col=0: pl.BlockSpec((None, L, n), lambda bi, ci: (bi, ci, col // n))
    return pl.pallas_call(
        _gdn_body,
        grid=(Bsz, S // L),
        in_specs=[rows(W3, PROJ_QKV), rows(W, PROJ_Z), rows(LANES, PROJ_SMALL),
                  full(GDN_CONV, W3), full(1, H), full(1, H), full(1, GDN_HEAD_DIM)],
        out_specs=rows(W),
        out_shape=jax.ShapeDtypeStruct((Bsz, S, W), f32),
        scratch_shapes=[pltpu.VMEM((SUBLANES, W3), f32), pltpu.VMEM((H, GDN_HEAD_DIM, GDN_HEAD_DIM), f32)],
        compiler_params=pltpu.CompilerParams(dimension_semantics=("arbitrary", "arbitrary"),
                                             vmem_limit_bytes=48 * 1024 * 1024),
        name="gdn_mixer",
    )(proj, proj, proj, conv_w.astype(f32), a_log.astype(f32).reshape(1, H),
      dt_bias.astype(f32).reshape(1, H), norm_w.astype(f32).reshape(1, GDN_HEAD_DIM))


def nsa_mixer(q, kv, gate_logits, q_norm, k_norm, cmp_pe, cmp_w1, cmp_w2, cos, sin):
    Bsz, S, _ = q.shape
    f32 = jnp.float32
    G, R, dh = NSA_KV_HEADS, NSA_HEADS // NSA_KV_HEADS, NSA_HEAD_DIM
    scale = dh ** -0.5
    q = rope(rms_norm(q.reshape(Bsz, S, NSA_HEADS, dh), q_norm), cos, sin)
    kc_raw, vc_raw, ks, vs, kw, vw = (t.reshape(Bsz, S, G, dh) for t in jnp.split(kv, 6, axis=-1))
    vs, vw = lax.optimization_barrier((vs, vw))
    ks = rope(rms_norm(ks, k_norm[1]), cos, sin)
    kw = rope(rms_norm(kw, k_norm[2]), cos, sin)

    assert CMP_LEN == 2 * CMP_STRIDE and S % CMP_STRIDE == 0
    n_cmp = (S - CMP_LEN) // CMP_STRIDE + 1
    win_idx = jnp.arange(n_cmp)[:, None] * CMP_STRIDE + jnp.arange(CMP_LEN)[None, :]
    cmp_end = win_idx[:, -1]

    def compress(t, pe, w1, w2):
        grp = t.reshape(Bsz, S // CMP_STRIDE, CMP_STRIDE, G, dh)
        blocks = jnp.concatenate([grp[:, :-1], grp[:, 1:]], axis=2) + pe[:, None, :]
        blocks = jnp.moveaxis(blocks, 3, 2).reshape(Bsz, n_cmp, G, CMP_LEN * dh)
        return jax.nn.gelu(blocks @ w1) @ w2

    cos_c, sin_c = rope_tables(cmp_end, dh)
    k_cmp = rope(rms_norm(compress(kc_raw, cmp_pe[0], cmp_w1[0], cmp_w2[0]), k_norm[0]), cos_c, sin_c)
    v_cmp = compress(vc_raw, cmp_pe[1], cmp_w1[1], cmp_w2[1])

    n_slc = S // SLC_BLOCK
    n_top = min(SLC_TOPK, n_slc)
    c0 = jnp.arange(n_cmp) * CMP_STRIDE
    s0 = jnp.arange(n_slc) * SLC_BLOCK
    overlap = jnp.clip(jnp.minimum(c0[:, None] + CMP_LEN, s0[None, :] + SLC_BLOCK)
                       - jnp.maximum(c0[:, None], s0[None, :]), 0, None).astype(f32) / CMP_LEN
    gates = jax.nn.sigmoid(gate_logits.astype(f32)).reshape(Bsz, S, G, R * 3).transpose(0, 2, 3, 1)

    bf16 = jnp.bfloat16
    n_cmp_pad = S // CMP_STRIDE
    pad_cmp = lambda t: jnp.pad(t.astype(bf16), ((0, 0), (0, n_cmp_pad - n_cmp), (0, 0), (0, 0)))
    keys = lambda t: t.astype(bf16).transpose(0, 2, 1, 3)
    vals = lambda t: t.astype(bf16).transpose(0, 2, 3, 1)
    q_rows = (q * scale).astype(bf16).reshape(Bsz, S, NSA_WIDTH)
    overlap_t = jnp.pad(overlap.T, ((0, 0), (0, n_cmp_pad - n_cmp))).astype(bf16)
    expand_t = ((jnp.arange(S) // SLC_BLOCK)[:, None] == jnp.arange(n_slc)[None, :]).astype(bf16)
    return _nsa_attention(q_rows, keys(pad_cmp(k_cmp)), vals(pad_cmp(v_cmp)), keys(ks), vals(vs), keys(kw), vals(vw),
                          gates, overlap_t, expand_t, n_top)


NSA_R = NSA_HEADS // NSA_KV_HEADS
SLC_TILE = 1024
MASK_NEG = -1e30


def _dot_nt(a, b):
    return lax.dot_general(a, b, (((1,), (1,)), ((), ())), preferred_element_type=jnp.float32)


def _nsa_body(q_ref, kc_ref, vct_ref, ks_ref, vst_ref, kw_ref, vwt_ref, gate_ref, ovt_ref, expt_ref, o_ref,
              *, n_top):
    f32, bf16 = jnp.float32, jnp.bfloat16
    QB, dh, R = q_ref.shape[0], ks_ref.shape[1], NSA_R
    S = ks_ref.shape[0]
    n_slc, n_cmp_pad = ovt_ref.shape
    M = R * QB
    t0 = pl.program_id(2) * QB
    q_blk = q_ref[...]
    q = jnp.concatenate([q_blk[:, r * dh:(r + 1) * dh] for r in range(R)], axis=0)
    per_head = lambda t: jnp.concatenate([t] * R, axis=1)

    band = S if S < WINDOW + QB else WINDOW + QB
    start = pl.multiple_of(jnp.maximum(t0 + QB - band, 0), QB)
    s = _dot_nt(kw_ref[pl.ds(start, band), :], q)
    diff = (t0 - start) + (lax.broadcasted_iota(jnp.int32, (band, QB), 1) - lax.broadcasted_iota(jnp.int32, (band, QB), 0))
    s = s + per_head(jnp.where((diff >= 0) & (diff < WINDOW), 0.0, -jnp.inf))
    p = jnp.exp(s - jnp.max(s, axis=0, keepdims=True))
    o_win = jnp.dot(vwt_ref[:, pl.ds(start, band)], p.astype(bf16), preferred_element_type=f32)
    o_win = o_win / jnp.sum(p, axis=0, keepdims=True)

    s = _dot_nt(kc_ref[...], q)
    n_idx = lax.broadcasted_iota(jnp.int32, (n_cmp_pad, QB), 0)
    t_idx = t0 + lax.broadcasted_iota(jnp.int32, (n_cmp_pad, QB), 1)
    valid = (n_idx * CMP_STRIDE + (CMP_LEN - 1) <= t_idx) & (n_idx < n_cmp_pad - 1)
    s = s + per_head(jnp.where(valid, 0.0, -jnp.inf))
    m = jnp.max(s, axis=0, keepdims=True)
    m = jnp.where(jnp.isfinite(m), m, 0.0)
    p = jnp.exp(s - m)
    p = p / jnp.maximum(jnp.sum(p, axis=0, keepdims=True), jnp.finfo(f32).tiny)
    o_cmp = jnp.dot(vct_ref[...], p.astype(bf16), preferred_element_type=f32)

    p_sum = p[:, 0:QB]
    for r in range(1, R):
        p_sum = p_sum + p[:, r * QB:(r + 1) * QB]
    imp = sum(jnp.dot(ovt_ref[...], t, preferred_element_type=f32) for t in _bf16_terms(p_sum, 3))
    j_idx = lax.broadcasted_iota(jnp.int32, (n_slc, QB), 0)
    t_lane = t0 + lax.broadcasted_iota(jnp.int32, (n_slc, QB), 1)
    cur = lax.shift_right_logical(t_lane, 6)
    forced = (j_idx == 0) | (j_idx == cur) | (j_idx == cur - 1)
    started = j_idx * SLC_BLOCK <= t_lane
    imp = jnp.where(forced, BIG, jnp.where(started, imp, -BIG))
    rank = jnp.zeros((n_slc, QB), f32)
    for jp in range(n_slc):
        row = imp[jp:jp + 1, :]
        ahead = (row > imp) | ((row == imp) & (j_idx > jp))
        rank = rank + jnp.where(ahead, 1.0, 0.0)
    sel = jnp.where(rank < n_top, 1.0, 0.0).astype(bf16)
    k_rel = lax.broadcasted_iota(jnp.int32, (SLC_TILE, QB), 0)
    t_q = t0 + lax.broadcasted_iota(jnp.int32, (SLC_TILE, QB), 1)

    def slc_step(kt, carry):
        m, l, acc = carry
        k0 = pl.multiple_of(kt * SLC_TILE, SLC_TILE)
        s = _dot_nt(ks_ref[pl.ds(k0, SLC_TILE), :], q)
        sel_keys = jnp.dot(expt_ref[pl.ds(k0, SLC_TILE), :], sel, preferred_element_type=f32)
        bias = jnp.where((sel_keys > 0.5) & (k_rel + k0 <= t_q), 0.0, MASK_NEG)
        s = s + per_head(bias)
        m_new = jnp.maximum(m, jnp.max(s, axis=0, keepdims=True))
        alpha = jnp.exp(m - m_new)
        p = jnp.exp(s - m_new)
        l = alpha * l + jnp.sum(p, axis=0, keepdims=True)
        acc = alpha * acc + jnp.dot(vst_ref[:, pl.ds(k0, SLC_TILE)], p.astype(bf16), preferred_element_type=f32)
        return m_new, l, acc

    n_tiles = (t0 + QB - 1) // SLC_TILE + 1
    init = (jnp.full((1, M), MASK_NEG, f32), jnp.zeros((1, M), f32), jnp.zeros((dh, M), f32))
    _, l, acc = lax.fori_loop(0, n_tiles, slc_step, init)
    o_slc = acc / l

    gate = gate_ref[...]
    g_cmp, g_slc, g_win = (jnp.concatenate([gate[3 * r + c:3 * r + c + 1, :] for r in range(R)], axis=1)
                           for c in range(3))
    o_t = g_cmp * o_cmp + g_slc * o_slc + g_win * o_win
    o_ref[...] = jnp.concatenate([o_t[:, r * QB:(r + 1) * QB].T for r in range(R)], axis=1)


def _nsa_attention(q, kc, vct, ks, vst, kw, vwt, gates, overlap_t, expand_t, n_top):
    Bsz, G, S, dh = ks.shape
    R = NSA_R
    n_cmp_pad = kc.shape[2]
    assert S % SLC_TILE == 0 and SLC_TILE % Q_BLOCK == 0, (S, SLC_TILE, Q_BLOCK)
    keys = lambda n: pl.BlockSpec((None, None, n, dh), lambda b, g, i: (b, g, 0, 0))
    vals = lambda n: pl.BlockSpec((None, None, dh, n), lambda b, g, i: (b, g, 0, 0))
    const = lambda a: pl.BlockSpec(a.shape, lambda b, g, i: (0, 0))
    return pl.pallas_call(
        functools.partial(_nsa_body, n_top=n_top),
        grid=(Bsz, G, S // Q_BLOCK),
        in_specs=[pl.BlockSpec((None, Q_BLOCK, R * dh), lambda b, g, i: (b, i, g)),
                  keys(n_cmp_pad), vals(n_cmp_pad), keys(S), vals(S), keys(S), vals(S),
                  pl.BlockSpec((None, None, 3 * R, Q_BLOCK), lambda b, g, i: (b, g, 0, i)),
                  const(overlap_t), const(expand_t)],
        out_specs=pl.BlockSpec((None, Q_BLOCK, R * dh), lambda b, g, i: (b, i, g)),
        out_shape=jax.ShapeDtypeStruct((Bsz, S, G * R * dh), jnp.float32),
        compiler_params=pltpu.CompilerParams(dimension_semantics=("arbitrary", "arbitrary", "arbitrary"),
                                             vmem_limit_bytes=48 * 1024 * 1024),
        name="nsa_attention",
    )(q, kc, vct, ks, vst, kw, vwt, gates, overlap_t, expand_t)


def hybrid_mixer(x, ln_mix, w_in, lam_re, lam_im, b_re, b_im, c_re, c_im, s5_d, s5_log_dt, glu_w, glu_b,
                 s5_out_norm, conv_w, a_log, dt_bias, gdn_norm_w, q_norm, k_norm, cmp_pe, cmp_w1,
                 cmp_w2, cos, sin):
    Bsz, S, D = x.shape
    proj = _in_proj(x.reshape(-1, D), ln_mix, _in_proj_weight(w_in)).reshape(Bsz, S, PROJ_W)
    y_s5 = s5_mixer_normed(proj, lam_re, lam_im, b_re, b_im, c_re, c_im, s5_d, s5_log_dt, glu_w, glu_b, s5_out_norm)
    y_gdn = gdn_mixer(proj, conv_w, a_log, dt_bias, gdn_norm_w)
    y_nsa = nsa_mixer(proj[..., PROJ_Q:PROJ_Q + NSA_WIDTH], proj[..., PROJ_KV:PROJ_KV + 6 * NSA_KV_WIDTH],
                      proj[..., PROJ_GATES:PROJ_GATES + 3 * NSA_HEADS], q_norm, k_norm, cmp_pe, cmp_w1, cmp_w2,
                      cos, sin)
    return y_s5, y_gdn, y_nsa


def hier_moe(x, h, logits, layer, w_gate, w_up, w_down):
    T, D = x.shape
    f32 = jnp.float32
    g_logits = logits[:, :N_GROUPS]
    grp = jnp.argmax(g_logits, axis=-1)
    p_grp = jnp.take_along_axis(jax.nn.softmax(g_logits, axis=-1), grp[:, None], axis=-1)
    e_logits = logits[:, N_GROUPS:N_ROUTER].reshape(T, N_GROUPS, EXPERTS_PER_GROUP)
    e_logits = jnp.take_along_axis(e_logits, grp[:, None, None], axis=1)[:, 0]
    local = jnp.arange(EXPERTS_PER_GROUP, dtype=jnp.int32)[None, :]
    rest, top_logit, top_local = e_logits, [], []
    for _ in range(TOP_K):
        idx = jnp.argmax(rest, axis=-1).astype(jnp.int32)
        top_logit.append(jnp.take_along_axis(rest, idx[:, None], axis=-1)[:, 0])
        top_local.append(idx)
        rest = jnp.where(local == idx[:, None], -jnp.inf, rest)
    top_logit, top_local = jnp.stack(top_logit, axis=-1), jnp.stack(top_local, axis=-1)
    weights = jax.nn.softmax(top_logit, axis=-1) * p_grp
    experts = grp[:, None].astype(jnp.int32) * EXPERTS_PER_GROUP + top_local
    n_assign = T * TOP_K
    e_flat = experts.reshape(-1)
    tok_flat = jnp.repeat(jnp.arange(T, dtype=jnp.int32), TOP_K)
    onehot = (e_flat[:, None] == jnp.arange(N_EXPERTS, dtype=jnp.int32)[None, :]).astype(f32)
    blocks = onehot.reshape(n_assign // COUNT_BLOCK, COUNT_BLOCK, N_EXPERTS)
    tri = jnp.tril(jnp.ones((COUNT_BLOCK, COUNT_BLOCK), f32))
    inside = jnp.einsum('ij,bje->bie', tri, blocks)
    before = jnp.cumsum(inside[:, -1, :], axis=0) - inside[:, -1, :]
    running = (inside + before[:, None, :]).reshape(n_assign, N_EXPERTS).astype(jnp.int32)
    counts = running[-1]
    padded = (counts + MOE_ROWS - 1) // MOE_ROWS * MOE_ROWS
    pad_start = jnp.cumsum(padded) - padded
    dest = (pad_start[e_flat] + jnp.take_along_axis(running, e_flat[:, None], axis=1)[:, 0] - 1).astype(jnp.int32)
    n_blk = (n_assign + MOE_ROWS - 1) // MOE_ROWS + N_EXPERTS
    cap = n_blk * MOE_ROWS
    slot_tok = jnp.zeros((cap,), jnp.int32).at[dest].set(tok_flat)
    seg_end = jnp.cumsum(padded)
    blk_start = jnp.arange(n_blk, dtype=seg_end.dtype) * MOE_ROWS
    blk_expert = jnp.minimum(jnp.sum((seg_end[None, :] <= blk_start[:, None]).astype(jnp.int32), axis=1),
                             N_EXPERTS - 1).astype(jnp.int32)
    n_used = (seg_end[-1] // MOE_ROWS).astype(jnp.int32).reshape(1)
    y = _moe_ffn(blk_expert, slot_tok, n_used, x, layer, w_gate.astype(f32), w_up.astype(f32),
                 w_down.astype(f32))
    return _moe_combine(dest, h, weights, y)


CMB_ROWS = 128


def _moe_combine_body(d_ref, h_ref, w_ref, y_hbm, o_ref, buf, sem):
    i = pl.program_id(0)
    n = pl.num_programs(0)
    rows = h_ref.shape[0]
    slot = i & 1

    def row_copy(step, r, k, to):
        src = y_hbm.at[pl.ds(d_ref[(step * rows + r) * TOP_K + k], 1), :]
        return pltpu.make_async_copy(src, buf.at[to, k, pl.ds(r, 1), :], sem.at[to])

    def start_step(step, to):
        for r in range(rows):
            for k in range(TOP_K):
                row_copy(step, r, k, to).start(priority=(r * TOP_K + k) % 2)

    def finish_step():
        for r in range(rows):
            for k in range(TOP_K):
                row_copy(i, r, k, slot).wait()
        out = h_ref[...]
        for k in range(TOP_K):
            out = out + w_ref[:, k:k + 1] * buf[slot, k]
        o_ref[...] = out

    @pl.when(i == 0)
    def _():
        start_step(0, 0)

    @pl.when(i + 1 < n)
    def _():
        start_step(i + 1, 1 - slot)
        finish_step()

    @pl.when(i + 1 == n)
    def _():
        finish_step()


def _moe_combine(dest, h, weights, y):
    T, D = h.shape
    rows = min(CMB_ROWS, T)
    return pl.pallas_call(
        _moe_combine_body,
        grid_spec=pltpu.PrefetchScalarGridSpec(
            num_scalar_prefetch=1, grid=(T // rows,),
            in_specs=[pl.BlockSpec((rows, D), lambda i, d: (i, 0)),
                      pl.BlockSpec((rows, TOP_K), lambda i, d: (i, 0)),
                      pl.BlockSpec(memory_space=pl.ANY)],
            out_specs=pl.BlockSpec((rows, D), lambda i, d: (i, 0)),
            scratch_shapes=[pltpu.VMEM((2, TOP_K, rows, D), jnp.float32), pltpu.SemaphoreType.DMA((2,))]),
        out_shape=jax.ShapeDtypeStruct((T, D), jnp.float32),
        compiler_params=pltpu.CompilerParams(dimension_semantics=("arbitrary",),
                                             vmem_limit_bytes=32 * 1024 * 1024),
        name="moe_combine",
    )(dest, h, weights, y)


MOE_ROWS = 256
COUNT_BLOCK = 128


def _moe_ffn_body(be_ref, tok_ref, nu_ref, x_hbm, wg_ref, wu_ref, wd_ref, o_ref, xbuf, sem, wg_bf, wu_bf, wd_bf):
    i = pl.program_id(0)
    n = nu_ref[0]
    slot = i & 1

    def row_copy(step, r, to):
        src = x_hbm.at[pl.ds(tok_ref[step * MOE_ROWS + r], 1), :]
        return pltpu.make_async_copy(src, xbuf.at[to, pl.ds(r, 1), :], sem.at[to])

    def start_step(step, to):
        for r in range(MOE_ROWS):
            row_copy(step, r, to).start(priority=r % 2)

    def finish_step():
        for r in range(MOE_ROWS):
            row_copy(i, r, slot).wait()
        x = xbuf[slot].astype(jnp.bfloat16)
        gate = jnp.dot(x, wg_bf[...], preferred_element_type=jnp.float32)
        up = jnp.dot(x, wu_bf[...], preferred_element_type=jnp.float32)
        hid = (jax.nn.silu(gate) * up).astype(jnp.bfloat16)
        o_ref[...] = jnp.dot(hid, wd_bf[...], preferred_element_type=jnp.float32)

    @pl.when(i == 0)
    def _():
        start_step(0, 0)

    @pl.when((i < n) & ((i == 0) | (be_ref[i] != be_ref[jnp.maximum(i - 1, 0)])))
    def _():
        wg_bf[...] = wg_ref[...].astype(jnp.bfloat16)
        wu_bf[...] = wu_ref[...].astype(jnp.bfloat16)
        wd_bf[...] = wd_ref[...].astype(jnp.bfloat16)

    @pl.when(i + 1 < n)
    def _():
        start_step(i + 1, 1 - slot)
        finish_step()

    @pl.when(i + 1 == n)
    def _():
        finish_step()

    @pl.when(i >= n)
    def _():
        o_ref[...] = jnp.zeros_like(o_ref)


def _moe_ffn(blk_expert, slot_tok, n_used, x, layer, w_gate, w_up, w_down):
    cap = slot_tok.shape[0]
    D = x.shape[1]
    n_blk = cap // MOE_ROWS
    De = w_gate.shape[-1]
    return pl.pallas_call(
        _moe_ffn_body,
        grid_spec=pltpu.PrefetchScalarGridSpec(
            num_scalar_prefetch=3, grid=(n_blk,),
            in_specs=[pl.BlockSpec(memory_space=pl.ANY),
                      pl.BlockSpec((None, None, D, De), lambda i, be, tok, nu: (layer, be[i], 0, 0)),
                      pl.BlockSpec((None, None, D, De), lambda i, be, tok, nu: (layer, be[i], 0, 0)),
                      pl.BlockSpec((None, None, De, D), lambda i, be, tok, nu: (layer, be[i], 0, 0))],
            out_specs=pl.BlockSpec((MOE_ROWS, D), lambda i, be, tok, nu: (i, 0)),
            scratch_shapes=[pltpu.VMEM((2, MOE_ROWS, D), jnp.float32), pltpu.SemaphoreType.DMA((2,)),
                            pltpu.VMEM((D, De), jnp.bfloat16), pltpu.VMEM((D, De), jnp.bfloat16),
                            pltpu.VMEM((De, D), jnp.bfloat16)]),
        out_shape=jax.ShapeDtypeStruct((cap, D), jnp.float32),
        compiler_params=pltpu.CompilerParams(dimension_semantics=("arbitrary",),
                                             vmem_limit_bytes=56 * 1024 * 1024),
        name="moe_ffn",
    )(blk_expert, slot_tok, n_used, x, w_gate, w_up, w_down)


def kernel(x, ln_mix, w_in, s5_lambda_re, s5_lambda_im, s5_b_re, s5_b_im, s5_c_re, s5_c_im, s5_d,
           s5_log_dt, s5_glu_w, s5_glu_b, s5_out_norm, gdn_conv_w, gdn_a_log, gdn_dt_bias, gdn_norm_w,
           nsa_q_norm, nsa_k_norm, nsa_cmp_pe, nsa_cmp_w1, nsa_cmp_w2, nsa_out_norm, w_out, ln_ffn,
           moe_group_w, moe_group_b, moe_expert_w, moe_expert_b, moe_w_gate, moe_w_up, moe_w_down):
    Bsz, S, D = x.shape
    cos, sin = rope_tables(jnp.arange(S), NSA_HEAD_DIM)
    for l in range(DEPTH):
        y_s5, y_gdn, y_nsa = hybrid_mixer(
            x, ln_mix[l], w_in[l], s5_lambda_re[l], s5_lambda_im[l], s5_b_re[l], s5_b_im[l], s5_c_re[l], s5_c_im[l],
            s5_d[l], s5_log_dt[l], s5_glu_w[l], s5_glu_b[l], s5_out_norm[l], gdn_conv_w[l], gdn_a_log[l],
            gdn_dt_bias[l], gdn_norm_w[l], nsa_q_norm[l], nsa_k_norm[l], nsa_cmp_pe[l], nsa_cmp_w1[l], nsa_cmp_w2[l],
            cos, sin)
        router_w = jnp.concatenate([moe_group_w[l], moe_expert_w[l]], axis=-1)
        router_b = jnp.concatenate([moe_group_b[l], moe_expert_b[l]], axis=-1)
        h, a, logits = _out_proj(x.reshape(-1, D), y_s5.reshape(-1, S5_WIDTH), y_gdn.reshape(-1, GDN_WIDTH),
                                 y_nsa.reshape(-1, NSA_WIDTH), nsa_out_norm[l], w_out[l], ln_ffn[l], router_w, router_b)
        x = hier_moe(a, h, logits, l, moe_w_gate, moe_w_up, moe_w_down).reshape(Bsz, S, D)
    return x
```

```python
import functools
import jax, jax.numpy as jnp
from jax import lax
import numpy as np
from jax.experimental import pallas as pl
from jax.experimental.pallas import tpu as pltpu

D_MODEL = 2048
BATCH = 4
SEQ = 4096
DEPTH = 2

EPS = 1e-6
S5_WIDTH = D_MODEL // 4
S5_GROUP = 16
S5_GROUPS = S5_WIDTH // S5_GROUP
S5_STATE = 64
GDN_HEAD_DIM = 128
GDN_WIDTH = D_MODEL // 4
GDN_HEADS = GDN_WIDTH // GDN_HEAD_DIM
GDN_CONV = 4
GDN_CHUNK = 64
NSA_HEAD_DIM = 64
NSA_WIDTH = D_MODEL // 2
NSA_HEADS = NSA_WIDTH // NSA_HEAD_DIM
NSA_KV_HEADS = 4
NSA_KV_WIDTH = NSA_KV_HEADS * NSA_HEAD_DIM
CMP_LEN = 32
CMP_STRIDE = 16
CMP_HIDDEN = 256
SLC_BLOCK = 64
SLC_TOPK = 16
WINDOW = 512
Q_BLOCK = 128
ROPE_THETA = 10000.0
BIG = 1e9
N_GROUPS = 4
EXPERTS_PER_GROUP = 8
N_EXPERTS = N_GROUPS * EXPERTS_PER_GROUP
TOP_K = 2
D_EXPERT = 512
IN_SIZES = (S5_WIDTH, 3 * GDN_WIDTH, GDN_WIDTH, GDN_HEADS, GDN_HEADS, NSA_WIDTH, 6 * NSA_KV_WIDTH, 3 * NSA_HEADS)
D_IN = sum(IN_SIZES)


PROJ_QKV, PROJ_KV, PROJ_Q, PROJ_U, PROJ_Z, PROJ_SMALL = 0, 1536, 3072, 4096, 4608, 5120
PROJ_SMALL_W = 512
PROJ_W = PROJ_SMALL + PROJ_SMALL_W
PROJ_GATES = PROJ_SMALL + 2 * GDN_HEADS
LANES = 128
IN_ROWS, IN_COLS = 1024, 512
OUT_ROWS = 256
N_ROUTER = N_GROUPS + N_EXPERTS


def _in_proj_weight(w_in):
    u, qkv, z, ga, gb, q, kv, gl = jnp.split(w_in, np.cumsum(IN_SIZES)[:-1].tolist(), axis=-1)
    small = jnp.concatenate([ga, gb, gl], axis=-1)
    small = jnp.pad(small, ((0, 0), (0, PROJ_SMALL_W - small.shape[-1])))
    return jnp.concatenate([qkv, kv, q, u, z, small], axis=-1)


def _in_proj_body(x_ref, g_ref, w_ref, o_ref, a_ref):
    @pl.when(pl.program_id(1) == 0)
    def _():
        x = x_ref[...]
        a_ref[...] = (x * lax.rsqrt(jnp.mean(x * x, axis=-1, keepdims=True) + EPS) * g_ref[...]).astype(jnp.bfloat16)

    o_ref[...] = jnp.dot(a_ref[...], w_ref[...].astype(jnp.bfloat16), preferred_element_type=jnp.float32)


def _in_proj(x, gain, w):
    T, D = x.shape
    N = w.shape[1]
    tm = min(IN_ROWS, T)
    return pl.pallas_call(
        _in_proj_body,
        grid=(T // tm, N // IN_COLS),
        in_specs=[pl.BlockSpec((tm, D), lambda i, j: (i, 0)),
                  pl.BlockSpec((1, D), lambda i, j: (0, 0)),
                  pl.BlockSpec((D, IN_COLS), lambda i, j: (0, j))],
        out_specs=pl.BlockSpec((tm, IN_COLS), lambda i, j: (i, j)),
        out_shape=jax.ShapeDtypeStruct((T, N), jnp.float32),
        scratch_shapes=[pltpu.VMEM((tm, D), jnp.bfloat16)],
        compiler_params=pltpu.CompilerParams(dimension_semantics=("arbitrary", "arbitrary"),
                                             vmem_limit_bytes=48 * 1024 * 1024),
        name="in_proj",
    )(x, gain.astype(jnp.float32).reshape(1, D), w)


def _out_proj_body(x_ref, s5_ref, gdn_ref, nsa_ref, ng_ref, w_ref, fg_ref, rw_ref, rb_ref, h_ref, a_ref, lg_ref):
    f32, bf16 = jnp.float32, jnp.bfloat16
    nsa = nsa_ref[...]
    nsa = nsa * lax.rsqrt(jnp.mean(nsa * nsa, axis=-1, keepdims=True) + EPS) * ng_ref[...]
    k0, k1 = S5_WIDTH, S5_WIDTH + GDN_WIDTH
    y = (jnp.dot(s5_ref[...].astype(bf16), w_ref[0:k0, :], preferred_element_type=f32)
         + jnp.dot(gdn_ref[...].astype(bf16), w_ref[k0:k1, :], preferred_element_type=f32)
         + jnp.dot(nsa.astype(bf16), w_ref[k1:, :], preferred_element_type=f32))
    h = x_ref[...] + y
    h_ref[...] = h
    a = h * lax.rsqrt(jnp.mean(h * h, axis=-1, keepdims=True) + EPS) * fg_ref[...]
    a_ref[...] = a
    lg_ref[...] = jnp.dot(a.astype(bf16), rw_ref[...], preferred_element_type=f32) + rb_ref[...]


def _out_proj(x, y_s5, y_gdn, y_nsa, nsa_gain, w_out, ffn_gain, router_w, router_b):
    T, D = x.shape
    tm = min(OUT_ROWS, T)
    f32, bf16 = jnp.float32, jnp.bfloat16
    rw = jnp.pad(router_w.astype(bf16), ((0, 0), (0, LANES - N_ROUTER)))
    rb = jnp.pad(router_b.astype(f32), (0, LANES - N_ROUTER)).reshape(1, LANES)
    rows = lambda n: pl.BlockSpec((tm, n), lambda i: (i, 0))
    full = lambda *shape: pl.BlockSpec(shape, lambda i: (0,) * len(shape))
    return pl.pallas_call(
        _out_proj_body,
        grid=(T // tm,),
        in_specs=[rows(D), rows(S5_WIDTH), rows(GDN_WIDTH), rows(NSA_WIDTH), full(1, NSA_WIDTH), full(D, D),
                  full(1, D), full(D, LANES), full(1, LANES)],
        out_specs=[rows(D), rows(D), rows(LANES)],
        out_shape=[jax.ShapeDtypeStruct((T, D), f32), jax.ShapeDtypeStruct((T, D), f32),
                   jax.ShapeDtypeStruct((T, LANES), f32)],
        compiler_params=pltpu.CompilerParams(dimension_semantics=("arbitrary",),
                                             vmem_limit_bytes=48 * 1024 * 1024),
        name="out_proj",
    )(x, y_s5, y_gdn, y_nsa, nsa_gain.astype(f32).reshape(1, NSA_WIDTH), w_out.astype(bf16),
      ffn_gain.astype(f32).reshape(1, D), rw, rb)


def rms_norm(x, gain):
    xf = x.astype(jnp.float32)
    y = xf * lax.rsqrt(jnp.mean(xf * xf, axis=-1, keepdims=True) + EPS)
    return (y * gain.astype(jnp.float32)).astype(x.dtype)


def rope_tables(pos, dim):
    inv_freq = ROPE_THETA ** (-jnp.arange(0, dim, 2, dtype=jnp.float32) / dim)
    ang = pos.astype(jnp.float32)[:, None] * inv_freq[None, :]
    return jnp.cos(ang)[:, None, :], jnp.sin(ang)[:, None, :]


def rope(x, cos, sin):
    x1, x2 = jnp.split(x, 2, axis=-1)
    return jnp.concatenate([x1 * cos - x2 * sin, x2 * cos + x1 * sin], axis=-1).astype(x.dtype)


S5_OCT = 8
S5_NOCT = S5_GROUPS // S5_OCT
S5_OCT_IN = S5_OCT * S5_GROUP
S5_OCT_ST = S5_OCT * S5_STATE
SUBLANES = 8
S5_SCAN_SHIFTS = (1, 2, 4)


def _s5_constants(lam_re, lam_im, b_re, b_im, c_re, c_im, log_dt):
    f32 = jnp.float32
    lr, li = lam_re.astype(f32), lam_im.astype(f32)
    dt = jnp.exp(log_dt.astype(f32))[:, None]

    def lam_pow(k):
        mag = jnp.exp(k * lr * dt)
        return mag * jnp.cos(k * li * dt), mag * jnp.sin(k * li * dt)

    ar, ai = lam_pow(1.0)
    nr, ni = ar - 1.0, ai
    den = lr * lr + li * li
    qr, qi = (nr * lr + ni * li) / den, (ni * lr - nr * li) / den
    br, bi = b_re.astype(f32), b_im.astype(f32)
    bbr = qr[:, :, None] * br - qi[:, :, None] * bi
    bbi = qr[:, :, None] * bi + qi[:, :, None] * br
    eye = jnp.eye(S5_OCT, dtype=f32)

    def blk_b(t):
        t = t.reshape(S5_NOCT, S5_OCT, S5_STATE, S5_GROUP)
        return jnp.einsum('qgph,gk->qghkp', t, eye).reshape(S5_NOCT, S5_OCT_IN, S5_OCT_ST)

    def blk_c(t):
        t = t.reshape(S5_NOCT, S5_OCT, S5_GROUP, S5_STATE)
        return jnp.einsum('qghp,gk->qgpkh', t, eye).reshape(S5_NOCT, S5_OCT_ST, S5_OCT_IN)

    b_mat = jnp.stack([blk_b(bbr), blk_b(bbi)]).astype(jnp.bfloat16)
    c_mat = jnp.stack([blk_c(c_re.astype(f32)), blk_c(-c_im.astype(f32))]).astype(jnp.bfloat16)
    rows = jnp.arange(SUBLANES, dtype=f32)[:, None, None]
    tiles = []
    for k in S5_SCAN_SHIFTS:
        pr, pi = lam_pow(jnp.full_like(rows, float(k)))
        keep = rows >= k
        tiles.append(jnp.stack([jnp.where(keep, pr, 0.0), jnp.where(keep, pi, 0.0)]))
    pr, pi = lam_pow(rows + 1.0)
    tiles.append(jnp.stack([pr, pi]))
    a_mat = jnp.stack(tiles).reshape(len(tiles), 2, SUBLANES, S5_GROUPS * S5_STATE)
    return b_mat, c_mat, a_mat


def _s5_body(u_ref, b_ref, c_ref, a_ref, d_ref, gw_ref, gb_ref, gain_ref, o_ref, st_ref, xr_ref, xi_ref):
    bf16 = jnp.bfloat16
    n_tiles = u_ref.shape[0] // SUBLANES

    @pl.when(pl.program_id(1) == 0)
    def _():
        st_ref[...] = jnp.zeros_like(st_ref)

    u = u_ref[...]
    ys = []
    for q in range(S5_NOCT):
        lanes = slice(q * S5_OCT_ST, (q + 1) * S5_OCT_ST)
        uq = u[:, q * S5_OCT_IN:(q + 1) * S5_OCT_IN].astype(bf16)
        xr_ref[...] = jnp.dot(uq, b_ref[0, q], preferred_element_type=jnp.float32)
        xi_ref[...] = jnp.dot(uq, b_ref[1, q], preferred_element_type=jnp.float32)

        def tile_step(i, carry, lanes=lanes):
            cr, ci = carry
            r0 = pl.multiple_of(i * SUBLANES, SUBLANES)
            xr = xr_ref[pl.ds(r0, SUBLANES), :]
            xi = xi_ref[pl.ds(r0, SUBLANES), :]
            for lvl, k in enumerate(S5_SCAN_SHIFTS):
                pr, pi = a_ref[lvl, 0, :, lanes], a_ref[lvl, 1, :, lanes]
                sr, si = pltpu.roll(xr, k, 0), pltpu.roll(xi, k, 0)
                xr, xi = xr + (pr * sr - pi * si), xi + (pr * si + pi * sr)
            pr, pi = a_ref[len(S5_SCAN_SHIFTS), 0, :, lanes], a_ref[len(S5_SCAN_SHIFTS), 1, :, lanes]
            cr = jnp.broadcast_to(cr, xr.shape)
            ci = jnp.broadcast_to(ci, xi.shape)
            xr, xi = xr + (pr * cr - pi * ci), xi + (pr * ci + pi * cr)
            xr_ref[pl.ds(r0, SUBLANES), :] = xr
            xi_ref[pl.ds(r0, SUBLANES), :] = xi
            return xr[SUBLANES - 1:, :], xi[SUBLANES - 1:, :]

        cr, ci = lax.fori_loop(0, n_tiles, tile_step, (st_ref[0, :, lanes], st_ref[1, :, lanes]))
        st_ref[0, :, lanes] = cr
        st_ref[1, :, lanes] = ci
        ys.append(jnp.dot(xr_ref[...].astype(bf16), c_ref[0, q], preferred_element_type=jnp.float32)
                  + jnp.dot(xi_ref[...].astype(bf16), c_ref[1, q], preferred_element_type=jnp.float32))
    y = jnp.concatenate(ys, axis=1) + d_ref[...] * u
    y = jax.nn.gelu(y)
    z = jnp.dot(y.astype(bf16), gw_ref[...], preferred_element_type=jnp.float32) + gb_ref[...]
    y = y * jax.nn.sigmoid(z)
    y = y * lax.rsqrt(jnp.mean(y * y, axis=-1, keepdims=True) + EPS)
    o_ref[...] = y * gain_ref[...]


def s5_mixer_normed(proj, lam_re, lam_im, b_re, b_im, c_re, c_im, d_skip, log_dt, glu_w, glu_b, out_norm, chunk=512):
    Bsz, S, _ = proj.shape
    W = S5_WIDTH
    f32 = jnp.float32
    L = min(chunk, S)
    b_mat, c_mat, a_mat = _s5_constants(lam_re, lam_im, b_re, b_im, c_re, c_im, log_dt)
    n_state = S5_GROUPS * S5_STATE
    full = lambda *shape: pl.BlockSpec(shape, lambda b, c: (0,) * len(shape))
    return pl.pallas_call(
        _s5_body,
        grid=(Bsz, S // L),
        in_specs=[pl.BlockSpec((None, L, W), lambda b, c: (b, c, PROJ_U // W)),
                  full(*b_mat.shape), full(*c_mat.shape), full(*a_mat.shape),
                  full(1, W), full(W, W), full(1, W), full(1, W)],
        out_specs=pl.BlockSpec((None, L, W), lambda b, c: (b, c, 0)),
        out_shape=jax.ShapeDtypeStruct((Bsz, S, W), f32),
        scratch_shapes=[pltpu.VMEM((2, 1, n_state), f32),
                        pltpu.VMEM((L, S5_OCT_ST), f32), pltpu.VMEM((L, S5_OCT_ST), f32)],
        compiler_params=pltpu.CompilerParams(dimension_semantics=("arbitrary", "arbitrary"),
                                             vmem_limit_bytes=48 * 1024 * 1024),
        name="s5_mixer",
    )(proj, b_mat, c_mat, a_mat, d_skip.astype(f32).reshape(1, W), glu_w.astype(jnp.bfloat16),
      glu_b.astype(f32).reshape(1, W), out_norm.astype(f32).reshape(1, W))


GDN_ROWS = 256
GDN_PACK = GDN_HEADS * GDN_CHUNK


def _bf16_terms(x, n):
    terms, rem = [], x
    for i in range(n):
        t = rem.astype(jnp.bfloat16)
        terms.append(t)
        if i + 1 < n:
            rem = rem - t.astype(jnp.float32)
    return terms


def _dot_exact_lhs(lhs01, x):
    lhs = lhs01.astype(jnp.bfloat16)
    return sum(jnp.dot(lhs, t, preferred_element_type=jnp.float32) for t in _bf16_terms(x, 3))


def _gdn_packed_unit_lower_inverse(a_mats, row, col, blk_diag):
    f32 = jnp.float32

    def mm(x, y):
        xh, xl = _bf16_terms(x, 2)
        yh, yl = (jnp.where(blk_diag, jnp.concatenate([t] * GDN_HEADS, axis=0), 0.0) for t in _bf16_terms(y, 2))
        return (jnp.dot(xh, yh, preferred_element_type=f32)
                + (jnp.dot(xh, yl, preferred_element_type=f32) + jnp.dot(xl, yh, preferred_element_type=f32)))

    same16 = (row >> 4) == (col >> 4)
    same32 = (row >> 5) == (col >> 5)
    eye = jnp.where(row == col, 1.0, 0.0)
    d1 = [jnp.where(same16, a, 0.0) for a in a_mats]
    d2 = [mm(d, d) for d in d1]
    d4 = [mm(d, d) for d in d2]
    d8 = [mm(d, d) for d in d4]
    lo = [mm(eye - a, eye + b) for a, b in zip(d1, d2)]
    hi = [mm(eye + a, eye + b) for a, b in zip(d4, d8)]
    t = [mm(a, b) for a, b in zip(lo, hi)]
    for pick in (lambda a: jnp.where(same32 & jnp.logical_not(same16), a, 0.0), lambda a: jnp.where(same32, 0.0, a)):
        mid = [mm(ti, pick(a)) for ti, a in zip(t, a_mats)]
        t = [ti - mm(m, ti) for ti, m in zip(t, mid)]
    return t


def _gdn_body(qkv_ref, z_ref, ab_ref, cw_ref, alog_ref, dtb_ref, nw_ref, o_ref, halo_ref, st_ref):
    f32, bf16 = jnp.float32, jnp.bfloat16
    L = qkv_ref.shape[0]
    C, H, dh, W = GDN_CHUNK, GDN_HEADS, GDN_HEAD_DIM, GDN_WIDTH

    @pl.when(pl.program_id(1) == 0)
    def _():
        halo_ref[...] = jnp.zeros_like(halo_ref)
        st_ref[...] = jnp.zeros_like(st_ref)

    x = qkv_ref[...]
    prev = halo_ref[...]
    row8 = lax.broadcasted_iota(jnp.int32, prev.shape, 0)
    acc = cw_ref[GDN_CONV - 1:GDN_CONV, :] * x
    for s in range(1, GDN_CONV):
        xs = pltpu.roll(x, s, 0)
        head = jnp.where(row8 < s, pltpu.roll(prev, s, 0), xs[0:SUBLANES])
        xs = jnp.concatenate([head, xs[SUBLANES:]], axis=0)
        acc = acc + cw_ref[GDN_CONV - 1 - s:GDN_CONV - s, :] * xs
    halo_ref[...] = x[L - SUBLANES:L]
    y = jax.nn.silu(acc)

    a_in, b_in = ab_ref[:, 0:H], ab_ref[:, H:2 * H]
    beta_all = jax.nn.sigmoid(b_in)
    g_all = -jnp.exp(alog_ref[...]) * jax.nn.softplus(a_in + dtb_ref[...])

    row = lax.broadcasted_iota(jnp.int32, (C, GDN_PACK), 0)
    col = lax.broadcasted_iota(jnp.int32, (C, GDN_PACK), 1) & (C - 1)
    strict = row > col
    bd_r = lax.broadcasted_iota(jnp.int32, (GDN_PACK, GDN_PACK), 0)
    bd_c = lax.broadcasted_iota(jnp.int32, (GDN_PACK, GDN_PACK), 1)
    blk_diag = (bd_r >> 6) == (bd_c >> 6)

    lr = lax.broadcasted_iota(jnp.int32, (L, L), 0)
    lc = lax.broadcasted_iota(jnp.int32, (L, L), 1)
    same_chunk = (lr >> 6) == (lc >> 6)
    tri = jnp.where(same_chunk & (lr >= lc), 1.0, 0.0)
    head_of_lane = lax.broadcasted_iota(jnp.int32, (H, GDN_PACK), 1) >> 6
    spread = jnp.where(head_of_lane == lax.broadcasted_iota(jnp.int32, (H, GDN_PACK), 0), 1.0, 0.0)
    pr = lax.broadcasted_iota(jnp.int32, (L, GDN_PACK), 0) & (C - 1)
    pc = lax.broadcasted_iota(jnp.int32, (L, GDN_PACK), 1) & (C - 1)
    gc_all = _dot_exact_lhs(tri, g_all)
    gi_all = sum(jnp.dot(t, spread.astype(bf16), preferred_element_type=f32) for t in _bf16_terms(gc_all, 3))
    gj_all = _dot_exact_lhs(jnp.where(same_chunk, 1.0, 0.0), jnp.where(pr == pc, gi_all, 0.0))
    decay_all = jnp.exp(jnp.where(pr >= pc, gi_all - gj_all, -jnp.inf))
    egc_all = jnp.exp(gc_all)

    def l2n(t):
        return t * lax.rsqrt(jnp.sum(t * t, axis=-1, keepdims=True) + EPS)

    n_chunks = L // C
    chunk_rows = [slice(c * C, (c + 1) * C) for c in range(n_chunks)]
    qs, ks, vs, kbs, a_mats, qk_ps = [], [], [], [], [], []
    for rows in chunk_rows:
        kk, qk = [], []
        for h in range(H):
            q = l2n(y[rows, h * dh:(h + 1) * dh]) * dh ** -0.5
            k = l2n(y[rows, W + h * dh:W + (h + 1) * dh])
            kb = k * beta_all[rows, h:h + 1]
            qs.append(q), ks.append(k), vs.append(y[rows, 2 * W + h * dh:2 * W + (h + 1) * dh]), kbs.append(kb)
            kk.append(_dot_nt(kb.astype(bf16), k.astype(bf16)))
            qk.append(_dot_nt(q.astype(bf16), k.astype(bf16)))
        a_mats.append(jnp.where(strict, jnp.concatenate(kk, axis=1) * decay_all[rows], 0.0))
        qk_ps.append(jnp.concatenate(qk, axis=1) * decay_all[rows])
    t_ps = _gdn_packed_unit_lower_inverse(a_mats, row, col, blk_diag)
    for c, rows in enumerate(chunk_rows):
        beta, gc, egc = beta_all[rows], gc_all[rows], egc_all[rows]
        outs = []
        for h in range(H):
            i = c * H + h
            t_h = t_ps[c][:, h * C:(h + 1) * C].astype(bf16)
            e_h = egc[:, h:h + 1]
            u = jnp.dot(t_h, (vs[i] * beta[:, h:h + 1]).astype(bf16), preferred_element_type=f32)
            w = jnp.dot(t_h, (kbs[i] * e_h).astype(bf16), preferred_element_type=f32)
            state = st_ref[h]
            sb = state.astype(bf16)
            v_new = u - jnp.dot(w.astype(bf16), sb, preferred_element_type=f32)
            o = (jnp.dot((qs[i] * e_h).astype(bf16), sb, preferred_element_type=f32)
                 + jnp.dot(qk_ps[c][:, h * C:(h + 1) * C].astype(bf16), v_new.astype(bf16),
                           preferred_element_type=f32))
            g_last = gc[C - 1:C, h:h + 1]
            k_dec = (ks[i] * jnp.exp(g_last - gc[:, h:h + 1])).astype(bf16)
            st_ref[h] = state * jnp.exp(g_last) + lax.dot_general(
                k_dec, v_new.astype(bf16), (((0,), (0,)), ((), ())), preferred_element_type=f32)
            o = o * lax.rsqrt(jnp.mean(o * o, axis=-1, keepdims=True) + EPS) * nw_ref[...]
            outs.append(o * jax.nn.silu(z_ref[rows, h * dh:(h + 1) * dh]))
        o_ref[rows, :] = jnp.concatenate(outs, axis=1)


def gdn_mixer(proj, conv_w, a_log, dt_bias, norm_w):
    Bsz, S, _ = proj.shape
    f32 = jnp.float32
    L = min(GDN_ROWS, S)
    H, W, W3 = GDN_HEADS, GDN_WIDTH, 3 * GDN_WIDTH
    full = lambda *shape: pl.BlockSpec(shape, lambda bi, ci: (0,) * len(shape))
    rows = lambda n, col=0: pl.BlockSpec((None, L, n), lambda bi, ci: (bi, ci, col // n))
    return pl.pallas_call(
        _gdn_body,
        grid=(Bsz, S // L),
        in_specs=[rows(W3, PROJ_QKV), rows(W, PROJ_Z), rows(LANES, PROJ_SMALL),
                  full(GDN_CONV, W3), full(1, H), full(1, H), full(1, GDN_HEAD_DIM)],
        out_specs=rows(W),
        out_shape=jax.ShapeDtypeStruct((Bsz, S, W), f32),
        scratch_shapes=[pltpu.VMEM((SUBLANES, W3), f32), pltpu.VMEM((H, GDN_HEAD_DIM, GDN_HEAD_DIM), f32)],
        compiler_params=pltpu.CompilerParams(dimension_semantics=("arbitrary", "arbitrary"),
                                             vmem_limit_bytes=48 * 1024 * 1024),
        name="gdn_mixer",
    )(proj, proj, proj, conv_w.astype(f32), a_log.astype(f32).reshape(1, H),
      dt_bias.astype(f32).reshape(1, H), norm_w.astype(f32).reshape(1, GDN_HEAD_DIM))


def nsa_mixer(q, kv, gate_logits, q_norm, k_norm, cmp_pe, cmp_w1, cmp_w2, cos, sin):
    Bsz, S, _ = q.shape
    f32 = jnp.float32
    G, R, dh = NSA_KV_HEADS, NSA_HEADS // NSA_KV_HEADS, NSA_HEAD_DIM
    scale = dh ** -0.5
    q = rope(rms_norm(q.reshape(Bsz, S, NSA_HEADS, dh), q_norm), cos, sin)
    kc_raw, vc_raw, ks, vs, kw, vw = (t.reshape(Bsz, S, G, dh) for t in jnp.split(kv, 6, axis=-1))
    vs, vw = lax.optimization_barrier((vs, vw))
    ks = rope(rms_norm(ks, k_norm[1]), cos, sin)
    kw = rope(rms_norm(kw, k_norm[2]), cos, sin)

    assert CMP_LEN == 2 * CMP_STRIDE and S % CMP_STRIDE == 0
    n_cmp = (S - CMP_LEN) // CMP_STRIDE + 1
    win_idx = jnp.arange(n_cmp)[:, None] * CMP_STRIDE + jnp.arange(CMP_LEN)[None, :]
    cmp_end = win_idx[:, -1]

    def compress(t, pe, w1, w2):
        grp = t.reshape(Bsz, S // CMP_STRIDE, CMP_STRIDE, G, dh)
        blocks = jnp.concatenate([grp[:, :-1], grp[:, 1:]], axis=2) + pe[:, None, :]
        blocks = jnp.moveaxis(blocks, 3, 2).reshape(Bsz, n_cmp, G, CMP_LEN * dh)
        return jax.nn.gelu(blocks @ w1) @ w2

    cos_c, sin_c = rope_tables(cmp_end, dh)
    k_cmp = rope(rms_norm(compress(kc_raw, cmp_pe[0], cmp_w1[0], cmp_w2[0]), k_norm[0]), cos_c, sin_c)
    v_cmp = compress(vc_raw, cmp_pe[1], cmp_w1[1], cmp_w2[1])

    n_slc = S // SLC_BLOCK
    n_top = min(SLC_TOPK, n_slc)
    c0 = jnp.arange(n_cmp) * CMP_STRIDE
    s0 = jnp.arange(n_slc) * SLC_BLOCK
    overlap = jnp.clip(jnp.minimum(c0[:, None] + CMP_LEN, s0[None, :] + SLC_BLOCK)
                       - jnp.maximum(c0[:, None], s0[None, :]), 0, None).astype(f32) / CMP_LEN
    gates = jax.nn.sigmoid(gate_logits.astype(f32)).reshape(Bsz, S, G, R * 3).transpose(0, 2, 3, 1)

    bf16 = jnp.bfloat16
    n_cmp_pad = S // CMP_STRIDE
    pad_cmp = lambda t: jnp.pad(t.astype(bf16), ((0, 0), (0, n_cmp_pad - n_cmp), (0, 0), (0, 0)))
    keys = lambda t: t.astype(bf16).transpose(0, 2, 1, 3)
    vals = lambda t: t.astype(bf16).transpose(0, 2, 3, 1)
    q_rows = (q * scale).astype(bf16).reshape(Bsz, S, NSA_WIDTH)
    overlap_t = jnp.pad(overlap.T, ((0, 0), (0, n_cmp_pad - n_cmp))).astype(bf16)
    expand_t = ((jnp.arange(S) // SLC_BLOCK)[:, None] == jnp.arange(n_slc)[None, :]).astype(bf16)
    return _nsa_attention(q_rows, keys(pad_cmp(k_cmp)), vals(pad_cmp(v_cmp)), keys(ks), vals(vs), keys(kw), vals(vw),
                          gates, overlap_t, expand_t, n_top)


NSA_R = NSA_HEADS // NSA_KV_HEADS
SLC_TILE = 1024
MASK_NEG = -1e30


def _dot_nt(a, b):
    return lax.dot_general(a, b, (((1,), (1,)), ((), ())), preferred_element_type=jnp.float32)


def _nsa_body(q_ref, kc_ref, vct_ref, ks_ref, vst_ref, kw_ref, vwt_ref, gate_ref, ovt_ref, expt_ref, o_ref,
              *, n_top):
    f32, bf16 = jnp.float32, jnp.bfloat16
    QB, dh, R = q_ref.shape[0], ks_ref.shape[1], NSA_R
    S = ks_ref.shape[0]
    n_slc, n_cmp_pad = ovt_ref.shape
    M = R * QB
    t0 = pl.program_id(2) * QB
    q_blk = q_ref[...]
    q = jnp.concatenate([q_blk[:, r * dh:(r + 1) * dh] for r in range(R)], axis=0)
    per_head = lambda t: jnp.concatenate([t] * R, axis=1)

    band = S if S < WINDOW + QB else WINDOW + QB
    start = pl.multiple_of(jnp.maximum(t0 + QB - band, 0), QB)
    s = _dot_nt(kw_ref[pl.ds(start, band), :], q)
    diff = (t0 - start) + (lax.broadcasted_iota(jnp.int32, (band, QB), 1) - lax.broadcasted_iota(jnp.int32, (band, QB), 0))
    s = s + per_head(jnp.where((diff >= 0) & (diff < WINDOW), 0.0, -jnp.inf))
    p = jnp.exp(s - jnp.max(s, axis=0, keepdims=True))
    o_win = jnp.dot(vwt_ref[:, pl.ds(start, band)], p.astype(bf16), preferred_element_type=f32)
    o_win = o_win / jnp.sum(p, axis=0, keepdims=True)

    s = _dot_nt(kc_ref[...], q)
    n_idx = lax.broadcasted_iota(jnp.int32, (n_cmp_pad, QB), 0)
    t_idx = t0 + lax.broadcasted_iota(jnp.int32, (n_cmp_pad, QB), 1)
    valid = (n_idx * CMP_STRIDE + (CMP_LEN - 1) <= t_idx) & (n_idx < n_cmp_pad - 1)
    s = s + per_head(jnp.where(valid, 0.0, -jnp.inf))
    m = jnp.max(s, axis=0, keepdims=True)
    m = jnp.where(jnp.isfinite(m), m, 0.0)
    p = jnp.exp(s - m)
    p = p / jnp.maximum(jnp.sum(p, axis=0, keepdims=True), jnp.finfo(f32).tiny)
    o_cmp = jnp.dot(vct_ref[...], p.astype(bf16), preferred_element_type=f32)

    p_sum = p[:, 0:QB]
    for r in range(1, R):
        p_sum = p_sum + p[:, r * QB:(r + 1) * QB]
    imp = sum(jnp.dot(ovt_ref[...], t, preferred_element_type=f32) for t in _bf16_terms(p_sum, 3))
    j_idx = lax.broadcasted_iota(jnp.int32, (n_slc, QB), 0)
    t_lane = t0 + lax.broadcasted_iota(jnp.int32, (n_slc, QB), 1)
    cur = lax.shift_right_logical(t_lane, 6)
    forced = (j_idx == 0) | (j_idx == cur) | (j_idx == cur - 1)
    started = j_idx * SLC_BLOCK <= t_lane
    imp = jnp.where(forced, BIG, jnp.where(started, imp, -BIG))
    rank = jnp.zeros((n_slc, QB), f32)
    for jp in range(n_slc):
        row = imp[jp:jp + 1, :]
        ahead = (row > imp) | ((row == imp) & (j_idx > jp))
        rank = rank + jnp.where(ahead, 1.0, 0.0)
    sel = jnp.where(rank < n_top, 1.0, 0.0).astype(bf16)
    k_rel = lax.broadcasted_iota(jnp.int32, (SLC_TILE, QB), 0)
    t_q = t0 + lax.broadcasted_iota(jnp.int32, (SLC_TILE, QB), 1)

    def slc_step(kt, carry):
        m, l, acc = carry
        k0 = pl.multiple_of(kt * SLC_TILE, SLC_TILE)
        s = _dot_nt(ks_ref[pl.ds(k0, SLC_TILE), :], q)
        sel_keys = jnp.dot(expt_ref[pl.ds(k0, SLC_TILE), :], sel, preferred_element_type=f32)
        bias = jnp.where((sel_keys > 0.5) & (k_rel + k0 <= t_q), 0.0, MASK_NEG)
        s = s + per_head(bias)
        m_new = jnp.maximum(m, jnp.max(s, axis=0, keepdims=True))
        alpha = jnp.exp(m - m_new)
        p = jnp.exp(s - m_new)
        l = alpha * l + jnp.sum(p, axis=0, keepdims=True)
        acc = alpha * acc + jnp.dot(vst_ref[:, pl.ds(k0, SLC_TILE)], p.astype(bf16), preferred_element_type=f32)
        return m_new, l, acc

    n_tiles = (t0 + QB - 1) // SLC_TILE + 1
    init = (jnp.full((1, M), MASK_NEG, f32), jnp.zeros((1, M), f32), jnp.zeros((dh, M), f32))
    _, l, acc = lax.fori_loop(0, n_tiles, slc_step, init)
    o_slc = acc / l

    gate = gate_ref[...]
    g_cmp, g_slc, g_win = (jnp.concatenate([gate[3 * r + c:3 * r + c + 1, :] for r in range(R)], axis=1)
                           for c in range(3))
    o_t = g_cmp * o_cmp + g_slc * o_slc + g_win * o_win
    o_ref[...] = jnp.concatenate([o_t[:, r * QB:(r + 1) * QB].T for r in range(R)], axis=1)


def _nsa_attention(q, kc, vct, ks, vst, kw, vwt, gates, overlap_t, expand_t, n_top):
    Bsz, G, S, dh = ks.shape
    R = NSA_R
    n_cmp_pad = kc.shape[2]
    assert S % SLC_TILE == 0 and SLC_TILE % Q_BLOCK == 0, (S, SLC_TILE, Q_BLOCK)
    keys = lambda n: pl.BlockSpec((None, None, n, dh), lambda b, g, i: (b, g, 0, 0))
    vals = lambda n: pl.BlockSpec((None, None, dh, n), lambda b, g, i: (b, g, 0, 0))
    const = lambda a: pl.BlockSpec(a.shape, lambda b, g, i: (0, 0))
    return pl.pallas_call(
        functools.partial(_nsa_body, n_top=n_top),
        grid=(Bsz, G, S // Q_BLOCK),
        in_specs=[pl.BlockSpec((None, Q_BLOCK, R * dh), lambda b, g, i: (b, i, g)),
                  keys(n_cmp_pad), vals(n_cmp_pad), keys(S), vals(S), keys(S), vals(S),
                  pl.BlockSpec((None, None, 3 * R, Q_BLOCK), lambda b, g, i: (b, g, 0, i)),
                  const(overlap_t), const(expand_t)],
        out_specs=pl.BlockSpec((None, Q_BLOCK, R * dh), lambda b, g, i: (b, i, g)),
        out_shape=jax.ShapeDtypeStruct((Bsz, S, G * R * dh), jnp.float32),
        compiler_params=pltpu.CompilerParams(dimension_semantics=("arbitrary", "arbitrary", "arbitrary"),
                                             vmem_limit_bytes=48 * 1024 * 1024),
        name="nsa_attention",
    )(q, kc, vct, ks, vst, kw, vwt, gates, overlap_t, expand_t)


def hybrid_mixer(x, ln_mix, w_in, lam_re, lam_im, b_re, b_im, c_re, c_im, s5_d, s5_log_dt, glu_w, glu_b,
                 s5_out_norm, conv_w, a_log, dt_bias, gdn_norm_w, q_norm, k_norm, cmp_pe, cmp_w1,
                 cmp_w2, cos, sin):
    Bsz, S, D = x.shape
    proj = _in_proj(x.reshape(-1, D), ln_mix, _in_proj_weight(w_in)).reshape(Bsz, S, PROJ_W)
    y_s5 = s5_mixer_normed(proj, lam_re, lam_im, b_re, b_im, c_re, c_im, s5_d, s5_log_dt, glu_w, glu_b, s5_out_norm)
    y_gdn = gdn_mixer(proj, conv_w, a_log, dt_bias, gdn_norm_w)
    y_nsa = nsa_mixer(proj[..., PROJ_Q:PROJ_Q + NSA_WIDTH], proj[..., PROJ_KV:PROJ_KV + 6 * NSA_KV_WIDTH],
                      proj[..., PROJ_GATES:PROJ_GATES + 3 * NSA_HEADS], q_norm, k_norm, cmp_pe, cmp_w1, cmp_w2,
                      cos, sin)
    return y_s5, y_gdn, y_nsa


def hier_moe(x, h, logits, layer, w_gate, w_up, w_down):
    T, D = x.shape
    f32 = jnp.float32
    g_logits = logits[:, :N_GROUPS]
    grp = jnp.argmax(g_logits, axis=-1)
    p_grp = jnp.take_along_axis(jax.nn.softmax(g_logits, axis=-1), grp[:, None], axis=-1)
    e_logits = logits[:, N_GROUPS:N_ROUTER].reshape(T, N_GROUPS, EXPERTS_PER_GROUP)
    e_logits = jnp.take_along_axis(e_logits, grp[:, None, None], axis=1)[:, 0]
    local = jnp.arange(EXPERTS_PER_GROUP, dtype=jnp.int32)[None, :]
    rest, top_logit, top_local = e_logits, [], []
    for _ in range(TOP_K):
        idx = jnp.argmax(rest, axis=-1).astype(jnp.int32)
        top_logit.append(jnp.take_along_axis(rest, idx[:, None], axis=-1)[:, 0])
        top_local.append(idx)
        rest = jnp.where(local == idx[:, None], -jnp.inf, rest)
    top_logit, top_local = jnp.stack(top_logit, axis=-1), jnp.stack(top_local, axis=-1)
    weights = jax.nn.softmax(top_logit, axis=-1) * p_grp
    experts = grp[:, None].astype(jnp.int32) * EXPERTS_PER_GROUP + top_local
    n_assign = T * TOP_K
    e_flat = experts.reshape(-1)
    tok_flat = jnp.repeat(jnp.arange(T, dtype=jnp.int32), TOP_K)
    onehot = (e_flat[:, None] == jnp.arange(N_EXPERTS, dtype=jnp.int32)[None, :]).astype(f32)
    blocks = onehot.reshape(n_assign // COUNT_BLOCK, COUNT_BLOCK, N_EXPERTS)
    tri = jnp.tril(jnp.ones((COUNT_BLOCK, COUNT_BLOCK), f32))
    inside = jnp.einsum('ij,bje->bie', tri, blocks)
    before = jnp.cumsum(inside[:, -1, :], axis=0) - inside[:, -1, :]
    running = (inside + before[:, None, :]).reshape(n_assign, N_EXPERTS).astype(jnp.int32)
    counts = running[-1]
    padded = (counts + MOE_ROWS - 1) // MOE_ROWS * MOE_ROWS
    pad_start = jnp.cumsum(padded) - padded
    dest = (pad_start[e_flat] + jnp.take_along_axis(running, e_flat[:, None], axis=1)[:, 0] - 1).astype(jnp.int32)
    n_blk = (n_assign + MOE_ROWS - 1) // MOE_ROWS + N_EXPERTS
    cap = n_blk * MOE_ROWS
    slot_tok = jnp.zeros((cap,), jnp.int32).at[dest].set(tok_flat)
    seg_end = jnp.cumsum(padded)
    blk_start = jnp.arange(n_blk, dtype=seg_end.dtype) * MOE_ROWS
    blk_expert = jnp.minimum(jnp.sum((seg_end[None, :] <= blk_start[:, None]).astype(jnp.int32), axis=1),
                             N_EXPERTS - 1).astype(jnp.int32)
    n_used = (seg_end[-1] // MOE_ROWS).astype(jnp.int32).reshape(1)
    y = _moe_ffn(blk_expert, slot_tok, n_used, x, layer, w_gate.astype(f32), w_up.astype(f32),
                 w_down.astype(f32))
    return _moe_combine(dest, h, weights, y)


CMB_ROWS = 256


def _moe_combine_body(d_ref, h_ref, w_ref, y_hbm, o_ref, buf, sem):
    i = pl.program_id(0)
    n = pl.num_programs(0)
    rows = h_ref.shape[0]
    slot = i & 1

    def row_copy(step, r, k, to):
        src = y_hbm.at[pl.ds(d_ref[(step * rows + r) * TOP_K + k], 1), :]
        return pltpu.make_async_copy(src, buf.at[to, k, pl.ds(r, 1), :], sem.at[to])

    def start_step(step, to):
        for r in range(rows):
            for k in range(TOP_K):
                row_copy(step, r, k, to).start(priority=(r * TOP_K + k) % 2)

    def finish_step():
        for r in range(rows):
            for k in range(TOP_K):
                row_copy(i, r, k, slot).wait()
        out = h_ref[...]
        for k in range(TOP_K):
            out = out + w_ref[:, k:k + 1] * buf[slot, k]
        o_ref[...] = out

    @pl.when(i == 0)
    def _():
        start_step(0, 0)

    @pl.when(i + 1 < n)
    def _():
        start_step(i + 1, 1 - slot)
        finish_step()

    @pl.when(i + 1 == n)
    def _():
        finish_step()


def _moe_combine(dest, h, weights, y):
    T, D = h.shape
    rows = min(CMB_ROWS, T)
    return pl.pallas_call(
        _moe_combine_body,
        grid_spec=pltpu.PrefetchScalarGridSpec(
            num_scalar_prefetch=1, grid=(T // rows,),
            in_specs=[pl.BlockSpec((rows, D), lambda i, d: (i, 0)),
                      pl.BlockSpec((rows, TOP_K), lambda i, d: (i, 0)),
                      pl.BlockSpec(memory_space=pl.ANY)],
            out_specs=pl.BlockSpec((rows, D), lambda i, d: (i, 0)),
            scratch_shapes=[pltpu.VMEM((2, TOP_K, rows, D), jnp.float32), pltpu.SemaphoreType.DMA((2,))]),
        out_shape=jax.ShapeDtypeStruct((T, D), jnp.float32),
        compiler_params=pltpu.CompilerParams(dimension_semantics=("arbitrary",),
                                             vmem_limit_bytes=32 * 1024 * 1024),
        name="moe_combine",
    )(dest, h, weights, y)


MOE_ROWS = 256
COUNT_BLOCK = 128


def _moe_ffn_body(be_ref, tok_ref, nu_ref, x_hbm, wg_ref, wu_ref, wd_ref, o_ref, xbuf, sem, wg_bf, wu_bf, wd_bf):
    i = pl.program_id(0)
    n = nu_ref[0]
    slot = i & 1

    def row_copy(step, r, to):
        src = x_hbm.at[pl.ds(tok_ref[step * MOE_ROWS + r], 1), :]
        return pltpu.make_async_copy(src, xbuf.at[to, pl.ds(r, 1), :], sem.at[to])

    def start_step(step, to):
        for r in range(MOE_ROWS):
            row_copy(step, r, to).start(priority=r % 2)

    def finish_step():
        for r in range(MOE_ROWS):
            row_copy(i, r, slot).wait()
        x = xbuf[slot].astype(jnp.bfloat16)
        gate = jnp.dot(x, wg_bf[...], preferred_element_type=jnp.float32)
        up = jnp.dot(x, wu_bf[...], preferred_element_type=jnp.float32)
        hid = (jax.nn.silu(gate) * up).astype(jnp.bfloat16)
        o_ref[...] = jnp.dot(hid, wd_bf[...], preferred_element_type=jnp.float32)

    @pl.when(i == 0)
    def _():
        start_step(0, 0)

    @pl.when((i < n) & ((i == 0) | (be_ref[i] != be_ref[jnp.maximum(i - 1, 0)])))
    def _():
        wg_bf[...] = wg_ref[...].astype(jnp.bfloat16)
        wu_bf[...] = wu_ref[...].astype(jnp.bfloat16)
        wd_bf[...] = wd_ref[...].astype(jnp.bfloat16)

    @pl.when(i + 1 < n)
    def _():
        start_step(i + 1, 1 - slot)
        finish_step()

    @pl.when(i + 1 == n)
    def _():
        finish_step()

    @pl.when(i >= n)
    def _():
        o_ref[...] = jnp.zeros_like(o_ref)


def _moe_ffn(blk_expert, slot_tok, n_used, x, layer, w_gate, w_up, w_down):
    cap = slot_tok.shape[0]
    D = x.shape[1]
    n_blk = cap // MOE_ROWS
    De = w_gate.shape[-1]
    return pl.pallas_call(
        _moe_ffn_body,
        grid_spec=pltpu.PrefetchScalarGridSpec(
            num_scalar_prefetch=3, grid=(n_blk,),
            in_specs=[pl.BlockSpec(memory_space=pl.ANY),
                      pl.BlockSpec((None, None, D, De), lambda i, be, tok, nu: (layer, be[i], 0, 0)),
                      pl.BlockSpec((None, None, D, De), lambda i, be, tok, nu: (layer, be[i], 0, 0)),
                      pl.BlockSpec((None, None, De, D), lambda i, be, tok, nu: (layer, be[i], 0, 0))],
            out_specs=pl.BlockSpec((MOE_ROWS, D), lambda i, be, tok, nu: (i, 0)),
            scratch_shapes=[pltpu.VMEM((2, MOE_ROWS, D), jnp.float32), pltpu.SemaphoreType.DMA((2,)),
                            pltpu.VMEM((D, De), jnp.bfloat16), pltpu.VMEM((D, De), jnp.bfloat16),
                            pltpu.VMEM((De, D), jnp.bfloat16)]),
        out_shape=jax.ShapeDtypeStruct((cap, D), jnp.float32),
        compiler_params=pltpu.CompilerParams(dimension_semantics=("arbitrary",),
                                             vmem_limit_bytes=56 * 1024 * 1024),
        name="moe_ffn",
    )(blk_expert, slot_tok, n_used, x, w_gate, w_up, w_down)


def kernel(x, ln_mix, w_in, s5_lambda_re, s5_lambda_im, s5_b_re, s5_b_im, s5_c_re, s5_c_im, s5_d,
           s5_log_dt, s5_glu_w, s5_glu_b, s5_out_norm, gdn_conv_w, gdn_a_log, gdn_dt_bias, gdn_norm_w,
           nsa_q_norm, nsa_k_norm, nsa_cmp_pe, nsa_cmp_w1, nsa_cmp_w2, nsa_out_norm, w_out, ln_ffn,
           moe_group_w, moe_group_b, moe_expert_w, moe_expert_b, moe_w_gate, moe_w_up, moe_w_down):
    Bsz, S, D = x.shape
    cos, sin = rope_tables(jnp.arange(S), NSA_HEAD_DIM)
    for l in range(DEPTH):
        y_s5, y_gdn, y_nsa = hybrid_mixer(
            x, ln_mix[l], w_in[l], s5_lambda_re[l], s5_lambda_im[l], s5_b_re[l], s5_b_im[l], s5_c_re[l], s5_c_im[l],
            s5_d[l], s5_log_dt[l], s5_glu_w[l], s5_glu_b[l], s5_out_norm[l], gdn_conv_w[l], gdn_a_log[l],
            gdn_dt_bias[l], gdn_norm_w[l], nsa_q_norm[l], nsa_k_norm[l], nsa_cmp_pe[l], nsa_cmp_w1[l], nsa_cmp_w2[l],
            cos, sin)
        router_w = jnp.concatenate([moe_group_w[l], moe_expert_w[l]], axis=-1)
        router_b = jnp.concatenate([moe_group_b[l], moe_expert_b[l]], axis=-1)
        h, a, logits = _out_proj(x.reshape(-1, D), y_s5.reshape(-1, S5_WIDTH), y_gdn.reshape(-1, GDN_WIDTH),
                                 y_nsa.reshape(-1, NSA_WIDTH), nsa_out_norm[l], w_out[l], ln_ffn[l], router_w, router_b)
        x = hier_moe(a, h, logits, l, moe_w_gate, moe_w_up, moe_w_down).reshape(Bsz, S, D)
    return x
```

```python
import functools
import jax, jax.numpy as jnp
from jax import lax
import numpy as np
from jax.experimental import pallas as pl
from jax.experimental.pallas import tpu as pltpu

D_MODEL = 2048
BATCH = 4
SEQ = 4096
DEPTH = 2

EPS = 1e-6
S5_WIDTH = D_MODEL // 4
S5_GROUP = 16
S5_GROUPS = S5_WIDTH // S5_GROUP
S5_STATE = 64
GDN_HEAD_DIM = 128
GDN_WIDTH = D_MODEL // 4
GDN_HEADS = GDN_WIDTH // GDN_HEAD_DIM
GDN_CONV = 4
GDN_CHUNK = 64
NSA_HEAD_DIM = 64
NSA_WIDTH = D_MODEL // 2
NSA_HEADS = NSA_WIDTH // NSA_HEAD_DIM
NSA_KV_HEADS = 4
NSA_KV_WIDTH = NSA_KV_HEADS * NSA_HEAD_DIM
CMP_LEN = 32
CMP_STRIDE = 16
CMP_HIDDEN = 256
SLC_BLOCK = 64
SLC_TOPK = 16
WINDOW = 512
Q_BLOCK = 128
ROPE_THETA = 10000.0
BIG = 1e9
N_GROUPS = 4
EXPERTS_PER_GROUP = 8
N_EXPERTS = N_GROUPS * EXPERTS_PER_GROUP
TOP_K = 2
D_EXPERT = 512
IN_SIZES = (S5_WIDTH, 3 * GDN_WIDTH, GDN_WIDTH, GDN_HEADS, GDN_HEADS, NSA_WIDTH, 6 * NSA_KV_WIDTH, 3 * NSA_HEADS)
D_IN = sum(IN_SIZES)


PROJ_QKV, PROJ_KV, PROJ_Q, PROJ_U, PROJ_Z, PROJ_SMALL = 0, 1536, 3072, 4096, 4608, 5120
PROJ_SMALL_W = 512
PROJ_W = PROJ_SMALL + PROJ_SMALL_W
PROJ_GATES = PROJ_SMALL + 2 * GDN_HEADS
LANES = 128
IN_ROWS, IN_COLS = 1024, 512
OUT_ROWS = 256
N_ROUTER = N_GROUPS + N_EXPERTS


def _in_proj_weight(w_in):
    u, qkv, z, ga, gb, q, kv, gl = jnp.split(w_in, np.cumsum(IN_SIZES)[:-1].tolist(), axis=-1)
    small = jnp.concatenate([ga, gb, gl], axis=-1)
    small = jnp.pad(small, ((0, 0), (0, PROJ_SMALL_W - small.shape[-1])))
    return jnp.concatenate([qkv, kv, q, u, z, small], axis=-1).astype(jnp.bfloat16)


def _in_proj_body(x_ref, g_ref, w_ref, o_ref, a_ref):
    @pl.when(pl.program_id(1) == 0)
    def _():
        x = x_ref[...]
        a_ref[...] = (x * lax.rsqrt(jnp.mean(x * x, axis=-1, keepdims=True) + EPS) * g_ref[...]).astype(jnp.bfloat16)

    o_ref[...] = jnp.dot(a_ref[...], w_ref[...], preferred_element_type=jnp.float32)


def _in_proj(x, gain, w):
    T, D = x.shape
    N = w.shape[1]
    tm = min(IN_ROWS, T)
    return pl.pallas_call(
        _in_proj_body,
        grid=(T // tm, N // IN_COLS),
        in_specs=[pl.BlockSpec((tm, D), lambda i, j: (i, 0)),
                  pl.BlockSpec((1, D), lambda i, j: (0, 0)),
                  pl.BlockSpec((D, IN_COLS), lambda i, j: (0, j))],
        out_specs=pl.BlockSpec((tm, IN_COLS), lambda i, j: (i, j)),
        out_shape=jax.ShapeDtypeStruct((T, N), jnp.float32),
        scratch_shapes=[pltpu.VMEM((tm, D), jnp.bfloat16)],
        compiler_params=pltpu.CompilerParams(dimension_semantics=("arbitrary", "arbitrary"),
                                             vmem_limit_bytes=48 * 1024 * 1024),
        name="in_proj",
    )(x, gain.astype(jnp.float32).reshape(1, D), w)


def _out_proj_body(x_ref, s5_ref, gdn_ref, nsa_ref, ng_ref, w_ref, fg_ref, rw_ref, rb_ref, h_ref, a_ref, lg_ref):
    f32, bf16 = jnp.float32, jnp.bfloat16
    nsa = nsa_ref[...]
    nsa = nsa * lax.rsqrt(jnp.mean(nsa * nsa, axis=-1, keepdims=True) + EPS) * ng_ref[...]
    k0, k1 = S5_WIDTH, S5_WIDTH + GDN_WIDTH
    y = (jnp.dot(s5_ref[...].astype(bf16), w_ref[0:k0, :], preferred_element_type=f32)
         + jnp.dot(gdn_ref[...].astype(bf16), w_ref[k0:k1, :], preferred_element_type=f32)
         + jnp.dot(nsa.astype(bf16), w_ref[k1:, :], preferred_element_type=f32))
    h = x_ref[...] + y
    h_ref[...] = h
    a = h * lax.rsqrt(jnp.mean(h * h, axis=-1, keepdims=True) + EPS) * fg_ref[...]
    a_ref[...] = a
    lg_ref[...] = jnp.dot(a.astype(bf16), rw_ref[...], preferred_element_type=f32) + rb_ref[...]


def _out_proj(x, y_s5, y_gdn, y_nsa, nsa_gain, w_out, ffn_gain, router_w, router_b):
    T, D = x.shape
    tm = min(OUT_ROWS, T)
    f32, bf16 = jnp.float32, jnp.bfloat16
    rw = jnp.pad(router_w.astype(bf16), ((0, 0), (0, LANES - N_ROUTER)))
    rb = jnp.pad(router_b.astype(f32), (0, LANES - N_ROUTER)).reshape(1, LANES)
    rows = lambda n: pl.BlockSpec((tm, n), lambda i: (i, 0))
    full = lambda *shape: pl.BlockSpec(shape, lambda i: (0,) * len(shape))
    return pl.pallas_call(
        _out_proj_body,
        grid=(T // tm,),
        in_specs=[rows(D), rows(S5_WIDTH), rows(GDN_WIDTH), rows(NSA_WIDTH), full(1, NSA_WIDTH), full(D, D),
                  full(1, D), full(D, LANES), full(1, LANES)],
        out_specs=[rows(D), rows(D), rows(LANES)],
        out_shape=[jax.ShapeDtypeStruct((T, D), f32), jax.ShapeDtypeStruct((T, D), f32),
                   jax.ShapeDtypeStruct((T, LANES), f32)],
        compiler_params=pltpu.CompilerParams(dimension_semantics=("arbitrary",),
                                             vmem_limit_bytes=48 * 1024 * 1024),
        name="out_proj",
    )(x, y_s5, y_gdn, y_nsa, nsa_gain.astype(f32).reshape(1, NSA_WIDTH), w_out.astype(bf16),
      ffn_gain.astype(f32).reshape(1, D), rw, rb)


def rms_norm(x, gain):
    xf = x.astype(jnp.float32)
    y = xf * lax.rsqrt(jnp.mean(xf * xf, axis=-1, keepdims=True) + EPS)
    return (y * gain.astype(jnp.float32)).astype(x.dtype)


def rope_tables(pos, dim):
    inv_freq = ROPE_THETA ** (-jnp.arange(0, dim, 2, dtype=jnp.float32) / dim)
    ang = pos.astype(jnp.float32)[:, None] * inv_freq[None, :]
    return jnp.cos(ang)[:, None, :], jnp.sin(ang)[:, None, :]


def rope(x, cos, sin):
    x1, x2 = jnp.split(x, 2, axis=-1)
    return jnp.concatenate([x1 * cos - x2 * sin, x2 * cos + x1 * sin], axis=-1).astype(x.dtype)


S5_OCT = 8
S5_NOCT = S5_GROUPS // S5_OCT
S5_OCT_IN = S5_OCT * S5_GROUP
S5_OCT_ST = S5_OCT * S5_STATE
SUBLANES = 8
S5_SCAN_SHIFTS = (1, 2, 4)


def _s5_constants(lam_re, lam_im, b_re, b_im, c_re, c_im, log_dt):
    f32 = jnp.float32
    lr, li = lam_re.astype(f32), lam_im.astype(f32)
    dt = jnp.exp(log_dt.astype(f32))[:, None]

    def lam_pow(k):
        mag = jnp.exp(k * lr * dt)
        return mag * jnp.cos(k * li * dt), mag * jnp.sin(k * li * dt)

    ar, ai = lam_pow(1.0)
    nr, ni = ar - 1.0, ai
    den = lr * lr + li * li
    qr, qi = (nr * lr + ni * li) / den, (ni * lr - nr * li) / den
    br, bi = b_re.astype(f32), b_im.astype(f32)
    bbr = qr[:, :, None] * br - qi[:, :, None] * bi
    bbi = qr[:, :, None] * bi + qi[:, :, None] * br
    eye = jnp.eye(S5_OCT, dtype=f32)

    def blk_b(t):
        t = t.reshape(S5_NOCT, S5_OCT, S5_STATE, S5_GROUP)
        return jnp.einsum('qgph,gk->qghkp', t, eye).reshape(S5_NOCT, S5_OCT_IN, S5_OCT_ST)

    def blk_c(t):
        t = t.reshape(S5_NOCT, S5_OCT, S5_GROUP, S5_STATE)
        return jnp.einsum('qghp,gk->qgpkh', t, eye).reshape(S5_NOCT, S5_OCT_ST, S5_OCT_IN)

    b_mat = jnp.stack([blk_b(bbr), blk_b(bbi)]).astype(jnp.bfloat16)
    c_mat = jnp.stack([blk_c(c_re.astype(f32)), blk_c(-c_im.astype(f32))]).astype(jnp.bfloat16)
    rows = jnp.arange(SUBLANES, dtype=f32)[:, None, None]
    tiles = []
    for k in S5_SCAN_SHIFTS:
        pr, pi = lam_pow(jnp.full_like(rows, float(k)))
        keep = rows >= k
        tiles.append(jnp.stack([jnp.where(keep, pr, 0.0), jnp.where(keep, pi, 0.0)]))
    pr, pi = lam_pow(rows + 1.0)
    tiles.append(jnp.stack([pr, pi]))
    a_mat = jnp.stack(tiles).reshape(len(tiles), 2, SUBLANES, S5_GROUPS * S5_STATE)
    return b_mat, c_mat, a_mat


def _s5_body(u_ref, b_ref, c_ref, a_ref, d_ref, gw_ref, gb_ref, gain_ref, o_ref, st_ref, xr_ref, xi_ref):
    bf16 = jnp.bfloat16
    n_tiles = u_ref.shape[0] // SUBLANES

    @pl.when(pl.program_id(1) == 0)
    def _():
        st_ref[...] = jnp.zeros_like(st_ref)

    u = u_ref[...]
    ys = []
    for q in range(S5_NOCT):
        lanes = slice(q * S5_OCT_ST, (q + 1) * S5_OCT_ST)
        uq = u[:, q * S5_OCT_IN:(q + 1) * S5_OCT_IN].astype(bf16)
        xr_ref[...] = jnp.dot(uq, b_ref[0, q], preferred_element_type=jnp.float32)
        xi_ref[...] = jnp.dot(uq, b_ref[1, q], preferred_element_type=jnp.float32)

        def tile_step(i, carry, lanes=lanes):
            cr, ci = carry
            r0 = pl.multiple_of(i * SUBLANES, SUBLANES)
            xr = xr_ref[pl.ds(r0, SUBLANES), :]
            xi = xi_ref[pl.ds(r0, SUBLANES), :]
            for lvl, k in enumerate(S5_SCAN_SHIFTS):
                pr, pi = a_ref[lvl, 0, :, lanes], a_ref[lvl, 1, :, lanes]
                sr, si = pltpu.roll(xr, k, 0), pltpu.roll(xi, k, 0)
                xr, xi = xr + (pr * sr - pi * si), xi + (pr * si + pi * sr)
            pr, pi = a_ref[len(S5_SCAN_SHIFTS), 0, :, lanes], a_ref[len(S5_SCAN_SHIFTS), 1, :, lanes]
            cr = jnp.broadcast_to(cr, xr.shape)
            ci = jnp.broadcast_to(ci, xi.shape)
            xr, xi = xr + (pr * cr - pi * ci), xi + (pr * ci + pi * cr)
            xr_ref[pl.ds(r0, SUBLANES), :] = xr
            xi_ref[pl.ds(r0, SUBLANES), :] = xi
            return xr[SUBLANES - 1:, :], xi[SUBLANES - 1:, :]

        cr, ci = lax.fori_loop(0, n_tiles, tile_step, (st_ref[0, :, lanes], st_ref[1, :, lanes]))
        st_ref[0, :, lanes] = cr
        st_ref[1, :, lanes] = ci
        ys.append(jnp.dot(xr_ref[...].astype(bf16), c_ref[0, q], preferred_element_type=jnp.float32)
                  + jnp.dot(xi_ref[...].astype(bf16), c_ref[1, q], preferred_element_type=jnp.float32))
    y = jnp.concatenate(ys, axis=1) + d_ref[...] * u
    y = jax.nn.gelu(y)
    z = jnp.dot(y.astype(bf16), gw_ref[...], preferred_element_type=jnp.float32) + gb_ref[...]
    y = y * jax.nn.sigmoid(z)
    y = y * lax.rsqrt(jnp.mean(y * y, axis=-1, keepdims=True) + EPS)
    o_ref[...] = y * gain_ref[...]


def s5_mixer_normed(proj, lam_re, lam_im, b_re, b_im, c_re, c_im, d_skip, log_dt, glu_w, glu_b, out_norm, chunk=512):
    Bsz, S, _ = proj.shape
    W = S5_WIDTH
    f32 = jnp.float32
    L = min(chunk, S)
    b_mat, c_mat, a_mat = _s5_constants(lam_re, lam_im, b_re, b_im, c_re, c_im, log_dt)
    n_state = S5_GROUPS * S5_STATE
    full = lambda *shape: pl.BlockSpec(shape, lambda b, c: (0,) * len(shape))
    return pl.pallas_call(
        _s5_body,
        grid=(Bsz, S // L),
        in_specs=[pl.BlockSpec((None, L, W), lambda b, c: (b, c, PROJ_U // W)),
                  full(*b_mat.shape), full(*c_mat.shape), full(*a_mat.shape),
                  full(1, W), full(W, W), full(1, W), full(1, W)],
        out_specs=pl.BlockSpec((None, L, W), lambda b, c: (b, c, 0)),
        out_shape=jax.ShapeDtypeStruct((Bsz, S, W), f32),
        scratch_shapes=[pltpu.VMEM((2, 1, n_state), f32),
                        pltpu.VMEM((L, S5_OCT_ST), f32), pltpu.VMEM((L, S5_OCT_ST), f32)],
        compiler_params=pltpu.CompilerParams(dimension_semantics=("arbitrary", "arbitrary"),
                                             vmem_limit_bytes=48 * 1024 * 1024),
        name="s5_mixer",
    )(proj, b_mat, c_mat, a_mat, d_skip.astype(f32).reshape(1, W), glu_w.astype(jnp.bfloat16),
      glu_b.astype(f32).reshape(1, W), out_norm.astype(f32).reshape(1, W))


GDN_ROWS = 256
GDN_PACK = GDN_HEADS * GDN_CHUNK


def _bf16_terms(x, n):
    terms, rem = [], x
    for i in range(n):
        t = rem.astype(jnp.bfloat16)
        terms.append(t)
        if i + 1 < n:
            rem = rem - t.astype(jnp.float32)
    return terms


def _dot_exact_lhs(lhs01, x):
    lhs = lhs01.astype(jnp.bfloat16)
    return sum(jnp.dot(lhs, t, preferred_element_type=jnp.float32) for t in _bf16_terms(x, 3))


def _gdn_packed_unit_lower_inverse(a_mats, row, col, blk_diag):
    f32 = jnp.float32

    def mm(x, y):
        xh, xl = _bf16_terms(x, 2)
        yh, yl = (jnp.where(blk_diag, jnp.concatenate([t] * GDN_HEADS, axis=0), 0.0) for t in _bf16_terms(y, 2))
        return (jnp.dot(xh, yh, preferred_element_type=f32)
                + (jnp.dot(xh, yl, preferred_element_type=f32) + jnp.dot(xl, yh, preferred_element_type=f32)))

    same16 = (row >> 4) == (col >> 4)
    same32 = (row >> 5) == (col >> 5)
    eye = jnp.where(row == col, 1.0, 0.0)
    d1 = [jnp.where(same16, a, 0.0) for a in a_mats]
    d2 = [mm(d, d) for d in d1]
    d4 = [mm(d, d) for d in d2]
    d8 = [mm(d, d) for d in d4]
    lo = [mm(eye - a, eye + b) for a, b in zip(d1, d2)]
    hi = [mm(eye + a, eye + b) for a, b in zip(d4, d8)]
    t = [mm(a, b) for a, b in zip(lo, hi)]
    for pick in (lambda a: jnp.where(same32 & jnp.logical_not(same16), a, 0.0), lambda a: jnp.where(same32, 0.0, a)):
        mid = [mm(ti, pick(a)) for ti, a in zip(t, a_mats)]
        t = [ti - mm(m, ti) for ti, m in zip(t, mid)]
    return t


def _gdn_body(qkv_ref, z_ref, ab_ref, cw_ref, alog_ref, dtb_ref, nw_ref, o_ref, halo_ref, st_ref):
    f32, bf16 = jnp.float32, jnp.bfloat16
    L = qkv_ref.shape[0]
    C, H, dh, W = GDN_CHUNK, GDN_HEADS, GDN_HEAD_DIM, GDN_WIDTH

    @pl.when(pl.program_id(1) == 0)
    def _():
        halo_ref[...] = jnp.zeros_like(halo_ref)
        st_ref[...] = jnp.zeros_like(st_ref)

    x = qkv_ref[...]
    prev = halo_ref[...]
    row8 = lax.broadcasted_iota(jnp.int32, prev.shape, 0)
    acc = cw_ref[GDN_CONV - 1:GDN_CONV, :] * x
    for s in range(1, GDN_CONV):
        xs = pltpu.roll(x, s, 0)
        head = jnp.where(row8 < s, pltpu.roll(prev, s, 0), xs[0:SUBLANES])
        xs = jnp.concatenate([head, xs[SUBLANES:]], axis=0)
        acc = acc + cw_ref[GDN_CONV - 1 - s:GDN_CONV - s, :] * xs
    halo_ref[...] = x[L - SUBLANES:L]
    y = jax.nn.silu(acc)

    a_in, b_in = ab_ref[:, 0:H], ab_ref[:, H:2 * H]
    beta_all = jax.nn.sigmoid(b_in)
    g_all = -jnp.exp(alog_ref[...]) * jax.nn.softplus(a_in + dtb_ref[...])

    row = lax.broadcasted_iota(jnp.int32, (C, GDN_PACK), 0)
    col = lax.broadcasted_iota(jnp.int32, (C, GDN_PACK), 1) & (C - 1)
    strict = row > col
    bd_r = lax.broadcasted_iota(jnp.int32, (GDN_PACK, GDN_PACK), 0)
    bd_c = lax.broadcasted_iota(jnp.int32, (GDN_PACK, GDN_PACK), 1)
    blk_diag = (bd_r >> 6) == (bd_c >> 6)

    lr = lax.broadcasted_iota(jnp.int32, (L, L), 0)
    lc = lax.broadcasted_iota(jnp.int32, (L, L), 1)
    same_chunk = (lr >> 6) == (lc >> 6)
    tri = jnp.where(same_chunk & (lr >= lc), 1.0, 0.0)
    head_of_lane = lax.broadcasted_iota(jnp.int32, (H, GDN_PACK), 1) >> 6
    spread = jnp.where(head_of_lane == lax.broadcasted_iota(jnp.int32, (H, GDN_PACK), 0), 1.0, 0.0)
    pr = lax.broadcasted_iota(jnp.int32, (L, GDN_PACK), 0) & (C - 1)
    pc = lax.broadcasted_iota(jnp.int32, (L, GDN_PACK), 1) & (C - 1)
    gc_all = _dot_exact_lhs(tri, g_all)
    gi_all = sum(jnp.dot(t, spread.astype(bf16), preferred_element_type=f32) for t in _bf16_terms(gc_all, 3))
    gj_all = _dot_exact_lhs(jnp.where(same_chunk, 1.0, 0.0), jnp.where(pr == pc, gi_all, 0.0))
    decay_all = jnp.exp(jnp.where(pr >= pc, gi_all - gj_all, -jnp.inf))
    egc_all = jnp.exp(gc_all)

    def l2n(t):
        return t * lax.rsqrt(jnp.sum(t * t, axis=-1, keepdims=True) + EPS)

    n_chunks = L // C
    chunk_rows = [slice(c * C, (c + 1) * C) for c in range(n_chunks)]
    qs, ks, vs, kbs, a_mats, qk_ps = [], [], [], [], [], []
    for rows in chunk_rows:
        kk, qk = [], []
        for h in range(H):
            q = l2n(y[rows, h * dh:(h + 1) * dh]) * dh ** -0.5
            k = l2n(y[rows, W + h * dh:W + (h + 1) * dh])
            kb = k * beta_all[rows, h:h + 1]
            qs.append(q), ks.append(k), vs.append(y[rows, 2 * W + h * dh:2 * W + (h + 1) * dh]), kbs.append(kb)
            kk.append(_dot_nt(kb.astype(bf16), k.astype(bf16)))
            qk.append(_dot_nt(q.astype(bf16), k.astype(bf16)))
        a_mats.append(jnp.where(strict, jnp.concatenate(kk, axis=1) * decay_all[rows], 0.0))
        qk_ps.append(jnp.concatenate(qk, axis=1) * decay_all[rows])
    t_ps = _gdn_packed_unit_lower_inverse(a_mats, row, col, blk_diag)
    for c, rows in enumerate(chunk_rows):
        beta, gc, egc = beta_all[rows], gc_all[rows], egc_all[rows]
        outs = []
        for h in range(H):
            i = c * H + h
            t_h = t_ps[c][:, h * C:(h + 1) * C].astype(bf16)
            e_h = egc[:, h:h + 1]
            u = jnp.dot(t_h, (vs[i] * beta[:, h:h + 1]).astype(bf16), preferred_element_type=f32)
            w = jnp.dot(t_h, (kbs[i] * e_h).astype(bf16), preferred_element_type=f32)
            state = st_ref[h]
            sb = state.astype(bf16)
            v_new = u - jnp.dot(w.astype(bf16), sb, preferred_element_type=f32)
            o = (jnp.dot((qs[i] * e_h).astype(bf16), sb, preferred_element_type=f32)
                 + jnp.dot(qk_ps[c][:, h * C:(h + 1) * C].astype(bf16), v_new.astype(bf16),
                           preferred_element_type=f32))
            g_last = gc[C - 1:C, h:h + 1]
            k_dec = (ks[i] * jnp.exp(g_last - gc[:, h:h + 1])).astype(bf16)
            st_ref[h] = state * jnp.exp(g_last) + lax.dot_general(
                k_dec, v_new.astype(bf16), (((0,), (0,)), ((), ())), preferred_element_type=f32)
            o = o * lax.rsqrt(jnp.mean(o * o, axis=-1, keepdims=True) + EPS) * nw_ref[...]
            outs.append(o * jax.nn.silu(z_ref[rows, h * dh:(h + 1) * dh]))
        o_ref[rows, :] = jnp.concatenate(outs, axis=1)


def gdn_mixer(proj, conv_w, a_log, dt_bias, norm_w):
    Bsz, S, _ = proj.shape
    f32 = jnp.float32
    L = min(GDN_ROWS, S)
    H, W, W3 = GDN_HEADS, GDN_WIDTH, 3 * GDN_WIDTH
    full = lambda *shape: pl.BlockSpec(shape, lambda bi, ci: (0,) * len(shape))
    rows = lambda n, col=0: pl.BlockSpec((None, L, n), lambda bi, ci: (bi, ci, col // n))
    return pl.pallas_call(
        _gdn_body,
        grid=(Bsz, S // L),
        in_specs=[rows(W3, PROJ_QKV), rows(W, PROJ_Z), rows(LANES, PROJ_SMALL),
                  full(GDN_CONV, W3), full(1, H), full(1, H), full(1, GDN_HEAD_DIM)],
        out_specs=rows(W),
        out_shape=jax.ShapeDtypeStruct((Bsz, S, W), f32),
        scratch_shapes=[pltpu.VMEM((SUBLANES, W3), f32), pltpu.VMEM((H, GDN_HEAD_DIM, GDN_HEAD_DIM), f32)],
        compiler_params=pltpu.CompilerParams(dimension_semantics=("arbitrary", "arbitrary"),
                                             vmem_limit_bytes=48 * 1024 * 1024),
        name="gdn_mixer",
    )(proj, proj, proj, conv_w.astype(f32), a_log.astype(f32).reshape(1, H),
      dt_bias.astype(f32).reshape(1, H), norm_w.astype(f32).reshape(1, GDN_HEAD_DIM))


def nsa_mixer(q, kv, gate_logits, q_norm, k_norm, cmp_pe, cmp_w1, cmp_w2, cos, sin):
    Bsz, S, _ = q.shape
    f32 = jnp.float32
    G, R, dh = NSA_KV_HEADS, NSA_HEADS // NSA_KV_HEADS, NSA_HEAD_DIM
    scale = dh ** -0.5
    q = rope(rms_norm(q.reshape(Bsz, S, NSA_HEADS, dh), q_norm), cos, sin)
    kc_raw, vc_raw, ks, vs, kw, vw = (t.reshape(Bsz, S, G, dh) for t in jnp.split(kv, 6, axis=-1))
    vs, vw = lax.optimization_barrier((vs, vw))
    ks = rope(rms_norm(ks, k_norm[1]), cos, sin)
    kw = rope(rms_norm(kw, k_norm[2]), cos, sin)

    assert CMP_LEN == 2 * CMP_STRIDE and S % CMP_STRIDE == 0
    n_cmp = (S - CMP_LEN) // CMP_STRIDE + 1
    win_idx = jnp.arange(n_cmp)[:, None] * CMP_STRIDE + jnp.arange(CMP_LEN)[None, :]
    cmp_end = win_idx[:, -1]

    def compress(t, pe, w1, w2):
        grp = t.reshape(Bsz, S // CMP_STRIDE, CMP_STRIDE, G, dh)
        blocks = jnp.concatenate([grp[:, :-1], grp[:, 1:]], axis=2) + pe[:, None, :]
        blocks = jnp.moveaxis(blocks, 3, 2).reshape(Bsz, n_cmp, G, CMP_LEN * dh)
        return jax.nn.gelu(blocks @ w1) @ w2

    cos_c, sin_c = rope_tables(cmp_end, dh)
    k_cmp = rope(rms_norm(compress(kc_raw, cmp_pe[0], cmp_w1[0], cmp_w2[0]), k_norm[0]), cos_c, sin_c)
    v_cmp = compress(vc_raw, cmp_pe[1], cmp_w1[1], cmp_w2[1])

    n_slc = S // SLC_BLOCK
    n_top = min(SLC_TOPK, n_slc)
    c0 = jnp.arange(n_cmp) * CMP_STRIDE
    s0 = jnp.arange(n_slc) * SLC_BLOCK
    overlap = jnp.clip(jnp.minimum(c0[:, None] + CMP_LEN, s0[None, :] + SLC_BLOCK)
                       - jnp.maximum(c0[:, None], s0[None, :]), 0, None).astype(f32) / CMP_LEN
    gates = jax.nn.sigmoid(gate_logits.astype(f32)).reshape(Bsz, S, G, R * 3).transpose(0, 2, 3, 1)

    bf16 = jnp.bfloat16
    n_cmp_pad = S // CMP_STRIDE
    pad_cmp = lambda t: jnp.pad(t.astype(bf16), ((0, 0), (0, n_cmp_pad - n_cmp), (0, 0), (0, 0)))
    keys = lambda t: t.astype(bf16).transpose(0, 2, 1, 3)
    vals = lambda t: t.astype(bf16).transpose(0, 2, 3, 1)
    q_rows = (q * scale).astype(bf16).reshape(Bsz, S, NSA_WIDTH)
    overlap_t = jnp.pad(overlap.T, ((0, 0), (0, n_cmp_pad - n_cmp))).astype(bf16)
    expand_t = ((jnp.arange(S) // SLC_BLOCK)[:, None] == jnp.arange(n_slc)[None, :]).astype(bf16)
    return _nsa_attention(q_rows, keys(pad_cmp(k_cmp)), vals(pad_cmp(v_cmp)), keys(ks), vals(vs), keys(kw), vals(vw),
                          gates, overlap_t, expand_t, n_top)


NSA_R = NSA_HEADS // NSA_KV_HEADS
SLC_TILE = 1024
MASK_NEG = -1e30


def _dot_nt(a, b):
    return lax.dot_general(a, b, (((1,), (1,)), ((), ())), preferred_element_type=jnp.float32)


def _nsa_body(q_ref, kc_ref, vct_ref, ks_ref, vst_ref, kw_ref, vwt_ref, gate_ref, ovt_ref, expt_ref, o_ref,
              *, n_top):
    f32, bf16 = jnp.float32, jnp.bfloat16
    QB, dh, R = q_ref.shape[0], ks_ref.shape[1], NSA_R
    S = ks_ref.shape[0]
    n_slc, n_cmp_pad = ovt_ref.shape
    M = R * QB
    t0 = pl.program_id(2) * QB
    q_blk = q_ref[...]
    q = jnp.concatenate([q_blk[:, r * dh:(r + 1) * dh] for r in range(R)], axis=0)
    per_head = lambda t: jnp.concatenate([t] * R, axis=1)

    band = S if S < WINDOW + QB else WINDOW + QB
    start = pl.multiple_of(jnp.maximum(t0 + QB - band, 0), QB)
    s = _dot_nt(kw_ref[pl.ds(start, band), :], q)
    diff = (t0 - start) + (lax.broadcasted_iota(jnp.int32, (band, QB), 1) - lax.broadcasted_iota(jnp.int32, (band, QB), 0))
    s = s + per_head(jnp.where((diff >= 0) & (diff < WINDOW), 0.0, -jnp.inf))
    p = jnp.exp(s - jnp.max(s, axis=0, keepdims=True))
    o_win = jnp.dot(vwt_ref[:, pl.ds(start, band)], p.astype(bf16), preferred_element_type=f32)
    o_win = o_win / jnp.sum(p, axis=0, keepdims=True)

    s = _dot_nt(kc_ref[...], q)
    n_idx = lax.broadcasted_iota(jnp.int32, (n_cmp_pad, QB), 0)
    t_idx = t0 + lax.broadcasted_iota(jnp.int32, (n_cmp_pad, QB), 1)
    valid = (n_idx * CMP_STRIDE + (CMP_LEN - 1) <= t_idx) & (n_idx < n_cmp_pad - 1)
    s = s + per_head(jnp.where(valid, 0.0, -jnp.inf))
    m = jnp.max(s, axis=0, keepdims=True)
    m = jnp.where(jnp.isfinite(m), m, 0.0)
    p = jnp.exp(s - m)
    p = p / jnp.maximum(jnp.sum(p, axis=0, keepdims=True), jnp.finfo(f32).tiny)
    o_cmp = jnp.dot(vct_ref[...], p.astype(bf16), preferred_element_type=f32)

    p_sum = p[:, 0:QB]
    for r in range(1, R):
        p_sum = p_sum + p[:, r * QB:(r + 1) * QB]
    imp = sum(jnp.dot(ovt_ref[...], t, preferred_element_type=f32) for t in _bf16_terms(p_sum, 3))
    j_idx = lax.broadcasted_iota(jnp.int32, (n_slc, QB), 0)
    t_lane = t0 + lax.broadcasted_iota(jnp.int32, (n_slc, QB), 1)
    cur = lax.shift_right_logical(t_lane, 6)
    forced = (j_idx == 0) | (j_idx == cur) | (j_idx == cur - 1)
    started = j_idx * SLC_BLOCK <= t_lane
    imp = jnp.where(forced, BIG, jnp.where(started, imp, -BIG))
    rank = jnp.zeros((n_slc, QB), f32)
    for jp in range(n_slc):
        row = imp[jp:jp + 1, :]
        ahead = (row > imp) | ((row == imp) & (j_idx > jp))
        rank = rank + jnp.where(ahead, 1.0, 0.0)
    sel = jnp.where(rank < n_top, 1.0, 0.0).astype(bf16)
    k_rel = lax.broadcasted_iota(jnp.int32, (SLC_TILE, QB), 0)
    t_q = t0 + lax.broadcasted_iota(jnp.int32, (SLC_TILE, QB), 1)

    def slc_step(kt, carry):
        m, l, acc = carry
        k0 = pl.multiple_of(kt * SLC_TILE, SLC_TILE)
        s = _dot_nt(ks_ref[pl.ds(k0, SLC_TILE), :], q)
        sel_keys = jnp.dot(expt_ref[pl.ds(k0, SLC_TILE), :], sel, preferred_element_type=f32)
        bias = jnp.where((sel_keys > 0.5) & (k_rel + k0 <= t_q), 0.0, MASK_NEG)
        s = s + per_head(bias)
        m_new = jnp.maximum(m, jnp.max(s, axis=0, keepdims=True))
        alpha = jnp.exp(m - m_new)
        p = jnp.exp(s - m_new)
        l = alpha * l + jnp.sum(p, axis=0, keepdims=True)
        acc = alpha * acc + jnp.dot(vst_ref[:, pl.ds(k0, SLC_TILE)], p.astype(bf16), preferred_element_type=f32)
        return m_new, l, acc

    n_tiles = (t0 + QB - 1) // SLC_TILE + 1
    init = (jnp.full((1, M), MASK_NEG, f32), jnp.zeros((1, M), f32), jnp.zeros((dh, M), f32))
    _, l, acc = lax.fori_loop(0, n_tiles, slc_step, init)
    o_slc = acc / l

    gate = gate_ref[...]
    g_cmp, g_slc, g_win = (jnp.concatenate([gate[3 * r + c:3 * r + c + 1, :] for r in range(R)], axis=1)
                           for c in range(3))
    o_t = g_cmp * o_cmp + g_slc * o_slc + g_win * o_win
    o_ref[...] = jnp.concatenate([o_t[:, r * QB:(r + 1) * QB].T for r in range(R)], axis=1)


def _nsa_attention(q, kc, vct, ks, vst, kw, vwt, gates, overlap_t, expand_t, n_top):
    Bsz, G, S, dh = ks.shape
    R = NSA_R
    n_cmp_pad = kc.shape[2]
    assert S % SLC_TILE == 0 and SLC_TILE % Q_BLOCK == 0, (S, SLC_TILE, Q_BLOCK)
    keys = lambda n: pl.BlockSpec((None, None, n, dh), lambda b, g, i: (b, g, 0, 0))
    vals = lambda n: pl.BlockSpec((None, None, dh, n), lambda b, g, i: (b, g, 0, 0))
    const = lambda a: pl.BlockSpec(a.shape, lambda b, g, i: (0, 0))
    return pl.pallas_call(
        functools.partial(_nsa_body, n_top=n_top),
        grid=(Bsz, G, S // Q_BLOCK),
        in_specs=[pl.BlockSpec((None, Q_BLOCK, R * dh), lambda b, g, i: (b, i, g)),
                  keys(n_cmp_pad), vals(n_cmp_pad), keys(S), vals(S), keys(S), vals(S),
                  pl.BlockSpec((None, None, 3 * R, Q_BLOCK), lambda b, g, i: (b, g, 0, i)),
                  const(overlap_t), const(expand_t)],
        out_specs=pl.BlockSpec((None, Q_BLOCK, R * dh), lambda b, g, i: (b, i, g)),
        out_shape=jax.ShapeDtypeStruct((Bsz, S, G * R * dh), jnp.float32),
        compiler_params=pltpu.CompilerParams(dimension_semantics=("arbitrary", "arbitrary", "arbitrary"),
                                             vmem_limit_bytes=48 * 1024 * 1024),
        name="nsa_attention",
    )(q, kc, vct, ks, vst, kw, vwt, gates, overlap_t, expand_t)


def hybrid_mixer(x, ln_mix, w_in, lam_re, lam_im, b_re, b_im, c_re, c_im, s5_d, s5_log_dt, glu_w, glu_b,
                 s5_out_norm, conv_w, a_log, dt_bias, gdn_norm_w, q_norm, k_norm, cmp_pe, cmp_w1,
                 cmp_w2, cos, sin):
    Bsz, S, D = x.shape
    proj = _in_proj(x.reshape(-1, D), ln_mix, _in_proj_weight(w_in)).reshape(Bsz, S, PROJ_W)
    y_s5 = s5_mixer_normed(proj, lam_re, lam_im, b_re, b_im, c_re, c_im, s5_d, s5_log_dt, glu_w, glu_b, s5_out_norm)
    y_gdn = gdn_mixer(proj, conv_w, a_log, dt_bias, gdn_norm_w)
    y_nsa = nsa_mixer(proj[..., PROJ_Q:PROJ_Q + NSA_WIDTH], proj[..., PROJ_KV:PROJ_KV + 6 * NSA_KV_WIDTH],
                      proj[..., PROJ_GATES:PROJ_GATES + 3 * NSA_HEADS], q_norm, k_norm, cmp_pe, cmp_w1, cmp_w2,
                      cos, sin)
    return y_s5, y_gdn, y_nsa


def hier_moe(x, h, logits, layer, w_gate, w_up, w_down):
    T, D = x.shape
    f32 = jnp.float32
    g_logits = logits[:, :N_GROUPS]
    grp = jnp.argmax(g_logits, axis=-1)
    p_grp = jnp.take_along_axis(jax.nn.softmax(g_logits, axis=-1), grp[:, None], axis=-1)
    e_logits = logits[:, N_GROUPS:N_ROUTER].reshape(T, N_GROUPS, EXPERTS_PER_GROUP)
    e_logits = jnp.take_along_axis(e_logits, grp[:, None, None], axis=1)[:, 0]
    local = jnp.arange(EXPERTS_PER_GROUP, dtype=jnp.int32)[None, :]
    rest, top_logit, top_local = e_logits, [], []
    for _ in range(TOP_K):
        idx = jnp.argmax(rest, axis=-1).astype(jnp.int32)
        top_logit.append(jnp.take_along_axis(rest, idx[:, None], axis=-1)[:, 0])
        top_local.append(idx)
        rest = jnp.where(local == idx[:, None], -jnp.inf, rest)
    top_logit, top_local = jnp.stack(top_logit, axis=-1), jnp.stack(top_local, axis=-1)
    weights = jax.nn.softmax(top_logit, axis=-1) * p_grp
    experts = grp[:, None].astype(jnp.int32) * EXPERTS_PER_GROUP + top_local
    n_assign = T * TOP_K
    e_flat = experts.reshape(-1)
    tok_flat = jnp.repeat(jnp.arange(T, dtype=jnp.int32), TOP_K)
    onehot = (e_flat[:, None] == jnp.arange(N_EXPERTS, dtype=jnp.int32)[None, :]).astype(f32)
    blocks = onehot.reshape(n_assign // COUNT_BLOCK, COUNT_BLOCK, N_EXPERTS)
    tri = jnp.tril(jnp.ones((COUNT_BLOCK, COUNT_BLOCK), f32))
    inside = jnp.einsum('ij,bje->bie', tri, blocks)
    before = jnp.cumsum(inside[:, -1, :], axis=0) - inside[:, -1, :]
    running = (inside + before[:, None, :]).reshape(n_assign, N_EXPERTS).astype(jnp.int32)
    counts = running[-1]
    padded = (counts + MOE_ROWS - 1) // MOE_ROWS * MOE_ROWS
    pad_start = jnp.cumsum(padded) - padded
    dest = (pad_start[e_flat] + jnp.take_along_axis(running, e_flat[:, None], axis=1)[:, 0] - 1).astype(jnp.int32)
    n_blk = (n_assign + MOE_ROWS - 1) // MOE_ROWS + N_EXPERTS
    cap = n_blk * MOE_ROWS
    slot_tok = jnp.zeros((cap,), jnp.int32).at[dest].set(tok_flat)
    seg_end = jnp.cumsum(padded)
    blk_start = jnp.arange(n_blk, dtype=seg_end.dtype) * MOE_ROWS
    blk_expert = jnp.minimum(jnp.sum((seg_end[None, :] <= blk_start[:, None]).astype(jnp.int32), axis=1),
                             N_EXPERTS - 1).astype(jnp.int32)
    n_used = (seg_end[-1] // MOE_ROWS).astype(jnp.int32).reshape(1)
    y = _moe_ffn(blk_expert, slot_tok, n_used, x, layer, w_gate.astype(f32), w_up.astype(f32),
                 w_down.astype(f32))
    return _moe_combine(dest, h, weights, y)


CMB_ROWS = 256


def _moe_combine_body(d_ref, h_ref, w_ref, y_hbm, o_ref, buf, sem):
    i = pl.program_id(0)
    n = pl.num_programs(0)
    rows = h_ref.shape[0]
    slot = i & 1

    def row_copy(step, r, k, to):
        src = y_hbm.at[pl.ds(d_ref[(step * rows + r) * TOP_K + k], 1), :]
        return pltpu.make_async_copy(src, buf.at[to, k, pl.ds(r, 1), :], sem.at[to])

    def start_step(step, to):
        for r in range(rows):
            for k in range(TOP_K):
                row_copy(step, r, k, to).start(priority=(r * TOP_K + k) % 2)

    def finish_step():
        for r in range(rows):
            for k in range(TOP_K):
                row_copy(i, r, k, slot).wait()
        out = h_ref[...]
        for k in range(TOP_K):
            out = out + w_ref[:, k:k + 1] * buf[slot, k]
        o_ref[...] = out

    @pl.when(i == 0)
    def _():
        start_step(0, 0)

    @pl.when(i + 1 < n)
    def _():
        start_step(i + 1, 1 - slot)
        finish_step()

    @pl.when(i + 1 == n)
    def _():
        finish_step()


def _moe_combine(dest, h, weights, y):
    T, D = h.shape
    rows = min(CMB_ROWS, T)
    return pl.pallas_call(
        _moe_combine_body,
        grid_spec=pltpu.PrefetchScalarGridSpec(
            num_scalar_prefetch=1, grid=(T // rows,),
            in_specs=[pl.BlockSpec((rows, D), lambda i, d: (i, 0)),
                      pl.BlockSpec((rows, TOP_K), lambda i, d: (i, 0)),
                      pl.BlockSpec(memory_space=pl.ANY)],
            out_specs=pl.BlockSpec((rows, D), lambda i, d: (i, 0)),
            scratch_shapes=[pltpu.VMEM((2, TOP_K, rows, D), jnp.float32), pltpu.SemaphoreType.DMA((2,))]),
        out_shape=jax.ShapeDtypeStruct((T, D), jnp.float32),
        compiler_params=pltpu.CompilerParams(dimension_semantics=("arbitrary",),
                                             vmem_limit_bytes=32 * 1024 * 1024),
        name="moe_combine",
    )(dest, h, weights, y)


MOE_ROWS = 256
COUNT_BLOCK = 128


def _moe_ffn_body(be_ref, tok_ref, nu_ref, x_hbm, wg_ref, wu_ref, wd_ref, o_ref, xbuf, sem, wg_bf, wu_bf, wd_bf):
    i = pl.program_id(0)
    n = nu_ref[0]
    slot = i & 1

    def row_copy(step, r, to):
        src = x_hbm.at[pl.ds(tok_ref[step * MOE_ROWS + r], 1), :]
        return pltpu.make_async_copy(src, xbuf.at[to, pl.ds(r, 1), :], sem.at[to])

    def start_step(step, to):
        for r in range(MOE_ROWS):
            row_copy(step, r, to).start(priority=r % 2)

    def finish_step():
        for r in range(MOE_ROWS):
            row_copy(i, r, slot).wait()
        x = xbuf[slot].astype(jnp.bfloat16)
        gate = jnp.dot(x, wg_bf[...], preferred_element_type=jnp.float32)
        up = jnp.dot(x, wu_bf[...], preferred_element_type=jnp.float32)
        hid = (jax.nn.silu(gate) * up).astype(jnp.bfloat16)
        o_ref[...] = jnp.dot(hid, wd_bf[...], preferred_element_type=jnp.float32)

    @pl.when(i == 0)
    def _():
        start_step(0, 0)

    @pl.when((i < n) & ((i == 0) | (be_ref[i] != be_ref[jnp.maximum(i - 1, 0)])))
    def _():
        wg_bf[...] = wg_ref[...].astype(jnp.bfloat16)
        wu_bf[...] = wu_ref[...].astype(jnp.bfloat16)
        wd_bf[...] = wd_ref[...].astype(jnp.bfloat16)

    @pl.when(i + 1 < n)
    def _():
        start_step(i + 1, 1 - slot)
        finish_step()

    @pl.when(i + 1 == n)
    def _():
        finish_step()

    @pl.when(i >= n)
    def _():
        o_ref[...] = jnp.zeros_like(o_ref)


def _moe_ffn(blk_expert, slot_tok, n_used, x, layer, w_gate, w_up, w_down):
    cap = slot_tok.shape[0]
    D = x.shape[1]
    n_blk = cap // MOE_ROWS
    De = w_gate.shape[-1]
    return pl.pallas_call(
        _moe_ffn_body,
        grid_spec=pltpu.PrefetchScalarGridSpec(
            num_scalar_prefetch=3, grid=(n_blk,),
            in_specs=[pl.BlockSpec(memory_space=pl.ANY),
                      pl.BlockSpec((None, None, D, De), lambda i, be, tok, nu: (layer, be[i], 0, 0)),
                      pl.BlockSpec((None, None, D, De), lambda i, be, tok, nu: (layer, be[i], 0, 0)),
                      pl.BlockSpec((None, None, De, D), lambda i, be, tok, nu: (layer, be[i], 0, 0))],
            out_specs=pl.BlockSpec((MOE_ROWS, D), lambda i, be, tok, nu: (i, 0)),
            scratch_shapes=[pltpu.VMEM((2, MOE_ROWS, D), jnp.float32), pltpu.SemaphoreType.DMA((2,)),
                            pltpu.VMEM((D, De), jnp.bfloat16), pltpu.VMEM((D, De), jnp.bfloat16),
                            pltpu.VMEM((De, D), jnp.bfloat16)]),
        out_shape=jax.ShapeDtypeStruct((cap, D), jnp.float32),
        compiler_params=pltpu.CompilerParams(dimension_semantics=("arbitrary",),
                                             vmem_limit_bytes=56 * 1024 * 1024),
        name="moe_ffn",
    )(blk_expert, slot_tok, n_used, x, w_gate, w_up, w_down)


def kernel(x, ln_mix, w_in, s5_lambda_re, s5_lambda_im, s5_b_re, s5_b_im, s5_c_re, s5_c_im, s5_d,
           s5_log_dt, s5_glu_w, s5_glu_b, s5_out_norm, gdn_conv_w, gdn_a_log, gdn_dt_bias, gdn_norm_w,
           nsa_q_norm, nsa_k_norm, nsa_cmp_pe, nsa_cmp_w1, nsa_cmp_w2, nsa_out_norm, w_out, ln_ffn,
           moe_group_w, moe_group_b, moe_expert_w, moe_expert_b, moe_w_gate, moe_w_up, moe_w_down):
    Bsz, S, D = x.shape
    cos, sin = rope_tables(jnp.arange(S), NSA_HEAD_DIM)
    for l in range(DEPTH):
        y_s5, y_gdn, y_nsa = hybrid_mixer(
            x, ln_mix[l], w_in[l], s5_lambda_re[l], s5_lambda_im[l], s5_b_re[l], s5_b_im[l], s5_c_re[l], s5_c_im[l],
            s5_d[l], s5_log_dt[l], s5_glu_w[l], s5_glu_b[l], s5_out_norm[l], gdn_conv_w[l], gdn_a_log[l],
            gdn_dt_bias[l], gdn_norm_w[l], nsa_q_norm[l], nsa_k_norm[l], nsa_cmp_pe[l], nsa_cmp_w1[l], nsa_cmp_w2[l],
            cos, sin)
        router_w = jnp.concatenate([moe_group_w[l], moe_expert_w[l]], axis=-1)
        router_b = jnp.concatenate([moe_group_b[l], moe_expert_b[l]], axis=-1)
        h, a, logits = _out_proj(x.reshape(-1, D), y_s5.reshape(-1, S5_WIDTH), y_gdn.reshape(-1, GDN_WIDTH),
                                 y_nsa.reshape(-1, NSA_WIDTH), nsa_out_norm[l], w_out[l], ln_ffn[l], router_w, router_b)
        x = hier_moe(a, h, logits, l, moe_w_gate, moe_w_up, moe_w_down).reshape(Bsz, S, D)
    return x
```
